```python
import functools
import jax, jax.numpy as jnp
from jax import lax
import numpy as np

D_MODEL = 2048
BATCH = 4
SEQ = 2048
DEPTH = 1
DEC_BATCH = 32
DEC_SEQ = 8
PAST_LEN = 8192
PAGE_SIZE = 128

HEAD_DIM = 64
N_HEADS = (D_MODEL // 2) // HEAD_DIM
N_KV_HEADS = N_HEADS // 4
ATTN_W = N_HEADS * HEAD_DIM
KV_W = N_KV_HEADS * HEAD_DIM
N_IDX_HEADS = 8
IDX_DIM = 64
TOPK_MAX = 256
Q_BLOCK = 128
ROPE_THETA = 10000.0
D_RNN = D_MODEL // 2
N_RNN_BLOCKS = 16
RNN_BLOCK = D_RNN // N_RNN_BLOCKS
CONV_W = 4
LRU_C = 8.0
N_GROUPS = 4
EXPERTS_PER_GROUP = 8
N_EXPERTS = N_GROUPS * EXPERTS_PER_GROUP
TOP_K_EXPERTS = 2
D_EXPERT = D_MODEL // 8
PLE_DIM = 256
EPS = 1e-6
IN_SPLITS = (D_RNN, D_RNN, ATTN_W, KV_W, KV_W, N_IDX_HEADS * IDX_DIM, IDX_DIM, N_IDX_HEADS, D_MODEL, D_MODEL)
D_IN = 2 * D_RNN + ATTN_W + 2 * KV_W + N_IDX_HEADS * IDX_DIM + IDX_DIM + N_IDX_HEADS + 2 * D_MODEL

kernel_name = 'hawk_dsa_hmoe_decode_step'


def _rmsnorm(x, g):
    xf = x.astype(jnp.float32)
    y = xf * lax.rsqrt(jnp.mean(xf * xf, axis=-1, keepdims=True) + EPS)
    return (y * g.astype(jnp.float32)).astype(x.dtype)


def _rope(x, pos):
    half = x.shape[-1] // 2
    inv = ROPE_THETA ** (-jnp.arange(half, dtype=jnp.float32) / half)
    ang = pos.astype(jnp.float32)[:, None] * inv[None, :]
    cos = jnp.cos(ang)[None, :, None, :]
    sin = jnp.sin(ang)[None, :, None, :]
    xf = x.astype(jnp.float32)
    x1, x2 = xf[..., :half], xf[..., half:]
    return jnp.concatenate([x1 * cos - x2 * sin, x2 * cos + x1 * sin], axis=-1).astype(x.dtype)


def _take(a, idx):
    return jax.vmap(lambda ab, ib: ab[ib])(a, idx)


def _dsa_attend(q, qi, wi, qpos, kidx, gather_kv, topk):
    f32 = jnp.float32
    L = kidx.shape[1]
    s = jnp.einsum('bthd,bsd->bths', qi.astype(f32), kidx.astype(f32)) * (IDX_DIM ** -0.5)
    score = jnp.einsum('bths,bth->bts', jax.nn.relu(s), wi.astype(f32)) * (N_IDX_HEADS ** -0.5)
    causal = jnp.arange(L, dtype=jnp.int32)[None, :] <= qpos[:, None]
    score = jnp.where(causal[None], score, jnp.finfo(f32).min)
    _, idx = lax.top_k(score, topk)
    valid = idx <= qpos[None, :, None]
    k_sel, v_sel = gather_kv(idx)
    b, tq = q.shape[:2]
    qg = q.reshape(b, tq, N_KV_HEADS, N_HEADS // N_KV_HEADS, HEAD_DIM).astype(f32)
    logits = jnp.einsum('btkgd,btskd->btkgs', qg, k_sel.astype(f32)) * (HEAD_DIM ** -0.5)
    logits = jnp.where(valid[:, :, None, None, :], logits, -jnp.inf)
    prob = jax.nn.softmax(logits, axis=-1)
    out = jnp.einsum('btkgs,btskd->btkgd', prob, v_sel.astype(f32))
    return out.reshape(b, tq, ATTN_W).astype(q.dtype)


def _prompt_attention(q, k, v, qi, ki, wi, pos):
    b, t = q.shape[:2]
    topk = min(TOPK_MAX, t // 4)
    nb = t // Q_BLOCK

    def gather_kv(idx):
        return _take(k, idx), _take(v, idx)

    def to_blocks(a):
        return jnp.swapaxes(a.reshape(b, nb, Q_BLOCK, *a.shape[2:]), 0, 1)

    def one_block(args):
        qb, qib, wib, pb = args
        return _dsa_attend(qb, qib, wib, pb, ki, gather_kv, topk)

    out = lax.map(one_block, (to_blocks(q), to_blocks(qi), to_blocks(wi), pos.reshape(nb, Q_BLOCK)))
    return jnp.swapaxes(out, 0, 1).reshape(b, t, ATTN_W)


def _sample_attention(q, k, v, qi, ki, wi, pos, cache_k, cache_v, cache_kidx, page_table):
    b, t = q.shape[:2]
    past = page_table.shape[1] * PAGE_SIZE
    topk = min(TOPK_MAX, (past + t) // 4)
    kidx_past = cache_kidx[page_table].reshape(b, past, IDX_DIM).astype(ki.dtype)
    kidx_all = jnp.concatenate([kidx_past, ki], axis=1)
    k_pool = cache_k.reshape(-1, N_KV_HEADS, HEAD_DIM)
    v_pool = cache_v.reshape(-1, N_KV_HEADS, HEAD_DIM)

    def gather_kv(idx):
        in_past = (idx < past)[..., None, None]
        pi = jnp.minimum(idx, past - 1)
        rows = _take(page_table, pi // PAGE_SIZE) * PAGE_SIZE + pi % PAGE_SIZE
        ni = jnp.clip(idx - past, 0, t - 1)
        k_sel = jnp.where(in_past, k_pool[rows].astype(k.dtype), _take(k, ni))
        v_sel = jnp.where(in_past, v_pool[rows].astype(v.dtype), _take(v, ni))
        return k_sel, v_sel

    return _dsa_attend(q, qi, wi, pos, kidx_all, gather_kv, topk)


def _rglru_branch(xr, gr, conv_prev, h_prev, conv_w, conv_b, w_a, b_a, w_x, b_x, lam):
    f32 = jnp.float32
    b, t, _ = xr.shape
    xpad = jnp.concatenate([conv_prev.astype(xr.dtype), xr], axis=1)
    xc = conv_b + sum(xpad[:, j:j + t] * conv_w[j] for j in range(CONV_W))
    conv_new = xpad[:, t:]
    xb = xc.reshape(b, t, N_RNN_BLOCKS, RNN_BLOCK)
    r = jax.nn.sigmoid(jnp.einsum('btni,nij->btnj', xb, w_a).reshape(b, t, D_RNN) + b_a)
    i = jax.nn.sigmoid(jnp.einsum('btni,nij->btnj', xb, w_x).reshape(b, t, D_RNN) + b_x)
    log_a = -LRU_C * r.astype(f32) * jax.nn.softplus(-lam.astype(f32))
    a = jnp.exp(log_a)
    u = jnp.sqrt(-jnp.expm1(2.0 * log_a)) * (i * xc).astype(f32)

    def step(h, au):
        a_t, u_t = au
        h = a_t * h + u_t
        return h, h

    h_last, hs = lax.scan(step, h_prev.astype(f32), (jnp.swapaxes(a, 0, 1), jnp.swapaxes(u, 0, 1)))
    hs = jnp.swapaxes(hs, 0, 1).astype(xr.dtype)
    return hs * jax.nn.gelu(gr), conv_new, h_last.astype(xr.dtype)


def _hier_moe(h, w_rg, b_rg, w_re, b_re, w1, w3, w2):
    f32 = jnp.float32
    b, t, _ = h.shape
    g_logit = (h @ w_rg).astype(f32) + b_rg.astype(f32)
    g_prob = jax.nn.softmax(g_logit, axis=-1)
    g_sel = jnp.argmax(g_logit, axis=-1)
    g_onehot = jax.nn.one_hot(g_sel, N_GROUPS, dtype=f32)
    e_logit = ((h @ w_re).astype(f32) + b_re.astype(f32)).reshape(b, t, N_GROUPS, EXPERTS_PER_GROUP)
    e_in_group = jnp.einsum('btge,btg->bte', e_logit, g_onehot)
    top_v, top_i = lax.top_k(e_in_group, TOP_K_EXPERTS)
    w_top = jax.nn.softmax(top_v, axis=-1) * jnp.max(g_prob, axis=-1, keepdims=True)
    e_idx = g_sel[..., None] * EXPERTS_PER_GROUP + top_i
    gates = jnp.einsum('btke,btk->bte', jax.nn.one_hot(e_idx, N_EXPERTS, dtype=f32), w_top).astype(h.dtype)
    hid = jax.nn.silu(jnp.einsum('btd,edf->btef', h, w1)) * jnp.einsum('btd,edf->btef', h, w3)
    return jnp.einsum('btef,efd->btd', hid * gates[..., None], w2)


def _layer(x, p, pos, conv_prev, h_prev, attend, ln1, w_in, q_norm, k_norm, conv_w, conv_b,
           w_a, b_a, w_x, b_x, lam, w_br_rnn, w_br_attn, w_out, ln2, w_rg, b_rg, w_re, b_re,
           w1, w3, w2, ln3, w_ple_gate, w_ple_proj):
    b, t, _ = x.shape
    h = _rmsnorm(x, ln1)
    offs = np.cumsum(IN_SPLITS)[:-1].tolist()
    xr, gr, q, k, v, qi, ki, wi, gm_r, gm_a = jnp.split(h @ w_in, offs, axis=-1)
    q = _rope(_rmsnorm(q.reshape(b, t, N_HEADS, HEAD_DIM), q_norm), pos)
    k = _rope(_rmsnorm(k.reshape(b, t, N_KV_HEADS, HEAD_DIM), k_norm), pos)
    v = v.reshape(b, t, N_KV_HEADS, HEAD_DIM)
    qi = _rope(qi.reshape(b, t, N_IDX_HEADS, IDX_DIM), pos)
    ki = _rope(ki[:, :, None, :], pos)[:, :, 0, :]
    attn = attend(q, k, v, qi, ki, wi, pos)
    rnn, conv_new, h_new = _rglru_branch(xr, gr, conv_prev, h_prev, conv_w, conv_b, w_a, b_a, w_x, b_x, lam)
    mixed = jax.nn.sigmoid(gm_r) * (rnn @ w_br_rnn) + jax.nn.sigmoid(gm_a) * (attn @ w_br_attn)
    x = x + mixed @ w_out
    x = x + _hier_moe(_rmsnorm(x, ln2), w_rg, b_rg, w_re, b_re, w1, w3, w2)
    x = x + jax.nn.sigmoid(_rmsnorm(x, ln3) @ w_ple_gate) * (p @ w_ple_proj)
    return x, (k, v, ki, conv_new, h_new)


def setup_inputs(seed: int = 0) -> dict:
    key = jax.random.key(seed)
    keys = jax.random.split(key, 40)
    ctr = [0]

    def nk():
        ctr[0] += 1
        return keys[ctr[0] - 1]

    def nrm(shape, scale):
        return scale * jax.random.normal(nk(), shape, jnp.float32)

    def gain(shape):
        return 1.0 + nrm(shape, 0.05)

    n_pages = PAST_LEN // PAGE_SIZE
    n_used = DEC_BATCH * n_pages
    n_pool = n_used + max(1, n_used // 4)
    page_table = jax.random.permutation(nk(), n_pool)[:n_used].reshape(DEC_BATCH, n_pages).astype(jnp.int32)
    a_pow = jax.random.uniform(nk(), (DEPTH, D_RNN), jnp.float32, 0.9, 0.999)
    a_base = a_pow ** (1.0 / LRU_C)
    lam = jnp.log(a_base) - jnp.log1p(-a_base)
    return {
        'x_prompt': nrm((BATCH, SEQ, D_MODEL), 1.0),
        'x_sample': nrm((DEC_BATCH, DEC_SEQ, D_MODEL), 1.0),
        'p_prompt': nrm((DEPTH, BATCH, SEQ, PLE_DIM), 1.0),
        'p_sample': nrm((DEPTH, DEC_BATCH, DEC_SEQ, PLE_DIM), 1.0),
        'cache_k': nrm((DEPTH, n_pool, PAGE_SIZE, N_KV_HEADS, HEAD_DIM), 1.0),
        'cache_v': nrm((DEPTH, n_pool, PAGE_SIZE, N_KV_HEADS, HEAD_DIM), 1.0),
        'cache_kidx': nrm((DEPTH, n_pool, PAGE_SIZE, IDX_DIM), 1.0),
        'state_conv': nrm((DEPTH, DEC_BATCH, CONV_W - 1, D_RNN), 1.0),
        'state_h': nrm((DEPTH, DEC_BATCH, D_RNN), 0.5),
        'page_table': page_table,
        'ln1': gain((DEPTH, D_MODEL)),
        'w_in': nrm((DEPTH, D_MODEL, D_IN), D_MODEL ** -0.5),
        'q_norm': gain((DEPTH, HEAD_DIM)),
        'k_norm': gain((DEPTH, HEAD_DIM)),
        'conv_w': nrm((DEPTH, CONV_W, D_RNN), CONV_W ** -0.5),
        'conv_b': nrm((DEPTH, D_RNN), 0.01),
        'w_a': nrm((DEPTH, N_RNN_BLOCKS, RNN_BLOCK, RNN_BLOCK), RNN_BLOCK ** -0.5),
        'b_a': nrm((DEPTH, D_RNN), 0.01),
        'w_x': nrm((DEPTH, N_RNN_BLOCKS, RNN_BLOCK, RNN_BLOCK), RNN_BLOCK ** -0.5),
        'b_x': nrm((DEPTH, D_RNN), 0.01),
        'lam': lam,
        'w_br_rnn': nrm((DEPTH, D_RNN, D_MODEL), D_RNN ** -0.5),
        'w_br_attn': nrm((DEPTH, ATTN_W, D_MODEL), ATTN_W ** -0.5),
        'w_out': nrm((DEPTH, D_MODEL, D_MODEL), D_MODEL ** -0.5),
        'ln2': gain((DEPTH, D_MODEL)),
        'w_rg': nrm((DEPTH, D_MODEL, N_GROUPS), D_MODEL ** -0.5),
        'b_rg': nrm((DEPTH, N_GROUPS), 0.01),
        'w_re': nrm((DEPTH, D_MODEL, N_EXPERTS), D_MODEL ** -0.5),
        'b_re': nrm((DEPTH, N_EXPERTS), 0.01),
        'w1': nrm((DEPTH, N_EXPERTS, D_MODEL, D_EXPERT), D_MODEL ** -0.5),
        'w3': nrm((DEPTH, N_EXPERTS, D_MODEL, D_EXPERT), D_MODEL ** -0.5),
        'w2': nrm((DEPTH, N_EXPERTS, D_EXPERT, D_MODEL), D_EXPERT ** -0.5),
        'ln3': gain((DEPTH, D_MODEL)),
        'w_ple_gate': nrm((DEPTH, D_MODEL, D_MODEL), D_MODEL ** -0.5),
        'w_ple_proj': nrm((DEPTH, PLE_DIM, D_MODEL), PLE_DIM ** -0.5),
    }


def reference(x_prompt, x_sample, p_prompt, p_sample, cache_k, cache_v, cache_kidx, state_conv, state_h,
              page_table, ln1, w_in, q_norm, k_norm, conv_w, conv_b, w_a, b_a, w_x, b_x, lam,
              w_br_rnn, w_br_attn, w_out, ln2, w_rg, b_rg, w_re, b_re, w1, w3, w2, ln3,
              w_ple_gate, w_ple_proj):
    bp, tp, _ = x_prompt.shape
    ts = x_sample.shape[1]
    past = page_table.shape[1] * PAGE_SIZE
    pos_p = jnp.arange(tp, dtype=jnp.int32)
    pos_s = past + jnp.arange(ts, dtype=jnp.int32)
    conv0 = jnp.zeros((bp, CONV_W - 1, D_RNN), x_prompt.dtype)
    h0 = jnp.zeros((bp, D_RNN), x_prompt.dtype)
    weights = (ln1, w_in, q_norm, k_norm, conv_w, conv_b, w_a, b_a, w_x, b_x, lam, w_br_rnn, w_br_attn,
               w_out, ln2, w_rg, b_rg, w_re, b_re, w1, w3, w2, ln3, w_ple_gate, w_ple_proj)
    yp, ys = x_prompt, x_sample
    st_p, st_s = [], []
    for i in range(DEPTH):
        lw = [w[i] for w in weights]
        yp, sp = _layer(yp, p_prompt[i], pos_p, conv0, h0, _prompt_attention, *lw)
        sample_attend = functools.partial(_sample_attention, cache_k=cache_k[i], cache_v=cache_v[i],
                                          cache_kidx=cache_kidx[i], page_table=page_table)
        ys, ss = _layer(ys, p_sample[i], pos_s, state_conv[i], state_h[i], sample_attend, *lw)
        st_p.append(sp)
        st_s.append(ss)

    def stack(sts, j):
        return jnp.stack([s[j] for s in sts])

    return (yp, ys, stack(st_p, 0), stack(st_p, 1), stack(st_p, 2), stack(st_p, 3), stack(st_p, 4),
            stack(st_s, 0), stack(st_s, 1), stack(st_s, 2), stack(st_s, 3), stack(st_s, 4))
```

```python
import functools

import jax
import jax.numpy as jnp
import numpy as np
from jax import lax
from jax.experimental import pallas as pl
from jax.experimental.pallas import tpu as pltpu

F32 = jnp.float32
BF16 = jnp.bfloat16

D_MODEL = 2048
HEAD_DIM = 64
N_HEADS = 16
N_KV_HEADS = 4
ATTN_W = N_HEADS * HEAD_DIM
KV_W = N_KV_HEADS * HEAD_DIM
N_IDX_HEADS = 8
IDX_DIM = 64
TOPK_MAX = 256
ROPE_THETA = 10000.0
D_RNN = 1024
N_RNN_BLOCKS = 16
RNN_BLOCK = 64
CONV_W = 4
LRU_C = 8.0
N_GROUPS = 4
EXPERTS_PER_GROUP = 8
N_EXPERTS = 32
D_EXPERT = 256
PLE_DIM = 256
PAGE_SIZE = 128
EPS = 1e-6

LANES = 128
N_MAIN = 8192
COL_RAW_END = 2 * D_RNN
COL_NR_END = 4096
IDX_SCALE = (IDX_DIM ** -0.5) * (N_IDX_HEADS ** -0.5)
QK_SCALE = HEAD_DIM ** -0.5
F32_MIN = float(np.finfo(np.float32).min)
INT_MIN = -2147483648
KEY_NEG_INF = INT_MIN + 0x7FFFFF
VMEM_LIMIT = 56 * 1024 * 1024


def _dot(a, b):
    return jnp.dot(a, b, preferred_element_type=F32)


def _dot_nt(a, b):
    return lax.dot_general(a, b, (((1,), (1,)), ((), ())), preferred_element_type=F32)


def _sigmoid(x):
    return 1.0 / (1.0 + jnp.exp(-x))


def _rms(x, g):
    return x * lax.rsqrt(jnp.mean(x * x, axis=-1, keepdims=True) + EPS) * g


def _split_bf16(x):
    hi = x.astype(BF16)
    lo = (x - hi.astype(F32)).astype(BF16)
    return hi, lo


def _rope_chunks(y, c, s):
    lane = lax.broadcasted_iota(jnp.int32, (1, LANES), 1)
    first_half = (lane % HEAD_DIM) < (HEAD_DIM // 2)
    outs = []
    for k in range(y.shape[1] // LANES):
        yc = y[:, k * LANES:(k + 1) * LANES]
        partner = jnp.where(first_half, pltpu.roll(yc, LANES - HEAD_DIM // 2, 1),
                            pltpu.roll(yc, HEAD_DIM // 2, 1))
        outs.append(yc * c + partner * s)
    return outs[0] if len(outs) == 1 else jnp.concatenate(outs, axis=1)


def _inproj_kernel(x_ref, ln_ref, w_ref, ws_ref, ctl_ref, cs_ref, sn_ref, bd_ref,
                   o_ref, os_ref, h_ref, *, tn):
    j = pl.program_id(1)

    @pl.when(j == 0)
    def _():
        hb = _rms(x_ref[...], ln_ref[...]).astype(BF16)
        h_ref[...] = hb
        ys = _dot(hb, ws_ref[...])
        lane = lax.broadcasted_iota(jnp.int32, (1, LANES), 1)
        os_ref[...] = jnp.where(lane < IDX_DIM, _rope_chunks(ys, cs_ref[...], sn_ref[...]), ys)

    y = _dot(h_ref[...], w_ref[...])

    @pl.when(j < COL_RAW_END // tn)
    def _():
        o_ref[...] = y

    @pl.when((j >= COL_RAW_END // tn) & (j < COL_NR_END // tn))
    def _():
        ctl = ctl_ref[...]
        gain, norm_on, rope_on, post = ctl[0:1], ctl[1:2], ctl[2:3], ctl[3:4]
        hi, lo = _split_bf16(y * y)
        ss = _dot(hi, bd_ref[...]) + _dot(lo, bd_ref[...])
        yn = jnp.where(norm_on > 0.0, y * lax.rsqrt(ss * (1.0 / HEAD_DIM) + EPS) * gain, y)
        yr = jnp.where(rope_on > 0.0, _rope_chunks(yn, cs_ref[...], sn_ref[...]), yn)
        o_ref[...] = yr * post

    @pl.when(j >= COL_NR_END // tn)
    def _():
        o_ref[...] = _sigmoid(y)


def _inproj(x, ln1, w_main, w_small, colctl, cs, sn, bd, *, tm, tn=512):
    t = x.shape[0]
    grid = (t // tm, N_MAIN // tn)
    return pl.pallas_call(
        functools.partial(_inproj_kernel, tn=tn),
        grid=grid,
        in_specs=[
            pl.BlockSpec((tm, D_MODEL), lambda i, j: (i, 0)),
            pl.BlockSpec((1, D_MODEL), lambda i, j: (0, 0)),
            pl.BlockSpec((D_MODEL, tn), lambda i, j: (0, j)),
            pl.BlockSpec((D_MODEL, LANES), lambda i, j: (0, 0)),
            pl.BlockSpec((8, tn), lambda i, j: (0, j)),
            pl.BlockSpec((tm, LANES), lambda i, j: (i, 0)),
            pl.BlockSpec((tm, LANES), lambda i, j: (i, 0)),
            pl.BlockSpec((tn, tn), lambda i, j: (0, 0)),
        ],
        out_specs=[
            pl.BlockSpec((tm, tn), lambda i, j: (i, j)),
            pl.BlockSpec((tm, LANES), lambda i, j: (i, 0)),
        ],
        out_shape=[
            jax.ShapeDtypeStruct((t, N_MAIN), F32),
            jax.ShapeDtypeStruct((t, LANES), F32),
        ],
        scratch_shapes=[pltpu.VMEM((tm, D_MODEL), BF16)],
        compiler_params=pltpu.CompilerParams(
            dimension_semantics=("arbitrary", "arbitrary"), vmem_limit_bytes=VMEM_LIMIT),
        name="inproj",
    )(x, ln1, w_main, w_small, colctl, cs, sn, bd)


def _rglru_kernel(x_ref, g_ref, c0_ref, h0_ref, cw_ref, cb_ref, wa_ref, ba_ref, wx_ref, bx_ref,
                  lam_ref, o_ref, hl_ref, xs_ref, a_ref, b_ref, hc_ref):
    t = pl.program_id(1)
    tt = x_ref.shape[0]

    @pl.when(t == 0)
    def _():
        xs_ref[0:8, :] = c0_ref[...]
        hc_ref[...] = h0_ref[...]

    xs_ref[8:8 + tt, :] = x_ref[...]
    cw = cw_ref[...]
    taps = (xs_ref[5:5 + tt, :] * cw[0:1] + xs_ref[6:6 + tt, :] * cw[1:2]
            + xs_ref[7:7 + tt, :] * cw[2:3] + xs_ref[8:8 + tt, :] * cw[3:4])
    xc = cb_ref[...] + taps
    xs_ref[0:8, :] = xs_ref[tt:tt + 8, :]

    xcb = xc.astype(BF16)
    ra, ri = [], []
    for c in range(wa_ref.shape[0]):
        blk = xcb[:, c * 256:(c + 1) * 256]
        ra.append(_dot(blk, wa_ref[c]))
        ri.append(_dot(blk, wx_ref[c]))
    r = _sigmoid(jnp.concatenate(ra, axis=1) + ba_ref[...])
    ig = _sigmoid(jnp.concatenate(ri, axis=1) + bx_ref[...])
    nlam = -lam_ref[...]
    softplus = jnp.maximum(nlam, 0.0) + jnp.log1p(jnp.exp(-jnp.abs(nlam)))
    log_a = (-LRU_C) * r * softplus
    a = jnp.exp(log_a)
    u = jnp.sqrt(jnp.tanh(-log_a) * (a * a + 1.0)) * (ig * xc)

    n8 = tt // 8
    a3 = a.reshape(n8, 8, D_RNN)
    b3 = u.reshape(n8, 8, D_RNN)
    sub = lax.broadcasted_iota(jnp.int32, (1, 8, 1), 1)
    for s in (1, 2, 4):
        a_prev = pltpu.roll(a3, s, 1)
        b_prev = pltpu.roll(b3, s, 1)
        m = sub >= s
        b3 = jnp.where(m, a3 * b_prev + b3, b3)
        a3 = jnp.where(m, a3 * a_prev, a3)
    a_ref[...] = a3.reshape(tt, D_RNN)
    b_ref[...] = b3.reshape(tt, D_RNN)

    def chain(k, carry):
        i0 = pl.multiple_of(k * 8, 8)
        h8 = a_ref[pl.ds(i0, 8), :] * carry + b_ref[pl.ds(i0, 8), :]
        b_ref[pl.ds(i0, 8), :] = h8
        return h8[7:8, :]

    carry = lax.fori_loop(0, n8, chain, hc_ref[...])
    hc_ref[...] = carry
    g = g_ref[...]
    gelu = 0.5 * g * (1.0 + jnp.tanh(0.7978845608028654 * (g + 0.044715 * (g * g * g))))
    o_ref[...] = b_ref[...] * gelu

    @pl.when(t == pl.num_programs(1) - 1)
    def _():
        hl_ref[...] = carry


def _rglru(proj, conv0, h0, cw, cb, wa_bd, ba, wx_bd, bx, lam, *, n_seq, tt):
    t_total = proj.shape[0]
    nt = t_total // (n_seq * tt)
    full = lambda shape: pl.BlockSpec(shape, lambda b, t: (0,) * len(shape))
    return pl.pallas_call(
        _rglru_kernel,
        grid=(n_seq, nt),
        in_specs=[
            pl.BlockSpec((tt, D_RNN), lambda b, t: (b * nt + t, 0)),
            pl.BlockSpec((tt, D_RNN), lambda b, t: (b * nt + t, 1)),
            pl.BlockSpec((None, 8, D_RNN), lambda b, t: (b, 0, 0)),
            pl.BlockSpec((None, 1, D_RNN), lambda b, t: (b, 0, 0)),
            full((CONV_W, D_RNN)), full((1, D_RNN)),
            full(wa_bd.shape), full((1, D_RNN)),
            full(wx_bd.shape), full((1, D_RNN)),
            full((1, D_RNN)),
        ],
        out_specs=[
            pl.BlockSpec((tt, D_RNN), lambda b, t: (b * nt + t, 0)),
            pl.BlockSpec((None, 1, D_RNN), lambda b, t: (b, 0, 0)),
        ],
        out_shape=[
            jax.ShapeDtypeStruct((t_total, D_RNN), F32),
            jax.ShapeDtypeStruct((n_seq, 1, D_RNN), F32),
        ],
        scratch_shapes=[
            pltpu.VMEM((tt + 8, D_RNN), F32),
            pltpu.VMEM((tt, D_RNN), F32),
            pltpu.VMEM((tt, D_RNN), F32),
            pltpu.VMEM((1, D_RNN), F32),
        ],
        compiler_params=pltpu.CompilerParams(
            dimension_semantics=("arbitrary", "arbitrary"), vmem_limit_bytes=VMEM_LIMIT),
        name="rglru",
    )(proj, proj, conv0, h0, cw, cb, wa_bd, ba, wx_bd, bx, lam)


def _select_topk(s, kk):
    rows, n = s.shape
    kkf = float(kk)

    def key_to_f32(w):
        k = w ^ INT_MIN
        bits = jnp.where(k >= 0, k, k ^ 0x7FFFFFFF)
        return k, lax.bitcast_convert_type(bits, F32)

    def vbody(it, w):
        cand_w = w | jnp.left_shift(jnp.int32(1), 31 - it)
        cand_k, cand_f = key_to_f32(cand_w)
        cnt = jnp.sum(jnp.where(s >= cand_f, 1.0, 0.0), axis=1, keepdims=True)
        ok = (cnt >= kkf) | (cand_k < KEY_NEG_INF)
        return jnp.where(ok, cand_w, w)

    w = lax.fori_loop(0, 32, vbody, jnp.zeros((rows, 1), jnp.int32))
    _, thr = key_to_f32(w)
    gt = s > thr
    eq = s == thr
    need = kkf - jnp.sum(jnp.where(gt, 1.0, 0.0), axis=1, keepdims=True)
    col = lax.broadcasted_iota(jnp.int32, (1, n), 1)
    nbits = int(n).bit_length()

    def jbody(it, jmax):
        cand = jmax | jnp.left_shift(jnp.int32(1), nbits - 1 - it)
        cnt = jnp.sum(jnp.where(eq & (col < cand), 1.0, 0.0), axis=1, keepdims=True)
        return jnp.where(cnt <= need, cand, jmax)

    jmax = lax.fori_loop(0, nbits, jbody, jnp.zeros((rows, 1), jnp.int32))
    return gt | (eq & (col < jmax))


def _pattn_kernel(q_ref, qi_ref, sm_ref, k_ref, v_ref, ksm_ref, o_ref,
                  kb_ref, vb_ref, kib_ref, s_ref, *, kc):
    i = pl.program_id(1)
    tq = q_ref.shape[0]
    n_keys = k_ref.shape[0]
    lane = lax.broadcasted_iota(jnp.int32, (1, LANES), 1)

    @pl.when(i == 0)
    def _():
        kb_ref[...] = k_ref[...].astype(BF16)
        vb_ref[...] = v_ref[...].astype(BF16)
        kib_ref[...] = jnp.where(lane < IDX_DIM, ksm_ref[...], 0.0).astype(BF16)

    sm = sm_ref[...]
    qi = qi_ref[...]
    qrows, wrows = [], []
    for h in range(N_IDX_HEADS):
        blk = qi[:, (h // 2) * LANES:(h // 2 + 1) * LANES]
        if h % 2 == 1:
            blk = pltpu.roll(blk, IDX_DIM, 1)
        qrows.append(jnp.where(lane < IDX_DIM, blk, 0.0))
        wrows.append(sm[:, IDX_DIM + h:IDX_DIM + h + 1])
    qst = jnp.concatenate(qrows, axis=0).astype(BF16)
    wst = jnp.concatenate(wrows, axis=0) * IDX_SCALE
    qpos = i * tq + lax.broadcasted_iota(jnp.int32, (tq, 1), 0)
    for c in range(n_keys // kc):
        s = jnp.maximum(_dot_nt(qst, kib_ref[c * kc:(c + 1) * kc, :]), 0.0) * wst
        sc = s[0:tq]
        for h in range(1, N_IDX_HEADS):
            sc = sc + s[h * tq:(h + 1) * tq]
        col = c * kc + lax.broadcasted_iota(jnp.int32, (1, kc), 1)
        s_ref[:, c * kc:(c + 1) * kc] = jnp.where(col <= qpos, sc, F32_MIN)

    sel = _select_topk(s_ref[...], min(TOPK_MAX, n_keys // 4))
    colf = lax.broadcasted_iota(jnp.int32, (1, n_keys), 1)
    s_ref[...] = jnp.where(sel & (colf <= qpos), 0.0, -jnp.inf)
    bias = s_ref[...]
    bias4 = jnp.concatenate([bias] * 4, axis=0)

    q = q_ref[...]
    outs = [None] * N_HEADS
    for g in range(N_KV_HEADS):
        lo = (g % 2) * HEAD_DIM
        keep = (lane >= lo) & (lane < lo + HEAD_DIM)
        rows = []
        for j in range(4):
            h = 4 * g + j
            blk = q[:, (h // 2) * LANES:(h // 2 + 1) * LANES]
            if h % 2 != g % 2:
                blk = pltpu.roll(blk, HEAD_DIM, 1)
            piece = jnp.where(keep, blk, 0.0)
            zero = jnp.zeros_like(piece)
            rows.append(jnp.concatenate([piece, zero] if g < 2 else [zero, piece], axis=1))
        qbd = jnp.concatenate(rows, axis=0).astype(BF16)
        logits = _dot_nt(qbd, kb_ref[...]) + bias4
        m = jnp.max(logits, axis=1, keepdims=True)
        p = jnp.exp(logits - m)
        denom = jnp.sum(p, axis=1, keepdims=True)
        acc = _dot(p.astype(BF16), vb_ref[...]) / denom
        for j in range(4):
            outs[4 * g + j] = acc[j * tq:(j + 1) * tq, (g // 2) * LANES:(g // 2 + 1) * LANES]
    for c in range(N_HEADS // 2):
        g = (2 * c) // 4
        even, odd = outs[2 * c], outs[2 * c + 1]
        if g % 2 == 1:
            even = pltpu.roll(even, HEAD_DIM, 1)
        else:
            odd = pltpu.roll(odd, HEAD_DIM, 1)
        o_ref[:, c * LANES:(c + 1) * LANES] = jnp.where(lane < HEAD_DIM, even, odd).astype(BF16)


def _prompt_attention(proj, small, *, n_batch, seq, tq=128, kc=512):
    nq = seq // tq
    return pl.pallas_call(
        functools.partial(_pattn_kernel, kc=kc),
        grid=(n_batch, nq),
        in_specs=[
            pl.BlockSpec((tq, ATTN_W), lambda b, i: (b * nq + i, 2)),
            pl.BlockSpec((tq, N_IDX_HEADS * IDX_DIM), lambda b, i: (b * nq + i, 7)),
            pl.BlockSpec((tq, LANES), lambda b, i: (b * nq + i, 0)),
            pl.BlockSpec((seq, KV_W), lambda b, i: (b, 12)),
            pl.BlockSpec((seq, KV_W), lambda b, i: (b, 13)),
            pl.BlockSpec((seq, LANES), lambda b, i: (b, 0)),
        ],
        out_specs=pl.BlockSpec((tq, ATTN_W), lambda b, i: (b * nq + i, 0)),
        out_shape=jax.ShapeDtypeStruct((n_batch * seq, ATTN_W), BF16),
        scratch_shapes=[
            pltpu.VMEM((seq, KV_W), BF16),
            pltpu.VMEM((seq, KV_W), BF16),
            pltpu.VMEM((seq, LANES), BF16),
            pltpu.VMEM((tq, seq), F32),
        ],
        compiler_params=pltpu.CompilerParams(
            dimension_semantics=("arbitrary", "arbitrary"), vmem_limit_bytes=VMEM_LIMIT),
        name="prompt_attention",
    )(proj, proj, small, proj, proj, small)


PAGES_PER_STEP = 8


def _sidx_kernel(pt_ref, qst_ref, w_ref, sm_ref, *refs, n_chunks, n_new):
    pages = refs[:PAGES_PER_STEP]
    o_ref, s_ref = refs[PAGES_PER_STEP], refs[PAGES_PER_STEP + 1]
    c = pl.program_id(1)
    qst = qst_ref[...]
    w = w_ref[...] * IDX_SCALE

    def score(kp):
        s = jnp.maximum(_dot_nt(qst, kp), 0.0) * w
        out = s[0:n_new]
        for h in range(1, N_IDX_HEADS):
            out = out + s[h * n_new:(h + 1) * n_new]
        return out

    for r in range(PAGES_PER_STEP):
        s_ref[PAGES_PER_STEP * c + r] = score(pages[r][...].astype(BF16))

    @pl.when(c == n_chunks - 1)
    def _():
        n_past_blocks = n_chunks * PAGES_PER_STEP
        past = n_past_blocks * PAGE_SIZE
        k_new = sm_ref[...][:, 0:IDX_DIM]
        kp = jnp.concatenate([k_new, jnp.zeros((PAGE_SIZE - n_new, IDX_DIM), F32)], axis=0)
        lane = lax.broadcasted_iota(jnp.int32, (n_new, LANES), 1)
        trow = lax.broadcasted_iota(jnp.int32, (n_new, LANES), 0)
        s_new = score(kp.astype(BF16))
        s_ref[n_past_blocks] = jnp.where(lane < n_new, jnp.where(lane <= trow, s_new, F32_MIN), -jnp.inf)
        s_all = jnp.concatenate([s_ref[k] for k in range(n_past_blocks + 1)], axis=1)
        n_all = past + LANES
        sel = _select_topk(s_all, min(TOPK_MAX, (past + n_new) // 4))
        col = lax.broadcasted_iota(jnp.int32, (n_new, n_all), 1)
        tq = lax.broadcasted_iota(jnp.int32, (n_new, n_all), 0)
        valid = (col - past) <= tq
        o_ref[...] = jnp.where(sel & valid, 0.0, -jnp.inf)


def _sample_select(page_table, qst, wcol, small, cache_kidx3, *, n_new):
    n_seq, n_pages = page_table.shape
    n_chunks = n_pages // PAGES_PER_STEP
    n_all = n_pages * PAGE_SIZE + LANES
    page_specs = [
        pl.BlockSpec((None, PAGE_SIZE, IDX_DIM),
                     lambda b, c, pt, r=r: (pt[b, PAGES_PER_STEP * c + r], 0, 0))
        for r in range(PAGES_PER_STEP)
    ]
    rows = N_IDX_HEADS * n_new
    grid_spec = pltpu.PrefetchScalarGridSpec(
        num_scalar_prefetch=1,
        grid=(n_seq, n_chunks),
        in_specs=[
            pl.BlockSpec((None, rows, IDX_DIM), lambda b, c, pt: (b, 0, 0)),
            pl.BlockSpec((None, rows, 1), lambda b, c, pt: (b, 0, 0)),
            pl.BlockSpec((n_new, LANES), lambda b, c, pt: (b, 0)),
        ] + page_specs,
        out_specs=pl.BlockSpec((None, n_new, n_all), lambda b, c, pt: (b, 0, 0)),
        scratch_shapes=[pltpu.VMEM((n_pages + 1, n_new, LANES), F32)],
    )
    return pl.pallas_call(
        functools.partial(_sidx_kernel, n_chunks=n_chunks, n_new=n_new),
        grid_spec=grid_spec,
        out_shape=jax.ShapeDtypeStruct((n_seq, n_new, n_all), F32),
        compiler_params=pltpu.CompilerParams(
            dimension_semantics=("arbitrary", "arbitrary"), vmem_limit_bytes=VMEM_LIMIT),
        name="sample_select",
    )(page_table, qst, wcol, small, *([cache_kidx3] * PAGES_PER_STEP))


def _sattn_kernel(pt_ref, q_ref, bias_ref, biasn_ref, kn_ref, vn_ref, *refs, n_chunks, n_new):
    kpages = refs[:PAGES_PER_STEP]
    vpages = refs[PAGES_PER_STEP:2 * PAGES_PER_STEP]
    o_ref, m_ref, l_ref, acc_ref = refs[2 * PAGES_PER_STEP:]
    c = pl.program_id(1)
    rows = q_ref.shape[0]
    reps = rows // n_new

    @pl.when(c == 0)
    def _():
        m_ref[...] = jnp.full(m_ref.shape, -1e30, F32)
        l_ref[...] = jnp.zeros(l_ref.shape, F32)
        acc_ref[...] = jnp.zeros(acc_ref.shape, F32)

    def update(kb, vb, bias):
        logits = _dot_nt(q_ref[...], kb) + jnp.concatenate([bias] * reps, axis=0)
        m_old = m_ref[...]
        m_new = jnp.maximum(m_old, jnp.max(logits, axis=1, keepdims=True))
        alpha = jnp.exp(m_old - m_new)
        p = jnp.exp(logits - m_new)
        l_ref[...] = alpha * l_ref[...] + jnp.sum(p, axis=1, keepdims=True)
        acc_ref[...] = alpha * acc_ref[...] + _dot(p.astype(BF16), vb)
        m_ref[...] = m_new

    kb = jnp.concatenate([kp[...] for kp in kpages], axis=0).astype(BF16)
    vb = jnp.concatenate([vp[...] for vp in vpages], axis=0).astype(BF16)
    update(kb, vb, bias_ref[...])

    @pl.when(c == n_chunks - 1)
    def _():
        pad = jnp.zeros((PAGE_SIZE - n_new, KV_W), F32)
        kn = jnp.concatenate([kn_ref[...], pad], axis=0).astype(BF16)
        vn = jnp.concatenate([vn_ref[...], pad], axis=0).astype(BF16)
        update(kn, vn, biasn_ref[...])
        o_ref[...] = acc_ref[...] / l_ref[...]


def _sample_attend(page_table, qbd, bias, proj, cache_k3, cache_v3, *, n_new):
    n_seq, n_pages = page_table.shape
    n_chunks = n_pages // PAGES_PER_STEP
    rows = qbd.shape[1]
    chunk_keys = PAGES_PER_STEP * PAGE_SIZE
    page_specs = [
        pl.BlockSpec((None, PAGE_SIZE, KV_W),
                     lambda b, c, pt, r=r: (pt[b, PAGES_PER_STEP * c + r], 0, 0))
        for r in range(PAGES_PER_STEP)
    ]
    grid_spec = pltpu.PrefetchScalarGridSpec(
        num_scalar_prefetch=1,
        grid=(n_seq, n_chunks),
        in_specs=[
            pl.BlockSpec((None, rows, KV_W), lambda b, c, pt: (b, 0, 0)),
            pl.BlockSpec((None, n_new, chunk_keys), lambda b, c, pt: (b, 0, c)),
            pl.BlockSpec((None, n_new, LANES), lambda b, c, pt: (b, 0, n_pages)),
            pl.BlockSpec((n_new, KV_W), lambda b, c, pt: (b, 12)),
            pl.BlockSpec((n_new, KV_W), lambda b, c, pt: (b, 13)),
        ] + page_specs + page_specs,
        out_specs=pl.BlockSpec((None, rows, KV_W), lambda b, c, pt: (b, 0, 0)),
        scratch_shapes=[
            pltpu.VMEM((rows, 1), F32),
            pltpu.VMEM((rows, 1), F32),
            pltpu.VMEM((rows, KV_W), F32),
        ],
    )
    return pl.pallas_call(
        functools.partial(_sattn_kernel, n_chunks=n_chunks, n_new=n_new),
        grid_spec=grid_spec,
        out_shape=jax.ShapeDtypeStruct((n_seq, rows, KV_W), F32),
        compiler_params=pltpu.CompilerParams(
            dimension_semantics=("arbitrary", "arbitrary"), vmem_limit_bytes=VMEM_LIMIT),
        name="sample_attend",
    )(page_table, qbd, bias, bias, proj, proj,
      *([cache_k3] * PAGES_PER_STEP), *([cache_v3] * PAGES_PER_STEP))


def _merge_kernel(x_ref, rnn_ref, att_ref, gr_ref, ga_ref, wr_ref, wa_ref, wo_ref, o_ref, mix_ref):
    n = pl.program_id(1)

    @pl.when(n == 0)
    def _():
        mixed = (gr_ref[...] * _dot(rnn_ref[...].astype(BF16), wr_ref[...])
                 + ga_ref[...] * _dot(att_ref[...].astype(BF16), wa_ref[...]))
        mix_ref[...] = mixed.astype(BF16)

    o_ref[...] = x_ref[...] + _dot(mix_ref[...], wo_ref[...])


def _merge(x, rnn, attn, proj, wr, wa, wo, *, tm, tn=512):
    t = x.shape[0]
    return pl.pallas_call(
        _merge_kernel,
        grid=(t // tm, D_MODEL // tn),
        in_specs=[
            pl.BlockSpec((tm, tn), lambda i, n: (i, n)),
            pl.BlockSpec((tm, D_RNN), lambda i, n: (i, 0)),
            pl.BlockSpec((tm, ATTN_W), lambda i, n: (i, 0)),
            pl.BlockSpec((tm, D_MODEL), lambda i, n: (i, 2)),
            pl.BlockSpec((tm, D_MODEL), lambda i, n: (i, 3)),
            pl.BlockSpec((D_RNN, D_MODEL), lambda i, n: (0, 0)),
            pl.BlockSpec((ATTN_W, D_MODEL), lambda i, n: (0, 0)),
            pl.BlockSpec((D_MODEL, tn), lambda i, n: (0, n)),
        ],
        out_specs=pl.BlockSpec((tm, tn), lambda i, n: (i, n)),
        out_shape=jax.ShapeDtypeStruct((t, D_MODEL), F32),
        scratch_shapes=[pltpu.VMEM((tm, D_MODEL), BF16)],
        compiler_params=pltpu.CompilerParams(
            dimension_semantics=("arbitrary", "arbitrary"), vmem_limit_bytes=VMEM_LIMIT),
        name="merge",
    )(x, rnn, attn, proj, proj, wr, wa, wo)


def _moe_kernel(x_ref, ln_ref, rwh_ref, rwl_ref, rb_ref, w13_ref, w2_ref, o_ref, h_ref, gate_ref):
    e = pl.program_id(1)
    lane = lax.broadcasted_iota(jnp.int32, (1, LANES), 1)
    lanef = lane.astype(F32)

    @pl.when(e == 0)
    def _():
        h = _rms(x_ref[...], ln_ref[...])
        hh, hl = _split_bf16(h)
        h_ref[...] = hh
        lg = (_dot(hh, rwh_ref[...]) + _dot(hl, rwh_ref[...]) + _dot(hh, rwl_ref[...])) + rb_ref[...]
        is_g = (lane >= N_EXPERTS) & (lane < N_EXPERTS + N_GROUPS)
        gl = jnp.where(is_g, lg, -jnp.inf)
        gmax = jnp.max(gl, axis=1, keepdims=True)
        gprob = 1.0 / jnp.sum(jnp.exp(gl - gmax), axis=1, keepdims=True)
        gsel = jnp.min(jnp.where(is_g & (lg == gmax), lanef - N_EXPERTS, 1e9), axis=1, keepdims=True)
        in_grp = (lane < N_EXPERTS) & (jnp.floor(lanef * (1.0 / EXPERTS_PER_GROUP)) == gsel)
        v1 = jnp.where(in_grp, lg, -jnp.inf)
        t1 = jnp.max(v1, axis=1, keepdims=True)
        i1 = jnp.min(jnp.where(in_grp & (lg == t1), lanef, 1e9), axis=1, keepdims=True)
        rest = in_grp & (lanef != i1)
        v2 = jnp.where(rest, lg, -jnp.inf)
        t2 = jnp.max(v2, axis=1, keepdims=True)
        i2 = jnp.min(jnp.where(rest & (lg == t2), lanef, 1e9), axis=1, keepdims=True)
        d = jnp.exp(t2 - t1)
        w1 = 1.0 / (1.0 + d)
        w2 = d / (1.0 + d)
        gate_ref[...] = (jnp.where(lanef == i1, w1, 0.0) + jnp.where(lanef == i2, w2, 0.0)) * gprob

    ge = jnp.sum(jnp.where(lane == e, gate_ref[...], 0.0), axis=1, keepdims=True)
    a = _dot(h_ref[...], w13_ref[...])
    up = a[:, 0:D_EXPERT]
    hid = (up * _sigmoid(up)) * a[:, D_EXPERT:2 * D_EXPERT]
    contrib = _dot((hid * ge).astype(BF16), w2_ref[...])

    @pl.when(e == 0)
    def _():
        o_ref[...] = x_ref[...] + contrib

    @pl.when(e > 0)
    def _():
        o_ref[...] += contrib


def _moe(x, ln2, rw_hi, rw_lo, rb, w13, w2, *, tm):
    t = x.shape[0]
    return pl.pallas_call(
        _moe_kernel,
        grid=(t // tm, N_EXPERTS),
        in_specs=[
            pl.BlockSpec((tm, D_MODEL), lambda i, e: (i, 0)),
            pl.BlockSpec((1, D_MODEL), lambda i, e: (0, 0)),
            pl.BlockSpec((D_MODEL, LANES), lambda i, e: (0, 0)),
            pl.BlockSpec((D_MODEL, LANES), lambda i, e: (0, 0)),
            pl.BlockSpec((1, LANES), lambda i, e: (0, 0)),
            pl.BlockSpec((None, D_MODEL, 2 * D_EXPERT), lambda i, e: (e, 0, 0)),
            pl.BlockSpec((None, D_EXPERT, D_MODEL), lambda i, e: (e, 0, 0)),
        ],
        out_specs=pl.BlockSpec((tm, D_MODEL), lambda i, e: (i, 0)),
        out_shape=jax.ShapeDtypeStruct((t, D_MODEL), F32),
        scratch_shapes=[pltpu.VMEM((tm, D_MODEL), BF16), pltpu.VMEM((tm, LANES), F32)],
        compiler_params=pltpu.CompilerParams(
            dimension_semantics=("arbitrary", "arbitrary"), vmem_limit_bytes=VMEM_LIMIT),
        name="moe",
    )(x, ln2, rw_hi, rw_lo, rb, w13, w2)


def _ple_kernel(xf_ref, xt_ref, p_ref, ln_ref, wg_ref, wp_ref, o_ref, h_ref, pb_ref):
    n = pl.program_id(1)

    @pl.when(n == 0)
    def _():
        h_ref[...] = _rms(xf_ref[...], ln_ref[...]).astype(BF16)
        pb_ref[...] = p_ref[...].astype(BF16)

    gate = _sigmoid(_dot(h_ref[...], wg_ref[...]))
    o_ref[...] = xt_ref[...] + gate * _dot(pb_ref[...], wp_ref[...])


def _ple(x, p, ln3, wg, wp, *, tm, tn=512):
    t = x.shape[0]
    return pl.pallas_call(
        _ple_kernel,
        grid=(t // tm, D_MODEL // tn),
        in_specs=[
            pl.BlockSpec((tm, D_MODEL), lambda i, n: (i, 0)),
            pl.BlockSpec((tm, tn), lambda i, n: (i, n)),
            pl.BlockSpec((tm, PLE_DIM), lambda i, n: (i, 0)),
            pl.BlockSpec((1, D_MODEL), lambda i, n: (0, 0)),
            pl.BlockSpec((D_MODEL, tn), lambda i, n: (0, n)),
            pl.BlockSpec((PLE_DIM, tn), lambda i, n: (0, n)),
        ],
        out_specs=pl.BlockSpec((tm, tn), lambda i, n: (i, n)),
        out_shape=jax.ShapeDtypeStruct((t, D_MODEL), F32),
        scratch_shapes=[pltpu.VMEM((tm, D_MODEL), BF16), pltpu.VMEM((tm, PLE_DIM), BF16)],
        compiler_params=pltpu.CompilerParams(
            dimension_semantics=("arbitrary", "arbitrary"), vmem_limit_bytes=VMEM_LIMIT),
        name="ple",
    )(x, x, p, ln3, wg, wp)


def _rope_tables(pos):
    half = HEAD_DIM // 2
    inv = ROPE_THETA ** (-jnp.arange(half, dtype=F32) / half)
    ang = pos.astype(F32)[:, None] * inv[None, :]
    cos, sin = jnp.cos(ang), jnp.sin(ang)
    return (jnp.concatenate([cos, cos, cos, cos], axis=1),
            jnp.concatenate([-sin, sin, -sin, sin], axis=1))


def _block_diag(w, per):
    n, r, _ = w.shape
    eye = jnp.eye(per, dtype=w.dtype)
    wg = w.reshape(n // per, per, r, r)
    return jnp.einsum("gpij,pq->gpiqj", wg, eye).reshape(n // per, per * r, per * r)


def _layer_weights(ln1, w_in, q_norm, k_norm, conv_w, conv_b, w_a, b_a, w_x, b_x, lam, w_br_rnn,
                   w_br_attn, w_out, ln2, w_rg, b_rg, w_re, b_re, w1, w3, w2, ln3, w_ple_gate,
                   w_ple_proj):
    o_q = 2 * D_RNN
    o_k = o_q + ATTN_W
    o_v = o_k + KV_W
    o_qi = o_v + KV_W
    o_ki = o_qi + N_IDX_HEADS * IDX_DIM
    o_wi = o_ki + IDX_DIM
    o_gr = o_wi + N_IDX_HEADS
    o_ga = o_gr + D_MODEL
    w_main = jnp.concatenate([w_in[:, :o_ki], w_in[:, o_gr:o_ga + D_MODEL]], axis=1).astype(BF16)
    w_small = jnp.concatenate(
        [w_in[:, o_ki:o_gr], jnp.zeros((D_MODEL, LANES - IDX_DIM - N_IDX_HEADS), F32)], axis=1).astype(BF16)
    ones = lambda n: jnp.ones((n,), F32)
    zeros = lambda n: jnp.zeros((n,), F32)
    n_gate = 2 * D_MODEL
    gain = jnp.concatenate([ones(o_q), jnp.tile(q_norm, N_HEADS), jnp.tile(k_norm, N_KV_HEADS),
                            ones(KV_W + N_IDX_HEADS * IDX_DIM + n_gate)])
    norm_on = jnp.concatenate([zeros(o_q), ones(ATTN_W + KV_W), zeros(KV_W + N_IDX_HEADS * IDX_DIM + n_gate)])
    rope_on = jnp.concatenate([zeros(o_q), ones(ATTN_W + KV_W), zeros(KV_W), ones(N_IDX_HEADS * IDX_DIM),
                               zeros(n_gate)])
    post = jnp.concatenate([ones(o_q), jnp.full((ATTN_W,), QK_SCALE, F32),
                            ones(2 * KV_W + N_IDX_HEADS * IDX_DIM + n_gate)])
    colctl = jnp.concatenate([jnp.stack([gain, norm_on, rope_on, post]), jnp.zeros((4, N_MAIN), F32)], axis=0)
    tn = 512
    head_of = jnp.arange(tn) // HEAD_DIM
    bd = (head_of[:, None] == head_of[None, :]).astype(BF16)
    rw = jnp.concatenate([w_re, w_rg, jnp.zeros((D_MODEL, LANES - N_EXPERTS - N_GROUPS), F32)], axis=1)
    rw_hi = rw.astype(BF16)
    rw_lo = (rw - rw_hi.astype(F32)).astype(BF16)
    rb = jnp.concatenate([b_re, b_rg, jnp.zeros((LANES - N_EXPERTS - N_GROUPS,), F32)])[None, :]
    return dict(
        ln1=ln1[None, :], w_main=w_main, w_small=w_small, colctl=colctl, bd=bd,
        cw=conv_w, cb=conv_b[None, :],
        wa_bd=_block_diag(w_a, 4).astype(BF16), ba=b_a[None, :],
        wx_bd=_block_diag(w_x, 4).astype(BF16), bx=b_x[None, :], lam=lam[None, :],
        wr=w_br_rnn.astype(BF16), wa=w_br_attn.astype(BF16), wo=w_out.astype(BF16),
        ln2=ln2[None, :], rw_hi=rw_hi, rw_lo=rw_lo, rb=rb,
        w13=jnp.concatenate([w1, w3], axis=2).astype(BF16), w2=w2.astype(BF16),
        ln3=ln3[None, :], wg=w_ple_gate.astype(BF16), wp=w_ple_proj.astype(BF16),
    )


def _tail(x, rnn, attn, proj, p, w):
    t = x.shape[0]
    x1 = _merge(x, rnn, attn, proj, w["wr"], w["wa"], w["wo"], tm=min(t, 256))
    x2 = _moe(x1, w["ln2"], w["rw_hi"], w["rw_lo"], w["rb"], w["w13"], w["w2"], tm=min(t, 512))
    return _ple(x2, p, w["ln3"], w["wg"], w["wp"], tm=min(t, 512))


def _prompt_layer(x, p, w):
    bp, tp, _ = x.shape
    xt = x.reshape(bp * tp, D_MODEL)
    cs, sn = _rope_tables(jnp.tile(jnp.arange(tp, dtype=jnp.int32), bp))
    proj, small = _inproj(xt, w["ln1"], w["w_main"], w["w_small"], w["colctl"], cs, sn, w["bd"],
                          tm=min(bp * tp, 1024))
    conv0 = jnp.zeros((bp, 8, D_RNN), F32)
    h0 = jnp.zeros((bp, 1, D_RNN), F32)
    rnn, h_last = _rglru(proj, conv0, h0, w["cw"], w["cb"], w["wa_bd"], w["ba"], w["wx_bd"], w["bx"],
                         w["lam"], n_seq=bp, tt=min(tp, 256))
    attn = _prompt_attention(proj, small, n_batch=bp, seq=tp)
    y = _tail(xt, rnn, attn, proj, p.reshape(bp * tp, PLE_DIM), w)
    o_k = 2 * D_RNN + ATTN_W
    k = proj[:, o_k:o_k + KV_W].reshape(bp, tp, N_KV_HEADS, HEAD_DIM)
    v = proj[:, o_k + KV_W:o_k + 2 * KV_W].reshape(bp, tp, N_KV_HEADS, HEAD_DIM)
    ki = small[:, :IDX_DIM].reshape(bp, tp, IDX_DIM)
    conv_new = proj[:, :D_RNN].reshape(bp, tp, D_RNN)[:, tp - (CONV_W - 1):]
    return y.reshape(bp, tp, D_MODEL), (k, v, ki, conv_new, h_last.reshape(bp, D_RNN))


def _sample_layer(x, p, cache_k, cache_v, cache_kidx, state_conv, state_h, page_table, w):
    bs, ts, _ = x.shape
    n_pages = page_table.shape[1]
    past = n_pages * PAGE_SIZE
    xt = x.reshape(bs * ts, D_MODEL)
    cs, sn = _rope_tables(past + jnp.tile(jnp.arange(ts, dtype=jnp.int32), bs))
    proj, small = _inproj(xt, w["ln1"], w["w_main"], w["w_small"], w["colctl"], cs, sn, w["bd"], tm=bs * ts)
    conv0 = jnp.concatenate([jnp.zeros((bs, 8 - (CONV_W - 1), D_RNN), F32), state_conv], axis=1)
    rnn, h_last = _rglru(proj, conv0, state_h[:, None, :], w["cw"], w["cb"], w["wa_bd"], w["ba"],
                         w["wx_bd"], w["bx"], w["lam"], n_seq=bs, tt=ts)
    o_q = 2 * D_RNN
    o_qi = o_q + ATTN_W + 2 * KV_W
    qi = proj[:, o_qi:o_qi + N_IDX_HEADS * IDX_DIM].reshape(bs, ts, N_IDX_HEADS, IDX_DIM)
    qst = jnp.transpose(qi, (0, 2, 1, 3)).reshape(bs, N_IDX_HEADS * ts, IDX_DIM).astype(BF16)
    wi = small[:, IDX_DIM:IDX_DIM + N_IDX_HEADS].reshape(bs, ts, N_IDX_HEADS)
    wcol = jnp.transpose(wi, (0, 2, 1)).reshape(bs, N_IDX_HEADS * ts, 1)
    bias = _sample_select(page_table, qst, wcol, small, cache_kidx.reshape(-1, PAGE_SIZE, IDX_DIM), n_new=ts)
    q = proj[:, o_q:o_q + ATTN_W].reshape(bs, ts, N_KV_HEADS, N_HEADS // N_KV_HEADS, HEAD_DIM)
    eye = jnp.eye(N_KV_HEADS, dtype=F32)
    qbd = jnp.einsum("btgjd,gk->bgjtkd", q, eye).reshape(bs, N_HEADS * ts, KV_W).astype(BF16)
    att = _sample_attend(page_table, qbd, bias, proj, cache_k.reshape(-1, PAGE_SIZE, KV_W),
                         cache_v.reshape(-1, PAGE_SIZE, KV_W), n_new=ts)
    att = att.reshape(bs, N_KV_HEADS, N_HEADS // N_KV_HEADS, ts, N_KV_HEADS, HEAD_DIM)
    att = jnp.stack([att[:, g, :, :, g, :] for g in range(N_KV_HEADS)], axis=1)
    attn = jnp.transpose(att, (0, 3, 1, 2, 4)).reshape(bs * ts, ATTN_W)
    y = _tail(xt, rnn, attn, proj, p.reshape(bs * ts, PLE_DIM), w)
    o_k = o_q + ATTN_W
    k = proj[:, o_k:o_k + KV_W].reshape(bs, ts, N_KV_HEADS, HEAD_DIM)
    v = proj[:, o_k + KV_W:o_k + 2 * KV_W].reshape(bs, ts, N_KV_HEADS, HEAD_DIM)
    ki = small[:, :IDX_DIM].reshape(bs, ts, IDX_DIM)
    conv_new = proj[:, :D_RNN].reshape(bs, ts, D_RNN)[:, ts - (CONV_W - 1):]
    return y.reshape(bs, ts, D_MODEL), (k, v, ki, conv_new, h_last.reshape(bs, D_RNN))


def kernel(x_prompt, x_sample, p_prompt, p_sample, cache_k, cache_v, cache_kidx, state_conv, state_h,
           page_table, ln1, w_in, q_norm, k_norm, conv_w, conv_b, w_a, b_a, w_x, b_x, lam, w_br_rnn,
           w_br_attn, w_out, ln2, w_rg, b_rg, w_re, b_re, w1, w3, w2, ln3, w_ple_gate, w_ple_proj):
    weights = (ln1, w_in, q_norm, k_norm, conv_w, conv_b, w_a, b_a, w_x, b_x, lam, w_br_rnn, w_br_attn,
               w_out, ln2, w_rg, b_rg, w_re, b_re, w1, w3, w2, ln3, w_ple_gate, w_ple_proj)
    depth = ln1.shape[0]
    yp, ys = x_prompt, x_sample
    st_p, st_s = [], []
    for i in range(depth):
        w = _layer_weights(*[wt[i] for wt in weights])
        yp, sp = _prompt_layer(yp, p_prompt[i], w)
        ys, ss = _sample_layer(ys, p_sample[i], cache_k[i], cache_v[i], cache_kidx[i], state_conv[i],
                               state_h[i], page_table, w)
        st_p.append(sp)
        st_s.append(ss)
    stack = lambda sts, j: jnp.stack([s[j] for s in sts])
    return (yp, ys, stack(st_p, 0), stack(st_p, 1), stack(st_p, 2), stack(st_p, 3), stack(st_p, 4),
            stack(st_s, 0), stack(st_s, 1), stack(st_s, 2), stack(st_s, 3), stack(st_s, 4))
```

```python
import functools

import jax
import jax.numpy as jnp
import numpy as np
from jax import lax
from jax.experimental import pallas as pl
from jax.experimental.pallas import tpu as pltpu

F32 = jnp.float32
BF16 = jnp.bfloat16

D_MODEL = 2048
HEAD_DIM = 64
N_HEADS = 16
N_KV_HEADS = 4
ATTN_W = N_HEADS * HEAD_DIM
KV_W = N_KV_HEADS * HEAD_DIM
N_IDX_HEADS = 8
IDX_DIM = 64
TOPK_MAX = 256
ROPE_THETA = 10000.0
D_RNN = 1024
N_RNN_BLOCKS = 16
RNN_BLOCK = 64
CONV_W = 4
LRU_C = 8.0
N_GROUPS = 4
EXPERTS_PER_GROUP = 8
N_EXPERTS = 32
D_EXPERT = 256
PLE_DIM = 256
PAGE_SIZE = 128
EPS = 1e-6

LANES = 128
N_MAIN = 8192
COL_RAW_END = 2 * D_RNN
COL_NR_END = 4096
IDX_SCALE = (IDX_DIM ** -0.5) * (N_IDX_HEADS ** -0.5)
QK_SCALE = HEAD_DIM ** -0.5
F32_MIN = float(np.finfo(np.float32).min)
INT_MIN = -2147483648
KEY_NEG_INF = INT_MIN + 0x7FFFFF
VMEM_LIMIT = 56 * 1024 * 1024


def _dot(a, b):
    return jnp.dot(a, b, preferred_element_type=F32)


def _dot_nt(a, b):
    return lax.dot_general(a, b, (((1,), (1,)), ((), ())), preferred_element_type=F32)


def _sigmoid(x):
    return 1.0 / (1.0 + jnp.exp(-x))


def _rms(x, g):
    return x * lax.rsqrt(jnp.mean(x * x, axis=-1, keepdims=True) + EPS) * g


def _split_bf16(x):
    hi = x.astype(BF16)
    lo = (x - hi.astype(F32)).astype(BF16)
    return hi, lo


def _rope_chunks(y, c, s):
    lane = lax.broadcasted_iota(jnp.int32, (1, LANES), 1)
    first_half = (lane % HEAD_DIM) < (HEAD_DIM // 2)
    outs = []
    for k in range(y.shape[1] // LANES):
        yc = y[:, k * LANES:(k + 1) * LANES]
        partner = jnp.where(first_half, pltpu.roll(yc, LANES - HEAD_DIM // 2, 1),
                            pltpu.roll(yc, HEAD_DIM // 2, 1))
        outs.append(yc * c + partner * s)
    return outs[0] if len(outs) == 1 else jnp.concatenate(outs, axis=1)


def _inproj_kernel(x_ref, ln_ref, w_ref, ws_ref, ctl_ref, cs_ref, sn_ref, bd_ref,
                   o_ref, os_ref, h_ref, *, tn):
    j = pl.program_id(1)

    @pl.when(j == 0)
    def _():
        hb = _rms(x_ref[...], ln_ref[...]).astype(BF16)
        h_ref[...] = hb
        ys = _dot(hb, ws_ref[...])
        lane = lax.broadcasted_iota(jnp.int32, (1, LANES), 1)
        os_ref[...] = jnp.where(lane < IDX_DIM, _rope_chunks(ys, cs_ref[...], sn_ref[...]), ys)

    y = _dot(h_ref[...], w_ref[...])

    @pl.when(j < COL_RAW_END // tn)
    def _():
        o_ref[...] = y

    @pl.when((j >= COL_RAW_END // tn) & (j < COL_NR_END // tn))
    def _():
        ctl = ctl_ref[...]
        gain, norm_on, rope_on, post = ctl[0:1], ctl[1:2], ctl[2:3], ctl[3:4]
        hi, lo = _split_bf16(y * y)
        ss = _dot(hi, bd_ref[...]) + _dot(lo, bd_ref[...])
        yn = jnp.where(norm_on > 0.0, y * lax.rsqrt(ss * (1.0 / HEAD_DIM) + EPS) * gain, y)
        yr = jnp.where(rope_on > 0.0, _rope_chunks(yn, cs_ref[...], sn_ref[...]), yn)
        o_ref[...] = yr * post

    @pl.when(j >= COL_NR_END // tn)
    def _():
        o_ref[...] = _sigmoid(y)


def _inproj(x, ln1, w_main, w_small, colctl, cs, sn, bd, *, tm, tn=512):
    t = x.shape[0]
    grid = (t // tm, N_MAIN // tn)
    return pl.pallas_call(
        functools.partial(_inproj_kernel, tn=tn),
        grid=grid,
        in_specs=[
            pl.BlockSpec((tm, D_MODEL), lambda i, j: (i, 0)),
            pl.BlockSpec((1, D_MODEL), lambda i, j: (0, 0)),
            pl.BlockSpec((D_MODEL, tn), lambda i, j: (0, j)),
            pl.BlockSpec((D_MODEL, LANES), lambda i, j: (0, 0)),
            pl.BlockSpec((8, tn), lambda i, j: (0, j)),
            pl.BlockSpec((tm, LANES), lambda i, j: (i, 0)),
            pl.BlockSpec((tm, LANES), lambda i, j: (i, 0)),
            pl.BlockSpec((tn, tn), lambda i, j: (0, 0)),
        ],
        out_specs=[
            pl.BlockSpec((tm, tn), lambda i, j: (i, j)),
            pl.BlockSpec((tm, LANES), lambda i, j: (i, 0)),
        ],
        out_shape=[
            jax.ShapeDtypeStruct((t, N_MAIN), F32),
            jax.ShapeDtypeStruct((t, LANES), F32),
        ],
        scratch_shapes=[pltpu.VMEM((tm, D_MODEL), BF16)],
        compiler_params=pltpu.CompilerParams(
            dimension_semantics=("arbitrary", "arbitrary"), vmem_limit_bytes=VMEM_LIMIT),
        name="inproj",
    )(x, ln1, w_main, w_small, colctl, cs, sn, bd)


def _rglru_kernel(x_ref, g_ref, c0_ref, h0_ref, cw_ref, cb_ref, wa_ref, ba_ref, wx_ref, bx_ref,
                  lam_ref, o_ref, hl_ref, xs_ref, a_ref, b_ref, hc_ref):
    t = pl.program_id(1)
    tt = x_ref.shape[0]

    @pl.when(t == 0)
    def _():
        xs_ref[0:8, :] = c0_ref[...]
        hc_ref[...] = h0_ref[...]

    xs_ref[8:8 + tt, :] = x_ref[...]
    cw = cw_ref[...]
    taps = (xs_ref[5:5 + tt, :] * cw[0:1] + xs_ref[6:6 + tt, :] * cw[1:2]
            + xs_ref[7:7 + tt, :] * cw[2:3] + xs_ref[8:8 + tt, :] * cw[3:4])
    xc = cb_ref[...] + taps
    xs_ref[0:8, :] = xs_ref[tt:tt + 8, :]

    xcb = xc.astype(BF16)
    ra, ri = [], []
    for c in range(wa_ref.shape[0]):
        blk = xcb[:, c * 256:(c + 1) * 256]
        ra.append(_dot(blk, wa_ref[c]))
        ri.append(_dot(blk, wx_ref[c]))
    r = _sigmoid(jnp.concatenate(ra, axis=1) + ba_ref[...])
    ig = _sigmoid(jnp.concatenate(ri, axis=1) + bx_ref[...])
    nlam = -lam_ref[...]
    softplus = jnp.maximum(nlam, 0.0) + jnp.log1p(jnp.exp(-jnp.abs(nlam)))
    log_a = (-LRU_C) * r * softplus
    a = jnp.exp(log_a)
    u = jnp.sqrt(jnp.tanh(-log_a) * (a * a + 1.0)) * (ig * xc)

    n8 = tt // 8
    a3 = a.reshape(n8, 8, D_RNN)
    b3 = u.reshape(n8, 8, D_RNN)
    sub = lax.broadcasted_iota(jnp.int32, (1, 8, 1), 1)
    for s in (1, 2, 4):
        a_prev = pltpu.roll(a3, s, 1)
        b_prev = pltpu.roll(b3, s, 1)
        m = sub >= s
        b3 = jnp.where(m, a3 * b_prev + b3, b3)
        a3 = jnp.where(m, a3 * a_prev, a3)
    a_ref[...] = a3.reshape(tt, D_RNN)
    b_ref[...] = b3.reshape(tt, D_RNN)

    def chain(k, carry):
        i0 = pl.multiple_of(k * 8, 8)
        h8 = a_ref[pl.ds(i0, 8), :] * carry + b_ref[pl.ds(i0, 8), :]
        b_ref[pl.ds(i0, 8), :] = h8
        return h8[7:8, :]

    carry = lax.fori_loop(0, n8, chain, hc_ref[...])
    hc_ref[...] = carry
    g = g_ref[...]
    gelu = 0.5 * g * (1.0 + jnp.tanh(0.7978845608028654 * (g + 0.044715 * (g * g * g))))
    o_ref[...] = b_ref[...] * gelu

    @pl.when(t == pl.num_programs(1) - 1)
    def _():
        hl_ref[...] = carry


def _rglru(proj, conv0, h0, cw, cb, wa_bd, ba, wx_bd, bx, lam, *, n_seq, tt):
    t_total = proj.shape[0]
    nt = t_total // (n_seq * tt)
    full = lambda shape: pl.BlockSpec(shape, lambda b, t: (0,) * len(shape))
    return pl.pallas_call(
        _rglru_kernel,
        grid=(n_seq, nt),
        in_specs=[
            pl.BlockSpec((tt, D_RNN), lambda b, t: (b * nt + t, 0)),
            pl.BlockSpec((tt, D_RNN), lambda b, t: (b * nt + t, 1)),
            pl.BlockSpec((None, 8, D_RNN), lambda b, t: (b, 0, 0)),
            pl.BlockSpec((None, 1, D_RNN), lambda b, t: (b, 0, 0)),
            full((CONV_W, D_RNN)), full((1, D_RNN)),
            full(wa_bd.shape), full((1, D_RNN)),
            full(wx_bd.shape), full((1, D_RNN)),
            full((1, D_RNN)),
        ],
        out_specs=[
            pl.BlockSpec((tt, D_RNN), lambda b, t: (b * nt + t, 0)),
            pl.BlockSpec((None, 1, D_RNN), lambda b, t: (b, 0, 0)),
        ],
        out_shape=[
            jax.ShapeDtypeStruct((t_total, D_RNN), F32),
            jax.ShapeDtypeStruct((n_seq, 1, D_RNN), F32),
        ],
        scratch_shapes=[
            pltpu.VMEM((tt + 8, D_RNN), F32),
            pltpu.VMEM((tt, D_RNN), F32),
            pltpu.VMEM((tt, D_RNN), F32),
            pltpu.VMEM((1, D_RNN), F32),
        ],
        compiler_params=pltpu.CompilerParams(
            dimension_semantics=("arbitrary", "arbitrary"), vmem_limit_bytes=VMEM_LIMIT),
        name="rglru",
    )(proj, proj, conv0, h0, cw, cb, wa_bd, ba, wx_bd, bx, lam)


def _select_topk(s, kk):
    rows, n = s.shape
    kkf = float(kk)

    def key_to_f32(w):
        k = w ^ INT_MIN
        bits = jnp.where(k >= 0, k, k ^ 0x7FFFFFFF)
        return k, lax.bitcast_convert_type(bits, F32)

    def vbody(it, w):
        cand_w = w | jnp.left_shift(jnp.int32(1), 31 - it)
        cand_k, cand_f = key_to_f32(cand_w)
        cnt = jnp.sum(jnp.where(s >= cand_f, 1.0, 0.0), axis=1, keepdims=True)
        ok = (cnt >= kkf) | (cand_k < KEY_NEG_INF)
        return jnp.where(ok, cand_w, w)

    w = lax.fori_loop(0, 32, vbody, jnp.zeros((rows, 1), jnp.int32))
    _, thr = key_to_f32(w)
    gt = s > thr
    eq = s == thr
    need = kkf - jnp.sum(jnp.where(gt, 1.0, 0.0), axis=1, keepdims=True)
    col = lax.broadcasted_iota(jnp.int32, (1, n), 1)
    nbits = int(n).bit_length()

    def jbody(it, jmax):
        cand = jmax | jnp.left_shift(jnp.int32(1), nbits - 1 - it)
        cnt = jnp.sum(jnp.where(eq & (col < cand), 1.0, 0.0), axis=1, keepdims=True)
        return jnp.where(cnt <= need, cand, jmax)

    n_ge = jnp.sum(jnp.where(s >= thr, 1.0, 0.0), axis=1, keepdims=True)
    jmax = lax.cond(
        jnp.max(n_ge) > kkf,
        lambda: lax.fori_loop(0, nbits, jbody, jnp.zeros((rows, 1), jnp.int32)),
        lambda: jnp.full((rows, 1), (1 << nbits) - 1, jnp.int32))
    return gt | (eq & (col < jmax))


def _pattn_kernel(q_ref, qi_ref, sm_ref, k_ref, v_ref, ksm_ref, o_ref,
                  kb_ref, vb_ref, kib_ref, s_ref, *, i0, n_keys, kc, topk):
    i = pl.program_id(1)
    tq = q_ref.shape[0]
    lane = lax.broadcasted_iota(jnp.int32, (1, LANES), 1)

    @pl.when(i == 0)
    def _():
        kb_ref[...] = k_ref[0:n_keys, :].astype(BF16)
        vb_ref[...] = v_ref[0:n_keys, :].astype(BF16)
        kib_ref[...] = jnp.where(lane < IDX_DIM, ksm_ref[0:n_keys, :], 0.0).astype(BF16)

    sm = sm_ref[...]
    qi = qi_ref[...]
    qrows, wrows = [], []
    for h in range(N_IDX_HEADS):
        blk = qi[:, (h // 2) * LANES:(h // 2 + 1) * LANES]
        if h % 2 == 1:
            blk = pltpu.roll(blk, IDX_DIM, 1)
        qrows.append(jnp.where(lane < IDX_DIM, blk, 0.0))
        wrows.append(sm[:, IDX_DIM + h:IDX_DIM + h + 1])
    qst = jnp.concatenate(qrows, axis=0).astype(BF16)
    wst = jnp.concatenate(wrows, axis=0) * IDX_SCALE
    qpos = (i0 + i) * tq + lax.broadcasted_iota(jnp.int32, (tq, 1), 0)
    for c in range(n_keys // kc):
        s = jnp.maximum(_dot_nt(qst, kib_ref[c * kc:(c + 1) * kc, :]), 0.0) * wst
        sc = s[0:tq]
        for h in range(1, N_IDX_HEADS):
            sc = sc + s[h * tq:(h + 1) * tq]
        col = c * kc + lax.broadcasted_iota(jnp.int32, (1, kc), 1)
        s_ref[:, c * kc:(c + 1) * kc] = jnp.where(col <= qpos, sc, F32_MIN)

    sel = _select_topk(s_ref[...], topk)
    colf = lax.broadcasted_iota(jnp.int32, (1, n_keys), 1)
    s_ref[...] = jnp.where(sel & (colf <= qpos), 0.0, -jnp.inf)
    bias = s_ref[...]
    bias4 = jnp.concatenate([bias] * 4, axis=0)

    q = q_ref[...]
    outs = [None] * N_HEADS
    for g in range(N_KV_HEADS):
        lo = (g % 2) * HEAD_DIM
        keep = (lane >= lo) & (lane < lo + HEAD_DIM)
        rows = []
        for j in range(4):
            h = 4 * g + j
            blk = q[:, (h // 2) * LANES:(h // 2 + 1) * LANES]
            if h % 2 != g % 2:
                blk = pltpu.roll(blk, HEAD_DIM, 1)
            piece = jnp.where(keep, blk, 0.0)
            zero = jnp.zeros_like(piece)
            rows.append(jnp.concatenate([piece, zero] if g < 2 else [zero, piece], axis=1))
        qbd = jnp.concatenate(rows, axis=0).astype(BF16)
        logits = _dot_nt(qbd, kb_ref[...]) + bias4
        m = jnp.max(logits, axis=1, keepdims=True)
        p = jnp.exp(logits - m)
        denom = jnp.sum(p, axis=1, keepdims=True)
        acc = _dot(p.astype(BF16), vb_ref[...]) / denom
        for j in range(4):
            outs[4 * g + j] = acc[j * tq:(j + 1) * tq, (g // 2) * LANES:(g // 2 + 1) * LANES]
    for c in range(N_HEADS // 2):
        g = (2 * c) // 4
        even, odd = outs[2 * c], outs[2 * c + 1]
        if g % 2 == 1:
            even = pltpu.roll(even, HEAD_DIM, 1)
        else:
            odd = pltpu.roll(odd, HEAD_DIM, 1)
        o_ref[:, c * LANES:(c + 1) * LANES] = jnp.where(lane < HEAD_DIM, even, odd).astype(BF16)


def _prompt_attention_part(proj, small, *, n_batch, seq, i0, n_tiles, tq):
    nq = seq // tq
    n_keys = (i0 + n_tiles) * tq
    kc = next(c for c in (512, 256, 128) if n_keys % c == 0)
    return pl.pallas_call(
        functools.partial(_pattn_kernel, i0=i0, n_keys=n_keys, kc=kc, topk=min(TOPK_MAX, seq // 4)),
        grid=(n_batch, n_tiles),
        in_specs=[
            pl.BlockSpec((tq, ATTN_W), lambda b, i: (b * nq + i0 + i, 2)),
            pl.BlockSpec((tq, N_IDX_HEADS * IDX_DIM), lambda b, i: (b * nq + i0 + i, 7)),
            pl.BlockSpec((tq, LANES), lambda b, i: (b * nq + i0 + i, 0)),
            pl.BlockSpec((seq, KV_W), lambda b, i: (b, 12)),
            pl.BlockSpec((seq, KV_W), lambda b, i: (b, 13)),
            pl.BlockSpec((seq, LANES), lambda b, i: (b, 0)),
        ],
        out_specs=pl.BlockSpec((None, tq, ATTN_W), lambda b, i: (b, i, 0)),
        out_shape=jax.ShapeDtypeStruct((n_batch, n_tiles * tq, ATTN_W), BF16),
        scratch_shapes=[
            pltpu.VMEM((n_keys, KV_W), BF16),
            pltpu.VMEM((n_keys, KV_W), BF16),
            pltpu.VMEM((n_keys, LANES), BF16),
            pltpu.VMEM((tq, n_keys), F32),
        ],
        compiler_params=pltpu.CompilerParams(
            dimension_semantics=("arbitrary", "arbitrary"), vmem_limit_bytes=VMEM_LIMIT),
        name=f"prompt_attention_{i0}",
    )(proj, proj, small, proj, proj, small)


def _prompt_attention(proj, small, *, n_batch, seq, tq=128, tiles_per_part=2):
    nq = seq // tq
    parts = [
        _prompt_attention_part(proj, small, n_batch=n_batch, seq=seq, i0=i0,
                               n_tiles=min(tiles_per_part, nq - i0), tq=tq)
        for i0 in range(0, nq, tiles_per_part)
    ]
    return jnp.concatenate(parts, axis=1).reshape(n_batch * seq, ATTN_W)


SELECT_PAGES_PER_STEP = 32
ATTEND_PAGES_PER_STEP = 32
SUB_PAGES = 8


def _sidx_kernel(pt_ref, qst_ref, w_ref, sm_ref, *refs, n_chunks, n_new, ps):
    pages = refs[:ps]
    o_ref, s_ref = refs[ps], refs[ps + 1]
    c = pl.program_id(1)
    qst = qst_ref[...]
    w = w_ref[...] * IDX_SCALE

    def head_sum(s):
        s = jnp.maximum(s, 0.0) * w
        out = s[0:n_new]
        for h in range(1, N_IDX_HEADS):
            out = out + s[h * n_new:(h + 1) * n_new]
        return out

    for r0 in range(0, ps, SUB_PAGES):
        kt = jnp.concatenate([pages[r0 + r][...] for r in range(SUB_PAGES)], axis=1).astype(BF16)
        part = head_sum(_dot(qst, kt))
        for r in range(SUB_PAGES):
            s_ref[ps * c + r0 + r] = part[:, r * PAGE_SIZE:(r + 1) * PAGE_SIZE]

    @pl.when(c == n_chunks - 1)
    def _():
        n_past_blocks = n_chunks * ps
        past = n_past_blocks * PAGE_SIZE
        k_new = sm_ref[...][:, 0:IDX_DIM]
        kp = jnp.concatenate([k_new, jnp.zeros((PAGE_SIZE - n_new, IDX_DIM), F32)], axis=0)
        lane = lax.broadcasted_iota(jnp.int32, (n_new, LANES), 1)
        trow = lax.broadcasted_iota(jnp.int32, (n_new, LANES), 0)
        s_new = head_sum(_dot_nt(qst, kp.astype(BF16)))
        s_ref[n_past_blocks] = jnp.where(lane < n_new, jnp.where(lane <= trow, s_new, F32_MIN), -jnp.inf)
        s_all = jnp.concatenate([s_ref[k] for k in range(n_past_blocks + 1)], axis=1)
        n_all = past + LANES
        sel = _select_topk(s_all, min(TOPK_MAX, (past + n_new) // 4))
        col = lax.broadcasted_iota(jnp.int32, (n_new, n_all), 1)
        tq = lax.broadcasted_iota(jnp.int32, (n_new, n_all), 0)
        valid = (col - past) <= tq
        o_ref[...] = jnp.where(sel & valid, 0.0, -jnp.inf)


def _sample_select(page_table, qst, wcol, small, cache_kidx_t, *, n_new):
    n_seq, n_pages = page_table.shape
    ps = min(SELECT_PAGES_PER_STEP, n_pages)
    n_chunks = n_pages // ps
    n_all = n_pages * PAGE_SIZE + LANES
    page_specs = [
        pl.BlockSpec((None, IDX_DIM, PAGE_SIZE), lambda b, c, pt, r=r: (pt[b, ps * c + r], 0, 0))
        for r in range(ps)
    ]
    rows = N_IDX_HEADS * n_new
    grid_spec = pltpu.PrefetchScalarGridSpec(
        num_scalar_prefetch=1,
        grid=(n_seq, n_chunks),
        in_specs=[
            pl.BlockSpec((None, rows, IDX_DIM), lambda b, c, pt: (b, 0, 0)),
            pl.BlockSpec((None, rows, 1), lambda b, c, pt: (b, 0, 0)),
            pl.BlockSpec((n_new, LANES), lambda b, c, pt: (b, 0)),
        ] + page_specs,
        out_specs=pl.BlockSpec((None, n_new, n_all), lambda b, c, pt: (b, 0, 0)),
        scratch_shapes=[pltpu.VMEM((n_pages + 1, n_new, LANES), F32)],
    )
    return pl.pallas_call(
        functools.partial(_sidx_kernel, n_chunks=n_chunks, n_new=n_new, ps=ps),
        grid_spec=grid_spec,
        out_shape=jax.ShapeDtypeStruct((n_seq, n_new, n_all), F32),
        compiler_params=pltpu.CompilerParams(
            dimension_semantics=("arbitrary", "arbitrary"), vmem_limit_bytes=VMEM_LIMIT),
        name="sample_select",
    )(page_table, qst, wcol, small, *([cache_kidx_t] * ps))


def _sattn_kernel(pt_ref, q_ref, bias_ref, biasn_ref, kn_ref, vn_ref, *refs, n_chunks, n_new, ps):
    kpages = refs[:ps]
    vpages = refs[ps:2 * ps]
    o_ref, m_ref, l_ref, acc_ref = refs[2 * ps:]
    c = pl.program_id(1)
    rows = q_ref.shape[0]
    reps = rows // n_new

    @pl.when(c == 0)
    def _():
        m_ref[...] = jnp.full(m_ref.shape, -1e30, F32)
        l_ref[...] = jnp.zeros(l_ref.shape, F32)
        acc_ref[...] = jnp.zeros(acc_ref.shape, F32)

    def update(logits, bias, pv):
        logits = logits + jnp.concatenate([bias] * reps, axis=0)
        m_old = m_ref[...]
        m_new = jnp.maximum(m_old, jnp.max(logits, axis=1, keepdims=True))
        alpha = jnp.exp(m_old - m_new)
        p = jnp.exp(logits - m_new)
        l_ref[...] = alpha * l_ref[...] + jnp.sum(p, axis=1, keepdims=True)
        acc_ref[...] = alpha * acc_ref[...] + pv(p.astype(BF16))
        m_ref[...] = m_new

    sub_keys = SUB_PAGES * PAGE_SIZE
    for r0 in range(0, ps, SUB_PAGES):
        kt = jnp.concatenate([kpages[r0 + r][...] for r in range(SUB_PAGES)], axis=1).astype(BF16)
        vt = jnp.concatenate([vpages[r0 + r][...] for r in range(SUB_PAGES)], axis=1).astype(BF16)
        k0 = (r0 // SUB_PAGES) * sub_keys
        update(_dot(q_ref[...], kt), bias_ref[:, k0:k0 + sub_keys], lambda p, vt=vt: _dot_nt(p, vt))

    @pl.when(c == n_chunks - 1)
    def _():
        pad = jnp.zeros((PAGE_SIZE - n_new, KV_W), F32)
        kn = jnp.concatenate([kn_ref[...], pad], axis=0).astype(BF16)
        vn = jnp.concatenate([vn_ref[...], pad], axis=0).astype(BF16)
        update(_dot_nt(q_ref[...], kn), biasn_ref[...], lambda p: _dot(p, vn))
        o_ref[...] = acc_ref[...] / l_ref[...]


def _sample_attend(page_table, qbd, bias, proj, cache_k_t, cache_v_t, *, n_new):
    n_seq, n_pages = page_table.shape
    ps = min(ATTEND_PAGES_PER_STEP, n_pages)
    n_chunks = n_pages // ps
    rows = qbd.shape[1]
    chunk_keys = ps * PAGE_SIZE
    page_specs = [
        pl.BlockSpec((None, KV_W, PAGE_SIZE), lambda b, c, pt, r=r: (pt[b, ps * c + r], 0, 0))
        for r in range(ps)
    ]
    grid_spec = pltpu.PrefetchScalarGridSpec(
        num_scalar_prefetch=1,
        grid=(n_seq, n_chunks),
        in_specs=[
            pl.BlockSpec((None, rows, KV_W), lambda b, c, pt: (b, 0, 0)),
            pl.BlockSpec((None, n_new, chunk_keys), lambda b, c, pt: (b, 0, c)),
            pl.BlockSpec((None, n_new, LANES), lambda b, c, pt: (b, 0, n_pages)),
            pl.BlockSpec((n_new, KV_W), lambda b, c, pt: (b, 12)),
            pl.BlockSpec((n_new, KV_W), lambda b, c, pt: (b, 13)),
        ] + page_specs + page_specs,
        out_specs=pl.BlockSpec((None, rows, KV_W), lambda b, c, pt: (b, 0, 0)),
        scratch_shapes=[
            pltpu.VMEM((rows, 1), F32),
            pltpu.VMEM((rows, 1), F32),
            pltpu.VMEM((rows, KV_W), F32),
        ],
    )
    return pl.pallas_call(
        functools.partial(_sattn_kernel, n_chunks=n_chunks, n_new=n_new, ps=ps),
        grid_spec=grid_spec,
        out_shape=jax.ShapeDtypeStruct((n_seq, rows, KV_W), F32),
        compiler_params=pltpu.CompilerParams(
            dimension_semantics=("arbitrary", "arbitrary"), vmem_limit_bytes=VMEM_LIMIT),
        name="sample_attend",
    )(page_table, qbd, bias, bias, proj, proj, *([cache_k_t] * ps), *([cache_v_t] * ps))


def _merge_kernel(x_ref, rnn_ref, att_ref, gr_ref, ga_ref, wr_ref, wa_ref, wo_ref, o_ref, mix_ref):
    n = pl.program_id(1)

    @pl.when(n == 0)
    def _():
        mixed = (gr_ref[...] * _dot(rnn_ref[...].astype(BF16), wr_ref[...])
                 + ga_ref[...] * _dot(att_ref[...].astype(BF16), wa_ref[...]))
        mix_ref[...] = mixed.astype(BF16)

    o_ref[...] = x_ref[...] + _dot(mix_ref[...], wo_ref[...])


def _merge(x, rnn, attn, proj, wr, wa, wo, *, tm, tn=512):
    t = x.shape[0]
    return pl.pallas_call(
        _merge_kernel,
        grid=(t // tm, D_MODEL // tn),
        in_specs=[
            pl.BlockSpec((tm, tn), lambda i, n: (i, n)),
            pl.BlockSpec((tm, D_RNN), lambda i, n: (i, 0)),
            pl.BlockSpec((tm, ATTN_W), lambda i, n: (i, 0)),
            pl.BlockSpec((tm, D_MODEL), lambda i, n: (i, 2)),
            pl.BlockSpec((tm, D_MODEL), lambda i, n: (i, 3)),
            pl.BlockSpec((D_RNN, D_MODEL), lambda i, n: (0, 0)),
            pl.BlockSpec((ATTN_W, D_MODEL), lambda i, n: (0, 0)),
            pl.BlockSpec((D_MODEL, tn), lambda i, n: (0, n)),
        ],
        out_specs=pl.BlockSpec((tm, tn), lambda i, n: (i, n)),
        out_shape=jax.ShapeDtypeStruct((t, D_MODEL), F32),
        scratch_shapes=[pltpu.VMEM((tm, D_MODEL), BF16)],
        compiler_params=pltpu.CompilerParams(
            dimension_semantics=("arbitrary", "arbitrary"), vmem_limit_bytes=VMEM_LIMIT),
        name="merge",
    )(x, rnn, attn, proj, proj, wr, wa, wo)


def _moe_kernel(x_ref, ln_ref, rwh_ref, rwl_ref, rb_ref, w13_ref, w2_ref, o_ref, h_ref, gate_ref):
    e = pl.program_id(1)
    lane = lax.broadcasted_iota(jnp.int32, (1, LANES), 1)
    lanef = lane.astype(F32)

    @pl.when(e == 0)
    def _():
        h = _rms(x_ref[...], ln_ref[...])
        hh, hl = _split_bf16(h)
        h_ref[...] = hh
        lg = (_dot(hh, rwh_ref[...]) + _dot(hl, rwh_ref[...]) + _dot(hh, rwl_ref[...])) + rb_ref[...]
        is_g = (lane >= N_EXPERTS) & (lane < N_EXPERTS + N_GROUPS)
        gl = jnp.where(is_g, lg, -jnp.inf)
        gmax = jnp.max(gl, axis=1, keepdims=True)
        gprob = 1.0 / jnp.sum(jnp.exp(gl - gmax), axis=1, keepdims=True)
        gsel = jnp.min(jnp.where(is_g & (lg == gmax), lanef - N_EXPERTS, 1e9), axis=1, keepdims=True)
        in_grp = (lane < N_EXPERTS) & (jnp.floor(lanef * (1.0 / EXPERTS_PER_GROUP)) == gsel)
        v1 = jnp.where(in_grp, lg, -jnp.inf)
        t1 = jnp.max(v1, axis=1, keepdims=True)
        i1 = jnp.min(jnp.where(in_grp & (lg == t1), lanef, 1e9), axis=1, keepdims=True)
        rest = in_grp & (lanef != i1)
        v2 = jnp.where(rest, lg, -jnp.inf)
        t2 = jnp.max(v2, axis=1, keepdims=True)
        i2 = jnp.min(jnp.where(rest & (lg == t2), lanef, 1e9), axis=1, keepdims=True)
        d = jnp.exp(t2 - t1)
        w1 = 1.0 / (1.0 + d)
        w2 = d / (1.0 + d)
        gate_ref[...] = (jnp.where(lanef == i1, w1, 0.0) + jnp.where(lanef == i2, w2, 0.0)) * gprob

    ge = jnp.sum(jnp.where(lane == e, gate_ref[...], 0.0), axis=1, keepdims=True)
    a = _dot(h_ref[...], w13_ref[...])
    up = a[:, 0:D_EXPERT]
    hid = (up * _sigmoid(up)) * a[:, D_EXPERT:2 * D_EXPERT]
    contrib = _dot((hid * ge).astype(BF16), w2_ref[...])

    @pl.when(e == 0)
    def _():
        o_ref[...] = x_ref[...] + contrib

    @pl.when(e > 0)
    def _():
        o_ref[...] += contrib


def _moe(x, ln2, rw_hi, rw_lo, rb, w13, w2, *, tm):
    t = x.shape[0]
    return pl.pallas_call(
        _moe_kernel,
        grid=(t // tm, N_EXPERTS),
        in_specs=[
            pl.BlockSpec((tm, D_MODEL), lambda i, e: (i, 0)),
            pl.BlockSpec((1, D_MODEL), lambda i, e: (0, 0)),
            pl.BlockSpec((D_MODEL, LANES), lambda i, e: (0, 0)),
            pl.BlockSpec((D_MODEL, LANES), lambda i, e: (0, 0)),
            pl.BlockSpec((1, LANES), lambda i, e: (0, 0)),
            pl.BlockSpec((None, D_MODEL, 2 * D_EXPERT), lambda i, e: (e, 0, 0)),
            pl.BlockSpec((None, D_EXPERT, D_MODEL), lambda i, e: (e, 0, 0)),
        ],
        out_specs=pl.BlockSpec((tm, D_MODEL), lambda i, e: (i, 0)),
        out_shape=jax.ShapeDtypeStruct((t, D_MODEL), F32),
        scratch_shapes=[pltpu.VMEM((tm, D_MODEL), BF16), pltpu.VMEM((tm, LANES), F32)],
        compiler_params=pltpu.CompilerParams(
            dimension_semantics=("arbitrary", "arbitrary"), vmem_limit_bytes=VMEM_LIMIT),
        name="moe",
    )(x, ln2, rw_hi, rw_lo, rb, w13, w2)


def _ple_kernel(xf_ref, xt_ref, p_ref, ln_ref, wg_ref, wp_ref, o_ref, h_ref, pb_ref):
    n = pl.program_id(1)

    @pl.when(n == 0)
    def _():
        h_ref[...] = _rms(xf_ref[...], ln_ref[...]).astype(BF16)
        pb_ref[...] = p_ref[...].astype(BF16)

    gate = _sigmoid(_dot(h_ref[...], wg_ref[...]))
    o_ref[...] = xt_ref[...] + gate * _dot(pb_ref[...], wp_ref[...])


def _ple(x, p, ln3, wg, wp, *, tm, tn=512):
    t = x.shape[0]
    return pl.pallas_call(
        _ple_kernel,
        grid=(t // tm, D_MODEL // tn),
        in_specs=[
            pl.BlockSpec((tm, D_MODEL), lambda i, n: (i, 0)),
            pl.BlockSpec((tm, tn), lambda i, n: (i, n)),
            pl.BlockSpec((tm, PLE_DIM), lambda i, n: (i, 0)),
            pl.BlockSpec((1, D_MODEL), lambda i, n: (0, 0)),
            pl.BlockSpec((D_MODEL, tn), lambda i, n: (0, n)),
            pl.BlockSpec((PLE_DIM, tn), lambda i, n: (0, n)),
        ],
        out_specs=pl.BlockSpec((tm, tn), lambda i, n: (i, n)),
        out_shape=jax.ShapeDtypeStruct((t, D_MODEL), F32),
        scratch_shapes=[pltpu.VMEM((tm, D_MODEL), BF16), pltpu.VMEM((tm, PLE_DIM), BF16)],
        compiler_params=pltpu.CompilerParams(
            dimension_semantics=("arbitrary", "arbitrary"), vmem_limit_bytes=VMEM_LIMIT),
        name="ple",
    )(x, x, p, ln3, wg, wp)


def _rope_tables(pos):
    half = HEAD_DIM // 2
    inv = ROPE_THETA ** (-jnp.arange(half, dtype=F32) / half)
    ang = pos.astype(F32)[:, None] * inv[None, :]
    cos, sin = jnp.cos(ang), jnp.sin(ang)
    return (jnp.concatenate([cos, cos, cos, cos], axis=1),
            jnp.concatenate([-sin, sin, -sin, sin], axis=1))


def _block_diag(w, per):
    n, r, _ = w.shape
    eye = jnp.eye(per, dtype=w.dtype)
    wg = w.reshape(n // per, per, r, r)
    return jnp.einsum("gpij,pq->gpiqj", wg, eye).reshape(n // per, per * r, per * r)


def _layer_weights(ln1, w_in, q_norm, k_norm, conv_w, conv_b, w_a, b_a, w_x, b_x, lam, w_br_rnn,
                   w_br_attn, w_out, ln2, w_rg, b_rg, w_re, b_re, w1, w3, w2, ln3, w_ple_gate,
                   w_ple_proj):
    o_q = 2 * D_RNN
    o_k = o_q + ATTN_W
    o_v = o_k + KV_W
    o_qi = o_v + KV_W
    o_ki = o_qi + N_IDX_HEADS * IDX_DIM
    o_wi = o_ki + IDX_DIM
    o_gr = o_wi + N_IDX_HEADS
    o_ga = o_gr + D_MODEL
    w_main = jnp.concatenate([w_in[:, :o_ki], w_in[:, o_gr:o_ga + D_MODEL]], axis=1).astype(BF16)
    w_small = jnp.concatenate(
        [w_in[:, o_ki:o_gr], jnp.zeros((D_MODEL, LANES - IDX_DIM - N_IDX_HEADS), F32)], axis=1).astype(BF16)
    ones = lambda n: jnp.ones((n,), F32)
    zeros = lambda n: jnp.zeros((n,), F32)
    n_gate = 2 * D_MODEL
    gain = jnp.concatenate([ones(o_q), jnp.tile(q_norm, N_HEADS), jnp.tile(k_norm, N_KV_HEADS),
                            ones(KV_W + N_IDX_HEADS * IDX_DIM + n_gate)])
    norm_on = jnp.concatenate([zeros(o_q), ones(ATTN_W + KV_W), zeros(KV_W + N_IDX_HEADS * IDX_DIM + n_gate)])
    rope_on = jnp.concatenate([zeros(o_q), ones(ATTN_W + KV_W), zeros(KV_W), ones(N_IDX_HEADS * IDX_DIM),
                               zeros(n_gate)])
    post = jnp.concatenate([ones(o_q), jnp.full((ATTN_W,), QK_SCALE, F32),
                            ones(2 * KV_W + N_IDX_HEADS * IDX_DIM + n_gate)])
    colctl = jnp.concatenate([jnp.stack([gain, norm_on, rope_on, post]), jnp.zeros((4, N_MAIN), F32)], axis=0)
    tn = 512
    head_of = jnp.arange(tn) // HEAD_DIM
    bd = (head_of[:, None] == head_of[None, :]).astype(BF16)
    rw = jnp.concatenate([w_re, w_rg, jnp.zeros((D_MODEL, LANES - N_EXPERTS - N_GROUPS), F32)], axis=1)
    rw_hi = rw.astype(BF16)
    rw_lo = (rw - rw_hi.astype(F32)).astype(BF16)
    rb = jnp.concatenate([b_re, b_rg, jnp.zeros((LANES - N_EXPERTS - N_GROUPS,), F32)])[None, :]
    return dict(
        ln1=ln1[None, :], w_main=w_main, w_small=w_small, colctl=colctl, bd=bd,
        cw=conv_w, cb=conv_b[None, :],
        wa_bd=_block_diag(w_a, 4).astype(BF16), ba=b_a[None, :],
        wx_bd=_block_diag(w_x, 4).astype(BF16), bx=b_x[None, :], lam=lam[None, :],
        wr=w_br_rnn.astype(BF16), wa=w_br_attn.astype(BF16), wo=w_out.astype(BF16),
        ln2=ln2[None, :], rw_hi=rw_hi, rw_lo=rw_lo, rb=rb,
        w13=jnp.concatenate([w1, w3], axis=2).astype(BF16), w2=w2.astype(BF16),
        ln3=ln3[None, :], wg=w_ple_gate.astype(BF16), wp=w_ple_proj.astype(BF16),
    )


def _tail(x, rnn, attn, proj, p, w):
    t = x.shape[0]
    x1 = _merge(x, rnn, attn, proj, w["wr"], w["wa"], w["wo"], tm=min(t, 256))
    x2 = _moe(x1, w["ln2"], w["rw_hi"], w["rw_lo"], w["rb"], w["w13"], w["w2"], tm=min(t, 512))
    return _ple(x2, p, w["ln3"], w["wg"], w["wp"], tm=min(t, 512))


def _prompt_layer(x, p, w):
    bp, tp, _ = x.shape
    xt = x.reshape(bp * tp, D_MODEL)
    cs, sn = _rope_tables(jnp.tile(jnp.arange(tp, dtype=jnp.int32), bp))
    proj, small = _inproj(xt, w["ln1"], w["w_main"], w["w_small"], w["colctl"], cs, sn, w["bd"],
                          tm=min(bp * tp, 1024))
    conv0 = jnp.zeros((bp, 8, D_RNN), F32)
    h0 = jnp.zeros((bp, 1, D_RNN), F32)
    rnn, h_last = _rglru(proj, conv0, h0, w["cw"], w["cb"], w["wa_bd"], w["ba"], w["wx_bd"], w["bx"],
                         w["lam"], n_seq=bp, tt=min(tp, 256))
    attn = _prompt_attention(proj, small, n_batch=bp, seq=tp)
    y = _tail(xt, rnn, attn, proj, p.reshape(bp * tp, PLE_DIM), w)
    o_k = 2 * D_RNN + ATTN_W
    k = proj[:, o_k:o_k + KV_W].reshape(bp, tp, N_KV_HEADS, HEAD_DIM)
    v = proj[:, o_k + KV_W:o_k + 2 * KV_W].reshape(bp, tp, N_KV_HEADS, HEAD_DIM)
    ki = small[:, :IDX_DIM].reshape(bp, tp, IDX_DIM)
    conv_new = proj[:, :D_RNN].reshape(bp, tp, D_RNN)[:, tp - (CONV_W - 1):]
    return y.reshape(bp, tp, D_MODEL), (k, v, ki, conv_new, h_last.reshape(bp, D_RNN))


def _sample_layer(x, p, cache_k, cache_v, cache_kidx, state_conv, state_h, page_table, w):
    bs, ts, _ = x.shape
    n_pages = page_table.shape[1]
    past = n_pages * PAGE_SIZE
    xt = x.reshape(bs * ts, D_MODEL)
    cs, sn = _rope_tables(past + jnp.tile(jnp.arange(ts, dtype=jnp.int32), bs))
    proj, small = _inproj(xt, w["ln1"], w["w_main"], w["w_small"], w["colctl"], cs, sn, w["bd"], tm=bs * ts)
    conv0 = jnp.concatenate([jnp.zeros((bs, 8 - (CONV_W - 1), D_RNN), F32), state_conv], axis=1)
    rnn, h_last = _rglru(proj, conv0, state_h[:, None, :], w["cw"], w["cb"], w["wa_bd"], w["ba"],
                         w["wx_bd"], w["bx"], w["lam"], n_seq=bs, tt=ts)
    o_q = 2 * D_RNN
    o_qi = o_q + ATTN_W + 2 * KV_W
    qi = proj[:, o_qi:o_qi + N_IDX_HEADS * IDX_DIM].reshape(bs, ts, N_IDX_HEADS, IDX_DIM)
    qst = jnp.transpose(qi, (0, 2, 1, 3)).reshape(bs, N_IDX_HEADS * ts, IDX_DIM).astype(BF16)
    wi = small[:, IDX_DIM:IDX_DIM + N_IDX_HEADS].reshape(bs, ts, N_IDX_HEADS)
    wcol = jnp.transpose(wi, (0, 2, 1)).reshape(bs, N_IDX_HEADS * ts, 1)
    n_pool = cache_k.shape[0]
    kidx_t = jnp.transpose(cache_kidx, (0, 2, 1))
    k_t = jnp.transpose(cache_k, (0, 2, 3, 1)).reshape(n_pool, KV_W, PAGE_SIZE)
    v_t = jnp.transpose(cache_v, (0, 2, 3, 1)).reshape(n_pool, KV_W, PAGE_SIZE)
    bias = _sample_select(page_table, qst, wcol, small, kidx_t, n_new=ts)
    q = proj[:, o_q:o_q + ATTN_W].reshape(bs, ts, N_KV_HEADS, N_HEADS // N_KV_HEADS, HEAD_DIM)
    eye = jnp.eye(N_KV_HEADS, dtype=F32)
    qbd = jnp.einsum("btgjd,gk->bgjtkd", q, eye).reshape(bs, N_HEADS * ts, KV_W).astype(BF16)
    att = _sample_attend(page_table, qbd, bias, proj, k_t, v_t, n_new=ts)
    att = att.reshape(bs, N_KV_HEADS, N_HEADS // N_KV_HEADS, ts, N_KV_HEADS, HEAD_DIM)
    att = jnp.stack([att[:, g, :, :, g, :] for g in range(N_KV_HEADS)], axis=1)
    attn = jnp.transpose(att, (0, 3, 1, 2, 4)).reshape(bs * ts, ATTN_W)
    y = _tail(xt, rnn, attn, proj, p.reshape(bs * ts, PLE_DIM), w)
    o_k = o_q + ATTN_W
    k = proj[:, o_k:o_k + KV_W].reshape(bs, ts, N_KV_HEADS, HEAD_DIM)
    v = proj[:, o_k + KV_W:o_k + 2 * KV_W].reshape(bs, ts, N_KV_HEADS, HEAD_DIM)
    ki = small[:, :IDX_DIM].reshape(bs, ts, IDX_DIM)
    conv_new = proj[:, :D_RNN].reshape(bs, ts, D_RNN)[:, ts - (CONV_W - 1):]
    return y.reshape(bs, ts, D_MODEL), (k, v, ki, conv_new, h_last.reshape(bs, D_RNN))


def kernel(x_prompt, x_sample, p_prompt, p_sample, cache_k, cache_v, cache_kidx, state_conv, state_h,
           page_table, ln1, w_in, q_norm, k_norm, conv_w, conv_b, w_a, b_a, w_x, b_x, lam, w_br_rnn,
           w_br_attn, w_out, ln2, w_rg, b_rg, w_re, b_re, w1, w3, w2, ln3, w_ple_gate, w_ple_proj):
    weights = (ln1, w_in, q_norm, k_norm, conv_w, conv_b, w_a, b_a, w_x, b_x, lam, w_br_rnn, w_br_attn,
               w_out, ln2, w_rg, b_rg, w_re, b_re, w1, w3, w2, ln3, w_ple_gate, w_ple_proj)
    depth = ln1.shape[0]
    yp, ys = x_prompt, x_sample
    st_p, st_s = [], []
    for i in range(depth):
        w = _layer_weights(*[wt[i] for wt in weights])
        yp, sp = _prompt_layer(yp, p_prompt[i], w)
        ys, ss = _sample_layer(ys, p_sample[i], cache_k[i], cache_v[i], cache_kidx[i], state_conv[i],
                               state_h[i], page_table, w)
        st_p.append(sp)
        st_s.append(ss)
    stack = lambda sts, j: jnp.stack([s[j] for s in sts])
    return (yp, ys, stack(st_p, 0), stack(st_p, 1), stack(st_p, 2), stack(st_p, 3), stack(st_p, 4),
            stack(st_s, 0), stack(st_s, 1), stack(st_s, 2), stack(st_s, 3), stack(st_s, 4))
```

```python
import functools

import jax
import jax.numpy as jnp
import numpy as np
from jax import lax
from jax.experimental import pallas as pl
from jax.experimental.pallas import tpu as pltpu

F32 = jnp.float32
BF16 = jnp.bfloat16

D_MODEL = 2048
HEAD_DIM = 64
N_HEADS = 16
N_KV_HEADS = 4
ATTN_W = N_HEADS * HEAD_DIM
KV_W = N_KV_HEADS * HEAD_DIM
N_IDX_HEADS = 8
IDX_DIM = 64
TOPK_MAX = 256
ROPE_THETA = 10000.0
D_RNN = 1024
N_RNN_BLOCKS = 16
RNN_BLOCK = 64
CONV_W = 4
LRU_C = 8.0
N_GROUPS = 4
EXPERTS_PER_GROUP = 8
N_EXPERTS = 32
D_EXPERT = 256
PLE_DIM = 256
PAGE_SIZE = 128
EPS = 1e-6

LANES = 128
N_MAIN = 8192
COL_RAW_END = 2 * D_RNN
COL_NR_END = 4096
IDX_SCALE = (IDX_DIM ** -0.5) * (N_IDX_HEADS ** -0.5)
QK_SCALE = HEAD_DIM ** -0.5
F32_MIN = float(np.finfo(np.float32).min)
INT_MIN = -2147483648
KEY_NEG_INF = INT_MIN + 0x7FFFFF
VMEM_LIMIT = 56 * 1024 * 1024


def _dot(a, b):
    return jnp.dot(a, b, preferred_element_type=F32)


def _dot_nt(a, b):
    return lax.dot_general(a, b, (((1,), (1,)), ((), ())), preferred_element_type=F32)


def _sigmoid(x):
    return 1.0 / (1.0 + jnp.exp(-x))


def _rms(x, g):
    return x * lax.rsqrt(jnp.mean(x * x, axis=-1, keepdims=True) + EPS) * g


def _split_bf16(x):
    hi = x.astype(BF16)
    lo = (x - hi.astype(F32)).astype(BF16)
    return hi, lo


def _rope_chunks(y, c, s):
    lane = lax.broadcasted_iota(jnp.int32, (1, LANES), 1)
    first_half = (lane % HEAD_DIM) < (HEAD_DIM // 2)
    outs = []
    for k in range(y.shape[1] // LANES):
        yc = y[:, k * LANES:(k + 1) * LANES]
        partner = jnp.where(first_half, pltpu.roll(yc, LANES - HEAD_DIM // 2, 1),
                            pltpu.roll(yc, HEAD_DIM // 2, 1))
        outs.append(yc * c + partner * s)
    return outs[0] if len(outs) == 1 else jnp.concatenate(outs, axis=1)


def _inproj_kernel(x_ref, ln_ref, w_ref, ws_ref, ctl_ref, cs_ref, sn_ref, bd_ref,
                   o_ref, os_ref, h_ref, *, tn):
    j = pl.program_id(1)

    @pl.when(j == 0)
    def _():
        hb = _rms(x_ref[...], ln_ref[...]).astype(BF16)
        h_ref[...] = hb
        ys = _dot(hb, ws_ref[...])
        lane = lax.broadcasted_iota(jnp.int32, (1, LANES), 1)
        os_ref[...] = jnp.where(lane < IDX_DIM, _rope_chunks(ys, cs_ref[...], sn_ref[...]), ys)

    y = _dot(h_ref[...], w_ref[...])

    @pl.when(j < COL_RAW_END // tn)
    def _():
        o_ref[...] = y

    @pl.when((j >= COL_RAW_END // tn) & (j < COL_NR_END // tn))
    def _():
        ctl = ctl_ref[...]
        gain, norm_on, rope_on, post = ctl[0:1], ctl[1:2], ctl[2:3], ctl[3:4]
        hi, lo = _split_bf16(y * y)
        ss = _dot(hi, bd_ref[...]) + _dot(lo, bd_ref[...])
        yn = jnp.where(norm_on > 0.0, y * lax.rsqrt(ss * (1.0 / HEAD_DIM) + EPS) * gain, y)
        yr = jnp.where(rope_on > 0.0, _rope_chunks(yn, cs_ref[...], sn_ref[...]), yn)
        o_ref[...] = yr * post

    @pl.when(j >= COL_NR_END // tn)
    def _():
        o_ref[...] = _sigmoid(y)


def _inproj(x, ln1, w_main, w_small, colctl, cs, sn, bd, *, tm, tn=512):
    t = x.shape[0]
    grid = (t // tm, N_MAIN // tn)
    return pl.pallas_call(
        functools.partial(_inproj_kernel, tn=tn),
        grid=grid,
        in_specs=[
            pl.BlockSpec((tm, D_MODEL), lambda i, j: (i, 0)),
            pl.BlockSpec((1, D_MODEL), lambda i, j: (0, 0)),
            pl.BlockSpec((D_MODEL, tn), lambda i, j: (0, j)),
            pl.BlockSpec((D_MODEL, LANES), lambda i, j: (0, 0)),
            pl.BlockSpec((8, tn), lambda i, j: (0, j)),
            pl.BlockSpec((tm, LANES), lambda i, j: (i, 0)),
            pl.BlockSpec((tm, LANES), lambda i, j: (i, 0)),
            pl.BlockSpec((tn, tn), lambda i, j: (0, 0)),
        ],
        out_specs=[
            pl.BlockSpec((tm, tn), lambda i, j: (i, j)),
            pl.BlockSpec((tm, LANES), lambda i, j: (i, 0)),
        ],
        out_shape=[
            jax.ShapeDtypeStruct((t, N_MAIN), F32),
            jax.ShapeDtypeStruct((t, LANES), F32),
        ],
        scratch_shapes=[pltpu.VMEM((tm, D_MODEL), BF16)],
        compiler_params=pltpu.CompilerParams(
            dimension_semantics=("arbitrary", "arbitrary"), vmem_limit_bytes=VMEM_LIMIT),
        name="inproj",
    )(x, ln1, w_main, w_small, colctl, cs, sn, bd)


def _rglru_kernel(x_ref, g_ref, c0_ref, h0_ref, cw_ref, cb_ref, wa_ref, ba_ref, wx_ref, bx_ref,
                  lam_ref, o_ref, hl_ref, xs_ref, a_ref, b_ref, hc_ref):
    t = pl.program_id(1)
    tt = x_ref.shape[0]

    @pl.when(t == 0)
    def _():
        xs_ref[0:8, :] = c0_ref[...]
        hc_ref[...] = h0_ref[...]

    xs_ref[8:8 + tt, :] = x_ref[...]
    cw = cw_ref[...]
    taps = (xs_ref[5:5 + tt, :] * cw[0:1] + xs_ref[6:6 + tt, :] * cw[1:2]
            + xs_ref[7:7 + tt, :] * cw[2:3] + xs_ref[8:8 + tt, :] * cw[3:4])
    xc = cb_ref[...] + taps
    xs_ref[0:8, :] = xs_ref[tt:tt + 8, :]

    xcb = xc.astype(BF16)
    ra, ri = [], []
    for c in range(wa_ref.shape[0]):
        blk = xcb[:, c * 256:(c + 1) * 256]
        ra.append(_dot(blk, wa_ref[c]))
        ri.append(_dot(blk, wx_ref[c]))
    r = _sigmoid(jnp.concatenate(ra, axis=1) + ba_ref[...])
    ig = _sigmoid(jnp.concatenate(ri, axis=1) + bx_ref[...])
    nlam = -lam_ref[...]
    softplus = jnp.maximum(nlam, 0.0) + jnp.log1p(jnp.exp(-jnp.abs(nlam)))
    log_a = (-LRU_C) * r * softplus
    a = jnp.exp(log_a)
    u = jnp.sqrt(jnp.tanh(-log_a) * (a * a + 1.0)) * (ig * xc)

    n8 = tt // 8
    a3 = a.reshape(n8, 8, D_RNN)
    b3 = u.reshape(n8, 8, D_RNN)
    sub = lax.broadcasted_iota(jnp.int32, (1, 8, 1), 1)
    for s in (1, 2, 4):
        a_prev = pltpu.roll(a3, s, 1)
        b_prev = pltpu.roll(b3, s, 1)
        m = sub >= s
        b3 = jnp.where(m, a3 * b_prev + b3, b3)
        a3 = jnp.where(m, a3 * a_prev, a3)
    a_ref[...] = a3.reshape(tt, D_RNN)
    b_ref[...] = b3.reshape(tt, D_RNN)

    def chain(k, carry):
        i0 = pl.multiple_of(k * 8, 8)
        h8 = a_ref[pl.ds(i0, 8), :] * carry + b_ref[pl.ds(i0, 8), :]
        b_ref[pl.ds(i0, 8), :] = h8
        return h8[7:8, :]

    carry = lax.fori_loop(0, n8, chain, hc_ref[...])
    hc_ref[...] = carry
    g = g_ref[...]
    gelu = 0.5 * g * (1.0 + jnp.tanh(0.7978845608028654 * (g + 0.044715 * (g * g * g))))
    o_ref[...] = b_ref[...] * gelu

    @pl.when(t == pl.num_programs(1) - 1)
    def _():
        hl_ref[...] = carry


def _rglru(proj, conv0, h0, cw, cb, wa_bd, ba, wx_bd, bx, lam, *, n_seq, tt):
    t_total = proj.shape[0]
    nt = t_total // (n_seq * tt)
    full = lambda shape: pl.BlockSpec(shape, lambda b, t: (0,) * len(shape))
    return pl.pallas_call(
        _rglru_kernel,
        grid=(n_seq, nt),
        in_specs=[
            pl.BlockSpec((tt, D_RNN), lambda b, t: (b * nt + t, 0)),
            pl.BlockSpec((tt, D_RNN), lambda b, t: (b * nt + t, 1)),
            pl.BlockSpec((None, 8, D_RNN), lambda b, t: (b, 0, 0)),
            pl.BlockSpec((None, 1, D_RNN), lambda b, t: (b, 0, 0)),
            full((CONV_W, D_RNN)), full((1, D_RNN)),
            full(wa_bd.shape), full((1, D_RNN)),
            full(wx_bd.shape), full((1, D_RNN)),
            full((1, D_RNN)),
        ],
        out_specs=[
            pl.BlockSpec((tt, D_RNN), lambda b, t: (b * nt + t, 0)),
            pl.BlockSpec((None, 1, D_RNN), lambda b, t: (b, 0, 0)),
        ],
        out_shape=[
            jax.ShapeDtypeStruct((t_total, D_RNN), F32),
            jax.ShapeDtypeStruct((n_seq, 1, D_RNN), F32),
        ],
        scratch_shapes=[
            pltpu.VMEM((tt + 8, D_RNN), F32),
            pltpu.VMEM((tt, D_RNN), F32),
            pltpu.VMEM((tt, D_RNN), F32),
            pltpu.VMEM((1, D_RNN), F32),
        ],
        compiler_params=pltpu.CompilerParams(
            dimension_semantics=("arbitrary", "arbitrary"), vmem_limit_bytes=VMEM_LIMIT),
        name="rglru",
    )(proj, proj, conv0, h0, cw, cb, wa_bd, ba, wx_bd, bx, lam)


def _select_topk(s, kk):
    rows, n = s.shape
    kkf = float(kk)

    def key_to_f32(w):
        k = w ^ INT_MIN
        bits = jnp.where(k >= 0, k, k ^ 0x7FFFFFFF)
        return k, lax.bitcast_convert_type(bits, F32)

    def vbody(it, w):
        cand_w = w | jnp.left_shift(jnp.int32(1), 31 - it)
        cand_k, cand_f = key_to_f32(cand_w)
        cnt = jnp.sum(jnp.where(s >= cand_f, 1.0, 0.0), axis=1, keepdims=True)
        ok = (cnt >= kkf) | (cand_k < KEY_NEG_INF)
        return jnp.where(ok, cand_w, w)

    w = lax.fori_loop(0, 32, vbody, jnp.zeros((rows, 1), jnp.int32))
    _, thr = key_to_f32(w)
    gt = s > thr
    eq = s == thr
    need = kkf - jnp.sum(jnp.where(gt, 1.0, 0.0), axis=1, keepdims=True)
    col = lax.broadcasted_iota(jnp.int32, (1, n), 1)
    nbits = int(n).bit_length()

    def jbody(it, jmax):
        cand = jmax | jnp.left_shift(jnp.int32(1), nbits - 1 - it)
        cnt = jnp.sum(jnp.where(eq & (col < cand), 1.0, 0.0), axis=1, keepdims=True)
        return jnp.where(cnt <= need, cand, jmax)

    n_ge = jnp.sum(jnp.where(s >= thr, 1.0, 0.0), axis=1, keepdims=True)
    jmax = lax.cond(
        jnp.max(n_ge) > kkf,
        lambda: lax.fori_loop(0, nbits, jbody, jnp.zeros((rows, 1), jnp.int32)),
        lambda: jnp.full((rows, 1), (1 << nbits) - 1, jnp.int32))
    return gt | (eq & (col < jmax))


def _pattn_kernel(q_ref, qi_ref, sm_ref, k_ref, v_ref, ksm_ref, o_ref,
                  kb_ref, vb_ref, kib_ref, s_ref, *, i0, n_keys, kc, topk):
    i = pl.program_id(1)
    tq = q_ref.shape[0]
    lane = lax.broadcasted_iota(jnp.int32, (1, LANES), 1)

    @pl.when(i == 0)
    def _():
        kb_ref[...] = k_ref[0:n_keys, :].astype(BF16)
        vb_ref[...] = v_ref[0:n_keys, :].astype(BF16)
        kib_ref[...] = jnp.where(lane < IDX_DIM, ksm_ref[0:n_keys, :], 0.0).astype(BF16)

    sm = sm_ref[...]
    qi = qi_ref[...]
    qrows, wrows = [], []
    for h in range(N_IDX_HEADS):
        blk = qi[:, (h // 2) * LANES:(h // 2 + 1) * LANES]
        if h % 2 == 1:
            blk = pltpu.roll(blk, IDX_DIM, 1)
        qrows.append(jnp.where(lane < IDX_DIM, blk, 0.0))
        wrows.append(sm[:, IDX_DIM + h:IDX_DIM + h + 1])
    qst = jnp.concatenate(qrows, axis=0).astype(BF16)
    wst = jnp.concatenate(wrows, axis=0) * IDX_SCALE
    qpos = (i0 + i) * tq + lax.broadcasted_iota(jnp.int32, (tq, 1), 0)
    for c in range(n_keys // kc):
        s = jnp.maximum(_dot_nt(qst, kib_ref[c * kc:(c + 1) * kc, :]), 0.0) * wst
        sc = s[0:tq]
        for h in range(1, N_IDX_HEADS):
            sc = sc + s[h * tq:(h + 1) * tq]
        col = c * kc + lax.broadcasted_iota(jnp.int32, (1, kc), 1)
        s_ref[:, c * kc:(c + 1) * kc] = jnp.where(col <= qpos, sc, F32_MIN)

    sel = _select_topk(s_ref[...], topk)
    colf = lax.broadcasted_iota(jnp.int32, (1, n_keys), 1)
    s_ref[...] = jnp.where(sel & (colf <= qpos), 0.0, -jnp.inf)
    bias = s_ref[...]
    bias4 = jnp.concatenate([bias] * 4, axis=0)

    q = q_ref[...]
    outs = [None] * N_HEADS
    for g in range(N_KV_HEADS):
        lo = (g % 2) * HEAD_DIM
        keep = (lane >= lo) & (lane < lo + HEAD_DIM)
        rows = []
        for j in range(4):
            h = 4 * g + j
            blk = q[:, (h // 2) * LANES:(h // 2 + 1) * LANES]
            if h % 2 != g % 2:
                blk = pltpu.roll(blk, HEAD_DIM, 1)
            piece = jnp.where(keep, blk, 0.0)
            zero = jnp.zeros_like(piece)
            rows.append(jnp.concatenate([piece, zero] if g < 2 else [zero, piece], axis=1))
        qbd = jnp.concatenate(rows, axis=0).astype(BF16)
        logits = _dot_nt(qbd, kb_ref[...]) + bias4
        m = jnp.max(logits, axis=1, keepdims=True)
        p = jnp.exp(logits - m)
        denom = jnp.sum(p, axis=1, keepdims=True)
        acc = _dot(p.astype(BF16), vb_ref[...]) / denom
        for j in range(4):
            outs[4 * g + j] = acc[j * tq:(j + 1) * tq, (g // 2) * LANES:(g // 2 + 1) * LANES]
    for c in range(N_HEADS // 2):
        g = (2 * c) // 4
        even, odd = outs[2 * c], outs[2 * c + 1]
        if g % 2 == 1:
            even = pltpu.roll(even, HEAD_DIM, 1)
        else:
            odd = pltpu.roll(odd, HEAD_DIM, 1)
        o_ref[:, c * LANES:(c + 1) * LANES] = jnp.where(lane < HEAD_DIM, even, odd).astype(BF16)


def _prompt_attention_part(proj, small, *, n_batch, seq, i0, n_tiles, tq):
    nq = seq // tq
    n_keys = (i0 + n_tiles) * tq
    kc = next(c for c in (512, 256, 128) if n_keys % c == 0)
    return pl.pallas_call(
        functools.partial(_pattn_kernel, i0=i0, n_keys=n_keys, kc=kc, topk=min(TOPK_MAX, seq // 4)),
        grid=(n_batch, n_tiles),
        in_specs=[
            pl.BlockSpec((tq, ATTN_W), lambda b, i: (b * nq + i0 + i, 2)),
            pl.BlockSpec((tq, N_IDX_HEADS * IDX_DIM), lambda b, i: (b * nq + i0 + i, 7)),
            pl.BlockSpec((tq, LANES), lambda b, i: (b * nq + i0 + i, 0)),
            pl.BlockSpec((seq, KV_W), lambda b, i: (b, 12)),
            pl.BlockSpec((seq, KV_W), lambda b, i: (b, 13)),
            pl.BlockSpec((seq, LANES), lambda b, i: (b, 0)),
        ],
        out_specs=pl.BlockSpec((None, tq, ATTN_W), lambda b, i: (b, i, 0)),
        out_shape=jax.ShapeDtypeStruct((n_batch, n_tiles * tq, ATTN_W), BF16),
        scratch_shapes=[
            pltpu.VMEM((n_keys, KV_W), BF16),
            pltpu.VMEM((n_keys, KV_W), BF16),
            pltpu.VMEM((n_keys, LANES), BF16),
            pltpu.VMEM((tq, n_keys), F32),
        ],
        compiler_params=pltpu.CompilerParams(
            dimension_semantics=("arbitrary", "arbitrary"), vmem_limit_bytes=VMEM_LIMIT),
        name=f"prompt_attention_{i0}",
    )(proj, proj, small, proj, proj, small)


def _prompt_attention(proj, small, *, n_batch, seq, tq=128, tiles_per_part=2):
    nq = seq // tq
    parts = [
        _prompt_attention_part(proj, small, n_batch=n_batch, seq=seq, i0=i0,
                               n_tiles=min(tiles_per_part, nq - i0), tq=tq)
        for i0 in range(0, nq, tiles_per_part)
    ]
    return jnp.concatenate(parts, axis=1).reshape(n_batch * seq, ATTN_W)


SELECT_PAGES_PER_STEP = 32
ATTEND_PAGES_PER_STEP = 32
SUB_PAGES = 8


def _sidx_kernel(pt_ref, qst_ref, w_ref, sm_ref, *refs, n_chunks, n_new, ps):
    pages = refs[:ps]
    o_ref, s_ref = refs[ps], refs[ps + 1]
    c = pl.program_id(1)
    qst = qst_ref[...]
    w = w_ref[...] * IDX_SCALE

    def head_sum(s):
        s = jnp.maximum(s, 0.0) * w
        out = s[0:n_new]
        for h in range(1, N_IDX_HEADS):
            out = out + s[h * n_new:(h + 1) * n_new]
        return out

    for r0 in range(0, ps, SUB_PAGES):
        kt = jnp.concatenate([pages[r0 + r][...] for r in range(SUB_PAGES)], axis=1).astype(BF16)
        part = head_sum(_dot(qst, kt))
        for r in range(SUB_PAGES):
            s_ref[ps * c + r0 + r] = part[:, r * PAGE_SIZE:(r + 1) * PAGE_SIZE]

    @pl.when(c == n_chunks - 1)
    def _():
        n_past_blocks = n_chunks * ps
        past = n_past_blocks * PAGE_SIZE
        k_new = sm_ref[...][:, 0:IDX_DIM]
        kp = jnp.concatenate([k_new, jnp.zeros((PAGE_SIZE - n_new, IDX_DIM), F32)], axis=0)
        lane = lax.broadcasted_iota(jnp.int32, (n_new, LANES), 1)
        trow = lax.broadcasted_iota(jnp.int32, (n_new, LANES), 0)
        s_new = head_sum(_dot_nt(qst, kp.astype(BF16)))
        s_ref[n_past_blocks] = jnp.where(lane < n_new, jnp.where(lane <= trow, s_new, F32_MIN), -jnp.inf)
        s_all = jnp.concatenate([s_ref[k] for k in range(n_past_blocks + 1)], axis=1)
        n_all = past + LANES
        sel = _select_topk(s_all, min(TOPK_MAX, (past + n_new) // 4))
        col = lax.broadcasted_iota(jnp.int32, (n_new, n_all), 1)
        tq = lax.broadcasted_iota(jnp.int32, (n_new, n_all), 0)
        valid = (col - past) <= tq
        o_ref[...] = jnp.where(sel & valid, 0.0, -jnp.inf)


def _sample_select(page_table, qst, wcol, small, cache_kidx_t, *, n_new):
    n_seq, n_pages = page_table.shape
    ps = min(SELECT_PAGES_PER_STEP, n_pages)
    n_chunks = n_pages // ps
    n_all = n_pages * PAGE_SIZE + LANES
    page_specs = [
        pl.BlockSpec((None, IDX_DIM, PAGE_SIZE), lambda b, c, pt, r=r: (pt[b, ps * c + r], 0, 0))
        for r in range(ps)
    ]
    rows = N_IDX_HEADS * n_new
    grid_spec = pltpu.PrefetchScalarGridSpec(
        num_scalar_prefetch=1,
        grid=(n_seq, n_chunks),
        in_specs=[
            pl.BlockSpec((None, rows, IDX_DIM), lambda b, c, pt: (b, 0, 0)),
            pl.BlockSpec((None, rows, 1), lambda b, c, pt: (b, 0, 0)),
            pl.BlockSpec((n_new, LANES), lambda b, c, pt: (b, 0)),
        ] + page_specs,
        out_specs=pl.BlockSpec((None, n_new, n_all), lambda b, c, pt: (b, 0, 0)),
        scratch_shapes=[pltpu.VMEM((n_pages + 1, n_new, LANES), F32)],
    )
    return pl.pallas_call(
        functools.partial(_sidx_kernel, n_chunks=n_chunks, n_new=n_new, ps=ps),
        grid_spec=grid_spec,
        out_shape=jax.ShapeDtypeStruct((n_seq, n_new, n_all), F32),
        compiler_params=pltpu.CompilerParams(
            dimension_semantics=("arbitrary", "arbitrary"), vmem_limit_bytes=VMEM_LIMIT),
        name="sample_select",
    )(page_table, qst, wcol, small, *([cache_kidx_t] * ps))


def _sattn_kernel(pt_ref, q_ref, bias_ref, biasn_ref, kn_ref, vn_ref, *refs, n_chunks, n_new, ps):
    kpages = refs[:ps]
    vpages = refs[ps:2 * ps]
    o_ref, m_ref, l_ref, acc_ref = refs[2 * ps:]
    c = pl.program_id(1)
    rows = q_ref.shape[0]
    reps = rows // n_new

    @pl.when(c == 0)
    def _():
        m_ref[...] = jnp.full(m_ref.shape, -1e30, F32)
        l_ref[...] = jnp.zeros(l_ref.shape, F32)
        acc_ref[...] = jnp.zeros(acc_ref.shape, F32)

    def update(logits, bias, pv):
        logits = logits + jnp.concatenate([bias] * reps, axis=0)
        m_old = m_ref[...]
        m_new = jnp.maximum(m_old, jnp.max(logits, axis=1, keepdims=True))
        alpha = jnp.exp(m_old - m_new)
        p = jnp.exp(logits - m_new)
        l_ref[...] = alpha * l_ref[...] + jnp.sum(p, axis=1, keepdims=True)
        acc_ref[...] = alpha * acc_ref[...] + pv(p.astype(BF16))
        m_ref[...] = m_new

    sub_keys = SUB_PAGES * PAGE_SIZE
    for r0 in range(0, ps, SUB_PAGES):
        kt = jnp.concatenate([kpages[r0 + r][...] for r in range(SUB_PAGES)], axis=1).astype(BF16)
        vt = jnp.concatenate([vpages[r0 + r][...] for r in range(SUB_PAGES)], axis=1).astype(BF16)
        k0 = (r0 // SUB_PAGES) * sub_keys
        update(_dot(q_ref[...], kt), bias_ref[:, k0:k0 + sub_keys], lambda p, vt=vt: _dot_nt(p, vt))

    @pl.when(c == n_chunks - 1)
    def _():
        pad = jnp.zeros((PAGE_SIZE - n_new, KV_W), F32)
        kn = jnp.concatenate([kn_ref[...], pad], axis=0).astype(BF16)
        vn = jnp.concatenate([vn_ref[...], pad], axis=0).astype(BF16)
        update(_dot_nt(q_ref[...], kn), biasn_ref[...], lambda p: _dot(p, vn))
        o_ref[...] = acc_ref[...] / l_ref[...]


def _sample_attend(page_table, qbd, bias, proj, cache_k_t, cache_v_t, *, n_new):
    n_seq, n_pages = page_table.shape
    ps = min(ATTEND_PAGES_PER_STEP, n_pages)
    n_chunks = n_pages // ps
    rows = qbd.shape[1]
    chunk_keys = ps * PAGE_SIZE
    page_specs = [
        pl.BlockSpec((None, KV_W, PAGE_SIZE), lambda b, c, pt, r=r: (pt[b, ps * c + r], 0, 0))
        for r in range(ps)
    ]
    grid_spec = pltpu.PrefetchScalarGridSpec(
        num_scalar_prefetch=1,
        grid=(n_seq, n_chunks),
        in_specs=[
            pl.BlockSpec((None, rows, KV_W), lambda b, c, pt: (b, 0, 0)),
            pl.BlockSpec((None, n_new, chunk_keys), lambda b, c, pt: (b, 0, c)),
            pl.BlockSpec((None, n_new, LANES), lambda b, c, pt: (b, 0, n_pages)),
            pl.BlockSpec((n_new, KV_W), lambda b, c, pt: (b, 12)),
            pl.BlockSpec((n_new, KV_W), lambda b, c, pt: (b, 13)),
        ] + page_specs + page_specs,
        out_specs=pl.BlockSpec((None, rows, KV_W), lambda b, c, pt: (b, 0, 0)),
        scratch_shapes=[
            pltpu.VMEM((rows, 1), F32),
            pltpu.VMEM((rows, 1), F32),
            pltpu.VMEM((rows, KV_W), F32),
        ],
    )
    return pl.pallas_call(
        functools.partial(_sattn_kernel, n_chunks=n_chunks, n_new=n_new, ps=ps),
        grid_spec=grid_spec,
        out_shape=jax.ShapeDtypeStruct((n_seq, rows, KV_W), F32),
        compiler_params=pltpu.CompilerParams(
            dimension_semantics=("arbitrary", "arbitrary"), vmem_limit_bytes=VMEM_LIMIT),
        name="sample_attend",
    )(page_table, qbd, bias, bias, proj, proj, *([cache_k_t] * ps), *([cache_v_t] * ps))


def _merge_kernel(x_ref, rnn_ref, att_ref, gr_ref, ga_ref, wr_ref, wa_ref, wo_ref, o_ref, mix_ref):
    n = pl.program_id(1)

    @pl.when(n == 0)
    def _():
        mixed = (gr_ref[...] * _dot(rnn_ref[...].astype(BF16), wr_ref[...])
                 + ga_ref[...] * _dot(att_ref[...].astype(BF16), wa_ref[...]))
        mix_ref[...] = mixed.astype(BF16)

    o_ref[...] = x_ref[...] + _dot(mix_ref[...], wo_ref[...])


def _merge(x, rnn, attn, proj, wr, wa, wo, *, tm, tn=512):
    t = x.shape[0]
    return pl.pallas_call(
        _merge_kernel,
        grid=(t // tm, D_MODEL // tn),
        in_specs=[
            pl.BlockSpec((tm, tn), lambda i, n: (i, n)),
            pl.BlockSpec((tm, D_RNN), lambda i, n: (i, 0)),
            pl.BlockSpec((tm, ATTN_W), lambda i, n: (i, 0)),
            pl.BlockSpec((tm, D_MODEL), lambda i, n: (i, 2)),
            pl.BlockSpec((tm, D_MODEL), lambda i, n: (i, 3)),
            pl.BlockSpec((D_RNN, D_MODEL), lambda i, n: (0, 0)),
            pl.BlockSpec((ATTN_W, D_MODEL), lambda i, n: (0, 0)),
            pl.BlockSpec((D_MODEL, tn), lambda i, n: (0, n)),
        ],
        out_specs=pl.BlockSpec((tm, tn), lambda i, n: (i, n)),
        out_shape=jax.ShapeDtypeStruct((t, D_MODEL), F32),
        scratch_shapes=[pltpu.VMEM((tm, D_MODEL), BF16)],
        compiler_params=pltpu.CompilerParams(
            dimension_semantics=("arbitrary", "arbitrary"), vmem_limit_bytes=VMEM_LIMIT),
        name="merge",
    )(x, rnn, attn, proj, proj, wr, wa, wo)


def _route(h, rwh, rwl, rb):
    lane = lax.broadcasted_iota(jnp.int32, (1, LANES), 1)
    lanef = lane.astype(F32)
    hh, hl = _split_bf16(h)
    lg = (_dot(hh, rwh) + _dot(hl, rwh) + _dot(hh, rwl)) + rb
    is_g = (lane >= N_EXPERTS) & (lane < N_EXPERTS + N_GROUPS)
    gl = jnp.where(is_g, lg, -jnp.inf)
    gmax = jnp.max(gl, axis=1, keepdims=True)
    gprob = 1.0 / jnp.sum(jnp.exp(gl - gmax), axis=1, keepdims=True)
    gsel = jnp.min(jnp.where(is_g & (lg == gmax), lanef - N_EXPERTS, 1e9), axis=1, keepdims=True)
    in_grp = (lane < N_EXPERTS) & (jnp.floor(lanef * (1.0 / EXPERTS_PER_GROUP)) == gsel)
    v1 = jnp.where(in_grp, lg, -jnp.inf)
    t1 = jnp.max(v1, axis=1, keepdims=True)
    i1 = jnp.min(jnp.where(in_grp & (lg == t1), lanef, 1e9), axis=1, keepdims=True)
    rest = in_grp & (lanef != i1)
    v2 = jnp.where(rest, lg, -jnp.inf)
    t2 = jnp.max(v2, axis=1, keepdims=True)
    i2 = jnp.min(jnp.where(rest & (lg == t2), lanef, 1e9), axis=1, keepdims=True)
    d = jnp.exp(t2 - t1)
    return i1, i2, gprob / (1.0 + d), gprob * d / (1.0 + d)


def _moe_kernel(x_ref, ln_ref, rwh_ref, rwl_ref, rb_ref, w1_ref, w3_ref, w2_ref, o_ref, h_ref, gate_ref):
    e = pl.program_id(1)
    lane = lax.broadcasted_iota(jnp.int32, (1, LANES), 1)

    @pl.when(e == 0)
    def _():
        h = _rms(x_ref[...], ln_ref[...])
        h_ref[...] = h.astype(BF16)
        i1, i2, g1, g2 = _route(h, rwh_ref[...], rwl_ref[...], rb_ref[...])
        lanef = lane.astype(F32)
        gate_ref[...] = jnp.where(lanef == i1, g1, 0.0) + jnp.where(lanef == i2, g2, 0.0)

    ge = jnp.sum(jnp.where(lane == e, gate_ref[...], 0.0), axis=1, keepdims=True)
    up = _dot(h_ref[...], w1_ref[...].astype(BF16))
    hid = (up * _sigmoid(up)) * _dot(h_ref[...], w3_ref[...].astype(BF16))
    contrib = _dot((hid * ge).astype(BF16), w2_ref[...].astype(BF16))

    @pl.when(e == 0)
    def _():
        o_ref[...] = x_ref[...] + contrib

    @pl.when(e > 0)
    def _():
        o_ref[...] += contrib


def _moe(x, ln2, rw_hi, rw_lo, rb, w1, w3, w2, *, tm):
    t = x.shape[0]
    return pl.pallas_call(
        _moe_kernel,
        grid=(t // tm, N_EXPERTS),
        in_specs=[
            pl.BlockSpec((tm, D_MODEL), lambda i, e: (i, 0)),
            pl.BlockSpec((1, D_MODEL), lambda i, e: (0, 0)),
            pl.BlockSpec((D_MODEL, LANES), lambda i, e: (0, 0)),
            pl.BlockSpec((D_MODEL, LANES), lambda i, e: (0, 0)),
            pl.BlockSpec((1, LANES), lambda i, e: (0, 0)),
            pl.BlockSpec((None, D_MODEL, D_EXPERT), lambda i, e: (e, 0, 0)),
            pl.BlockSpec((None, D_MODEL, D_EXPERT), lambda i, e: (e, 0, 0)),
            pl.BlockSpec((None, D_EXPERT, D_MODEL), lambda i, e: (e, 0, 0)),
        ],
        out_specs=pl.BlockSpec((tm, D_MODEL), lambda i, e: (i, 0)),
        out_shape=jax.ShapeDtypeStruct((t, D_MODEL), F32),
        scratch_shapes=[pltpu.VMEM((tm, D_MODEL), BF16), pltpu.VMEM((tm, LANES), F32)],
        compiler_params=pltpu.CompilerParams(
            dimension_semantics=("arbitrary", "arbitrary"), vmem_limit_bytes=VMEM_LIMIT),
        name="moe",
    )(x, ln2, rw_hi, rw_lo, rb, w1, w3, w2)


def _ple_kernel(xf_ref, xt_ref, p_ref, ln_ref, wg_ref, wp_ref, o_ref, h_ref, pb_ref):
    n = pl.program_id(1)

    @pl.when(n == 0)
    def _():
        h_ref[...] = _rms(xf_ref[...], ln_ref[...]).astype(BF16)
        pb_ref[...] = p_ref[...].astype(BF16)

    gate = _sigmoid(_dot(h_ref[...], wg_ref[...]))
    o_ref[...] = xt_ref[...] + gate * _dot(pb_ref[...], wp_ref[...])


def _ple(x, p, ln3, wg, wp, *, tm, tn=512):
    t = x.shape[0]
    return pl.pallas_call(
        _ple_kernel,
        grid=(t // tm, D_MODEL // tn),
        in_specs=[
            pl.BlockSpec((tm, D_MODEL), lambda i, n: (i, 0)),
            pl.BlockSpec((tm, tn), lambda i, n: (i, n)),
            pl.BlockSpec((tm, PLE_DIM), lambda i, n: (i, 0)),
            pl.BlockSpec((1, D_MODEL), lambda i, n: (0, 0)),
            pl.BlockSpec((D_MODEL, tn), lambda i, n: (0, n)),
            pl.BlockSpec((PLE_DIM, tn), lambda i, n: (0, n)),
        ],
        out_specs=pl.BlockSpec((tm, tn), lambda i, n: (i, n)),
        out_shape=jax.ShapeDtypeStruct((t, D_MODEL), F32),
        scratch_shapes=[pltpu.VMEM((tm, D_MODEL), BF16), pltpu.VMEM((tm, PLE_DIM), BF16)],
        compiler_params=pltpu.CompilerParams(
            dimension_semantics=("arbitrary", "arbitrary"), vmem_limit_bytes=VMEM_LIMIT),
        name="ple",
    )(x, x, p, ln3, wg, wp)


MOE_ROW_TILE = 256
META_E, META_G, META_RANK = 0, 2, 4


def _router_kernel(x_ref, ln_ref, rwh_ref, rwl_ref, rb_ref, tri_ref, meta_ref, cnt_ref, carry_ref):
    i = pl.program_id(0)
    lane = lax.broadcasted_iota(jnp.int32, (1, LANES), 1)
    lanef = lane.astype(F32)

    @pl.when(i == 0)
    def _():
        carry_ref[...] = jnp.zeros(carry_ref.shape, F32)

    i1, i2, g1, g2 = _route(_rms(x_ref[...], ln_ref[...]), rwh_ref[...], rwl_ref[...], rb_ref[...])
    onehot = jnp.where((lanef == i1) | (lanef == i2), 1.0, 0.0)
    before = _dot(tri_ref[...], onehot.astype(BF16)) + carry_ref[...]
    r1 = jnp.sum(jnp.where(lanef == i1, before, 0.0), axis=1, keepdims=True)
    r2 = jnp.sum(jnp.where(lanef == i2, before, 0.0), axis=1, keepdims=True)
    carry_ref[...] += jnp.sum(onehot, axis=0, keepdims=True)
    rec = jnp.zeros((x_ref.shape[0], LANES), F32)
    for k, val in ((META_E, i1), (META_E + 1, i2), (META_G, g1), (META_G + 1, g2),
                   (META_RANK, r1), (META_RANK + 1, r2)):
        rec = jnp.where(lane == k, val, rec)
    meta_ref[...] = rec
    cnt_ref[...] = jnp.broadcast_to(carry_ref[...], cnt_ref.shape)


def _router(x, ln2, rw_hi, rw_lo, rb, *, tm):
    t = x.shape[0]
    r = lax.broadcasted_iota(jnp.int32, (tm, tm), 0)
    c = lax.broadcasted_iota(jnp.int32, (tm, tm), 1)
    tri = (c < r).astype(BF16)
    return pl.pallas_call(
        _router_kernel,
        grid=(t // tm,),
        in_specs=[
            pl.BlockSpec((tm, D_MODEL), lambda i: (i, 0)),
            pl.BlockSpec((1, D_MODEL), lambda i: (0, 0)),
            pl.BlockSpec((D_MODEL, LANES), lambda i: (0, 0)),
            pl.BlockSpec((D_MODEL, LANES), lambda i: (0, 0)),
            pl.BlockSpec((1, LANES), lambda i: (0, 0)),
            pl.BlockSpec((tm, tm), lambda i: (0, 0)),
        ],
        out_specs=[
            pl.BlockSpec((tm, LANES), lambda i: (i, 0)),
            pl.BlockSpec((8, LANES), lambda i: (0, 0)),
        ],
        out_shape=[
            jax.ShapeDtypeStruct((t, LANES), F32),
            jax.ShapeDtypeStruct((8, LANES), F32),
        ],
        scratch_shapes=[pltpu.VMEM((1, LANES), F32)],
        compiler_params=pltpu.CompilerParams(
            dimension_semantics=("arbitrary",), vmem_limit_bytes=VMEM_LIMIT),
        name="moe_router",
    )(x, ln2, rw_hi, rw_lo, rb, tri)


def _row_copy(src_ref, src_row, dst_ref, dst_row, sem):
    return pltpu.make_async_copy(src_ref.at[pl.ds(src_row, 1), :], dst_ref.at[pl.ds(dst_row, 1), :], sem)


def _scatter_kernel(slot_ref, x_ref, hs_in_ref, hs_ref, sem, *, n_tok):
    del hs_in_ref
    tm = x_ref.shape[0]
    base = pl.program_id(0) * tm

    def start(r, carry):
        for k in range(2):
            _row_copy(x_ref, r, hs_ref, slot_ref[k * n_tok + base + r], sem).start()
        return carry

    def wait(r, carry):
        for k in range(2):
            _row_copy(x_ref, r, hs_ref, slot_ref[k * n_tok + base + r], sem).wait()
        return carry

    lax.fori_loop(0, tm, start, 0)
    lax.fori_loop(0, tm, wait, 0)


def _scatter_rows(slots, x, hs_zero, *, tm):
    t = x.shape[0]
    grid_spec = pltpu.PrefetchScalarGridSpec(
        num_scalar_prefetch=1,
        grid=(t // tm,),
        in_specs=[
            pl.BlockSpec((tm, D_MODEL), lambda i, s: (i, 0)),
            pl.BlockSpec(memory_space=pl.ANY),
        ],
        out_specs=pl.BlockSpec(memory_space=pl.ANY),
        scratch_shapes=[pltpu.SemaphoreType.DMA(())],
    )
    return pl.pallas_call(
        functools.partial(_scatter_kernel, n_tok=t),
        grid_spec=grid_spec,
        out_shape=jax.ShapeDtypeStruct(hs_zero.shape, F32),
        input_output_aliases={2: 0},
        compiler_params=pltpu.CompilerParams(
            dimension_semantics=("arbitrary",), vmem_limit_bytes=VMEM_LIMIT),
        name="moe_scatter",
    )(slots, x, hs_zero)


def _expert_kernel(te_ref, nu_ref, hs_ref, ln_ref, w1_ref, w3_ref, w2_ref, y_ref):
    i = pl.program_id(0)

    @pl.when(i < nu_ref[0])
    def _():
        h = _rms(hs_ref[...], ln_ref[...]).astype(BF16)
        up = _dot(h, w1_ref[...].astype(BF16))
        hid = (up * _sigmoid(up)) * _dot(h, w3_ref[...].astype(BF16))
        y_ref[...] = _dot(hid.astype(BF16), w2_ref[...].astype(BF16))

    @pl.when(i >= nu_ref[0])
    def _():
        y_ref[...] = jnp.zeros(y_ref.shape, F32)


def _expert_mlp(tile_expert, n_used, hs, ln2, w1, w3, w2):
    n_tiles = tile_expert.shape[0]
    grid_spec = pltpu.PrefetchScalarGridSpec(
        num_scalar_prefetch=2,
        grid=(n_tiles,),
        in_specs=[
            pl.BlockSpec((MOE_ROW_TILE, D_MODEL), lambda i, te, nu: (i, 0)),
            pl.BlockSpec((1, D_MODEL), lambda i, te, nu: (0, 0)),
            pl.BlockSpec((None, D_MODEL, D_EXPERT), lambda i, te, nu: (te[i], 0, 0)),
            pl.BlockSpec((None, D_MODEL, D_EXPERT), lambda i, te, nu: (te[i], 0, 0)),
            pl.BlockSpec((None, D_EXPERT, D_MODEL), lambda i, te, nu: (te[i], 0, 0)),
        ],
        out_specs=pl.BlockSpec((MOE_ROW_TILE, D_MODEL), lambda i, te, nu: (i, 0)),
    )
    return pl.pallas_call(
        _expert_kernel,
        grid_spec=grid_spec,
        out_shape=jax.ShapeDtypeStruct(hs.shape, F32),
        compiler_params=pltpu.CompilerParams(
            dimension_semantics=("arbitrary",), vmem_limit_bytes=VMEM_LIMIT),
        name="moe_experts",
    )(tile_expert, n_used, hs, ln2, w1, w3, w2)


def _combine_ple_kernel(slot_ref, x_ref, meta_ref, p_ref, ln_ref, wg_ref, wp_ref, y_ref, o_ref,
                        y1_ref, y2_ref, x2_ref, h_ref, pb_ref, sem, *, n_tok, tn):
    i = pl.program_id(0)
    n = pl.program_id(1)
    tm = x_ref.shape[0]

    @pl.when(n == 0)
    def _():
        base = i * tm
        bufs = (y1_ref, y2_ref)

        def start(r, carry):
            for k in range(2):
                _row_copy(y_ref, slot_ref[k * n_tok + base + r], bufs[k], r, sem).start()
            return carry

        def wait(r, carry):
            for k in range(2):
                _row_copy(y_ref, slot_ref[k * n_tok + base + r], bufs[k], r, sem).wait()
            return carry

        lax.fori_loop(0, tm, start, 0)
        lax.fori_loop(0, tm, wait, 0)
        meta = meta_ref[...]
        x2 = (x_ref[...] + meta[:, META_G:META_G + 1] * y1_ref[...]
              + meta[:, META_G + 1:META_G + 2] * y2_ref[...])
        for c in range(D_MODEL // tn):
            x2_ref[c] = x2[:, c * tn:(c + 1) * tn]
        h_ref[...] = _rms(x2, ln_ref[...]).astype(BF16)
        pb_ref[...] = p_ref[...].astype(BF16)

    gate = _sigmoid(_dot(h_ref[...], wg_ref[...]))
    o_ref[...] = x2_ref[n] + gate * _dot(pb_ref[...], wp_ref[...])


def _combine_ple(slots, x, meta, p, ln3, wg, wp, y, *, tm, tn=512):
    t = x.shape[0]
    grid_spec = pltpu.PrefetchScalarGridSpec(
        num_scalar_prefetch=1,
        grid=(t // tm, D_MODEL // tn),
        in_specs=[
            pl.BlockSpec((tm, D_MODEL), lambda i, n, s: (i, 0)),
            pl.BlockSpec((tm, LANES), lambda i, n, s: (i, 0)),
            pl.BlockSpec((tm, PLE_DIM), lambda i, n, s: (i, 0)),
            pl.BlockSpec((1, D_MODEL), lambda i, n, s: (0, 0)),
            pl.BlockSpec((D_MODEL, tn), lambda i, n, s: (0, n)),
            pl.BlockSpec((PLE_DIM, tn), lambda i, n, s: (0, n)),
            pl.BlockSpec(memory_space=pl.ANY),
        ],
        out_specs=pl.BlockSpec((tm, tn), lambda i, n, s: (i, n)),
        scratch_shapes=[
            pltpu.VMEM((tm, D_MODEL), F32),
            pltpu.VMEM((tm, D_MODEL), F32),
            pltpu.VMEM((D_MODEL // tn, tm, tn), F32),
            pltpu.VMEM((tm, D_MODEL), BF16),
            pltpu.VMEM((tm, PLE_DIM), BF16),
            pltpu.SemaphoreType.DMA(()),
        ],
    )
    return pl.pallas_call(
        functools.partial(_combine_ple_kernel, n_tok=t, tn=tn),
        grid_spec=grid_spec,
        out_shape=jax.ShapeDtypeStruct((t, D_MODEL), F32),
        compiler_params=pltpu.CompilerParams(
            dimension_semantics=("arbitrary", "arbitrary"), vmem_limit_bytes=VMEM_LIMIT),
        name="moe_combine_ple",
    )(slots, x, meta, p, ln3, wg, wp, y)


def _sparse_moe_ple(x, p, w):
    t = x.shape[0]
    rt = MOE_ROW_TILE
    meta, cnt = _router(x, w["ln2"], w["rw_hi"], w["rw_lo"], w["rb"], tm=256)
    counts = cnt[0, :N_EXPERTS].astype(jnp.int32)
    padded = ((counts + rt - 1) // rt) * rt
    ends = jnp.cumsum(padded)
    offs = ends - padded
    eid = meta[:, META_E:META_E + 2].astype(jnp.int32)
    rank = meta[:, META_RANK:META_RANK + 2].astype(jnp.int32)
    slots = jnp.transpose(offs[eid] + rank).reshape(2 * t)
    n_tiles = (2 * t + N_EXPERTS * (rt - 1)) // rt
    tile_start = jnp.arange(n_tiles, dtype=jnp.int32) * rt
    tile_expert = jnp.minimum(jnp.sum((tile_start[:, None] >= ends[None, :]).astype(jnp.int32), axis=1),
                              N_EXPERTS - 1)
    n_used = (ends[N_EXPERTS - 1] // rt).reshape(1)
    hs = _scatter_rows(slots, x, jnp.zeros((n_tiles * rt, D_MODEL), F32), tm=256)
    y = _expert_mlp(tile_expert, n_used, hs, w["ln2"], w["w1"], w["w3"], w["w2"])
    return _combine_ple(slots, x, meta, p, w["ln3"], w["wg"], w["wp"], y, tm=256)


def _rope_tables(pos):
    half = HEAD_DIM // 2
    inv = ROPE_THETA ** (-jnp.arange(half, dtype=F32) / half)
    ang = pos.astype(F32)[:, None] * inv[None, :]
    cos, sin = jnp.cos(ang), jnp.sin(ang)
    return (jnp.concatenate([cos, cos, cos, cos], axis=1),
            jnp.concatenate([-sin, sin, -sin, sin], axis=1))


def _block_diag(w, per):
    n, r, _ = w.shape
    eye = jnp.eye(per, dtype=w.dtype)
    wg = w.reshape(n // per, per, r, r)
    return jnp.einsum("gpij,pq->gpiqj", wg, eye).reshape(n // per, per * r, per * r)


def _layer_weights(ln1, w_in, q_norm, k_norm, conv_w, conv_b, w_a, b_a, w_x, b_x, lam, w_br_rnn,
                   w_br_attn, w_out, ln2, w_rg, b_rg, w_re, b_re, w1, w3, w2, ln3, w_ple_gate,
                   w_ple_proj):
    o_q = 2 * D_RNN
    o_k = o_q + ATTN_W
    o_v = o_k + KV_W
    o_qi = o_v + KV_W
    o_ki = o_qi + N_IDX_HEADS * IDX_DIM
    o_wi = o_ki + IDX_DIM
    o_gr = o_wi + N_IDX_HEADS
    o_ga = o_gr + D_MODEL
    w_main = jnp.concatenate([w_in[:, :o_ki], w_in[:, o_gr:o_ga + D_MODEL]], axis=1).astype(BF16)
    w_small = jnp.concatenate(
        [w_in[:, o_ki:o_gr], jnp.zeros((D_MODEL, LANES - IDX_DIM - N_IDX_HEADS), F32)], axis=1).astype(BF16)
    ones = lambda n: jnp.ones((n,), F32)
    zeros = lambda n: jnp.zeros((n,), F32)
    n_gate = 2 * D_MODEL
    gain = jnp.concatenate([ones(o_q), jnp.tile(q_norm, N_HEADS), jnp.tile(k_norm, N_KV_HEADS),
                            ones(KV_W + N_IDX_HEADS * IDX_DIM + n_gate)])
    norm_on = jnp.concatenate([zeros(o_q), ones(ATTN_W + KV_W), zeros(KV_W + N_IDX_HEADS * IDX_DIM + n_gate)])
    rope_on = jnp.concatenate([zeros(o_q), ones(ATTN_W + KV_W), zeros(KV_W), ones(N_IDX_HEADS * IDX_DIM),
                               zeros(n_gate)])
    post = jnp.concatenate([ones(o_q), jnp.full((ATTN_W,), QK_SCALE, F32),
                            ones(2 * KV_W + N_IDX_HEADS * IDX_DIM + n_gate)])
    colctl = jnp.concatenate([jnp.stack([gain, norm_on, rope_on, post]), jnp.zeros((4, N_MAIN), F32)], axis=0)
    tn = 512
    head_of = jnp.arange(tn) // HEAD_DIM
    bd = (head_of[:, None] == head_of[None, :]).astype(BF16)
    rw = jnp.concatenate([w_re, w_rg, jnp.zeros((D_MODEL, LANES - N_EXPERTS - N_GROUPS), F32)], axis=1)
    rw_hi = rw.astype(BF16)
    rw_lo = (rw - rw_hi.astype(F32)).astype(BF16)
    rb = jnp.concatenate([b_re, b_rg, jnp.zeros((LANES - N_EXPERTS - N_GROUPS,), F32)])[None, :]
    return dict(
        ln1=ln1[None, :], w_main=w_main, w_small=w_small, colctl=colctl, bd=bd,
        cw=conv_w, cb=conv_b[None, :],
        wa_bd=_block_diag(w_a, 4).astype(BF16), ba=b_a[None, :],
        wx_bd=_block_diag(w_x, 4).astype(BF16), bx=b_x[None, :], lam=lam[None, :],
        wr=w_br_rnn.astype(BF16), wa=w_br_attn.astype(BF16), wo=w_out.astype(BF16),
        ln2=ln2[None, :], rw_hi=rw_hi, rw_lo=rw_lo, rb=rb,
        w1=w1, w3=w3, w2=w2,
        ln3=ln3[None, :], wg=w_ple_gate.astype(BF16), wp=w_ple_proj.astype(BF16),
    )


def _tail(x, rnn, attn, proj, p, w):
    t = x.shape[0]
    x1 = _merge(x, rnn, attn, proj, w["wr"], w["wa"], w["wo"], tm=min(t, 256))
    if 2 * t >= N_EXPERTS * MOE_ROW_TILE:
        return _sparse_moe_ple(x1, p, w)
    x2 = _moe(x1, w["ln2"], w["rw_hi"], w["rw_lo"], w["rb"], w["w1"], w["w3"], w["w2"], tm=min(t, 512))
    return _ple(x2, p, w["ln3"], w["wg"], w["wp"], tm=min(t, 512))


def _prompt_layer(x, p, w):
    bp, tp, _ = x.shape
    xt = x.reshape(bp * tp, D_MODEL)
    cs, sn = _rope_tables(jnp.tile(jnp.arange(tp, dtype=jnp.int32), bp))
    proj, small = _inproj(xt, w["ln1"], w["w_main"], w["w_small"], w["colctl"], cs, sn, w["bd"],
                          tm=min(bp * tp, 1024))
    conv0 = jnp.zeros((bp, 8, D_RNN), F32)
    h0 = jnp.zeros((bp, 1, D_RNN), F32)
    rnn, h_last = _rglru(proj, conv0, h0, w["cw"], w["cb"], w["wa_bd"], w["ba"], w["wx_bd"], w["bx"],
                         w["lam"], n_seq=bp, tt=min(tp, 256))
    attn = _prompt_attention(proj, small, n_batch=bp, seq=tp)
    y = _tail(xt, rnn, attn, proj, p.reshape(bp * tp, PLE_DIM), w)
    o_k = 2 * D_RNN + ATTN_W
    k = proj[:, o_k:o_k + KV_W].reshape(bp, tp, N_KV_HEADS, HEAD_DIM)
    v = proj[:, o_k + KV_W:o_k + 2 * KV_W].reshape(bp, tp, N_KV_HEADS, HEAD_DIM)
    ki = small[:, :IDX_DIM].reshape(bp, tp, IDX_DIM)
    conv_new = proj.reshape(bp, tp, N_MAIN)[:, tp - (CONV_W - 1):, :D_RNN]
    return y.reshape(bp, tp, D_MODEL), (k, v, ki, conv_new, h_last.reshape(bp, D_RNN))


def _sample_layer(x, p, cache_k, cache_v, cache_kidx, state_conv, state_h, page_table, w):
    bs, ts, _ = x.shape
    n_pages = page_table.shape[1]
    past = n_pages * PAGE_SIZE
    xt = x.reshape(bs * ts, D_MODEL)
    cs, sn = _rope_tables(past + jnp.tile(jnp.arange(ts, dtype=jnp.int32), bs))
    proj, small = _inproj(xt, w["ln1"], w["w_main"], w["w_small"], w["colctl"], cs, sn, w["bd"], tm=bs * ts)
    conv0 = jnp.concatenate([jnp.zeros((bs, 8 - (CONV_W - 1), D_RNN), F32), state_conv], axis=1)
    rnn, h_last = _rglru(proj, conv0, state_h[:, None, :], w["cw"], w["cb"], w["wa_bd"], w["ba"],
                         w["wx_bd"], w["bx"], w["lam"], n_seq=bs, tt=ts)
    o_q = 2 * D_RNN
    o_qi = o_q + ATTN_W + 2 * KV_W
    qi = proj[:, o_qi:o_qi + N_IDX_HEADS * IDX_DIM].reshape(bs, ts, N_IDX_HEADS, IDX_DIM)
    qst = jnp.transpose(qi, (0, 2, 1, 3)).reshape(bs, N_IDX_HEADS * ts, IDX_DIM).astype(BF16)
    wi = small[:, IDX_DIM:IDX_DIM + N_IDX_HEADS].reshape(bs, ts, N_IDX_HEADS)
    wcol = jnp.transpose(wi, (0, 2, 1)).reshape(bs, N_IDX_HEADS * ts, 1)
    n_pool = cache_k.shape[0]
    kidx_t = jnp.transpose(cache_kidx, (0, 2, 1))
    k_t = jnp.transpose(cache_k, (0, 2, 3, 1)).reshape(n_pool, KV_W, PAGE_SIZE)
    v_t = jnp.transpose(cache_v, (0, 2, 3, 1)).reshape(n_pool, KV_W, PAGE_SIZE)
    bias = _sample_select(page_table, qst, wcol, small, kidx_t, n_new=ts)
    q = proj[:, o_q:o_q + ATTN_W].reshape(bs, ts, N_KV_HEADS, N_HEADS // N_KV_HEADS, HEAD_DIM)
    eye = jnp.eye(N_KV_HEADS, dtype=F32)
    qbd = jnp.einsum("btgjd,gk->bgjtkd", q, eye).reshape(bs, N_HEADS * ts, KV_W).astype(BF16)
    att = _sample_attend(page_table, qbd, bias, proj, k_t, v_t, n_new=ts)
    att = att.reshape(bs, N_KV_HEADS, N_HEADS // N_KV_HEADS, ts, N_KV_HEADS, HEAD_DIM)
    att = jnp.stack([att[:, g, :, :, g, :] for g in range(N_KV_HEADS)], axis=1)
    attn = jnp.transpose(att, (0, 3, 1, 2, 4)).reshape(bs * ts, ATTN_W)
    y = _tail(xt, rnn, attn, proj, p.reshape(bs * ts, PLE_DIM), w)
    o_k = o_q + ATTN_W
    k = proj[:, o_k:o_k + KV_W].reshape(bs, ts, N_KV_HEADS, HEAD_DIM)
    v = proj[:, o_k + KV_W:o_k + 2 * KV_W].reshape(bs, ts, N_KV_HEADS, HEAD_DIM)
    ki = small[:, :IDX_DIM].reshape(bs, ts, IDX_DIM)
    conv_new = proj.reshape(bs, ts, N_MAIN)[:, ts - (CONV_W - 1):, :D_RNN]
    return y.reshape(bs, ts, D_MODEL), (k, v, ki, conv_new, h_last.reshape(bs, D_RNN))


def kernel(x_prompt, x_sample, p_prompt, p_sample, cache_k, cache_v, cache_kidx, state_conv, state_h,
           page_table, ln1, w_in, q_norm, k_norm, conv_w, conv_b, w_a, b_a, w_x, b_x, lam, w_br_rnn,
           w_br_attn, w_out, ln2, w_rg, b_rg, w_re, b_re, w1, w3, w2, ln3, w_ple_gate, w_ple_proj):
    weights = (ln1, w_in, q_norm, k_norm, conv_w, conv_b, w_a, b_a, w_x, b_x, lam, w_br_rnn, w_br_attn,
               w_out, ln2, w_rg, b_rg, w_re, b_re, w1, w3, w2, ln3, w_ple_gate, w_ple_proj)
    depth = ln1.shape[0]
    yp, ys = x_prompt, x_sample
    st_p, st_s = [], []
    for i in range(depth):
        w = _layer_weights(*[wt[i] for wt in weights])
        yp, sp = _prompt_layer(yp, p_prompt[i], w)
        ys, ss = _sample_layer(ys, p_sample[i], cache_k[i], cache_v[i], cache_kidx[i], state_conv[i],
                               state_h[i], page_table, w)
        st_p.append(sp)
        st_s.append(ss)
    stack = lambda sts, j: jnp.stack([s[j] for s in sts])
    return (yp, ys, stack(st_p, 0), stack(st_p, 1), stack(st_p, 2), stack(st_p, 3), stack(st_p, 4),
            stack(st_s, 0), stack(st_s, 1), stack(st_s, 2), stack(st_s, 3), stack(st_s, 4))
```

```python
import functools

import jax
import jax.numpy as jnp
import numpy as np
from jax import lax
from jax.experimental import pallas as pl
from jax.experimental.pallas import tpu as pltpu

F32 = jnp.float32
BF16 = jnp.bfloat16

D_MODEL = 2048
HEAD_DIM = 64
N_HEADS = 16
N_KV_HEADS = 4
ATTN_W = N_HEADS * HEAD_DIM
KV_W = N_KV_HEADS * HEAD_DIM
N_IDX_HEADS = 8
IDX_DIM = 64
TOPK_MAX = 256
ROPE_THETA = 10000.0
D_RNN = 1024
N_RNN_BLOCKS = 16
RNN_BLOCK = 64
CONV_W = 4
LRU_C = 8.0
N_GROUPS = 4
EXPERTS_PER_GROUP = 8
N_EXPERTS = 32
D_EXPERT = 256
PLE_DIM = 256
PAGE_SIZE = 128
EPS = 1e-6

LANES = 128
N_MAIN = 8192
COL_RAW_END = 2 * D_RNN
COL_NR_END = 4096
IDX_SCALE = (IDX_DIM ** -0.5) * (N_IDX_HEADS ** -0.5)
QK_SCALE = HEAD_DIM ** -0.5
F32_MIN = float(np.finfo(np.float32).min)
INT_MIN = -2147483648
KEY_NEG_INF = INT_MIN + 0x7FFFFF
VMEM_LIMIT = 56 * 1024 * 1024


def _dot(a, b):
    return jnp.dot(a, b, preferred_element_type=F32)


def _dot_nt(a, b):
    return lax.dot_general(a, b, (((1,), (1,)), ((), ())), preferred_element_type=F32)


def _sigmoid(x):
    return 1.0 / (1.0 + jnp.exp(-x))


def _rms(x, g):
    return x * lax.rsqrt(jnp.mean(x * x, axis=-1, keepdims=True) + EPS) * g


def _split_bf16(x):
    hi = x.astype(BF16)
    lo = (x - hi.astype(F32)).astype(BF16)
    return hi, lo


def _rope_chunks(y, c, s):
    lane = lax.broadcasted_iota(jnp.int32, (1, LANES), 1)
    first_half = (lane % HEAD_DIM) < (HEAD_DIM // 2)
    outs = []
    for k in range(y.shape[1] // LANES):
        yc = y[:, k * LANES:(k + 1) * LANES]
        partner = jnp.where(first_half, pltpu.roll(yc, LANES - HEAD_DIM // 2, 1),
                            pltpu.roll(yc, HEAD_DIM // 2, 1))
        outs.append(yc * c + partner * s)
    return outs[0] if len(outs) == 1 else jnp.concatenate(outs, axis=1)


def _inproj_kernel(x_ref, ln_ref, w_ref, ws_ref, ctl_ref, cs_ref, sn_ref, bd_ref,
                   o_ref, os_ref, h_ref, *, tn):
    j = pl.program_id(1)

    @pl.when(j == 0)
    def _():
        hb = _rms(x_ref[...], ln_ref[...]).astype(BF16)
        h_ref[...] = hb
        ys = _dot(hb, ws_ref[...])
        lane = lax.broadcasted_iota(jnp.int32, (1, LANES), 1)
        os_ref[...] = jnp.where(lane < IDX_DIM, _rope_chunks(ys, cs_ref[...], sn_ref[...]), ys)

    o_ref[...] = _dot(h_ref[...], w_ref[...])

    @pl.when((j >= COL_RAW_END // tn) & (j < COL_NR_END // tn))
    def _():
        y = o_ref[...]
        ctl = ctl_ref[...]
        gain, norm_on, rope_on, post = ctl[0:1], ctl[1:2], ctl[2:3], ctl[3:4]
        hi, lo = _split_bf16(y * y)
        ss = _dot(hi, bd_ref[...]) + _dot(lo, bd_ref[...])
        yn = jnp.where(norm_on > 0.0, y * lax.rsqrt(ss * (1.0 / HEAD_DIM) + EPS) * gain, y)
        yr = jnp.where(rope_on > 0.0, _rope_chunks(yn, cs_ref[...], sn_ref[...]), yn)
        o_ref[...] = yr * post


def _inproj(x, ln1, w_main, w_small, colctl, cs, sn, bd, *, tm, tn=512):
    t = x.shape[0]
    grid = (t // tm, N_MAIN // tn)
    return pl.pallas_call(
        functools.partial(_inproj_kernel, tn=tn),
        grid=grid,
        in_specs=[
            pl.BlockSpec((tm, D_MODEL), lambda i, j: (i, 0)),
            pl.BlockSpec((1, D_MODEL), lambda i, j: (0, 0)),
            pl.BlockSpec((D_MODEL, tn), lambda i, j: (0, j)),
            pl.BlockSpec((D_MODEL, LANES), lambda i, j: (0, 0)),
            pl.BlockSpec((8, tn), lambda i, j: (0, j)),
            pl.BlockSpec((tm, LANES), lambda i, j: (i, 0)),
            pl.BlockSpec((tm, LANES), lambda i, j: (i, 0)),
            pl.BlockSpec((tn, tn), lambda i, j: (0, 0)),
        ],
        out_specs=[
            pl.BlockSpec((tm, tn), lambda i, j: (i, j)),
            pl.BlockSpec((tm, LANES), lambda i, j: (i, 0)),
        ],
        out_shape=[
            jax.ShapeDtypeStruct((t, N_MAIN), F32),
            jax.ShapeDtypeStruct((t, LANES), F32),
        ],
        scratch_shapes=[pltpu.VMEM((tm, D_MODEL), BF16)],
        compiler_params=pltpu.CompilerParams(
            dimension_semantics=("arbitrary", "arbitrary"), vmem_limit_bytes=VMEM_LIMIT),
        name="inproj",
    )(x, ln1, w_main, w_small, colctl, cs, sn, bd)


def _rglru_kernel(x_ref, g_ref, c0_ref, h0_ref, cw_ref, cb_ref, wa_ref, ba_ref, wx_ref, bx_ref,
                  lam_ref, o_ref, hl_ref, xs_ref, a_ref, b_ref, hc_ref):
    t = pl.program_id(1)
    tt = x_ref.shape[0]

    @pl.when(t == 0)
    def _():
        xs_ref[0:8, :] = c0_ref[...]
        hc_ref[...] = h0_ref[...]

    xs_ref[8:8 + tt, :] = x_ref[...]
    cw = cw_ref[...]
    taps = (xs_ref[5:5 + tt, :] * cw[0:1] + xs_ref[6:6 + tt, :] * cw[1:2]
            + xs_ref[7:7 + tt, :] * cw[2:3] + xs_ref[8:8 + tt, :] * cw[3:4])
    xc = cb_ref[...] + taps
    xs_ref[0:8, :] = xs_ref[tt:tt + 8, :]

    xcb = xc.astype(BF16)
    ra, ri = [], []
    for c in range(wa_ref.shape[0]):
        blk = xcb[:, c * 256:(c + 1) * 256]
        ra.append(_dot(blk, wa_ref[c]))
        ri.append(_dot(blk, wx_ref[c]))
    r = _sigmoid(jnp.concatenate(ra, axis=1) + ba_ref[...])
    ig = _sigmoid(jnp.concatenate(ri, axis=1) + bx_ref[...])
    nlam = -lam_ref[...]
    softplus = jnp.maximum(nlam, 0.0) + jnp.log1p(jnp.exp(-jnp.abs(nlam)))
    log_a = (-LRU_C) * r * softplus
    a = jnp.exp(log_a)
    u = jnp.sqrt(jnp.tanh(-log_a) * (a * a + 1.0)) * (ig * xc)

    n8 = tt // 8
    a3 = a.reshape(n8, 8, D_RNN)
    b3 = u.reshape(n8, 8, D_RNN)
    sub = lax.broadcasted_iota(jnp.int32, (1, 8, 1), 1)
    for s in (1, 2, 4):
        a_prev = pltpu.roll(a3, s, 1)
        b_prev = pltpu.roll(b3, s, 1)
        m = sub >= s
        b3 = jnp.where(m, a3 * b_prev + b3, b3)
        a3 = jnp.where(m, a3 * a_prev, a3)
    a_ref[...] = a3.reshape(tt, D_RNN)
    b_ref[...] = b3.reshape(tt, D_RNN)

    def chain(k, carry):
        i0 = pl.multiple_of(k * 8, 8)
        h8 = a_ref[pl.ds(i0, 8), :] * carry + b_ref[pl.ds(i0, 8), :]
        b_ref[pl.ds(i0, 8), :] = h8
        return h8[7:8, :]

    carry = lax.fori_loop(0, n8, chain, hc_ref[...])
    hc_ref[...] = carry
    g = g_ref[...]
    gelu = 0.5 * g * (1.0 + jnp.tanh(0.7978845608028654 * (g + 0.044715 * (g * g * g))))
    o_ref[...] = b_ref[...] * gelu

    @pl.when(t == pl.num_programs(1) - 1)
    def _():
        hl_ref[...] = carry


def _rglru(proj, conv0, h0, cw, cb, wa_bd, ba, wx_bd, bx, lam, *, n_seq, tt):
    t_total = proj.shape[0]
    nt = t_total // (n_seq * tt)
    full = lambda shape: pl.BlockSpec(shape, lambda b, t: (0,) * len(shape))
    return pl.pallas_call(
        _rglru_kernel,
        grid=(n_seq, nt),
        in_specs=[
            pl.BlockSpec((tt, D_RNN), lambda b, t: (b * nt + t, 0)),
            pl.BlockSpec((tt, D_RNN), lambda b, t: (b * nt + t, 1)),
            pl.BlockSpec((None, 8, D_RNN), lambda b, t: (b, 0, 0)),
            pl.BlockSpec((None, 1, D_RNN), lambda b, t: (b, 0, 0)),
            full((CONV_W, D_RNN)), full((1, D_RNN)),
            full(wa_bd.shape), full((1, D_RNN)),
            full(wx_bd.shape), full((1, D_RNN)),
            full((1, D_RNN)),
        ],
        out_specs=[
            pl.BlockSpec((tt, D_RNN), lambda b, t: (b * nt + t, 0)),
            pl.BlockSpec((None, 1, D_RNN), lambda b, t: (b, 0, 0)),
        ],
        out_shape=[
            jax.ShapeDtypeStruct((t_total, D_RNN), F32),
            jax.ShapeDtypeStruct((n_seq, 1, D_RNN), F32),
        ],
        scratch_shapes=[
            pltpu.VMEM((tt + 8, D_RNN), F32),
            pltpu.VMEM((tt, D_RNN), F32),
            pltpu.VMEM((tt, D_RNN), F32),
            pltpu.VMEM((1, D_RNN), F32),
        ],
        compiler_params=pltpu.CompilerParams(
            dimension_semantics=("arbitrary", "arbitrary"), vmem_limit_bytes=VMEM_LIMIT),
        name="rglru",
    )(proj, proj, conv0, h0, cw, cb, wa_bd, ba, wx_bd, bx, lam)


def _select_topk(s, kk):
    rows, n = s.shape
    kkf = float(kk)

    def key_to_f32(w):
        k = w ^ INT_MIN
        bits = jnp.where(k >= 0, k, k ^ 0x7FFFFFFF)
        return k, lax.bitcast_convert_type(bits, F32)

    def vbody(it, w):
        cand_w = w | jnp.left_shift(jnp.int32(1), 31 - it)
        cand_k, cand_f = key_to_f32(cand_w)
        cnt = jnp.sum(jnp.where(s >= cand_f, 1.0, 0.0), axis=1, keepdims=True)
        ok = (cnt >= kkf) | (cand_k < KEY_NEG_INF)
        return jnp.where(ok, cand_w, w)

    w = lax.fori_loop(0, 32, vbody, jnp.zeros((rows, 1), jnp.int32))
    _, thr = key_to_f32(w)
    gt = s > thr
    eq = s == thr
    need = kkf - jnp.sum(jnp.where(gt, 1.0, 0.0), axis=1, keepdims=True)
    col = lax.broadcasted_iota(jnp.int32, (1, n), 1)
    nbits = int(n).bit_length()

    def jbody(it, jmax):
        cand = jmax | jnp.left_shift(jnp.int32(1), nbits - 1 - it)
        cnt = jnp.sum(jnp.where(eq & (col < cand), 1.0, 0.0), axis=1, keepdims=True)
        return jnp.where(cnt <= need, cand, jmax)

    n_ge = jnp.sum(jnp.where(s >= thr, 1.0, 0.0), axis=1, keepdims=True)
    jmax = lax.cond(
        jnp.max(n_ge) > kkf,
        lambda: lax.fori_loop(0, nbits, jbody, jnp.zeros((rows, 1), jnp.int32)),
        lambda: jnp.full((rows, 1), (1 << nbits) - 1, jnp.int32))
    return gt | (eq & (col < jmax))


def _pattn_kernel(q_ref, qi_ref, sm_ref, k_ref, v_ref, ksm_ref, o_ref,
                  kb_ref, vb_ref, kib_ref, s_ref, *, i0, n_keys, kc, topk):
    i = pl.program_id(1)
    tq = q_ref.shape[0]
    lane = lax.broadcasted_iota(jnp.int32, (1, LANES), 1)

    @pl.when(i == 0)
    def _():
        kb_ref[...] = k_ref[0:n_keys, :].astype(BF16)
        vb_ref[...] = v_ref[0:n_keys, :].astype(BF16)
        kib_ref[...] = jnp.where(lane < IDX_DIM, ksm_ref[0:n_keys, :], 0.0).astype(BF16)

    sm = sm_ref[...]
    qi = qi_ref[...]
    qrows, wrows = [], []
    for h in range(N_IDX_HEADS):
        blk = qi[:, (h // 2) * LANES:(h // 2 + 1) * LANES]
        if h % 2 == 1:
            blk = pltpu.roll(blk, IDX_DIM, 1)
        qrows.append(jnp.where(lane < IDX_DIM, blk, 0.0))
        wrows.append(sm[:, IDX_DIM + h:IDX_DIM + h + 1])
    qst = jnp.concatenate(qrows, axis=0).astype(BF16)
    wst = jnp.concatenate(wrows, axis=0) * IDX_SCALE
    qpos = (i0 + i) * tq + lax.broadcasted_iota(jnp.int32, (tq, 1), 0)
    for c in range(n_keys // kc):
        s = jnp.maximum(_dot_nt(qst, kib_ref[c * kc:(c + 1) * kc, :]), 0.0) * wst
        sc = s[0:tq]
        for h in range(1, N_IDX_HEADS):
            sc = sc + s[h * tq:(h + 1) * tq]
        col = c * kc + lax.broadcasted_iota(jnp.int32, (1, kc), 1)
        s_ref[:, c * kc:(c + 1) * kc] = jnp.where(col <= qpos, sc, F32_MIN)

    sel = _select_topk(s_ref[...], topk)
    colf = lax.broadcasted_iota(jnp.int32, (1, n_keys), 1)
    s_ref[...] = jnp.where(sel & (colf <= qpos), 0.0, -jnp.inf)
    bias = s_ref[...]
    bias4 = jnp.concatenate([bias] * 4, axis=0)

    q = q_ref[...]
    outs = [None] * N_HEADS
    for g in range(N_KV_HEADS):
        lo = (g % 2) * HEAD_DIM
        keep = (lane >= lo) & (lane < lo + HEAD_DIM)
        rows = []
        for j in range(4):
            h = 4 * g + j
            blk = q[:, (h // 2) * LANES:(h // 2 + 1) * LANES]
            if h % 2 != g % 2:
                blk = pltpu.roll(blk, HEAD_DIM, 1)
            piece = jnp.where(keep, blk, 0.0)
            zero = jnp.zeros_like(piece)
            rows.append(jnp.concatenate([piece, zero] if g < 2 else [zero, piece], axis=1))
        qbd = jnp.concatenate(rows, axis=0).astype(BF16)
        logits = _dot_nt(qbd, kb_ref[...]) + bias4
        m = jnp.max(logits, axis=1, keepdims=True)
        p = jnp.exp(logits - m)
        denom = jnp.sum(p, axis=1, keepdims=True)
        acc = _dot(p.astype(BF16), vb_ref[...]) / denom
        for j in range(4):
            outs[4 * g + j] = acc[j * tq:(j + 1) * tq, (g // 2) * LANES:(g // 2 + 1) * LANES]
    for c in range(N_HEADS // 2):
        g = (2 * c) // 4
        even, odd = outs[2 * c], outs[2 * c + 1]
        if g % 2 == 1:
            even = pltpu.roll(even, HEAD_DIM, 1)
        else:
            odd = pltpu.roll(odd, HEAD_DIM, 1)
        o_ref[:, c * LANES:(c + 1) * LANES] = jnp.where(lane < HEAD_DIM, even, odd).astype(BF16)


def _prompt_attention_part(proj, small, *, n_batch, seq, i0, n_tiles, tq):
    nq = seq // tq
    n_keys = (i0 + n_tiles) * tq
    kc = next(c for c in (512, 256, 128) if n_keys % c == 0)
    return pl.pallas_call(
        functools.partial(_pattn_kernel, i0=i0, n_keys=n_keys, kc=kc, topk=min(TOPK_MAX, seq // 4)),
        grid=(n_batch, n_tiles),
        in_specs=[
            pl.BlockSpec((tq, ATTN_W), lambda b, i: (b * nq + i0 + i, 2)),
            pl.BlockSpec((tq, N_IDX_HEADS * IDX_DIM), lambda b, i: (b * nq + i0 + i, 7)),
            pl.BlockSpec((tq, LANES), lambda b, i: (b * nq + i0 + i, 0)),
            pl.BlockSpec((seq, KV_W), lambda b, i: (b, 12)),
            pl.BlockSpec((seq, KV_W), lambda b, i: (b, 13)),
            pl.BlockSpec((seq, LANES), lambda b, i: (b, 0)),
        ],
        out_specs=pl.BlockSpec((None, tq, ATTN_W), lambda b, i: (b, i, 0)),
        out_shape=jax.ShapeDtypeStruct((n_batch, n_tiles * tq, ATTN_W), BF16),
        scratch_shapes=[
            pltpu.VMEM((n_keys, KV_W), BF16),
            pltpu.VMEM((n_keys, KV_W), BF16),
            pltpu.VMEM((n_keys, LANES), BF16),
            pltpu.VMEM((tq, n_keys), F32),
        ],
        compiler_params=pltpu.CompilerParams(
            dimension_semantics=("arbitrary", "arbitrary"), vmem_limit_bytes=VMEM_LIMIT),
        name=f"prompt_attention_{i0}",
    )(proj, proj, small, proj, proj, small)


def _prompt_attention(proj, small, *, n_batch, seq, tq=128, tiles_per_part=2):
    nq = seq // tq
    parts = [
        _prompt_attention_part(proj, small, n_batch=n_batch, seq=seq, i0=i0,
                               n_tiles=min(tiles_per_part, nq - i0), tq=tq)
        for i0 in range(0, nq, tiles_per_part)
    ]
    return jnp.concatenate(parts, axis=1).reshape(n_batch * seq, ATTN_W)


SELECT_PAGES_PER_STEP = 32
ATTEND_PAGES_PER_STEP = 32
SUB_PAGES = 8


def _sidx_kernel(pt_ref, qst_ref, w_ref, sm_ref, *refs, n_chunks, n_new, ps):
    pages = refs[:ps]
    o_ref, s_ref = refs[ps], refs[ps + 1]
    c = pl.program_id(1)
    qst = qst_ref[...]
    w = w_ref[...] * IDX_SCALE

    def head_sum(s):
        s = jnp.maximum(s, 0.0) * w
        out = s[0:n_new]
        for h in range(1, N_IDX_HEADS):
            out = out + s[h * n_new:(h + 1) * n_new]
        return out

    for r0 in range(0, ps, SUB_PAGES):
        kt = jnp.concatenate([pages[r0 + r][...] for r in range(SUB_PAGES)], axis=1).astype(BF16)
        part = head_sum(_dot(qst, kt))
        for r in range(SUB_PAGES):
            s_ref[ps * c + r0 + r] = part[:, r * PAGE_SIZE:(r + 1) * PAGE_SIZE]

    @pl.when(c == n_chunks - 1)
    def _():
        n_past_blocks = n_chunks * ps
        past = n_past_blocks * PAGE_SIZE
        k_new = sm_ref[...][:, 0:IDX_DIM]
        kp = jnp.concatenate([k_new, jnp.zeros((PAGE_SIZE - n_new, IDX_DIM), F32)], axis=0)
        lane = lax.broadcasted_iota(jnp.int32, (n_new, LANES), 1)
        trow = lax.broadcasted_iota(jnp.int32, (n_new, LANES), 0)
        s_new = head_sum(_dot_nt(qst, kp.astype(BF16)))
        s_ref[n_past_blocks] = jnp.where(lane < n_new, jnp.where(lane <= trow, s_new, F32_MIN), -jnp.inf)
        o_ref[...] = jnp.concatenate([s_ref[k] for k in range(n_past_blocks + 1)], axis=1)


def _select_bias_kernel(s_ref, o_ref, *, n_new, past, topk):
    s = s_ref[...]
    rows, n_all = s.shape
    sel = _select_topk(s, topk)
    col = lax.broadcasted_iota(jnp.int32, (1, n_all), 1)
    tq = lax.broadcasted_iota(jnp.int32, (rows, 1), 0) % n_new
    o_ref[...] = jnp.where(sel & ((col - past) <= tq), 0.0, -jnp.inf)


def _select_bias(scores, *, n_new, past, rows_per_step=128):
    rows, n_all = scores.shape
    rows_per_step = min(rows_per_step, rows)
    return pl.pallas_call(
        functools.partial(_select_bias_kernel, n_new=n_new, past=past,
                          topk=min(TOPK_MAX, (past + n_new) // 4)),
        grid=(rows // rows_per_step,),
        in_specs=[pl.BlockSpec((rows_per_step, n_all), lambda i: (i, 0))],
        out_specs=pl.BlockSpec((rows_per_step, n_all), lambda i: (i, 0)),
        out_shape=jax.ShapeDtypeStruct((rows, n_all), F32),
        compiler_params=pltpu.CompilerParams(
            dimension_semantics=("arbitrary",), vmem_limit_bytes=VMEM_LIMIT),
        name="sample_select_bias",
    )(scores)


def _sample_select(page_table, qst, wcol, small, cache_kidx_t, *, n_new):
    n_seq, n_pages = page_table.shape
    ps = min(SELECT_PAGES_PER_STEP, n_pages)
    n_chunks = n_pages // ps
    n_all = n_pages * PAGE_SIZE + LANES
    page_specs = [
        pl.BlockSpec((None, IDX_DIM, PAGE_SIZE), lambda b, c, pt, r=r: (pt[b, ps * c + r], 0, 0))
        for r in range(ps)
    ]
    rows = N_IDX_HEADS * n_new
    grid_spec = pltpu.PrefetchScalarGridSpec(
        num_scalar_prefetch=1,
        grid=(n_seq, n_chunks),
        in_specs=[
            pl.BlockSpec((None, rows, IDX_DIM), lambda b, c, pt: (b, 0, 0)),
            pl.BlockSpec((None, rows, 1), lambda b, c, pt: (b, 0, 0)),
            pl.BlockSpec((n_new, LANES), lambda b, c, pt: (b, 0)),
        ] + page_specs,
        out_specs=pl.BlockSpec((None, n_new, n_all), lambda b, c, pt: (b, 0, 0)),
        scratch_shapes=[pltpu.VMEM((n_pages + 1, n_new, LANES), F32)],
    )
    return pl.pallas_call(
        functools.partial(_sidx_kernel, n_chunks=n_chunks, n_new=n_new, ps=ps),
        grid_spec=grid_spec,
        out_shape=jax.ShapeDtypeStruct((n_seq, n_new, n_all), F32),
        compiler_params=pltpu.CompilerParams(
            dimension_semantics=("arbitrary", "arbitrary"), vmem_limit_bytes=VMEM_LIMIT),
        name="sample_select",
    )(page_table, qst, wcol, small, *([cache_kidx_t] * ps))


def _sattn_kernel(pt_ref, q_ref, bias_ref, biasn_ref, kn_ref, vn_ref, *refs, n_chunks, n_new, ps):
    kpages = refs[:ps]
    vpages = refs[ps:2 * ps]
    o_ref, m_ref, l_ref, acc_ref = refs[2 * ps:]
    c = pl.program_id(1)
    rows = q_ref.shape[0]
    reps = rows // n_new

    @pl.when(c == 0)
    def _():
        m_ref[...] = jnp.full(m_ref.shape, -1e30, F32)
        l_ref[...] = jnp.zeros(l_ref.shape, F32)
        acc_ref[...] = jnp.zeros(acc_ref.shape, F32)

    def update(logits, bias, pv):
        logits = logits + jnp.concatenate([bias] * reps, axis=0)
        m_old = m_ref[...]
        m_new = jnp.maximum(m_old, jnp.max(logits, axis=1, keepdims=True))
        alpha = jnp.exp(m_old - m_new)
        p = jnp.exp(logits - m_new)
        l_ref[...] = alpha * l_ref[...] + jnp.sum(p, axis=1, keepdims=True)
        acc_ref[...] = alpha * acc_ref[...] + pv(p.astype(BF16))
        m_ref[...] = m_new

    sub_keys = SUB_PAGES * PAGE_SIZE
    for r0 in range(0, ps, SUB_PAGES):
        kt = jnp.concatenate([kpages[r0 + r][...] for r in range(SUB_PAGES)], axis=1).astype(BF16)
        vt = jnp.concatenate([vpages[r0 + r][...] for r in range(SUB_PAGES)], axis=1).astype(BF16)
        k0 = (r0 // SUB_PAGES) * sub_keys
        update(_dot(q_ref[...], kt), bias_ref[:, k0:k0 + sub_keys], lambda p, vt=vt: _dot_nt(p, vt))

    @pl.when(c == n_chunks - 1)
    def _():
        pad = jnp.zeros((PAGE_SIZE - n_new, KV_W), F32)
        kn = jnp.concatenate([kn_ref[...], pad], axis=0).astype(BF16)
        vn = jnp.concatenate([vn_ref[...], pad], axis=0).astype(BF16)
        update(_dot_nt(q_ref[...], kn), biasn_ref[...], lambda p: _dot(p, vn))
        o_ref[...] = acc_ref[...] / l_ref[...]


def _sample_attend(page_table, qbd, bias, proj, cache_k_t, cache_v_t, *, n_new):
    n_seq, n_pages = page_table.shape
    ps = min(ATTEND_PAGES_PER_STEP, n_pages)
    n_chunks = n_pages // ps
    rows = qbd.shape[1]
    chunk_keys = ps * PAGE_SIZE
    page_specs = [
        pl.BlockSpec((None, KV_W, PAGE_SIZE), lambda b, c, pt, r=r: (pt[b, ps * c + r], 0, 0))
        for r in range(ps)
    ]
    grid_spec = pltpu.PrefetchScalarGridSpec(
        num_scalar_prefetch=1,
        grid=(n_seq, n_chunks),
        in_specs=[
            pl.BlockSpec((None, rows, KV_W), lambda b, c, pt: (b, 0, 0)),
            pl.BlockSpec((None, n_new, chunk_keys), lambda b, c, pt: (b, 0, c)),
            pl.BlockSpec((None, n_new, LANES), lambda b, c, pt: (b, 0, n_pages)),
            pl.BlockSpec((n_new, KV_W), lambda b, c, pt: (b, 12)),
            pl.BlockSpec((n_new, KV_W), lambda b, c, pt: (b, 13)),
        ] + page_specs + page_specs,
        out_specs=pl.BlockSpec((None, rows, KV_W), lambda b, c, pt: (b, 0, 0)),
        scratch_shapes=[
            pltpu.VMEM((rows, 1), F32),
            pltpu.VMEM((rows, 1), F32),
            pltpu.VMEM((rows, KV_W), F32),
        ],
    )
    return pl.pallas_call(
        functools.partial(_sattn_kernel, n_chunks=n_chunks, n_new=n_new, ps=ps),
        grid_spec=grid_spec,
        out_shape=jax.ShapeDtypeStruct((n_seq, rows, KV_W), F32),
        compiler_params=pltpu.CompilerParams(
            dimension_semantics=("arbitrary", "arbitrary"), vmem_limit_bytes=VMEM_LIMIT),
        name="sample_attend",
    )(page_table, qbd, bias, bias, proj, proj, *([cache_k_t] * ps), *([cache_v_t] * ps))


def _merge_kernel(x_ref, rnn_ref, att_ref, gr_ref, ga_ref, wr_ref, wa_ref, wo_ref, o_ref):
    mixed = (_sigmoid(gr_ref[...]) * _dot(rnn_ref[...].astype(BF16), wr_ref[...])
             + _sigmoid(ga_ref[...]) * _dot(att_ref[...].astype(BF16), wa_ref[...]))
    o_ref[...] = x_ref[...] + _dot(mixed.astype(BF16), wo_ref[...])


def _merge(x, rnn, attn, proj, wr, wa, wo, *, tm):
    t = x.shape[0]
    return pl.pallas_call(
        _merge_kernel,
        grid=(t // tm,),
        in_specs=[
            pl.BlockSpec((tm, D_MODEL), lambda i: (i, 0)),
            pl.BlockSpec((tm, D_RNN), lambda i: (i, 0)),
            pl.BlockSpec((tm, ATTN_W), lambda i: (i, 0)),
            pl.BlockSpec((tm, D_MODEL), lambda i: (i, 2)),
            pl.BlockSpec((tm, D_MODEL), lambda i: (i, 3)),
            pl.BlockSpec((D_RNN, D_MODEL), lambda i: (0, 0)),
            pl.BlockSpec((ATTN_W, D_MODEL), lambda i: (0, 0)),
            pl.BlockSpec((D_MODEL, D_MODEL), lambda i: (0, 0)),
        ],
        out_specs=pl.BlockSpec((tm, D_MODEL), lambda i: (i, 0)),
        out_shape=jax.ShapeDtypeStruct((t, D_MODEL), F32),
        compiler_params=pltpu.CompilerParams(
            dimension_semantics=("arbitrary",), vmem_limit_bytes=VMEM_LIMIT),
        name="merge",
    )(x, rnn, attn, proj, proj, wr, wa, wo)


def _route(h, rwh, rwl, rb):
    lane = lax.broadcasted_iota(jnp.int32, (1, LANES), 1)
    lanef = lane.astype(F32)
    hh, hl = _split_bf16(h)
    lg = (_dot(hh, rwh) + _dot(hl, rwh) + _dot(hh, rwl)) + rb
    is_g = (lane >= N_EXPERTS) & (lane < N_EXPERTS + N_GROUPS)
    gl = jnp.where(is_g, lg, -jnp.inf)
    gmax = jnp.max(gl, axis=1, keepdims=True)
    gprob = 1.0 / jnp.sum(jnp.exp(gl - gmax), axis=1, keepdims=True)
    gsel = jnp.min(jnp.where(is_g & (lg == gmax), lanef - N_EXPERTS, 1e9), axis=1, keepdims=True)
    in_grp = (lane < N_EXPERTS) & (jnp.floor(lanef * (1.0 / EXPERTS_PER_GROUP)) == gsel)
    v1 = jnp.where(in_grp, lg, -jnp.inf)
    t1 = jnp.max(v1, axis=1, keepdims=True)
    i1 = jnp.min(jnp.where(in_grp & (lg == t1), lanef, 1e9), axis=1, keepdims=True)
    rest = in_grp & (lanef != i1)
    v2 = jnp.where(rest, lg, -jnp.inf)
    t2 = jnp.max(v2, axis=1, keepdims=True)
    i2 = jnp.min(jnp.where(rest & (lg == t2), lanef, 1e9), axis=1, keepdims=True)
    d = jnp.exp(t2 - t1)
    return i1, i2, gprob / (1.0 + d), gprob * d / (1.0 + d)


def _moe_kernel(x_ref, ln_ref, rwh_ref, rwl_ref, rb_ref, w1_ref, w3_ref, w2_ref, o_ref, h_ref, gate_ref):
    e = pl.program_id(1)
    lane = lax.broadcasted_iota(jnp.int32, (1, LANES), 1)

    @pl.when(e == 0)
    def _():
        h = _rms(x_ref[...], ln_ref[...])
        h_ref[...] = h.astype(BF16)
        i1, i2, g1, g2 = _route(h, rwh_ref[...], rwl_ref[...], rb_ref[...])
        lanef = lane.astype(F32)
        gate_ref[...] = jnp.where(lanef == i1, g1, 0.0) + jnp.where(lanef == i2, g2, 0.0)

    ge = jnp.sum(jnp.where(lane == e, gate_ref[...], 0.0), axis=1, keepdims=True)
    up = _dot(h_ref[...], w1_ref[...].astype(BF16))
    hid = (up * _sigmoid(up)) * _dot(h_ref[...], w3_ref[...].astype(BF16))
    contrib = _dot((hid * ge).astype(BF16), w2_ref[...].astype(BF16))

    @pl.when(e == 0)
    def _():
        o_ref[...] = x_ref[...] + contrib

    @pl.when(e > 0)
    def _():
        o_ref[...] += contrib


def _moe(x, ln2, rw_hi, rw_lo, rb, w1, w3, w2, *, tm):
    t = x.shape[0]
    return pl.pallas_call(
        _moe_kernel,
        grid=(t // tm, N_EXPERTS),
        in_specs=[
            pl.BlockSpec((tm, D_MODEL), lambda i, e: (i, 0)),
            pl.BlockSpec((1, D_MODEL), lambda i, e: (0, 0)),
            pl.BlockSpec((D_MODEL, LANES), lambda i, e: (0, 0)),
            pl.BlockSpec((D_MODEL, LANES), lambda i, e: (0, 0)),
            pl.BlockSpec((1, LANES), lambda i, e: (0, 0)),
            pl.BlockSpec((None, D_MODEL, D_EXPERT), lambda i, e: (e, 0, 0)),
            pl.BlockSpec((None, D_MODEL, D_EXPERT), lambda i, e: (e, 0, 0)),
            pl.BlockSpec((None, D_EXPERT, D_MODEL), lambda i, e: (e, 0, 0)),
        ],
        out_specs=pl.BlockSpec((tm, D_MODEL), lambda i, e: (i, 0)),
        out_shape=jax.ShapeDtypeStruct((t, D_MODEL), F32),
        scratch_shapes=[pltpu.VMEM((tm, D_MODEL), BF16), pltpu.VMEM((tm, LANES), F32)],
        compiler_params=pltpu.CompilerParams(
            dimension_semantics=("arbitrary", "arbitrary"), vmem_limit_bytes=VMEM_LIMIT),
        name="moe",
    )(x, ln2, rw_hi, rw_lo, rb, w1, w3, w2)


def _ple_update(x2, p, ln, wg, wp):
    gate = _sigmoid(_dot(_rms(x2, ln).astype(BF16), wg))
    return x2 + gate * _dot(p.astype(BF16), wp)


def _ple_kernel(x_ref, p_ref, ln_ref, wg_ref, wp_ref, o_ref):
    o_ref[...] = _ple_update(x_ref[...], p_ref[...], ln_ref[...], wg_ref[...], wp_ref[...])


def _ple(x, p, ln3, wg, wp, *, tm):
    t = x.shape[0]
    return pl.pallas_call(
        _ple_kernel,
        grid=(t // tm,),
        in_specs=[
            pl.BlockSpec((tm, D_MODEL), lambda i: (i, 0)),
            pl.BlockSpec((tm, PLE_DIM), lambda i: (i, 0)),
            pl.BlockSpec((1, D_MODEL), lambda i: (0, 0)),
            pl.BlockSpec((D_MODEL, D_MODEL), lambda i: (0, 0)),
            pl.BlockSpec((PLE_DIM, D_MODEL), lambda i: (0, 0)),
        ],
        out_specs=pl.BlockSpec((tm, D_MODEL), lambda i: (i, 0)),
        out_shape=jax.ShapeDtypeStruct((t, D_MODEL), F32),
        compiler_params=pltpu.CompilerParams(
            dimension_semantics=("arbitrary",), vmem_limit_bytes=VMEM_LIMIT),
        name="ple",
    )(x, p, ln3, wg, wp)


MOE_ROW_TILE = 256
META_E, META_G, META_RANK = 0, 2, 4


def _router_kernel(x_ref, ln_ref, rwh_ref, rwl_ref, rb_ref, tri_ref, meta_ref, cnt_ref, carry_ref):
    i = pl.program_id(0)
    lane = lax.broadcasted_iota(jnp.int32, (1, LANES), 1)
    lanef = lane.astype(F32)

    @pl.when(i == 0)
    def _():
        carry_ref[...] = jnp.zeros(carry_ref.shape, F32)

    i1, i2, g1, g2 = _route(_rms(x_ref[...], ln_ref[...]), rwh_ref[...], rwl_ref[...], rb_ref[...])
    onehot = jnp.where((lanef == i1) | (lanef == i2), 1.0, 0.0)
    before = _dot(tri_ref[...], onehot.astype(BF16)) + carry_ref[...]
    r1 = jnp.sum(jnp.where(lanef == i1, before, 0.0), axis=1, keepdims=True)
    r2 = jnp.sum(jnp.where(lanef == i2, before, 0.0), axis=1, keepdims=True)
    carry_ref[...] += jnp.sum(onehot, axis=0, keepdims=True)
    rec = jnp.zeros((x_ref.shape[0], LANES), F32)
    for k, val in ((META_E, i1), (META_E + 1, i2), (META_G, g1), (META_G + 1, g2),
                   (META_RANK, r1), (META_RANK + 1, r2)):
        rec = jnp.where(lane == k, val, rec)
    meta_ref[...] = rec
    cnt_ref[...] = jnp.broadcast_to(carry_ref[...], cnt_ref.shape)


def _router(x, ln2, rw_hi, rw_lo, rb, *, tm):
    t = x.shape[0]
    r = lax.broadcasted_iota(jnp.int32, (tm, tm), 0)
    c = lax.broadcasted_iota(jnp.int32, (tm, tm), 1)
    tri = (c < r).astype(BF16)
    return pl.pallas_call(
        _router_kernel,
        grid=(t // tm,),
        in_specs=[
            pl.BlockSpec((tm, D_MODEL), lambda i: (i, 0)),
            pl.BlockSpec((1, D_MODEL), lambda i: (0, 0)),
            pl.BlockSpec((D_MODEL, LANES), lambda i: (0, 0)),
            pl.BlockSpec((D_MODEL, LANES), lambda i: (0, 0)),
            pl.BlockSpec((1, LANES), lambda i: (0, 0)),
            pl.BlockSpec((tm, tm), lambda i: (0, 0)),
        ],
        out_specs=[
            pl.BlockSpec((tm, LANES), lambda i: (i, 0)),
            pl.BlockSpec((8, LANES), lambda i: (0, 0)),
        ],
        out_shape=[
            jax.ShapeDtypeStruct((t, LANES), F32),
            jax.ShapeDtypeStruct((8, LANES), F32),
        ],
        scratch_shapes=[pltpu.VMEM((1, LANES), F32)],
        compiler_params=pltpu.CompilerParams(
            dimension_semantics=("arbitrary",), vmem_limit_bytes=VMEM_LIMIT),
        name="moe_router",
    )(x, ln2, rw_hi, rw_lo, rb, tri)


def _row_copy(src_ref, src_row, dst_ref, dst_row, sem):
    return pltpu.make_async_copy(src_ref.at[pl.ds(src_row, 1), :], dst_ref.at[pl.ds(dst_row, 1), :], sem)


def _scatter_kernel(slot_ref, x_ref, hs_in_ref, hs_ref, sem, *, n_tok):
    del hs_in_ref
    tm = x_ref.shape[0]
    base = pl.program_id(0) * tm

    def start(r, carry):
        for k in range(2):
            _row_copy(x_ref, r, hs_ref, slot_ref[k * n_tok + base + r], sem).start()
        return carry

    lax.fori_loop(0, tm, start, 0)
    for k in range(2):
        pltpu.make_async_copy(x_ref, hs_ref.at[pl.ds(0, tm), :], sem).wait()


def _scatter_rows(slots, x, hs_zero, *, tm):
    t = x.shape[0]
    grid_spec = pltpu.PrefetchScalarGridSpec(
        num_scalar_prefetch=1,
        grid=(t // tm,),
        in_specs=[
            pl.BlockSpec((tm, D_MODEL), lambda i, s: (i, 0)),
            pl.BlockSpec(memory_space=pl.ANY),
        ],
        out_specs=pl.BlockSpec(memory_space=pl.ANY),
        scratch_shapes=[pltpu.SemaphoreType.DMA(())],
    )
    return pl.pallas_call(
        functools.partial(_scatter_kernel, n_tok=t),
        grid_spec=grid_spec,
        out_shape=jax.ShapeDtypeStruct(hs_zero.shape, F32),
        input_output_aliases={2: 0},
        compiler_params=pltpu.CompilerParams(
            dimension_semantics=("arbitrary",), vmem_limit_bytes=VMEM_LIMIT),
        name="moe_scatter",
    )(slots, x, hs_zero)


def _expert_kernel(te_ref, nu_ref, hs_ref, ln_ref, w1_ref, w3_ref, w2_ref, y_ref):
    i = pl.program_id(0)

    @pl.when(i < nu_ref[0])
    def _():
        h = _rms(hs_ref[...], ln_ref[...]).astype(BF16)
        up = _dot(h, w1_ref[...].astype(BF16))
        hid = (up * _sigmoid(up)) * _dot(h, w3_ref[...].astype(BF16))
        y_ref[...] = _dot(hid.astype(BF16), w2_ref[...].astype(BF16))

    @pl.when(i >= nu_ref[0])
    def _():
        y_ref[...] = jnp.zeros(y_ref.shape, F32)


def _expert_mlp(tile_expert, n_used, hs, ln2, w1, w3, w2):
    n_tiles = tile_expert.shape[0]
    grid_spec = pltpu.PrefetchScalarGridSpec(
        num_scalar_prefetch=2,
        grid=(n_tiles,),
        in_specs=[
            pl.BlockSpec((MOE_ROW_TILE, D_MODEL), lambda i, te, nu: (i, 0)),
            pl.BlockSpec((1, D_MODEL), lambda i, te, nu: (0, 0)),
            pl.BlockSpec((None, D_MODEL, D_EXPERT), lambda i, te, nu: (te[i], 0, 0)),
            pl.BlockSpec((None, D_MODEL, D_EXPERT), lambda i, te, nu: (te[i], 0, 0)),
            pl.BlockSpec((None, D_EXPERT, D_MODEL), lambda i, te, nu: (te[i], 0, 0)),
        ],
        out_specs=pl.BlockSpec((MOE_ROW_TILE, D_MODEL), lambda i, te, nu: (i, 0)),
    )
    return pl.pallas_call(
        _expert_kernel,
        grid_spec=grid_spec,
        out_shape=jax.ShapeDtypeStruct(hs.shape, F32),
        compiler_params=pltpu.CompilerParams(
            dimension_semantics=("arbitrary",), vmem_limit_bytes=VMEM_LIMIT),
        name="moe_experts",
    )(tile_expert, n_used, hs, ln2, w1, w3, w2)


def _combine_ple_kernel(slot_ref, x_ref, meta_ref, p_ref, ln_ref, wg_ref, wp_ref, y_ref, o_ref,
                        ybuf_ref, sem, *, n_tok):
    i = pl.program_id(0)
    tm = x_ref.shape[0]

    def gather(tile, buf):
        def start(r, carry):
            for k in range(2):
                _row_copy(y_ref, slot_ref[k * n_tok + tile * tm + r], ybuf_ref.at[buf, k], r,
                          sem.at[buf]).start()
            return carry
        lax.fori_loop(0, tm, start, 0)

    @pl.when(i == 0)
    def _():
        gather(0, 0)

    @pl.when(i + 1 < pl.num_programs(0))
    def _():
        gather(i + 1, (i + 1) % 2)

    cur = i % 2
    for k in range(2):
        pltpu.make_async_copy(y_ref.at[pl.ds(0, tm), :], ybuf_ref.at[cur, k], sem.at[cur]).wait()
    meta = meta_ref[...]
    x2 = (x_ref[...] + meta[:, META_G:META_G + 1] * ybuf_ref[cur, 0]
          + meta[:, META_G + 1:META_G + 2] * ybuf_ref[cur, 1])
    o_ref[...] = _ple_update(x2, p_ref[...], ln_ref[...], wg_ref[...], wp_ref[...])


def _combine_ple(slots, x, meta, p, ln3, wg, wp, y, *, tm):
    t = x.shape[0]
    grid_spec = pltpu.PrefetchScalarGridSpec(
        num_scalar_prefetch=1,
        grid=(t // tm,),
        in_specs=[
            pl.BlockSpec((tm, D_MODEL), lambda i, s: (i, 0)),
            pl.BlockSpec((tm, LANES), lambda i, s: (i, 0)),
            pl.BlockSpec((tm, PLE_DIM), lambda i, s: (i, 0)),
            pl.BlockSpec((1, D_MODEL), lambda i, s: (0, 0)),
            pl.BlockSpec((D_MODEL, D_MODEL), lambda i, s: (0, 0)),
            pl.BlockSpec((PLE_DIM, D_MODEL), lambda i, s: (0, 0)),
            pl.BlockSpec(memory_space=pl.ANY),
        ],
        out_specs=pl.BlockSpec((tm, D_MODEL), lambda i, s: (i, 0)),
        scratch_shapes=[
            pltpu.VMEM((2, 2, tm, D_MODEL), F32),
            pltpu.SemaphoreType.DMA((2,)),
        ],
    )
    return pl.pallas_call(
        functools.partial(_combine_ple_kernel, n_tok=t),
        grid_spec=grid_spec,
        out_shape=jax.ShapeDtypeStruct((t, D_MODEL), F32),
        compiler_params=pltpu.CompilerParams(
            dimension_semantics=("arbitrary",), vmem_limit_bytes=VMEM_LIMIT),
        name="moe_combine_ple",
    )(slots, x, meta, p, ln3, wg, wp, y)


def _sparse_moe_ple(x, p, w):
    t = x.shape[0]
    rt = MOE_ROW_TILE
    meta, cnt = _router(x, w["ln2"], w["rw_hi"], w["rw_lo"], w["rb"], tm=256)
    counts = cnt[0, :N_EXPERTS].astype(jnp.int32)
    padded = ((counts + rt - 1) // rt) * rt
    ends = jnp.cumsum(padded)
    offs = ends - padded
    eid = meta[:, META_E:META_E + 2].astype(jnp.int32)
    rank = meta[:, META_RANK:META_RANK + 2].astype(jnp.int32)
    slots = jnp.transpose(offs[eid] + rank).reshape(2 * t)
    n_tiles = (2 * t + N_EXPERTS * (rt - 1)) // rt
    tile_start = jnp.arange(n_tiles, dtype=jnp.int32) * rt
    tile_expert = jnp.minimum(jnp.sum((tile_start[:, None] >= ends[None, :]).astype(jnp.int32), axis=1),
                              N_EXPERTS - 1)
    n_used = (ends[N_EXPERTS - 1] // rt).reshape(1)
    hs = _scatter_rows(slots, x, jnp.zeros((n_tiles * rt, D_MODEL), F32), tm=256)
    y = _expert_mlp(tile_expert, n_used, hs, w["ln2"], w["w1"], w["w3"], w["w2"])
    return _combine_ple(slots, x, meta, p, w["ln3"], w["wg"], w["wp"], y, tm=256)


def _rope_tables(pos):
    half = HEAD_DIM // 2
    inv = ROPE_THETA ** (-jnp.arange(half, dtype=F32) / half)
    ang = pos.astype(F32)[:, None] * inv[None, :]
    cos, sin = jnp.cos(ang), jnp.sin(ang)
    return (jnp.concatenate([cos, cos, cos, cos], axis=1),
            jnp.concatenate([-sin, sin, -sin, sin], axis=1))


def _block_diag(w, per):
    n, r, _ = w.shape
    eye = jnp.eye(per, dtype=w.dtype)
    wg = w.reshape(n // per, per, r, r)
    return jnp.einsum("gpij,pq->gpiqj", wg, eye).reshape(n // per, per * r, per * r)


def _layer_weights(ln1, w_in, q_norm, k_norm, conv_w, conv_b, w_a, b_a, w_x, b_x, lam, w_br_rnn,
                   w_br_attn, w_out, ln2, w_rg, b_rg, w_re, b_re, w1, w3, w2, ln3, w_ple_gate,
                   w_ple_proj):
    o_q = 2 * D_RNN
    o_k = o_q + ATTN_W
    o_v = o_k + KV_W
    o_qi = o_v + KV_W
    o_ki = o_qi + N_IDX_HEADS * IDX_DIM
    o_wi = o_ki + IDX_DIM
    o_gr = o_wi + N_IDX_HEADS
    o_ga = o_gr + D_MODEL
    w_main = jnp.concatenate([w_in[:, :o_ki], w_in[:, o_gr:o_ga + D_MODEL]], axis=1).astype(BF16)
    w_small = jnp.concatenate(
        [w_in[:, o_ki:o_gr], jnp.zeros((D_MODEL, LANES - IDX_DIM - N_IDX_HEADS), F32)], axis=1).astype(BF16)
    ones = lambda n: jnp.ones((n,), F32)
    zeros = lambda n: jnp.zeros((n,), F32)
    n_gate = 2 * D_MODEL
    gain = jnp.concatenate([ones(o_q), jnp.tile(q_norm, N_HEADS), jnp.tile(k_norm, N_KV_HEADS),
                            ones(KV_W + N_IDX_HEADS * IDX_DIM + n_gate)])
    norm_on = jnp.concatenate([zeros(o_q), ones(ATTN_W + KV_W), zeros(KV_W + N_IDX_HEADS * IDX_DIM + n_gate)])
    rope_on = jnp.concatenate([zeros(o_q), ones(ATTN_W + KV_W), zeros(KV_W), ones(N_IDX_HEADS * IDX_DIM),
                               zeros(n_gate)])
    post = jnp.concatenate([ones(o_q), jnp.full((ATTN_W,), QK_SCALE, F32),
                            ones(2 * KV_W + N_IDX_HEADS * IDX_DIM + n_gate)])
    colctl = jnp.concatenate([jnp.stack([gain, norm_on, rope_on, post]), jnp.zeros((4, N_MAIN), F32)], axis=0)
    tn = 512
    head_of = jnp.arange(tn) // HEAD_DIM
    bd = (head_of[:, None] == head_of[None, :]).astype(BF16)
    rw = jnp.concatenate([w_re, w_rg, jnp.zeros((D_MODEL, LANES - N_EXPERTS - N_GROUPS), F32)], axis=1)
    rw_hi = rw.astype(BF16)
    rw_lo = (rw - rw_hi.astype(F32)).astype(BF16)
    rb = jnp.concatenate([b_re, b_rg, jnp.zeros((LANES - N_EXPERTS - N_GROUPS,), F32)])[None, :]
    return dict(
        ln1=ln1[None, :], w_main=w_main, w_small=w_small, colctl=colctl, bd=bd,
        cw=conv_w, cb=conv_b[None, :],
        wa_bd=_block_diag(w_a, 4).astype(BF16), ba=b_a[None, :],
        wx_bd=_block_diag(w_x, 4).astype(BF16), bx=b_x[None, :], lam=lam[None, :],
        wr=w_br_rnn.astype(BF16), wa=w_br_attn.astype(BF16), wo=w_out.astype(BF16),
        ln2=ln2[None, :], rw_hi=rw_hi, rw_lo=rw_lo, rb=rb,
        w1=w1, w3=w3, w2=w2,
        ln3=ln3[None, :], wg=w_ple_gate.astype(BF16), wp=w_ple_proj.astype(BF16),
    )


def _tail(x, rnn, attn, proj, p, w):
    t = x.shape[0]
    x1 = _merge(x, rnn, attn, proj, w["wr"], w["wa"], w["wo"], tm=min(t, 256))
    if 2 * t >= N_EXPERTS * MOE_ROW_TILE:
        return _sparse_moe_ple(x1, p, w)
    x2 = _moe(x1, w["ln2"], w["rw_hi"], w["rw_lo"], w["rb"], w["w1"], w["w3"], w["w2"], tm=min(t, 512))
    return _ple(x2, p, w["ln3"], w["wg"], w["wp"], tm=min(t, 512))


def _prompt_layer(x, p, w):
    bp, tp, _ = x.shape
    xt = x.reshape(bp * tp, D_MODEL)
    cs, sn = _rope_tables(jnp.tile(jnp.arange(tp, dtype=jnp.int32), bp))
    proj, small = _inproj(xt, w["ln1"], w["w_main"], w["w_small"], w["colctl"], cs, sn, w["bd"],
                          tm=min(bp * tp, 1024))
    conv0 = jnp.zeros((bp, 8, D_RNN), F32)
    h0 = jnp.zeros((bp, 1, D_RNN), F32)
    rnn, h_last = _rglru(proj, conv0, h0, w["cw"], w["cb"], w["wa_bd"], w["ba"], w["wx_bd"], w["bx"],
                         w["lam"], n_seq=bp, tt=min(tp, 256))
    attn = _prompt_attention(proj, small, n_batch=bp, seq=tp)
    y = _tail(xt, rnn, attn, proj, p.reshape(bp * tp, PLE_DIM), w)
    o_k = 2 * D_RNN + ATTN_W
    k = proj[:, o_k:o_k + KV_W].reshape(bp, tp, N_KV_HEADS, HEAD_DIM)
    v = proj[:, o_k + KV_W:o_k + 2 * KV_W].reshape(bp, tp, N_KV_HEADS, HEAD_DIM)
    ki = small[:, :IDX_DIM].reshape(bp, tp, IDX_DIM)
    conv_new = proj.reshape(bp, tp, N_MAIN)[:, tp - (CONV_W - 1):, :D_RNN]
    return y.reshape(bp, tp, D_MODEL), (k, v, ki, conv_new, h_last.reshape(bp, D_RNN))


def _sample_layer(x, p, cache_k, cache_v, cache_kidx, state_conv, state_h, page_table, w):
    bs, ts, _ = x.shape
    n_pages = page_table.shape[1]
    past = n_pages * PAGE_SIZE
    xt = x.reshape(bs * ts, D_MODEL)
    cs, sn = _rope_tables(past + jnp.tile(jnp.arange(ts, dtype=jnp.int32), bs))
    proj, small = _inproj(xt, w["ln1"], w["w_main"], w["w_small"], w["colctl"], cs, sn, w["bd"], tm=bs * ts)
    conv0 = jnp.concatenate([jnp.zeros((bs, 8 - (CONV_W - 1), D_RNN), F32), state_conv], axis=1)
    rnn, h_last = _rglru(proj, conv0, state_h[:, None, :], w["cw"], w["cb"], w["wa_bd"], w["ba"],
                         w["wx_bd"], w["bx"], w["lam"], n_seq=bs, tt=ts)
    o_q = 2 * D_RNN
    o_qi = o_q + ATTN_W + 2 * KV_W
    qi = proj[:, o_qi:o_qi + N_IDX_HEADS * IDX_DIM].reshape(bs, ts, N_IDX_HEADS, IDX_DIM)
    qst = jnp.transpose(qi, (0, 2, 1, 3)).reshape(bs, N_IDX_HEADS * ts, IDX_DIM).astype(BF16)
    wi = small[:, IDX_DIM:IDX_DIM + N_IDX_HEADS].reshape(bs, ts, N_IDX_HEADS)
    wcol = jnp.transpose(wi, (0, 2, 1)).reshape(bs, N_IDX_HEADS * ts, 1)
    n_pool = cache_k.shape[0]
    kidx_t = jnp.transpose(cache_kidx, (0, 2, 1))
    k_t = jnp.transpose(cache_k, (0, 2, 3, 1)).reshape(n_pool, KV_W, PAGE_SIZE)
    v_t = jnp.transpose(cache_v, (0, 2, 3, 1)).reshape(n_pool, KV_W, PAGE_SIZE)
    scores = _sample_select(page_table, qst, wcol, small, kidx_t, n_new=ts)
    bias = _select_bias(scores.reshape(bs * ts, -1), n_new=ts, past=past).reshape(scores.shape)
    q = proj[:, o_q:o_q + ATTN_W].reshape(bs, ts, N_KV_HEADS, N_HEADS // N_KV_HEADS, HEAD_DIM)
    eye = jnp.eye(N_KV_HEADS, dtype=F32)
    qbd = jnp.einsum("btgjd,gk->bgjtkd", q, eye).reshape(bs, N_HEADS * ts, KV_W).astype(BF16)
    att = _sample_attend(page_table, qbd, bias, proj, k_t, v_t, n_new=ts)
    att = att.reshape(bs, N_KV_HEADS, N_HEADS // N_KV_HEADS, ts, N_KV_HEADS, HEAD_DIM)
    att = jnp.stack([att[:, g, :, :, g, :] for g in range(N_KV_HEADS)], axis=1)
    attn = jnp.transpose(att, (0, 3, 1, 2, 4)).reshape(bs * ts, ATTN_W)
    y = _tail(xt, rnn, attn, proj, p.reshape(bs * ts, PLE_DIM), w)
    o_k = o_q + ATTN_W
    k = proj[:, o_k:o_k + KV_W].reshape(bs, ts, N_KV_HEADS, HEAD_DIM)
    v = proj[:, o_k + KV_W:o_k + 2 * KV_W].reshape(bs, ts, N_KV_HEADS, HEAD_DIM)
    ki = small[:, :IDX_DIM].reshape(bs, ts, IDX_DIM)
    conv_new = proj.reshape(bs, ts, N_MAIN)[:, ts - (CONV_W - 1):, :D_RNN]
    return y.reshape(bs, ts, D_MODEL), (k, v, ki, conv_new, h_last.reshape(bs, D_RNN))


def kernel(x_prompt, x_sample, p_prompt, p_sample, cache_k, cache_v, cache_kidx, state_conv, state_h,
           page_table, ln1, w_in, q_norm, k_norm, conv_w, conv_b, w_a, b_a, w_x, b_x, lam, w_br_rnn,
           w_br_attn, w_out, ln2, w_rg, b_rg, w_re, b_re, w1, w3, w2, ln3, w_ple_gate, w_ple_proj):
    weights = (ln1, w_in, q_norm, k_norm, conv_w, conv_b, w_a, b_a, w_x, b_x, lam, w_br_rnn, w_br_attn,
               w_out, ln2, w_rg, b_rg, w_re, b_re, w1, w3, w2, ln3, w_ple_gate, w_ple_proj)
    depth = ln1.shape[0]
    yp, ys = x_prompt, x_sample
    st_p, st_s = [], []
    for i in range(depth):
        w = _layer_weights(*[wt[i] for wt in weights])
        yp, sp = _prompt_layer(yp, p_prompt[i], w)
        ys, ss = _sample_layer(ys, p_sample[i], cache_k[i], cache_v[i], cache_kidx[i], state_conv[i],
                               state_h[i], page_table, w)
        st_p.append(sp)
        st_s.append(ss)
    stack = lambda sts, j: jnp.stack([s[j] for s in sts])
    return (yp, ys, stack(st_p, 0), stack(st_p, 1), stack(st_p, 2), stack(st_p, 3), stack(st_p, 4),
            stack(st_s, 0), stack(st_s, 1), stack(st_s, 2), stack(st_s, 3), stack(st_s, 4))
```

```python
import functools

import jax
import jax.numpy as jnp
import numpy as np
from jax import lax
from jax.experimental import pallas as pl
from jax.experimental.pallas import tpu as pltpu

F32 = jnp.float32
BF16 = jnp.bfloat16

D_MODEL = 2048
HEAD_DIM = 64
N_HEADS = 16
N_KV_HEADS = 4
ATTN_W = N_HEADS * HEAD_DIM
KV_W = N_KV_HEADS * HEAD_DIM
N_IDX_HEADS = 8
IDX_DIM = 64
TOPK_MAX = 256
ROPE_THETA = 10000.0
D_RNN = 1024
N_RNN_BLOCKS = 16
RNN_BLOCK = 64
CONV_W = 4
LRU_C = 8.0
N_GROUPS = 4
EXPERTS_PER_GROUP = 8
N_EXPERTS = 32
D_EXPERT = 256
PLE_DIM = 256
PAGE_SIZE = 128
EPS = 1e-6

LANES = 128
N_MAIN = 8192
COL_RAW_END = 2 * D_RNN
COL_NR_END = 4096
IDX_SCALE = (IDX_DIM ** -0.5) * (N_IDX_HEADS ** -0.5)
QK_SCALE = HEAD_DIM ** -0.5
F32_MIN = float(np.finfo(np.float32).min)
INT_MIN = -2147483648
KEY_NEG_INF = INT_MIN + 0x7FFFFF
VMEM_LIMIT = 56 * 1024 * 1024


def _dot(a, b):
    return jnp.dot(a, b, preferred_element_type=F32)


def _dot_nt(a, b):
    return lax.dot_general(a, b, (((1,), (1,)), ((), ())), preferred_element_type=F32)


def _sigmoid(x):
    return 1.0 / (1.0 + jnp.exp(-x))


def _rms(x, g):
    return x * lax.rsqrt(jnp.mean(x * x, axis=-1, keepdims=True) + EPS) * g


def _split_bf16(x):
    hi = x.astype(BF16)
    lo = (x - hi.astype(F32)).astype(BF16)
    return hi, lo


def _rope_chunks(y, c, s):
    lane = lax.broadcasted_iota(jnp.int32, (1, LANES), 1)
    first_half = (lane % HEAD_DIM) < (HEAD_DIM // 2)
    outs = []
    for k in range(y.shape[1] // LANES):
        yc = y[:, k * LANES:(k + 1) * LANES]
        partner = jnp.where(first_half, pltpu.roll(yc, LANES - HEAD_DIM // 2, 1),
                            pltpu.roll(yc, HEAD_DIM // 2, 1))
        outs.append(yc * c + partner * s)
    return outs[0] if len(outs) == 1 else jnp.concatenate(outs, axis=1)


def _inproj_kernel(x_ref, ln_ref, wlo_ref, whi_ref, ws_ref, ctl_ref, cs_ref, sn_ref, bd_ref,
                   o_ref, os_ref, h_ref, *, tn):
    j = pl.program_id(1)

    @pl.when(j == 0)
    def _():
        hb = _rms(x_ref[...], ln_ref[...]).astype(BF16)
        h_ref[...] = hb
        ys = _dot(hb, ws_ref[...].astype(BF16))
        lane = lax.broadcasted_iota(jnp.int32, (1, LANES), 1)
        os_ref[...] = jnp.where(lane < IDX_DIM, _rope_chunks(ys, cs_ref[...], sn_ref[...]), ys)

    @pl.when(j < COL_NR_END // tn)
    def _():
        o_ref[...] = _dot(h_ref[...], wlo_ref[...].astype(BF16))

    @pl.when(j >= COL_NR_END // tn)
    def _():
        o_ref[...] = _dot(h_ref[...], whi_ref[...].astype(BF16))

    @pl.when((j >= COL_RAW_END // tn) & (j < COL_NR_END // tn))
    def _():
        y = o_ref[...]
        ctl = ctl_ref[...]
        gain, norm_on, rope_on, post = ctl[0:1], ctl[1:2], ctl[2:3], ctl[3:4]
        hi, lo = _split_bf16(y * y)
        ss = _dot(hi, bd_ref[...]) + _dot(lo, bd_ref[...])
        yn = jnp.where(norm_on > 0.0, y * lax.rsqrt(ss * (1.0 / HEAD_DIM) + EPS) * gain, y)
        yr = jnp.where(rope_on > 0.0, _rope_chunks(yn, cs_ref[...], sn_ref[...]), yn)
        o_ref[...] = yr * post


def _inproj(x, ln1, w_in, w_gates, w_small, colctl, cs, sn, bd, *, tm, tn=512):
    t = x.shape[0]
    grid = (t // tm, N_MAIN // tn)
    n_lo = COL_NR_END // tn
    n_rope = cs.shape[0] // tm
    return pl.pallas_call(
        functools.partial(_inproj_kernel, tn=tn),
        grid=grid,
        in_specs=[
            pl.BlockSpec((tm, D_MODEL), lambda i, j: (i, 0)),
            pl.BlockSpec((1, D_MODEL), lambda i, j: (0, 0)),
            pl.BlockSpec((D_MODEL, tn), lambda i, j: (0, jnp.minimum(j, n_lo - 1))),
            pl.BlockSpec((D_MODEL, tn), lambda i, j: (0, jnp.maximum(j - n_lo, 0))),
            pl.BlockSpec((D_MODEL, LANES), lambda i, j: (0, 0)),
            pl.BlockSpec((8, tn), lambda i, j: (0, j)),
            pl.BlockSpec((tm, LANES), lambda i, j: (i % n_rope, 0)),
            pl.BlockSpec((tm, LANES), lambda i, j: (i % n_rope, 0)),
            pl.BlockSpec((tn, tn), lambda i, j: (0, 0)),
        ],
        out_specs=[
            pl.BlockSpec((tm, tn), lambda i, j: (i, j)),
            pl.BlockSpec((tm, LANES), lambda i, j: (i, 0)),
        ],
        out_shape=[
            jax.ShapeDtypeStruct((t, N_MAIN), F32),
            jax.ShapeDtypeStruct((t, LANES), F32),
        ],
        scratch_shapes=[pltpu.VMEM((tm, D_MODEL), BF16)],
        compiler_params=pltpu.CompilerParams(
            dimension_semantics=("arbitrary", "arbitrary"), vmem_limit_bytes=VMEM_LIMIT),
        name="inproj",
    )(x, ln1, w_in, w_gates, w_small, colctl, cs, sn, bd)


def _rglru_kernel(x_ref, g_ref, c0_ref, h0_ref, cw_ref, cb_ref, wa_ref, ba_ref, wx_ref, bx_ref,
                  lam_ref, o_ref, hl_ref, xs_ref, a_ref, b_ref, hc_ref):
    t = pl.program_id(1)
    tt = x_ref.shape[0]

    @pl.when(t == 0)
    def _():
        xs_ref[0:8, :] = c0_ref[...]
        hc_ref[...] = h0_ref[...]

    xs_ref[8:8 + tt, :] = x_ref[...]
    cw = cw_ref[...]
    taps = (xs_ref[5:5 + tt, :] * cw[0:1] + xs_ref[6:6 + tt, :] * cw[1:2]
            + xs_ref[7:7 + tt, :] * cw[2:3] + xs_ref[8:8 + tt, :] * cw[3:4])
    xc = cb_ref[...] + taps
    xs_ref[0:8, :] = xs_ref[tt:tt + 8, :]

    xcb = xc.astype(BF16)
    ra, ri = [], []
    for c in range(wa_ref.shape[0]):
        blk = xcb[:, c * 256:(c + 1) * 256]
        ra.append(_dot(blk, wa_ref[c]))
        ri.append(_dot(blk, wx_ref[c]))
    r = _sigmoid(jnp.concatenate(ra, axis=1) + ba_ref[...])
    ig = _sigmoid(jnp.concatenate(ri, axis=1) + bx_ref[...])
    nlam = -lam_ref[...]
    softplus = jnp.maximum(nlam, 0.0) + jnp.log1p(jnp.exp(-jnp.abs(nlam)))
    log_a = (-LRU_C) * r * softplus
    a = jnp.exp(log_a)
    u = jnp.sqrt(jnp.tanh(-log_a) * (a * a + 1.0)) * (ig * xc)

    n8 = tt // 8
    a3 = a.reshape(n8, 8, D_RNN)
    b3 = u.reshape(n8, 8, D_RNN)
    sub = lax.broadcasted_iota(jnp.int32, (1, 8, 1), 1)
    for s in (1, 2, 4):
        a_prev = pltpu.roll(a3, s, 1)
        b_prev = pltpu.roll(b3, s, 1)
        m = sub >= s
        b3 = jnp.where(m, a3 * b_prev + b3, b3)
        a3 = jnp.where(m, a3 * a_prev, a3)
    a_ref[...] = a3.reshape(tt, D_RNN)
    b_ref[...] = b3.reshape(tt, D_RNN)

    def chain(k, carry):
        i0 = pl.multiple_of(k * 8, 8)
        h8 = a_ref[pl.ds(i0, 8), :] * carry + b_ref[pl.ds(i0, 8), :]
        b_ref[pl.ds(i0, 8), :] = h8
        return h8[7:8, :]

    carry = lax.fori_loop(0, n8, chain, hc_ref[...])
    hc_ref[...] = carry
    g = g_ref[...]
    gelu = 0.5 * g * (1.0 + jnp.tanh(0.7978845608028654 * (g + 0.044715 * (g * g * g))))
    o_ref[...] = b_ref[...] * gelu

    @pl.when(t == pl.num_programs(1) - 1)
    def _():
        hl_ref[...] = carry


def _rglru(proj, conv0, h0, cw, cb, wa_bd, ba, wx_bd, bx, lam, *, n_seq, tt):
    t_total = proj.shape[0]
    nt = t_total // (n_seq * tt)
    full = lambda shape: pl.BlockSpec(shape, lambda b, t: (0,) * len(shape))
    return pl.pallas_call(
        _rglru_kernel,
        grid=(n_seq, nt),
        in_specs=[
            pl.BlockSpec((tt, D_RNN), lambda b, t: (b * nt + t, 0)),
            pl.BlockSpec((tt, D_RNN), lambda b, t: (b * nt + t, 1)),
            pl.BlockSpec((None, 8, D_RNN), lambda b, t: (b, 0, 0)),
            pl.BlockSpec((None, 1, D_RNN), lambda b, t: (b, 0, 0)),
            full((CONV_W, D_RNN)), full((1, D_RNN)),
            full(wa_bd.shape), full((1, D_RNN)),
            full(wx_bd.shape), full((1, D_RNN)),
            full((1, D_RNN)),
        ],
        out_specs=[
            pl.BlockSpec((tt, D_RNN), lambda b, t: (b * nt + t, 0)),
            pl.BlockSpec((None, 1, D_RNN), lambda b, t: (b, 0, 0)),
        ],
        out_shape=[
            jax.ShapeDtypeStruct((t_total, D_RNN), F32),
            jax.ShapeDtypeStruct((n_seq, 1, D_RNN), F32),
        ],
        scratch_shapes=[
            pltpu.VMEM((tt + 8, D_RNN), F32),
            pltpu.VMEM((tt, D_RNN), F32),
            pltpu.VMEM((tt, D_RNN), F32),
            pltpu.VMEM((1, D_RNN), F32),
        ],
        compiler_params=pltpu.CompilerParams(
            dimension_semantics=("arbitrary", "arbitrary"), vmem_limit_bytes=VMEM_LIMIT),
        name="rglru",
    )(proj, proj, conv0, h0, cw, cb, wa_bd, ba, wx_bd, bx, lam)


def _select_topk(s, kk):
    rows, n = s.shape
    kkf = float(kk)

    def key_to_f32(w):
        k = w ^ INT_MIN
        bits = jnp.where(k >= 0, k, k ^ 0x7FFFFFFF)
        return k, lax.bitcast_convert_type(bits, F32)

    def vbody(it, w):
        cand_w = w | jnp.left_shift(jnp.int32(1), 31 - it)
        cand_k, cand_f = key_to_f32(cand_w)
        cnt = jnp.sum(jnp.where(s >= cand_f, 1.0, 0.0), axis=1, keepdims=True)
        ok = (cnt >= kkf) | (cand_k < KEY_NEG_INF)
        return jnp.where(ok, cand_w, w)

    w = lax.fori_loop(0, 32, vbody, jnp.zeros((rows, 1), jnp.int32))
    _, thr = key_to_f32(w)
    gt = s > thr
    eq = s == thr
    need = kkf - jnp.sum(jnp.where(gt, 1.0, 0.0), axis=1, keepdims=True)
    col = lax.broadcasted_iota(jnp.int32, (1, n), 1)
    nbits = int(n).bit_length()

    def jbody(it, jmax):
        cand = jmax | jnp.left_shift(jnp.int32(1), nbits - 1 - it)
        cnt = jnp.sum(jnp.where(eq & (col < cand), 1.0, 0.0), axis=1, keepdims=True)
        return jnp.where(cnt <= need, cand, jmax)

    n_ge = jnp.sum(jnp.where(s >= thr, 1.0, 0.0), axis=1, keepdims=True)
    jmax = lax.cond(
        jnp.max(n_ge) > kkf,
        lambda: lax.fori_loop(0, nbits, jbody, jnp.zeros((rows, 1), jnp.int32)),
        lambda: jnp.full((rows, 1), (1 << nbits) - 1, jnp.int32))
    return gt | (eq & (col < jmax))


def _pattn_kernel(q_ref, qi_ref, sm_ref, k_ref, v_ref, ki_ref, o_ref, s_ref, *, i0, n_keys, kc, topk):
    i = pl.program_id(0)
    n_batch, tq = q_ref.shape[0], q_ref.shape[1]
    lane = lax.broadcasted_iota(jnp.int32, (1, LANES), 1)
    qpos = (i0 + i) * tq + lax.broadcasted_iota(jnp.int32, (tq, 1), 0)

    def score(b, carry):
        sm = sm_ref[b]
        qi = qi_ref[b]
        qrows, wrows = [], []
        for h in range(N_IDX_HEADS):
            blk = qi[:, (h // 2) * LANES:(h // 2 + 1) * LANES]
            if h % 2 == 1:
                blk = pltpu.roll(blk, IDX_DIM, 1)
            qrows.append(jnp.where(lane < IDX_DIM, blk, 0.0))
            wrows.append(sm[:, IDX_DIM + h:IDX_DIM + h + 1])
        qst = jnp.concatenate(qrows, axis=0).astype(BF16)
        wst = jnp.concatenate(wrows, axis=0) * IDX_SCALE
        r0 = pl.multiple_of(b * tq, tq)
        for c in range(n_keys // kc):
            s = jnp.maximum(_dot_nt(qst, ki_ref[b, c * kc:(c + 1) * kc, :]), 0.0) * wst
            sc = s[0:tq]
            for h in range(1, N_IDX_HEADS):
                sc = sc + s[h * tq:(h + 1) * tq]
            col = c * kc + lax.broadcasted_iota(jnp.int32, (1, kc), 1)
            s_ref[pl.ds(r0, tq), c * kc:(c + 1) * kc] = jnp.where(col <= qpos, sc, F32_MIN)
        return carry

    lax.fori_loop(0, n_batch, score, 0)

    sel = _select_topk(s_ref[...], topk)
    colf = lax.broadcasted_iota(jnp.int32, (1, n_keys), 1)
    qpos_all = jnp.concatenate([qpos] * n_batch, axis=0)
    s_ref[...] = jnp.where(sel & (colf <= qpos_all), 0.0, -jnp.inf)

    def attend(b, carry):
        _attend_tile(q_ref[b], s_ref[pl.ds(pl.multiple_of(b * tq, tq), tq), :],
                     k_ref[b, 0:n_keys, :], v_ref[b, 0:n_keys, :], o_ref.at[b])
        return carry

    lax.fori_loop(0, n_batch, attend, 0)


def _attend_tile(q, bias, kb, vb, o_ref):
    tq = q.shape[0]
    lane = lax.broadcasted_iota(jnp.int32, (1, LANES), 1)
    bias4 = jnp.concatenate([bias] * 4, axis=0)
    outs = [None] * N_HEADS
    for g in range(N_KV_HEADS):
        lo = (g % 2) * HEAD_DIM
        keep = (lane >= lo) & (lane < lo + HEAD_DIM)
        rows = []
        for j in range(4):
            h = 4 * g + j
            blk = q[:, (h // 2) * LANES:(h // 2 + 1) * LANES]
            if h % 2 != g % 2:
                blk = pltpu.roll(blk, HEAD_DIM, 1)
            piece = jnp.where(keep, blk, 0.0)
            zero = jnp.zeros_like(piece)
            rows.append(jnp.concatenate([piece, zero] if g < 2 else [zero, piece], axis=1))
        qbd = jnp.concatenate(rows, axis=0).astype(BF16)
        logits = _dot_nt(qbd, kb) + bias4
        m = jnp.max(logits, axis=1, keepdims=True)
        p = jnp.exp(logits - m)
        denom = jnp.sum(p, axis=1, keepdims=True)
        acc = _dot(p.astype(BF16), vb) / denom
        for j in range(4):
            outs[4 * g + j] = acc[j * tq:(j + 1) * tq, (g // 2) * LANES:(g // 2 + 1) * LANES]
    for c in range(N_HEADS // 2):
        g = (2 * c) // 4
        even, odd = outs[2 * c], outs[2 * c + 1]
        if g % 2 == 1:
            even = pltpu.roll(even, HEAD_DIM, 1)
        else:
            odd = pltpu.roll(odd, HEAD_DIM, 1)
        o_ref[:, c * LANES:(c + 1) * LANES] = jnp.where(lane < HEAD_DIM, even, odd).astype(BF16)


def _prompt_attention_part(proj3, small3, kb, vb, kib, *, i0, n_tiles, tq):
    n_batch, seq, _ = proj3.shape
    n_keys = (i0 + n_tiles) * tq
    kc = next(c for c in (512, 256, 128) if n_keys % c == 0)
    return pl.pallas_call(
        functools.partial(_pattn_kernel, i0=i0, n_keys=n_keys, kc=kc, topk=min(TOPK_MAX, seq // 4)),
        grid=(n_tiles,),
        in_specs=[
            pl.BlockSpec((n_batch, tq, ATTN_W), lambda i: (0, i0 + i, 2)),
            pl.BlockSpec((n_batch, tq, N_IDX_HEADS * IDX_DIM), lambda i: (0, i0 + i, 7)),
            pl.BlockSpec((n_batch, tq, LANES), lambda i: (0, i0 + i, 0)),
            pl.BlockSpec((n_batch, seq, KV_W), lambda i: (0, 0, 0)),
            pl.BlockSpec((n_batch, seq, KV_W), lambda i: (0, 0, 0)),
            pl.BlockSpec((n_batch, seq, LANES), lambda i: (0, 0, 0)),
        ],
        out_specs=pl.BlockSpec((n_batch, tq, ATTN_W), lambda i: (0, i, 0)),
        out_shape=jax.ShapeDtypeStruct((n_batch, n_tiles * tq, ATTN_W), BF16),
        scratch_shapes=[pltpu.VMEM((n_batch * tq, n_keys), F32)],
        compiler_params=pltpu.CompilerParams(
            dimension_semantics=("arbitrary",), vmem_limit_bytes=VMEM_LIMIT),
        name=f"prompt_attention_{i0}",
    )(proj3, proj3, small3, kb, vb, kib)


def _prompt_attention(proj, small, *, n_batch, seq, tq=128, tiles_per_part=2):
    nq = seq // tq
    proj3 = proj.reshape(n_batch, seq, N_MAIN)
    small3 = small.reshape(n_batch, seq, LANES)
    o_k = 2 * D_RNN + ATTN_W
    kb = proj3[:, :, o_k:o_k + KV_W].astype(BF16)
    vb = proj3[:, :, o_k + KV_W:o_k + 2 * KV_W].astype(BF16)
    kib = jnp.where(jnp.arange(LANES) < IDX_DIM, small3, 0.0).astype(BF16)
    parts = [
        _prompt_attention_part(proj3, small3, kb, vb, kib, i0=i0,
                               n_tiles=min(tiles_per_part, nq - i0), tq=tq)
        for i0 in range(0, nq, tiles_per_part)
    ]
    return jnp.concatenate(parts, axis=1).reshape(n_batch * seq, ATTN_W)


SELECT_PAGES_PER_STEP = 32
ATTEND_PAGES_PER_STEP = 32
SUB_PAGES = 8


def _sidx_kernel(pt_ref, qst_ref, w_ref, sm_ref, *refs, n_chunks, n_new, ps):
    pages = refs[:ps]
    o_ref, s_ref = refs[ps], refs[ps + 1]
    c = pl.program_id(1)
    qst = qst_ref[...]
    w = w_ref[...] * IDX_SCALE

    def head_sum(s):
        s = jnp.maximum(s, 0.0) * w
        out = s[0:n_new]
        for h in range(1, N_IDX_HEADS):
            out = out + s[h * n_new:(h + 1) * n_new]
        return out

    for r0 in range(0, ps, SUB_PAGES):
        kt = jnp.concatenate([pages[r0 + r][...] for r in range(SUB_PAGES)], axis=1).astype(BF16)
        part = head_sum(_dot(qst, kt))
        for r in range(SUB_PAGES):
            s_ref[ps * c + r0 + r] = part[:, r * PAGE_SIZE:(r + 1) * PAGE_SIZE]

    @pl.when(c == n_chunks - 1)
    def _():
        n_past_blocks = n_chunks * ps
        past = n_past_blocks * PAGE_SIZE
        k_new = sm_ref[...][:, 0:IDX_DIM]
        kp = jnp.concatenate([k_new, jnp.zeros((PAGE_SIZE - n_new, IDX_DIM), F32)], axis=0)
        lane = lax.broadcasted_iota(jnp.int32, (n_new, LANES), 1)
        trow = lax.broadcasted_iota(jnp.int32, (n_new, LANES), 0)
        s_new = head_sum(_dot_nt(qst, kp.astype(BF16)))
        s_ref[n_past_blocks] = jnp.where(lane < n_new, jnp.where(lane <= trow, s_new, F32_MIN), -jnp.inf)
        o_ref[...] = jnp.concatenate([s_ref[k] for k in range(n_past_blocks + 1)], axis=1)


def _select_bias_kernel(s_ref, o_ref, *, n_new, past, topk):
    s = s_ref[...]
    rows, n_all = s.shape
    sel = _select_topk(s, topk)
    col = lax.broadcasted_iota(jnp.int32, (1, n_all), 1)
    tq = lax.broadcasted_iota(jnp.int32, (rows, 1), 0) % n_new
    o_ref[...] = jnp.where(sel & ((col - past) <= tq), 0.0, -jnp.inf)


def _select_bias(scores, *, n_new, past, rows_per_step=128):
    rows, n_all = scores.shape
    rows_per_step = min(rows_per_step, rows)
    return pl.pallas_call(
        functools.partial(_select_bias_kernel, n_new=n_new, past=past,
                          topk=min(TOPK_MAX, (past + n_new) // 4)),
        grid=(rows // rows_per_step,),
        in_specs=[pl.BlockSpec((rows_per_step, n_all), lambda i: (i, 0))],
        out_specs=pl.BlockSpec((rows_per_step, n_all), lambda i: (i, 0)),
        out_shape=jax.ShapeDtypeStruct((rows, n_all), F32),
        compiler_params=pltpu.CompilerParams(
            dimension_semantics=("arbitrary",), vmem_limit_bytes=VMEM_LIMIT),
        name="sample_select_bias",
    )(scores)


def _sample_select(page_table, qst, wcol, small, cache_kidx_t, *, n_new):
    n_seq, n_pages = page_table.shape
    ps = min(SELECT_PAGES_PER_STEP, n_pages)
    n_chunks = n_pages // ps
    n_all = n_pages * PAGE_SIZE + LANES
    page_specs = [
        pl.BlockSpec((None, IDX_DIM, PAGE_SIZE), lambda b, c, pt, r=r: (pt[b, ps * c + r], 0, 0))
        for r in range(ps)
    ]
    rows = N_IDX_HEADS * n_new
    grid_spec = pltpu.PrefetchScalarGridSpec(
        num_scalar_prefetch=1,
        grid=(n_seq, n_chunks),
        in_specs=[
            pl.BlockSpec((None, rows, IDX_DIM), lambda b, c, pt: (b, 0, 0)),
            pl.BlockSpec((None, rows, 1), lambda b, c, pt: (b, 0, 0)),
            pl.BlockSpec((n_new, LANES), lambda b, c, pt: (b, 0)),
        ] + page_specs,
        out_specs=pl.BlockSpec((None, n_new, n_all), lambda b, c, pt: (b, 0, 0)),
        scratch_shapes=[pltpu.VMEM((n_pages + 1, n_new, LANES), F32)],
    )
    return pl.pallas_call(
        functools.partial(_sidx_kernel, n_chunks=n_chunks, n_new=n_new, ps=ps),
        grid_spec=grid_spec,
        out_shape=jax.ShapeDtypeStruct((n_seq, n_new, n_all), F32),
        compiler_params=pltpu.CompilerParams(
            dimension_semantics=("arbitrary", "arbitrary"), vmem_limit_bytes=VMEM_LIMIT),
        name="sample_select",
    )(page_table, qst, wcol, small, *([cache_kidx_t] * ps))


def _sattn_kernel(pt_ref, q_ref, bias_ref, biasn_ref, kn_ref, vn_ref, *refs, n_chunks, n_new, ps):
    kpages = refs[:ps]
    vpages = refs[ps:2 * ps]
    o_ref, m_ref, l_ref, acc_ref = refs[2 * ps:]
    c = pl.program_id(1)
    rows = q_ref.shape[0]
    reps = rows // n_new

    @pl.when(c == 0)
    def _():
        m_ref[...] = jnp.full(m_ref.shape, -1e30, F32)
        l_ref[...] = jnp.zeros(l_ref.shape, F32)
        acc_ref[...] = jnp.zeros(acc_ref.shape, F32)

    def update(logits, bias, pv):
        logits = logits + jnp.concatenate([bias] * reps, axis=0)
        m_old = m_ref[...]
        m_new = jnp.maximum(m_old, jnp.max(logits, axis=1, keepdims=True))
        alpha = jnp.exp(m_old - m_new)
        p = jnp.exp(logits - m_new)
        l_ref[...] = alpha * l_ref[...] + jnp.sum(p, axis=1, keepdims=True)
        acc_ref[...] = alpha * acc_ref[...] + pv(p.astype(BF16))
        m_ref[...] = m_new

    sub_keys = SUB_PAGES * PAGE_SIZE
    logits, vts = [], []
    for r0 in range(0, ps, SUB_PAGES):
        kt = jnp.concatenate([kpages[r0 + r][...] for r in range(SUB_PAGES)], axis=1).astype(BF16)
        vts.append(jnp.concatenate([vpages[r0 + r][...] for r in range(SUB_PAGES)], axis=1).astype(BF16))
        logits.append(_dot(q_ref[...], kt))

    def pv(p):
        acc = _dot_nt(p[:, 0:sub_keys], vts[0])
        for n in range(1, len(vts)):
            acc = acc + _dot_nt(p[:, n * sub_keys:(n + 1) * sub_keys], vts[n])
        return acc

    update(jnp.concatenate(logits, axis=1), bias_ref[...], pv)

    @pl.when(c == n_chunks - 1)
    def _():
        pad = jnp.zeros((PAGE_SIZE - n_new, KV_W), F32)
        kn = jnp.concatenate([kn_ref[...], pad], axis=0).astype(BF16)
        vn = jnp.concatenate([vn_ref[...], pad], axis=0).astype(BF16)
        update(_dot_nt(q_ref[...], kn), biasn_ref[...], lambda p: _dot(p, vn))
        o_ref[...] = acc_ref[...] / l_ref[...]


def _sample_attend(page_table, qbd, bias, proj, cache_k_t, cache_v_t, *, n_new):
    n_seq, n_pages = page_table.shape
    ps = min(ATTEND_PAGES_PER_STEP, n_pages)
    n_chunks = n_pages // ps
    rows = qbd.shape[1]
    chunk_keys = ps * PAGE_SIZE
    page_specs = [
        pl.BlockSpec((None, KV_W, PAGE_SIZE), lambda b, c, pt, r=r: (pt[b, ps * c + r], 0, 0))
        for r in range(ps)
    ]
    grid_spec = pltpu.PrefetchScalarGridSpec(
        num_scalar_prefetch=1,
        grid=(n_seq, n_chunks),
        in_specs=[
            pl.BlockSpec((None, rows, KV_W), lambda b, c, pt: (b, 0, 0)),
            pl.BlockSpec((None, n_new, chunk_keys), lambda b, c, pt: (b, 0, c)),
            pl.BlockSpec((None, n_new, LANES), lambda b, c, pt: (b, 0, n_pages)),
            pl.BlockSpec((n_new, KV_W), lambda b, c, pt: (b, 12)),
            pl.BlockSpec((n_new, KV_W), lambda b, c, pt: (b, 13)),
        ] + page_specs + page_specs,
        out_specs=pl.BlockSpec((None, rows, KV_W), lambda b, c, pt: (b, 0, 0)),
        scratch_shapes=[
            pltpu.VMEM((rows, 1), F32),
            pltpu.VMEM((rows, 1), F32),
            pltpu.VMEM((rows, KV_W), F32),
        ],
    )
    return pl.pallas_call(
        functools.partial(_sattn_kernel, n_chunks=n_chunks, n_new=n_new, ps=ps),
        grid_spec=grid_spec,
        out_shape=jax.ShapeDtypeStruct((n_seq, rows, KV_W), F32),
        compiler_params=pltpu.CompilerParams(
            dimension_semantics=("arbitrary", "arbitrary"), vmem_limit_bytes=VMEM_LIMIT),
        name="sample_attend",
    )(page_table, qbd, bias, bias, proj, proj, *([cache_k_t] * ps), *([cache_v_t] * ps))


def _merge_kernel(x_ref, rnn_ref, att_ref, gr_ref, ga_ref, wr_ref, wa_ref, wo_ref, o_ref):
    mixed = (_sigmoid(gr_ref[...]) * _dot(rnn_ref[...].astype(BF16), wr_ref[...])
             + _sigmoid(ga_ref[...]) * _dot(att_ref[...].astype(BF16), wa_ref[...]))
    o_ref[...] = x_ref[...] + _dot(mixed.astype(BF16), wo_ref[...])


def _merge(x, rnn, attn, proj, wr, wa, wo, *, tm):
    t = x.shape[0]
    return pl.pallas_call(
        _merge_kernel,
        grid=(t // tm,),
        in_specs=[
            pl.BlockSpec((tm, D_MODEL), lambda i: (i, 0)),
            pl.BlockSpec((tm, D_RNN), lambda i: (i, 0)),
            pl.BlockSpec((tm, ATTN_W), lambda i: (i, 0)),
            pl.BlockSpec((tm, D_MODEL), lambda i: (i, 2)),
            pl.BlockSpec((tm, D_MODEL), lambda i: (i, 3)),
            pl.BlockSpec((D_RNN, D_MODEL), lambda i: (0, 0)),
            pl.BlockSpec((ATTN_W, D_MODEL), lambda i: (0, 0)),
            pl.BlockSpec((D_MODEL, D_MODEL), lambda i: (0, 0)),
        ],
        out_specs=pl.BlockSpec((tm, D_MODEL), lambda i: (i, 0)),
        out_shape=jax.ShapeDtypeStruct((t, D_MODEL), F32),
        compiler_params=pltpu.CompilerParams(
            dimension_semantics=("arbitrary",), vmem_limit_bytes=VMEM_LIMIT),
        name="merge",
    )(x, rnn, attn, proj, proj, wr, wa, wo)


def _route(h, rwh, rwl, rb):
    lane = lax.broadcasted_iota(jnp.int32, (1, LANES), 1)
    lanef = lane.astype(F32)
    hh, hl = _split_bf16(h)
    lg = (_dot(hh, rwh) + _dot(hl, rwh) + _dot(hh, rwl)) + rb
    is_g = (lane >= N_EXPERTS) & (lane < N_EXPERTS + N_GROUPS)
    gl = jnp.where(is_g, lg, -jnp.inf)
    gmax = jnp.max(gl, axis=1, keepdims=True)
    gprob = 1.0 / jnp.sum(jnp.exp(gl - gmax), axis=1, keepdims=True)
    gsel = jnp.min(jnp.where(is_g & (lg == gmax), lanef - N_EXPERTS, 1e9), axis=1, keepdims=True)
    in_grp = (lane < N_EXPERTS) & (jnp.floor(lanef * (1.0 / EXPERTS_PER_GROUP)) == gsel)
    v1 = jnp.where(in_grp, lg, -jnp.inf)
    t1 = jnp.max(v1, axis=1, keepdims=True)
    i1 = jnp.min(jnp.where(in_grp & (lg == t1), lanef, 1e9), axis=1, keepdims=True)
    rest = in_grp & (lanef != i1)
    v2 = jnp.where(rest, lg, -jnp.inf)
    t2 = jnp.max(v2, axis=1, keepdims=True)
    i2 = jnp.min(jnp.where(rest & (lg == t2), lanef, 1e9), axis=1, keepdims=True)
    d = jnp.exp(t2 - t1)
    return i1, i2, gprob / (1.0 + d), gprob * d / (1.0 + d)


def _moe_kernel(x_ref, ln_ref, rwh_ref, rwl_ref, rb_ref, w1_ref, w3_ref, w2_ref, o_ref, h_ref, gate_ref):
    e = pl.program_id(1)
    lane = lax.broadcasted_iota(jnp.int32, (1, LANES), 1)

    @pl.when(e == 0)
    def _():
        h = _rms(x_ref[...], ln_ref[...])
        h_ref[...] = h.astype(BF16)
        i1, i2, g1, g2 = _route(h, rwh_ref[...], rwl_ref[...], rb_ref[...])
        lanef = lane.astype(F32)
        gate_ref[...] = jnp.where(lanef == i1, g1, 0.0) + jnp.where(lanef == i2, g2, 0.0)

    ge = jnp.sum(jnp.where(lane == e, gate_ref[...], 0.0), axis=1, keepdims=True)
    up = _dot(h_ref[...], w1_ref[...].astype(BF16))
    hid = (up * _sigmoid(up)) * _dot(h_ref[...], w3_ref[...].astype(BF16))
    contrib = _dot((hid * ge).astype(BF16), w2_ref[...].astype(BF16))

    @pl.when(e == 0)
    def _():
        o_ref[...] = x_ref[...] + contrib

    @pl.when(e > 0)
    def _():
        o_ref[...] += contrib


def _moe(x, ln2, rw_hi, rw_lo, rb, w1, w3, w2, *, tm):
    t = x.shape[0]
    return pl.pallas_call(
        _moe_kernel,
        grid=(t // tm, N_EXPERTS),
        in_specs=[
            pl.BlockSpec((tm, D_MODEL), lambda i, e: (i, 0)),
            pl.BlockSpec((1, D_MODEL), lambda i, e: (0, 0)),
            pl.BlockSpec((D_MODEL, LANES), lambda i, e: (0, 0)),
            pl.BlockSpec((D_MODEL, LANES), lambda i, e: (0, 0)),
            pl.BlockSpec((1, LANES), lambda i, e: (0, 0)),
            pl.BlockSpec((None, D_MODEL, D_EXPERT), lambda i, e: (e, 0, 0)),
            pl.BlockSpec((None, D_MODEL, D_EXPERT), lambda i, e: (e, 0, 0)),
            pl.BlockSpec((None, D_EXPERT, D_MODEL), lambda i, e: (e, 0, 0)),
        ],
        out_specs=pl.BlockSpec((tm, D_MODEL), lambda i, e: (i, 0)),
        out_shape=jax.ShapeDtypeStruct((t, D_MODEL), F32),
        scratch_shapes=[pltpu.VMEM((tm, D_MODEL), BF16), pltpu.VMEM((tm, LANES), F32)],
        compiler_params=pltpu.CompilerParams(
            dimension_semantics=("arbitrary", "arbitrary"), vmem_limit_bytes=VMEM_LIMIT),
        name="moe",
    )(x, ln2, rw_hi, rw_lo, rb, w1, w3, w2)


def _ple_update(x2, p, ln, wg, wp):
    gate = _sigmoid(_dot(_rms(x2, ln).astype(BF16), wg))
    return x2 + gate * _dot(p.astype(BF16), wp)


def _ple_kernel(x_ref, p_ref, ln_ref, wg_ref, wp_ref, o_ref):
    o_ref[...] = _ple_update(x_ref[...], p_ref[...], ln_ref[...], wg_ref[...], wp_ref[...])


def _ple(x, p, ln3, wg, wp, *, tm):
    t = x.shape[0]
    return pl.pallas_call(
        _ple_kernel,
        grid=(t // tm,),
        in_specs=[
            pl.BlockSpec((tm, D_MODEL), lambda i: (i, 0)),
            pl.BlockSpec((tm, PLE_DIM), lambda i: (i, 0)),
            pl.BlockSpec((1, D_MODEL), lambda i: (0, 0)),
            pl.BlockSpec((D_MODEL, D_MODEL), lambda i: (0, 0)),
            pl.BlockSpec((PLE_DIM, D_MODEL), lambda i: (0, 0)),
        ],
        out_specs=pl.BlockSpec((tm, D_MODEL), lambda i: (i, 0)),
        out_shape=jax.ShapeDtypeStruct((t, D_MODEL), F32),
        compiler_params=pltpu.CompilerParams(
            dimension_semantics=("arbitrary",), vmem_limit_bytes=VMEM_LIMIT),
        name="ple",
    )(x, p, ln3, wg, wp)


MOE_ROW_TILE = 256
META_E, META_G, META_RANK = 0, 2, 4


def _router_kernel(x_ref, ln_ref, rwh_ref, rwl_ref, rb_ref, tri_ref, meta_ref, cnt_ref, carry_ref):
    i = pl.program_id(0)
    lane = lax.broadcasted_iota(jnp.int32, (1, LANES), 1)
    lanef = lane.astype(F32)

    @pl.when(i == 0)
    def _():
        carry_ref[...] = jnp.zeros(carry_ref.shape, F32)

    i1, i2, g1, g2 = _route(_rms(x_ref[...], ln_ref[...]), rwh_ref[...], rwl_ref[...], rb_ref[...])
    onehot = jnp.where((lanef == i1) | (lanef == i2), 1.0, 0.0)
    before = _dot(tri_ref[...], onehot.astype(BF16)) + carry_ref[...]
    r1 = jnp.sum(jnp.where(lanef == i1, before, 0.0), axis=1, keepdims=True)
    r2 = jnp.sum(jnp.where(lanef == i2, before, 0.0), axis=1, keepdims=True)
    carry_ref[...] += jnp.sum(onehot, axis=0, keepdims=True)
    rec = jnp.zeros((x_ref.shape[0], LANES), F32)
    for k, val in ((META_E, i1), (META_E + 1, i2), (META_G, g1), (META_G + 1, g2),
                   (META_RANK, r1), (META_RANK + 1, r2)):
        rec = jnp.where(lane == k, val, rec)
    meta_ref[...] = rec
    cnt_ref[...] = jnp.broadcast_to(carry_ref[...], cnt_ref.shape)


def _router(x, ln2, rw_hi, rw_lo, rb, *, tm):
    t = x.shape[0]
    r = lax.broadcasted_iota(jnp.int32, (tm, tm), 0)
    c = lax.broadcasted_iota(jnp.int32, (tm, tm), 1)
    tri = (c < r).astype(BF16)
    return pl.pallas_call(
        _router_kernel,
        grid=(t // tm,),
        in_specs=[
            pl.BlockSpec((tm, D_MODEL), lambda i: (i, 0)),
            pl.BlockSpec((1, D_MODEL), lambda i: (0, 0)),
            pl.BlockSpec((D_MODEL, LANES), lambda i: (0, 0)),
            pl.BlockSpec((D_MODEL, LANES), lambda i: (0, 0)),
            pl.BlockSpec((1, LANES), lambda i: (0, 0)),
            pl.BlockSpec((tm, tm), lambda i: (0, 0)),
        ],
        out_specs=[
            pl.BlockSpec((tm, LANES), lambda i: (i, 0)),
            pl.BlockSpec((8, LANES), lambda i: (0, 0)),
        ],
        out_shape=[
            jax.ShapeDtypeStruct((t, LANES), F32),
            jax.ShapeDtypeStruct((8, LANES), F32),
        ],
        scratch_shapes=[pltpu.VMEM((1, LANES), F32)],
        compiler_params=pltpu.CompilerParams(
            dimension_semantics=("arbitrary",), vmem_limit_bytes=VMEM_LIMIT),
        name="moe_router",
    )(x, ln2, rw_hi, rw_lo, rb, tri)


def _row_copy(src_ref, src_row, dst_ref, dst_row, sem):
    return pltpu.make_async_copy(src_ref.at[pl.ds(src_row, 1), :], dst_ref.at[pl.ds(dst_row, 1), :], sem)


def _scatter_kernel(slot_ref, x_ref, hs_in_ref, hs_ref, sem, *, n_tok):
    del hs_in_ref
    tm = x_ref.shape[0]
    base = pl.program_id(0) * tm

    def start(r, carry):
        for k in range(2):
            _row_copy(x_ref, r, hs_ref, slot_ref[k * n_tok + base + r], sem).start()
        return carry

    lax.fori_loop(0, tm, start, 0)
    for k in range(2):
        pltpu.make_async_copy(x_ref, hs_ref.at[pl.ds(0, tm), :], sem).wait()


def _scatter_rows(slots, x, hs_zero, *, tm):
    t = x.shape[0]
    grid_spec = pltpu.PrefetchScalarGridSpec(
        num_scalar_prefetch=1,
        grid=(t // tm,),
        in_specs=[
            pl.BlockSpec((tm, D_MODEL), lambda i, s: (i, 0)),
            pl.BlockSpec(memory_space=pl.ANY),
        ],
        out_specs=pl.BlockSpec(memory_space=pl.ANY),
        scratch_shapes=[pltpu.SemaphoreType.DMA(())],
    )
    return pl.pallas_call(
        functools.partial(_scatter_kernel, n_tok=t),
        grid_spec=grid_spec,
        out_shape=jax.ShapeDtypeStruct(hs_zero.shape, F32),
        input_output_aliases={2: 0},
        compiler_params=pltpu.CompilerParams(
            dimension_semantics=("arbitrary",), vmem_limit_bytes=VMEM_LIMIT),
        name="moe_scatter",
    )(slots, x, hs_zero)


def _expert_kernel(te_ref, nu_ref, hs_ref, ln_ref, w1_ref, w3_ref, w2_ref, y_ref):
    i = pl.program_id(0)

    @pl.when(i < nu_ref[0])
    def _():
        h = _rms(hs_ref[...], ln_ref[...]).astype(BF16)
        up = _dot(h, w1_ref[...].astype(BF16))
        hid = (up * _sigmoid(up)) * _dot(h, w3_ref[...].astype(BF16))
        y_ref[...] = _dot(hid.astype(BF16), w2_ref[...].astype(BF16))

    @pl.when(i >= nu_ref[0])
    def _():
        y_ref[...] = jnp.zeros(y_ref.shape, F32)


def _expert_mlp(tile_expert, n_used, hs, ln2, w1, w3, w2):
    n_tiles = tile_expert.shape[0]
    grid_spec = pltpu.PrefetchScalarGridSpec(
        num_scalar_prefetch=2,
        grid=(n_tiles,),
        in_specs=[
            pl.BlockSpec((MOE_ROW_TILE, D_MODEL), lambda i, te, nu: (i, 0)),
            pl.BlockSpec((1, D_MODEL), lambda i, te, nu: (0, 0)),
            pl.BlockSpec((None, D_MODEL, D_EXPERT), lambda i, te, nu: (te[i], 0, 0)),
            pl.BlockSpec((None, D_MODEL, D_EXPERT), lambda i, te, nu: (te[i], 0, 0)),
            pl.BlockSpec((None, D_EXPERT, D_MODEL), lambda i, te, nu: (te[i], 0, 0)),
        ],
        out_specs=pl.BlockSpec((MOE_ROW_TILE, D_MODEL), lambda i, te, nu: (i, 0)),
    )
    return pl.pallas_call(
        _expert_kernel,
        grid_spec=grid_spec,
        out_shape=jax.ShapeDtypeStruct(hs.shape, F32),
        compiler_params=pltpu.CompilerParams(
            dimension_semantics=("arbitrary",), vmem_limit_bytes=VMEM_LIMIT),
        name="moe_experts",
    )(tile_expert, n_used, hs, ln2, w1, w3, w2)


def _combine_ple_kernel(slot_ref, x_ref, meta_ref, p_ref, ln_ref, wg_ref, wp_ref, y_ref, o_ref,
                        ybuf_ref, sem, *, n_tok):
    i = pl.program_id(0)
    tm = x_ref.shape[0]

    def gather(tile, buf):
        def start(r, carry):
            for k in range(2):
                _row_copy(y_ref, slot_ref[k * n_tok + tile * tm + r], ybuf_ref.at[buf, k], r,
                          sem.at[buf]).start()
            return carry
        lax.fori_loop(0, tm, start, 0)

    @pl.when(i == 0)
    def _():
        gather(0, 0)

    @pl.when(i + 1 < pl.num_programs(0))
    def _():
        gather(i + 1, (i + 1) % 2)

    cur = i % 2
    for k in range(2):
        pltpu.make_async_copy(y_ref.at[pl.ds(0, tm), :], ybuf_ref.at[cur, k], sem.at[cur]).wait()
    meta = meta_ref[...]
    x2 = (x_ref[...] + meta[:, META_G:META_G + 1] * ybuf_ref[cur, 0]
          + meta[:, META_G + 1:META_G + 2] * ybuf_ref[cur, 1])
    o_ref[...] = _ple_update(x2, p_ref[...], ln_ref[...], wg_ref[...], wp_ref[...])


def _combine_ple(slots, x, meta, p, ln3, wg, wp, y, *, tm):
    t = x.shape[0]
    grid_spec = pltpu.PrefetchScalarGridSpec(
        num_scalar_prefetch=1,
        grid=(t // tm,),
        in_specs=[
            pl.BlockSpec((tm, D_MODEL), lambda i, s: (i, 0)),
            pl.BlockSpec((tm, LANES), lambda i, s: (i, 0)),
            pl.BlockSpec((tm, PLE_DIM), lambda i, s: (i, 0)),
            pl.BlockSpec((1, D_MODEL), lambda i, s: (0, 0)),
            pl.BlockSpec((D_MODEL, D_MODEL), lambda i, s: (0, 0)),
            pl.BlockSpec((PLE_DIM, D_MODEL), lambda i, s: (0, 0)),
            pl.BlockSpec(memory_space=pl.ANY),
        ],
        out_specs=pl.BlockSpec((tm, D_MODEL), lambda i, s: (i, 0)),
        scratch_shapes=[
            pltpu.VMEM((2, 2, tm, D_MODEL), F32),
            pltpu.SemaphoreType.DMA((2,)),
        ],
    )
    return pl.pallas_call(
        functools.partial(_combine_ple_kernel, n_tok=t),
        grid_spec=grid_spec,
        out_shape=jax.ShapeDtypeStruct((t, D_MODEL), F32),
        compiler_params=pltpu.CompilerParams(
            dimension_semantics=("arbitrary",), vmem_limit_bytes=VMEM_LIMIT),
        name="moe_combine_ple",
    )(slots, x, meta, p, ln3, wg, wp, y)


def _sparse_moe_ple(x, p, w):
    t = x.shape[0]
    rt = MOE_ROW_TILE
    meta, cnt = _router(x, w["ln2"], w["rw_hi"], w["rw_lo"], w["rb"], tm=256)
    counts = cnt[0, :N_EXPERTS].astype(jnp.int32)
    padded = ((counts + rt - 1) // rt) * rt
    ends = jnp.cumsum(padded)
    offs = ends - padded
    eid = meta[:, META_E:META_E + 2].astype(jnp.int32)
    rank = meta[:, META_RANK:META_RANK + 2].astype(jnp.int32)
    base = jnp.sum(jnp.where(eid[:, :, None] == jnp.arange(N_EXPERTS), offs, 0), axis=-1)
    slots = jnp.transpose(base + rank).reshape(2 * t)
    n_tiles = (2 * t + N_EXPERTS * (rt - 1)) // rt
    tile_start = jnp.arange(n_tiles, dtype=jnp.int32) * rt
    tile_expert = jnp.minimum(jnp.sum((tile_start[:, None] >= ends[None, :]).astype(jnp.int32), axis=1),
                              N_EXPERTS - 1)
    n_used = (ends[N_EXPERTS - 1] // rt).reshape(1)
    hs = _scatter_rows(slots, x, jnp.zeros((n_tiles * rt, D_MODEL), F32), tm=256)
    y = _expert_mlp(tile_expert, n_used, hs, w["ln2"], w["w1"], w["w3"], w["w2"])
    return _combine_ple(slots, x, meta, p, w["ln3"], w["wg"], w["wp"], y, tm=256)


def _rope_tables(pos):
    half = HEAD_DIM // 2
    inv = ROPE_THETA ** (-jnp.arange(half, dtype=F32) / half)
    ang = pos.astype(F32)[:, None] * inv[None, :]
    cos, sin = jnp.cos(ang), jnp.sin(ang)
    return (jnp.concatenate([cos, cos, cos, cos], axis=1),
            jnp.concatenate([-sin, sin, -sin, sin], axis=1))


def _block_diag(w, per):
    n, r, _ = w.shape
    eye = jnp.eye(per, dtype=w.dtype)
    wg = w.reshape(n // per, per, r, r)
    return jnp.einsum("gpij,pq->gpiqj", wg, eye).reshape(n // per, per * r, per * r)


def _layer_weights(ln1, w_in, q_norm, k_norm, conv_w, conv_b, w_a, b_a, w_x, b_x, lam, w_br_rnn,
                   w_br_attn, w_out, ln2, w_rg, b_rg, w_re, b_re, w1, w3, w2, ln3, w_ple_gate,
                   w_ple_proj):
    o_q = 2 * D_RNN
    o_k = o_q + ATTN_W
    o_v = o_k + KV_W
    o_qi = o_v + KV_W
    o_ki = o_qi + N_IDX_HEADS * IDX_DIM
    o_wi = o_ki + IDX_DIM
    o_gr = o_wi + N_IDX_HEADS
    o_ga = o_gr + D_MODEL
    assert o_ki == COL_NR_END
    w_gates = w_in[:, o_gr:o_ga + D_MODEL]
    w_small = jnp.concatenate(
        [w_in[:, o_ki:o_gr], jnp.zeros((D_MODEL, LANES - IDX_DIM - N_IDX_HEADS), F32)], axis=1)
    ones = lambda n: jnp.ones((n,), F32)
    zeros = lambda n: jnp.zeros((n,), F32)
    n_gate = 2 * D_MODEL
    gain = jnp.concatenate([ones(o_q), jnp.tile(q_norm, N_HEADS), jnp.tile(k_norm, N_KV_HEADS),
                            ones(KV_W + N_IDX_HEADS * IDX_DIM + n_gate)])
    norm_on = jnp.concatenate([zeros(o_q), ones(ATTN_W + KV_W), zeros(KV_W + N_IDX_HEADS * IDX_DIM + n_gate)])
    rope_on = jnp.concatenate([zeros(o_q), ones(ATTN_W + KV_W), zeros(KV_W), ones(N_IDX_HEADS * IDX_DIM),
                               zeros(n_gate)])
    post = jnp.concatenate([ones(o_q), jnp.full((ATTN_W,), QK_SCALE, F32),
                            ones(2 * KV_W + N_IDX_HEADS * IDX_DIM + n_gate)])
    colctl = jnp.concatenate([jnp.stack([gain, norm_on, rope_on, post]), jnp.zeros((4, N_MAIN), F32)], axis=0)
    tn = 512
    head_of = jnp.arange(tn) // HEAD_DIM
    bd = (head_of[:, None] == head_of[None, :]).astype(BF16)
    rw = jnp.concatenate([w_re, w_rg, jnp.zeros((D_MODEL, LANES - N_EXPERTS - N_GROUPS), F32)], axis=1)
    rw_hi = rw.astype(BF16)
    rw_lo = (rw - rw_hi.astype(F32)).astype(BF16)
    rb = jnp.concatenate([b_re, b_rg, jnp.zeros((LANES - N_EXPERTS - N_GROUPS,), F32)])[None, :]
    return dict(
        ln1=ln1[None, :], w_in=w_in, w_gates=w_gates, w_small=w_small, colctl=colctl, bd=bd,
        cw=conv_w, cb=conv_b[None, :],
        wa_bd=_block_diag(w_a, 4).astype(BF16), ba=b_a[None, :],
        wx_bd=_block_diag(w_x, 4).astype(BF16), bx=b_x[None, :], lam=lam[None, :],
        wr=w_br_rnn.astype(BF16), wa=w_br_attn.astype(BF16), wo=w_out.astype(BF16),
        ln2=ln2[None, :], rw_hi=rw_hi, rw_lo=rw_lo, rb=rb,
        w1=w1, w3=w3, w2=w2,
        ln3=ln3[None, :], wg=w_ple_gate.astype(BF16), wp=w_ple_proj.astype(BF16),
    )


def _tail(x, rnn, attn, proj, p, w):
    t = x.shape[0]
    x1 = _merge(x, rnn, attn, proj, w["wr"], w["wa"], w["wo"], tm=min(t, 256))
    if 2 * t >= N_EXPERTS * MOE_ROW_TILE:
        return _sparse_moe_ple(x1, p, w)
    x2 = _moe(x1, w["ln2"], w["rw_hi"], w["rw_lo"], w["rb"], w["w1"], w["w3"], w["w2"], tm=min(t, 512))
    return _ple(x2, p, w["ln3"], w["wg"], w["wp"], tm=min(t, 512))


def _prompt_layer(x, p, w):
    bp, tp, _ = x.shape
    xt = x.reshape(bp * tp, D_MODEL)
    cs, sn = _rope_tables(jnp.arange(tp, dtype=jnp.int32))
    proj, small = _inproj(xt, w["ln1"], w["w_in"], w["w_gates"], w["w_small"], w["colctl"], cs, sn,
                          w["bd"], tm=min(tp, 1024))
    conv0 = jnp.zeros((bp, 8, D_RNN), F32)
    h0 = jnp.zeros((bp, 1, D_RNN), F32)
    rnn, h_last = _rglru(proj, conv0, h0, w["cw"], w["cb"], w["wa_bd"], w["ba"], w["wx_bd"], w["bx"],
                         w["lam"], n_seq=bp, tt=min(tp, 256))
    attn = _prompt_attention(proj, small, n_batch=bp, seq=tp)
    y = _tail(xt, rnn, attn, proj, p.reshape(bp * tp, PLE_DIM), w)
    o_k = 2 * D_RNN + ATTN_W
    k = proj[:, o_k:o_k + KV_W].reshape(bp, tp, N_KV_HEADS, HEAD_DIM)
    v = proj[:, o_k + KV_W:o_k + 2 * KV_W].reshape(bp, tp, N_KV_HEADS, HEAD_DIM)
    ki = small[:, :IDX_DIM].reshape(bp, tp, IDX_DIM)
    conv_new = proj.reshape(bp, tp, N_MAIN)[:, tp - (CONV_W - 1):, :D_RNN]
    return y.reshape(bp, tp, D_MODEL), (k, v, ki, conv_new, h_last.reshape(bp, D_RNN))


def _sample_layer(x, p, cache_k, cache_v, cache_kidx, state_conv, state_h, page_table, w):
    bs, ts, _ = x.shape
    n_pages = page_table.shape[1]
    past = n_pages * PAGE_SIZE
    xt = x.reshape(bs * ts, D_MODEL)
    cs, sn = _rope_tables(past + jnp.tile(jnp.arange(ts, dtype=jnp.int32), bs))
    proj, small = _inproj(xt, w["ln1"], w["w_in"], w["w_gates"], w["w_small"], w["colctl"], cs, sn,
                          w["bd"], tm=bs * ts)
    conv0 = jnp.concatenate([jnp.zeros((bs, 8 - (CONV_W - 1), D_RNN), F32), state_conv], axis=1)
    rnn, h_last = _rglru(proj, conv0, state_h[:, None, :], w["cw"], w["cb"], w["wa_bd"], w["ba"],
                         w["wx_bd"], w["bx"], w["lam"], n_seq=bs, tt=ts)
    o_q = 2 * D_RNN
    o_qi = o_q + ATTN_W + 2 * KV_W
    qi = proj[:, o_qi:o_qi + N_IDX_HEADS * IDX_DIM].reshape(bs, ts, N_IDX_HEADS, IDX_DIM)
    qst = jnp.transpose(qi, (0, 2, 1, 3)).reshape(bs, N_IDX_HEADS * ts, IDX_DIM).astype(BF16)
    wi = small[:, IDX_DIM:IDX_DIM + N_IDX_HEADS].reshape(bs, ts, N_IDX_HEADS)
    wcol = jnp.transpose(wi, (0, 2, 1)).reshape(bs, N_IDX_HEADS * ts, 1)
    n_pool = cache_k.shape[0]
    kidx_t = jnp.transpose(cache_kidx, (0, 2, 1))
    k_t = jnp.transpose(cache_k, (0, 2, 3, 1)).reshape(n_pool, KV_W, PAGE_SIZE)
    v_t = jnp.transpose(cache_v, (0, 2, 3, 1)).reshape(n_pool, KV_W, PAGE_SIZE)
    scores = _sample_select(page_table, qst, wcol, small, kidx_t, n_new=ts)
    bias = _select_bias(scores.reshape(bs * ts, -1), n_new=ts, past=past).reshape(scores.shape)
    q = proj[:, o_q:o_q + ATTN_W].reshape(bs, ts, N_KV_HEADS, N_HEADS // N_KV_HEADS, HEAD_DIM)
    eye = jnp.eye(N_KV_HEADS, dtype=F32)
    qbd = jnp.einsum("btgjd,gk->bgjtkd", q, eye).reshape(bs, N_HEADS * ts, KV_W).astype(BF16)
    att = _sample_attend(page_table, qbd, bias, proj, k_t, v_t, n_new=ts)
    att = att.reshape(bs, N_KV_HEADS, N_HEADS // N_KV_HEADS, ts, N_KV_HEADS, HEAD_DIM)
    att = jnp.stack([att[:, g, :, :, g, :] for g in range(N_KV_HEADS)], axis=1)
    attn = jnp.transpose(att, (0, 3, 1, 2, 4)).reshape(bs * ts, ATTN_W)
    y = _tail(xt, rnn, attn, proj, p.reshape(bs * ts, PLE_DIM), w)
    o_k = o_q + ATTN_W
    k = proj[:, o_k:o_k + KV_W].reshape(bs, ts, N_KV_HEADS, HEAD_DIM)
    v = proj[:, o_k + KV_W:o_k + 2 * KV_W].reshape(bs, ts, N_KV_HEADS, HEAD_DIM)
    ki = small[:, :IDX_DIM].reshape(bs, ts, IDX_DIM)
    conv_new = proj.reshape(bs, ts, N_MAIN)[:, ts - (CONV_W - 1):, :D_RNN]
    return y.reshape(bs, ts, D_MODEL), (k, v, ki, conv_new, h_last.reshape(bs, D_RNN))


def kernel(x_prompt, x_sample, p_prompt, p_sample, cache_k, cache_v, cache_kidx, state_conv, state_h,
           page_table, ln1, w_in, q_norm, k_norm, conv_w, conv_b, w_a, b_a, w_x, b_x, lam, w_br_rnn,
           w_br_attn, w_out, ln2, w_rg, b_rg, w_re, b_re, w1, w3, w2, ln3, w_ple_gate, w_ple_proj):
    weights = (ln1, w_in, q_norm, k_norm, conv_w, conv_b, w_a, b_a, w_x, b_x, lam, w_br_rnn, w_br_attn,
               w_out, ln2, w_rg, b_rg, w_re, b_re, w1, w3, w2, ln3, w_ple_gate, w_ple_proj)
    depth = ln1.shape[0]
    yp, ys = x_prompt, x_sample
    st_p, st_s = [], []
    for i in range(depth):
        w = _layer_weights(*[wt[i] for wt in weights])
        yp, sp = _prompt_layer(yp, p_prompt[i], w)
        ys, ss = _sample_layer(ys, p_sample[i], cache_k[i], cache_v[i], cache_kidx[i], state_conv[i],
                               state_h[i], page_table, w)
        st_p.append(sp)
        st_s.append(ss)
    stack = lambda sts, j: jnp.stack([s[j] for s in sts])
    return (yp, ys, stack(st_p, 0), stack(st_p, 1), stack(st_p, 2), stack(st_p, 3), stack(st_p, 4),
            stack(st_s, 0), stack(st_s, 1), stack(st_s, 2), stack(st_s, 3), stack(st_s, 4))
```

```python
import functools

import jax
import jax.numpy as jnp
import numpy as np
from jax import lax
from jax.experimental import pallas as pl
from jax.experimental.pallas import tpu as pltpu

F32 = jnp.float32
BF16 = jnp.bfloat16

D_MODEL = 2048
HEAD_DIM = 64
N_HEADS = 16
N_KV_HEADS = 4
ATTN_W = N_HEADS * HEAD_DIM
KV_W = N_KV_HEADS * HEAD_DIM
N_IDX_HEADS = 8
IDX_DIM = 64
TOPK_MAX = 256
ROPE_THETA = 10000.0
D_RNN = 1024
N_RNN_BLOCKS = 16
RNN_BLOCK = 64
CONV_W = 4
LRU_C = 8.0
N_GROUPS = 4
EXPERTS_PER_GROUP = 8
N_EXPERTS = 32
D_EXPERT = 256
PLE_DIM = 256
PAGE_SIZE = 128
EPS = 1e-6

LANES = 128
N_MAIN = 8192
COL_RAW_END = 2 * D_RNN
COL_NR_END = 4096
IDX_SCALE = (IDX_DIM ** -0.5) * (N_IDX_HEADS ** -0.5)
QK_SCALE = HEAD_DIM ** -0.5
F32_MIN = float(np.finfo(np.float32).min)
INT_MIN = -2147483648
KEY_NEG_INF = INT_MIN + 0x7FFFFF
VMEM_LIMIT = 56 * 1024 * 1024


def _dot(a, b):
    return jnp.dot(a, b, preferred_element_type=F32)


def _dot_nt(a, b):
    return lax.dot_general(a, b, (((1,), (1,)), ((), ())), preferred_element_type=F32)


def _sigmoid(x):
    return 1.0 / (1.0 + jnp.exp(-x))


def _rms(x, g):
    return x * lax.rsqrt(jnp.mean(x * x, axis=-1, keepdims=True) + EPS) * g


def _split_bf16(x):
    hi = x.astype(BF16)
    lo = (x - hi.astype(F32)).astype(BF16)
    return hi, lo


def _rope_chunks(y, c, s):
    lane = lax.broadcasted_iota(jnp.int32, (1, LANES), 1)
    first_half = (lane % HEAD_DIM) < (HEAD_DIM // 2)
    outs = []
    for k in range(y.shape[1] // LANES):
        yc = y[:, k * LANES:(k + 1) * LANES]
        partner = jnp.where(first_half, pltpu.roll(yc, LANES - HEAD_DIM // 2, 1),
                            pltpu.roll(yc, HEAD_DIM // 2, 1))
        outs.append(yc * c + partner * s)
    return outs[0] if len(outs) == 1 else jnp.concatenate(outs, axis=1)


def _inproj_kernel(x_ref, ln_ref, wlo_ref, whi_ref, ws_ref, ctl_ref, cs_ref, sn_ref, bd_ref,
                   o_ref, os_ref, kvb_ref, kib_ref, h_ref, *, tn):
    j = pl.program_id(1)

    @pl.when(j == 0)
    def _():
        hb = _rms(x_ref[...], ln_ref[...]).astype(BF16)
        h_ref[...] = hb
        ys = _dot(hb, ws_ref[...].astype(BF16))
        lane = lax.broadcasted_iota(jnp.int32, (1, LANES), 1)
        roped = _rope_chunks(ys, cs_ref[...], sn_ref[...])
        os_ref[...] = jnp.where(lane < IDX_DIM, roped, ys)
        kib_ref[...] = jnp.where(lane < IDX_DIM, roped, 0.0).astype(BF16)

    @pl.when(j < COL_NR_END // tn)
    def _():
        o_ref[...] = _dot(h_ref[...], wlo_ref[...].astype(BF16))

    @pl.when(j >= COL_NR_END // tn)
    def _():
        o_ref[...] = _dot(h_ref[...], whi_ref[...].astype(BF16))

    @pl.when((j >= COL_RAW_END // tn) & (j < COL_NR_END // tn))
    def _():
        y = o_ref[...]
        ctl = ctl_ref[...]
        gain, norm_on, rope_on, post = ctl[0:1], ctl[1:2], ctl[2:3], ctl[3:4]
        hi, lo = _split_bf16(y * y)
        ss = _dot(hi, bd_ref[...]) + _dot(lo, bd_ref[...])
        yn = jnp.where(norm_on > 0.0, y * lax.rsqrt(ss * (1.0 / HEAD_DIM) + EPS) * gain, y)
        yr = jnp.where(rope_on > 0.0, _rope_chunks(yn, cs_ref[...], sn_ref[...]), yn)
        o_ref[...] = yr * post

    @pl.when(j == (COL_RAW_END + ATTN_W) // tn)
    def _():
        kvb_ref[...] = o_ref[...].astype(BF16)


def _inproj(x, ln1, w_in, w_gates, w_small, colctl, cs, sn, bd, *, tm, tn=512):
    t = x.shape[0]
    assert tn == 2 * KV_W
    grid = (t // tm, N_MAIN // tn)
    n_lo = COL_NR_END // tn
    n_rope = cs.shape[0] // tm
    return pl.pallas_call(
        functools.partial(_inproj_kernel, tn=tn),
        grid=grid,
        in_specs=[
            pl.BlockSpec((tm, D_MODEL), lambda i, j: (i, 0)),
            pl.BlockSpec((1, D_MODEL), lambda i, j: (0, 0)),
            pl.BlockSpec((D_MODEL, tn), lambda i, j: (0, jnp.minimum(j, n_lo - 1))),
            pl.BlockSpec((D_MODEL, tn), lambda i, j: (0, jnp.maximum(j - n_lo, 0))),
            pl.BlockSpec((D_MODEL, LANES), lambda i, j: (0, 0)),
            pl.BlockSpec((8, tn), lambda i, j: (0, j)),
            pl.BlockSpec((tm, LANES), lambda i, j: (i % n_rope, 0)),
            pl.BlockSpec((tm, LANES), lambda i, j: (i % n_rope, 0)),
            pl.BlockSpec((tn, tn), lambda i, j: (0, 0)),
        ],
        out_specs=[
            pl.BlockSpec((tm, tn), lambda i, j: (i, j)),
            pl.BlockSpec((tm, LANES), lambda i, j: (i, 0)),
            pl.BlockSpec((tm, 2 * KV_W), lambda i, j: (i, 0)),
            pl.BlockSpec((tm, LANES), lambda i, j: (i, 0)),
        ],
        out_shape=[
            jax.ShapeDtypeStruct((t, N_MAIN), F32),
            jax.ShapeDtypeStruct((t, LANES), F32),
            jax.ShapeDtypeStruct((t, 2 * KV_W), BF16),
            jax.ShapeDtypeStruct((t, LANES), BF16),
        ],
        scratch_shapes=[pltpu.VMEM((tm, D_MODEL), BF16)],
        compiler_params=pltpu.CompilerParams(
            dimension_semantics=("arbitrary", "arbitrary"), vmem_limit_bytes=VMEM_LIMIT),
        name="inproj",
    )(x, ln1, w_in, w_gates, w_small, colctl, cs, sn, bd)


def _rglru_kernel(x_ref, g_ref, c0_ref, h0_ref, cw_ref, cb_ref, wa_ref, ba_ref, wx_ref, bx_ref,
                  lam_ref, o_ref, hl_ref, xs_ref, a_ref, b_ref, hc_ref):
    t = pl.program_id(1)
    tt = x_ref.shape[0]

    @pl.when(t == 0)
    def _():
        xs_ref[0:8, :] = c0_ref[...]
        hc_ref[...] = h0_ref[...]

    xs_ref[8:8 + tt, :] = x_ref[...]
    cw = cw_ref[...]
    taps = (xs_ref[5:5 + tt, :] * cw[0:1] + xs_ref[6:6 + tt, :] * cw[1:2]
            + xs_ref[7:7 + tt, :] * cw[2:3] + xs_ref[8:8 + tt, :] * cw[3:4])
    xc = cb_ref[...] + taps
    xs_ref[0:8, :] = xs_ref[tt:tt + 8, :]

    xcb = xc.astype(BF16)
    ra, ri = [], []
    for c in range(wa_ref.shape[0]):
        blk = xcb[:, c * 256:(c + 1) * 256]
        ra.append(_dot(blk, wa_ref[c]))
        ri.append(_dot(blk, wx_ref[c]))
    r = _sigmoid(jnp.concatenate(ra, axis=1) + ba_ref[...])
    ig = _sigmoid(jnp.concatenate(ri, axis=1) + bx_ref[...])
    nlam = -lam_ref[...]
    softplus = jnp.maximum(nlam, 0.0) + jnp.log1p(jnp.exp(-jnp.abs(nlam)))
    log_a = (-LRU_C) * r * softplus
    a = jnp.exp(log_a)
    u = jnp.sqrt(jnp.tanh(-log_a) * (a * a + 1.0)) * (ig * xc)

    n8 = tt // 8
    a3 = a.reshape(n8, 8, D_RNN)
    b3 = u.reshape(n8, 8, D_RNN)
    sub = lax.broadcasted_iota(jnp.int32, (1, 8, 1), 1)
    for s in (1, 2, 4):
        a_prev = pltpu.roll(a3, s, 1)
        b_prev = pltpu.roll(b3, s, 1)
        m = sub >= s
        b3 = jnp.where(m, a3 * b_prev + b3, b3)
        a3 = jnp.where(m, a3 * a_prev, a3)
    a_ref[...] = a3.reshape(tt, D_RNN)
    b_ref[...] = b3.reshape(tt, D_RNN)

    def chain(k, carry):
        i0 = pl.multiple_of(k * 8, 8)
        h8 = a_ref[pl.ds(i0, 8), :] * carry + b_ref[pl.ds(i0, 8), :]
        b_ref[pl.ds(i0, 8), :] = h8
        return h8[7:8, :]

    carry = lax.fori_loop(0, n8, chain, hc_ref[...])
    hc_ref[...] = carry
    g = g_ref[...]
    gelu = 0.5 * g * (1.0 + jnp.tanh(0.7978845608028654 * (g + 0.044715 * (g * g * g))))
    o_ref[...] = b_ref[...] * gelu

    @pl.when(t == pl.num_programs(1) - 1)
    def _():
        hl_ref[...] = carry


def _rglru(proj, conv0, h0, cw, cb, wa_bd, ba, wx_bd, bx, lam, *, n_seq, tt):
    t_total = proj.shape[0]
    nt = t_total // (n_seq * tt)
    full = lambda shape: pl.BlockSpec(shape, lambda b, t: (0,) * len(shape))
    return pl.pallas_call(
        _rglru_kernel,
        grid=(n_seq, nt),
        in_specs=[
            pl.BlockSpec((tt, D_RNN), lambda b, t: (b * nt + t, 0)),
            pl.BlockSpec((tt, D_RNN), lambda b, t: (b * nt + t, 1)),
            pl.BlockSpec((None, 8, D_RNN), lambda b, t: (b, 0, 0)),
            pl.BlockSpec((None, 1, D_RNN), lambda b, t: (b, 0, 0)),
            full((CONV_W, D_RNN)), full((1, D_RNN)),
            full(wa_bd.shape), full((1, D_RNN)),
            full(wx_bd.shape), full((1, D_RNN)),
            full((1, D_RNN)),
        ],
        out_specs=[
            pl.BlockSpec((tt, D_RNN), lambda b, t: (b * nt + t, 0)),
            pl.BlockSpec((None, 1, D_RNN), lambda b, t: (b, 0, 0)),
        ],
        out_shape=[
            jax.ShapeDtypeStruct((t_total, D_RNN), F32),
            jax.ShapeDtypeStruct((n_seq, 1, D_RNN), F32),
        ],
        scratch_shapes=[
            pltpu.VMEM((tt + 8, D_RNN), F32),
            pltpu.VMEM((tt, D_RNN), F32),
            pltpu.VMEM((tt, D_RNN), F32),
            pltpu.VMEM((1, D_RNN), F32),
        ],
        compiler_params=pltpu.CompilerParams(
            dimension_semantics=("arbitrary", "arbitrary"), vmem_limit_bytes=VMEM_LIMIT),
        name="rglru",
    )(proj, proj, conv0, h0, cw, cb, wa_bd, ba, wx_bd, bx, lam)


def _select_topk(s, kk):
    rows, n = s.shape
    kkf = float(kk)

    def key_to_f32(w):
        k = w ^ INT_MIN
        bits = jnp.where(k >= 0, k, k ^ 0x7FFFFFFF)
        return k, lax.bitcast_convert_type(bits, F32)

    def vbody(it, w):
        cand_w = w | jnp.left_shift(jnp.int32(1), 31 - it)
        cand_k, cand_f = key_to_f32(cand_w)
        cnt = jnp.sum(jnp.where(s >= cand_f, 1.0, 0.0), axis=1, keepdims=True)
        ok = (cnt >= kkf) | (cand_k < KEY_NEG_INF)
        return jnp.where(ok, cand_w, w)

    w = lax.fori_loop(0, 32, vbody, jnp.zeros((rows, 1), jnp.int32))
    _, thr = key_to_f32(w)
    gt = s > thr
    eq = s == thr
    need = kkf - jnp.sum(jnp.where(gt, 1.0, 0.0), axis=1, keepdims=True)
    col = lax.broadcasted_iota(jnp.int32, (1, n), 1)
    nbits = int(n).bit_length()

    def jbody(it, jmax):
        cand = jmax | jnp.left_shift(jnp.int32(1), nbits - 1 - it)
        cnt = jnp.sum(jnp.where(eq & (col < cand), 1.0, 0.0), axis=1, keepdims=True)
        return jnp.where(cnt <= need, cand, jmax)

    n_ge = jnp.sum(jnp.where(s >= thr, 1.0, 0.0), axis=1, keepdims=True)
    jmax = lax.cond(
        jnp.max(n_ge) > kkf,
        lambda: lax.fori_loop(0, nbits, jbody, jnp.zeros((rows, 1), jnp.int32)),
        lambda: jnp.full((rows, 1), (1 << nbits) - 1, jnp.int32))
    return gt | (eq & (col < jmax))


def _pattn_kernel(q_ref, qi_ref, sm_ref, k_ref, v_ref, ki_ref, o_ref, s_ref, *, i0, n_keys, kc, topk):
    i = pl.program_id(0)
    n_batch, tq = q_ref.shape[0], q_ref.shape[1]
    lane = lax.broadcasted_iota(jnp.int32, (1, LANES), 1)
    qpos = (i0 + i) * tq + lax.broadcasted_iota(jnp.int32, (tq, 1), 0)

    def score(b, carry):
        sm = sm_ref[b]
        qi = qi_ref[b]
        qrows, wrows = [], []
        for h in range(N_IDX_HEADS):
            blk = qi[:, (h // 2) * LANES:(h // 2 + 1) * LANES]
            if h % 2 == 1:
                blk = pltpu.roll(blk, IDX_DIM, 1)
            qrows.append(jnp.where(lane < IDX_DIM, blk, 0.0))
            wrows.append(sm[:, IDX_DIM + h:IDX_DIM + h + 1])
        qst = jnp.concatenate(qrows, axis=0).astype(BF16)
        wst = jnp.concatenate(wrows, axis=0) * IDX_SCALE
        r0 = pl.multiple_of(b * tq, tq)
        for c in range(n_keys // kc):
            s = jnp.maximum(_dot_nt(qst, ki_ref[b, c * kc:(c + 1) * kc, :]), 0.0) * wst
            sc = s[0:tq]
            for h in range(1, N_IDX_HEADS):
                sc = sc + s[h * tq:(h + 1) * tq]
            col = c * kc + lax.broadcasted_iota(jnp.int32, (1, kc), 1)
            s_ref[pl.ds(r0, tq), c * kc:(c + 1) * kc] = jnp.where(col <= qpos, sc, F32_MIN)
        return carry

    lax.fori_loop(0, n_batch, score, 0)

    sel = _select_topk(s_ref[...], topk)
    colf = lax.broadcasted_iota(jnp.int32, (1, n_keys), 1)
    qpos_all = jnp.concatenate([qpos] * n_batch, axis=0)
    s_ref[...] = jnp.where(sel & (colf <= qpos_all), 0.0, -jnp.inf)

    def attend(b, carry):
        _attend_tile(q_ref[b], s_ref[pl.ds(pl.multiple_of(b * tq, tq), tq), :],
                     k_ref[b, 0:n_keys, :], v_ref[b, 0:n_keys, :], o_ref.at[b])
        return carry

    lax.fori_loop(0, n_batch, attend, 0)


def _attend_tile(q, bias, kb, vb, o_ref):
    tq = q.shape[0]
    lane = lax.broadcasted_iota(jnp.int32, (1, LANES), 1)
    bias4 = jnp.concatenate([bias] * 4, axis=0)
    outs = [None] * N_HEADS
    for g in range(N_KV_HEADS):
        lo = (g % 2) * HEAD_DIM
        keep = (lane >= lo) & (lane < lo + HEAD_DIM)
        rows = []
        for j in range(4):
            h = 4 * g + j
            blk = q[:, (h // 2) * LANES:(h // 2 + 1) * LANES]
            if h % 2 != g % 2:
                blk = pltpu.roll(blk, HEAD_DIM, 1)
            piece = jnp.where(keep, blk, 0.0)
            zero = jnp.zeros_like(piece)
            rows.append(jnp.concatenate([piece, zero] if g < 2 else [zero, piece], axis=1))
        qbd = jnp.concatenate(rows, axis=0).astype(BF16)
        logits = _dot_nt(qbd, kb) + bias4
        m = jnp.max(logits, axis=1, keepdims=True)
        p = jnp.exp(logits - m)
        denom = jnp.sum(p, axis=1, keepdims=True)
        acc = _dot(p.astype(BF16), vb) / denom
        for j in range(4):
            outs[4 * g + j] = acc[j * tq:(j + 1) * tq, (g // 2) * LANES:(g // 2 + 1) * LANES]
    for c in range(N_HEADS // 2):
        g = (2 * c) // 4
        even, odd = outs[2 * c], outs[2 * c + 1]
        if g % 2 == 1:
            even = pltpu.roll(even, HEAD_DIM, 1)
        else:
            odd = pltpu.roll(odd, HEAD_DIM, 1)
        o_ref[:, c * LANES:(c + 1) * LANES] = jnp.where(lane < HEAD_DIM, even, odd).astype(BF16)


def _prompt_attention_part(proj3, small3, kvb, kib, *, i0, n_tiles, tq):
    n_batch, seq, _ = proj3.shape
    n_keys = (i0 + n_tiles) * tq
    kc = next(c for c in (512, 256, 128) if n_keys % c == 0)
    return pl.pallas_call(
        functools.partial(_pattn_kernel, i0=i0, n_keys=n_keys, kc=kc, topk=min(TOPK_MAX, seq // 4)),
        grid=(n_tiles,),
        in_specs=[
            pl.BlockSpec((n_batch, tq, ATTN_W), lambda i: (0, i0 + i, 2)),
            pl.BlockSpec((n_batch, tq, N_IDX_HEADS * IDX_DIM), lambda i: (0, i0 + i, 7)),
            pl.BlockSpec((n_batch, tq, LANES), lambda i: (0, i0 + i, 0)),
            pl.BlockSpec((n_batch, seq, KV_W), lambda i: (0, 0, 0)),
            pl.BlockSpec((n_batch, seq, KV_W), lambda i: (0, 0, 1)),
            pl.BlockSpec((n_batch, seq, LANES), lambda i: (0, 0, 0)),
        ],
        out_specs=pl.BlockSpec((n_batch, tq, ATTN_W), lambda i: (0, i, 0)),
        out_shape=jax.ShapeDtypeStruct((n_batch, n_tiles * tq, ATTN_W), BF16),
        scratch_shapes=[pltpu.VMEM((n_batch * tq, n_keys), F32)],
        compiler_params=pltpu.CompilerParams(
            dimension_semantics=("arbitrary",), vmem_limit_bytes=VMEM_LIMIT),
        name=f"prompt_attention_{i0}",
    )(proj3, proj3, small3, kvb, kvb, kib)


def _prompt_attention(proj, small, kvb, kib, *, n_batch, seq, tq=128, tiles_per_part=2):
    nq = seq // tq
    proj3 = proj.reshape(n_batch, seq, N_MAIN)
    small3 = small.reshape(n_batch, seq, LANES)
    kvb3 = kvb.reshape(n_batch, seq, 2 * KV_W)
    kib3 = kib.reshape(n_batch, seq, LANES)
    parts = [
        _prompt_attention_part(proj3, small3, kvb3, kib3, i0=i0,
                               n_tiles=min(tiles_per_part, nq - i0), tq=tq)
        for i0 in range(0, nq, tiles_per_part)
    ]
    return jnp.concatenate(parts, axis=1).reshape(n_batch * seq, ATTN_W)


SELECT_PAGES_PER_STEP = 32
ATTEND_PAGES_PER_STEP = 32
SUB_PAGES = 8


def _sidx_kernel(pt_ref, qst_ref, w_ref, sm_ref, *refs, n_chunks, n_new, ps):
    pages = refs[:ps]
    o_ref, s_ref = refs[ps], refs[ps + 1]
    c = pl.program_id(1)
    qst = qst_ref[...]
    w = w_ref[...] * IDX_SCALE

    def head_sum(s):
        s = jnp.maximum(s, 0.0) * w
        out = s[0:n_new]
        for h in range(1, N_IDX_HEADS):
            out = out + s[h * n_new:(h + 1) * n_new]
        return out

    for r0 in range(0, ps, SUB_PAGES):
        kt = jnp.concatenate([pages[r0 + r][...] for r in range(SUB_PAGES)], axis=1).astype(BF16)
        part = head_sum(_dot(qst, kt))
        for r in range(SUB_PAGES):
            s_ref[ps * c + r0 + r] = part[:, r * PAGE_SIZE:(r + 1) * PAGE_SIZE]

    @pl.when(c == n_chunks - 1)
    def _():
        n_past_blocks = n_chunks * ps
        past = n_past_blocks * PAGE_SIZE
        k_new = sm_ref[...][:, 0:IDX_DIM]
        kp = jnp.concatenate([k_new, jnp.zeros((PAGE_SIZE - n_new, IDX_DIM), F32)], axis=0)
        lane = lax.broadcasted_iota(jnp.int32, (n_new, LANES), 1)
        trow = lax.broadcasted_iota(jnp.int32, (n_new, LANES), 0)
        s_new = head_sum(_dot_nt(qst, kp.astype(BF16)))
        s_ref[n_past_blocks] = jnp.where(lane < n_new, jnp.where(lane <= trow, s_new, F32_MIN), -jnp.inf)
        o_ref[...] = jnp.concatenate([s_ref[k] for k in range(n_past_blocks + 1)], axis=1)


def _select_bias_kernel(s_ref, o_ref, *, n_new, past, topk):
    s = s_ref[...]
    rows, n_all = s.shape
    sel = _select_topk(s, topk)
    col = lax.broadcasted_iota(jnp.int32, (1, n_all), 1)
    tq = lax.broadcasted_iota(jnp.int32, (rows, 1), 0) % n_new
    o_ref[...] = jnp.where(sel & ((col - past) <= tq), 0.0, -jnp.inf)


def _select_bias(scores, *, n_new, past, rows_per_step=128):
    rows, n_all = scores.shape
    rows_per_step = min(rows_per_step, rows)
    return pl.pallas_call(
        functools.partial(_select_bias_kernel, n_new=n_new, past=past,
                          topk=min(TOPK_MAX, (past + n_new) // 4)),
        grid=(rows // rows_per_step,),
        in_specs=[pl.BlockSpec((rows_per_step, n_all), lambda i: (i, 0))],
        out_specs=pl.BlockSpec((rows_per_step, n_all), lambda i: (i, 0)),
        out_shape=jax.ShapeDtypeStruct((rows, n_all), F32),
        compiler_params=pltpu.CompilerParams(
            dimension_semantics=("arbitrary",), vmem_limit_bytes=VMEM_LIMIT),
        name="sample_select_bias",
    )(scores)


def _sample_select(page_table, qst, wcol, small, cache_kidx_t, *, n_new):
    n_seq, n_pages = page_table.shape
    ps = min(SELECT_PAGES_PER_STEP, n_pages)
    n_chunks = n_pages // ps
    n_all = n_pages * PAGE_SIZE + LANES
    page_specs = [
        pl.BlockSpec((None, IDX_DIM, PAGE_SIZE), lambda b, c, pt, r=r: (pt[b, ps * c + r], 0, 0))
        for r in range(ps)
    ]
    rows = N_IDX_HEADS * n_new
    grid_spec = pltpu.PrefetchScalarGridSpec(
        num_scalar_prefetch=1,
        grid=(n_seq, n_chunks),
        in_specs=[
            pl.BlockSpec((None, rows, IDX_DIM), lambda b, c, pt: (b, 0, 0)),
            pl.BlockSpec((None, rows, 1), lambda b, c, pt: (b, 0, 0)),
            pl.BlockSpec((n_new, LANES), lambda b, c, pt: (b, 0)),
        ] + page_specs,
        out_specs=pl.BlockSpec((None, n_new, n_all), lambda b, c, pt: (b, 0, 0)),
        scratch_shapes=[pltpu.VMEM((n_pages + 1, n_new, LANES), F32)],
    )
    return pl.pallas_call(
        functools.partial(_sidx_kernel, n_chunks=n_chunks, n_new=n_new, ps=ps),
        grid_spec=grid_spec,
        out_shape=jax.ShapeDtypeStruct((n_seq, n_new, n_all), F32),
        compiler_params=pltpu.CompilerParams(
            dimension_semantics=("arbitrary", "arbitrary"), vmem_limit_bytes=VMEM_LIMIT),
        name="sample_select",
    )(page_table, qst, wcol, small, *([cache_kidx_t] * ps))


def _sattn_kernel(pt_ref, q_ref, bias_ref, biasn_ref, kn_ref, vn_ref, *refs, n_chunks, n_new, ps):
    kpages = refs[:ps]
    vpages = refs[ps:2 * ps]
    o_ref, m_ref, l_ref, acc_ref = refs[2 * ps:]
    c = pl.program_id(1)
    rows = q_ref.shape[0]
    reps = rows // n_new

    @pl.when(c == 0)
    def _():
        m_ref[...] = jnp.full(m_ref.shape, -1e30, F32)
        l_ref[...] = jnp.zeros(l_ref.shape, F32)
        acc_ref[...] = jnp.zeros(acc_ref.shape, F32)

    def update(logits, bias, pv):
        logits = logits + jnp.concatenate([bias] * reps, axis=0)
        m_old = m_ref[...]
        m_new = jnp.maximum(m_old, jnp.max(logits, axis=1, keepdims=True))
        alpha = jnp.exp(m_old - m_new)
        p = jnp.exp(logits - m_new)
        l_ref[...] = alpha * l_ref[...] + jnp.sum(p, axis=1, keepdims=True)
        acc_ref[...] = alpha * acc_ref[...] + pv(p.astype(BF16))
        m_ref[...] = m_new

    sub_keys = SUB_PAGES * PAGE_SIZE
    logits, vts = [], []
    for r0 in range(0, ps, SUB_PAGES):
        kt = jnp.concatenate([kpages[r0 + r][...] for r in range(SUB_PAGES)], axis=1).astype(BF16)
        vts.append(jnp.concatenate([vpages[r0 + r][...] for r in range(SUB_PAGES)], axis=1).astype(BF16))
        logits.append(_dot(q_ref[...], kt))

    def pv(p):
        acc = _dot_nt(p[:, 0:sub_keys], vts[0])
        for n in range(1, len(vts)):
            acc = acc + _dot_nt(p[:, n * sub_keys:(n + 1) * sub_keys], vts[n])
        return acc

    update(jnp.concatenate(logits, axis=1), bias_ref[...], pv)

    @pl.when(c == n_chunks - 1)
    def _():
        pad = jnp.zeros((PAGE_SIZE - n_new, KV_W), F32)
        kn = jnp.concatenate([kn_ref[...], pad], axis=0).astype(BF16)
        vn = jnp.concatenate([vn_ref[...], pad], axis=0).astype(BF16)
        update(_dot_nt(q_ref[...], kn), biasn_ref[...], lambda p: _dot(p, vn))
        o_ref[...] = acc_ref[...] / l_ref[...]


def _sample_attend(page_table, qbd, bias, proj, cache_k_t, cache_v_t, *, n_new):
    n_seq, n_pages = page_table.shape
    ps = min(ATTEND_PAGES_PER_STEP, n_pages)
    n_chunks = n_pages // ps
    rows = qbd.shape[1]
    chunk_keys = ps * PAGE_SIZE
    page_specs = [
        pl.BlockSpec((None, KV_W, PAGE_SIZE), lambda b, c, pt, r=r: (pt[b, ps * c + r], 0, 0))
        for r in range(ps)
    ]
    grid_spec = pltpu.PrefetchScalarGridSpec(
        num_scalar_prefetch=1,
        grid=(n_seq, n_chunks),
        in_specs=[
            pl.BlockSpec((None, rows, KV_W), lambda b, c, pt: (b, 0, 0)),
            pl.BlockSpec((None, n_new, chunk_keys), lambda b, c, pt: (b, 0, c)),
            pl.BlockSpec((None, n_new, LANES), lambda b, c, pt: (b, 0, n_pages)),
            pl.BlockSpec((n_new, KV_W), lambda b, c, pt: (b, 12)),
            pl.BlockSpec((n_new, KV_W), lambda b, c, pt: (b, 13)),
        ] + page_specs + page_specs,
        out_specs=pl.BlockSpec((None, rows, KV_W), lambda b, c, pt: (b, 0, 0)),
        scratch_shapes=[
            pltpu.VMEM((rows, 1), F32),
            pltpu.VMEM((rows, 1), F32),
            pltpu.VMEM((rows, KV_W), F32),
        ],
    )
    return pl.pallas_call(
        functools.partial(_sattn_kernel, n_chunks=n_chunks, n_new=n_new, ps=ps),
        grid_spec=grid_spec,
        out_shape=jax.ShapeDtypeStruct((n_seq, rows, KV_W), F32),
        compiler_params=pltpu.CompilerParams(
            dimension_semantics=("arbitrary", "arbitrary"), vmem_limit_bytes=VMEM_LIMIT),
        name="sample_attend",
    )(page_table, qbd, bias, bias, proj, proj, *([cache_k_t] * ps), *([cache_v_t] * ps))


def _merge_kernel(x_ref, rnn_ref, att_ref, gr_ref, ga_ref, wr_ref, wa_ref, wo_ref, o_ref):
    mixed = (_sigmoid(gr_ref[...]) * _dot(rnn_ref[...].astype(BF16), wr_ref[...])
             + _sigmoid(ga_ref[...]) * _dot(att_ref[...].astype(BF16), wa_ref[...]))
    o_ref[...] = x_ref[...] + _dot(mixed.astype(BF16), wo_ref[...])


def _merge(x, rnn, attn, proj, wr, wa, wo, *, tm):
    t = x.shape[0]
    return pl.pallas_call(
        _merge_kernel,
        grid=(t // tm,),
        in_specs=[
            pl.BlockSpec((tm, D_MODEL), lambda i: (i, 0)),
            pl.BlockSpec((tm, D_RNN), lambda i: (i, 0)),
            pl.BlockSpec((tm, ATTN_W), lambda i: (i, 0)),
            pl.BlockSpec((tm, D_MODEL), lambda i: (i, 2)),
            pl.BlockSpec((tm, D_MODEL), lambda i: (i, 3)),
            pl.BlockSpec((D_RNN, D_MODEL), lambda i: (0, 0)),
            pl.BlockSpec((ATTN_W, D_MODEL), lambda i: (0, 0)),
            pl.BlockSpec((D_MODEL, D_MODEL), lambda i: (0, 0)),
        ],
        out_specs=pl.BlockSpec((tm, D_MODEL), lambda i: (i, 0)),
        out_shape=jax.ShapeDtypeStruct((t, D_MODEL), F32),
        compiler_params=pltpu.CompilerParams(
            dimension_semantics=("arbitrary",), vmem_limit_bytes=VMEM_LIMIT),
        name="merge",
    )(x, rnn, attn, proj, proj, wr, wa, wo)


def _route(h, rwh, rwl, rb):
    lane = lax.broadcasted_iota(jnp.int32, (1, LANES), 1)
    lanef = lane.astype(F32)
    hh, hl = _split_bf16(h)
    lg = (_dot(hh, rwh) + _dot(hl, rwh) + _dot(hh, rwl)) + rb
    is_g = (lane >= N_EXPERTS) & (lane < N_EXPERTS + N_GROUPS)
    gl = jnp.where(is_g, lg, -jnp.inf)
    gmax = jnp.max(gl, axis=1, keepdims=True)
    gprob = 1.0 / jnp.sum(jnp.exp(gl - gmax), axis=1, keepdims=True)
    gsel = jnp.min(jnp.where(is_g & (lg == gmax), lanef - N_EXPERTS, 1e9), axis=1, keepdims=True)
    in_grp = (lane < N_EXPERTS) & (jnp.floor(lanef * (1.0 / EXPERTS_PER_GROUP)) == gsel)
    v1 = jnp.where(in_grp, lg, -jnp.inf)
    t1 = jnp.max(v1, axis=1, keepdims=True)
    i1 = jnp.min(jnp.where(in_grp & (lg == t1), lanef, 1e9), axis=1, keepdims=True)
    rest = in_grp & (lanef != i1)
    v2 = jnp.where(rest, lg, -jnp.inf)
    t2 = jnp.max(v2, axis=1, keepdims=True)
    i2 = jnp.min(jnp.where(rest & (lg == t2), lanef, 1e9), axis=1, keepdims=True)
    d = jnp.exp(t2 - t1)
    return i1, i2, gprob / (1.0 + d), gprob * d / (1.0 + d)


def _moe_kernel(x_ref, ln_ref, rwh_ref, rwl_ref, rb_ref, w1_ref, w3_ref, w2_ref, o_ref, h_ref, gate_ref):
    e = pl.program_id(1)
    lane = lax.broadcasted_iota(jnp.int32, (1, LANES), 1)

    @pl.when(e == 0)
    def _():
        h = _rms(x_ref[...], ln_ref[...])
        h_ref[...] = h.astype(BF16)
        i1, i2, g1, g2 = _route(h, rwh_ref[...], rwl_ref[...], rb_ref[...])
        lanef = lane.astype(F32)
        gate_ref[...] = jnp.where(lanef == i1, g1, 0.0) + jnp.where(lanef == i2, g2, 0.0)

    ge = jnp.sum(jnp.where(lane == e, gate_ref[...], 0.0), axis=1, keepdims=True)
    up = _dot(h_ref[...], w1_ref[...].astype(BF16))
    hid = (up * _sigmoid(up)) * _dot(h_ref[...], w3_ref[...].astype(BF16))
    contrib = _dot((hid * ge).astype(BF16), w2_ref[...].astype(BF16))

    @pl.when(e == 0)
    def _():
        o_ref[...] = x_ref[...] + contrib

    @pl.when(e > 0)
    def _():
        o_ref[...] += contrib


def _moe(x, ln2, rw_hi, rw_lo, rb, w1, w3, w2, *, tm):
    t = x.shape[0]
    return pl.pallas_call(
        _moe_kernel,
        grid=(t // tm, N_EXPERTS),
        in_specs=[
            pl.BlockSpec((tm, D_MODEL), lambda i, e: (i, 0)),
            pl.BlockSpec((1, D_MODEL), lambda i, e: (0, 0)),
            pl.BlockSpec((D_MODEL, LANES), lambda i, e: (0, 0)),
            pl.BlockSpec((D_MODEL, LANES), lambda i, e: (0, 0)),
            pl.BlockSpec((1, LANES), lambda i, e: (0, 0)),
            pl.BlockSpec((None, D_MODEL, D_EXPERT), lambda i, e: (e, 0, 0)),
            pl.BlockSpec((None, D_MODEL, D_EXPERT), lambda i, e: (e, 0, 0)),
            pl.BlockSpec((None, D_EXPERT, D_MODEL), lambda i, e: (e, 0, 0)),
        ],
        out_specs=pl.BlockSpec((tm, D_MODEL), lambda i, e: (i, 0)),
        out_shape=jax.ShapeDtypeStruct((t, D_MODEL), F32),
        scratch_shapes=[pltpu.VMEM((tm, D_MODEL), BF16), pltpu.VMEM((tm, LANES), F32)],
        compiler_params=pltpu.CompilerParams(
            dimension_semantics=("arbitrary", "arbitrary"), vmem_limit_bytes=VMEM_LIMIT),
        name="moe",
    )(x, ln2, rw_hi, rw_lo, rb, w1, w3, w2)


def _ple_update(x2, p, ln, wg, wp):
    gate = _sigmoid(_dot(_rms(x2, ln).astype(BF16), wg))
    return x2 + gate * _dot(p.astype(BF16), wp)


def _ple_kernel(x_ref, p_ref, ln_ref, wg_ref, wp_ref, o_ref):
    o_ref[...] = _ple_update(x_ref[...], p_ref[...], ln_ref[...], wg_ref[...], wp_ref[...])


def _ple(x, p, ln3, wg, wp, *, tm):
    t = x.shape[0]
    return pl.pallas_call(
        _ple_kernel,
        grid=(t // tm,),
        in_specs=[
            pl.BlockSpec((tm, D_MODEL), lambda i: (i, 0)),
            pl.BlockSpec((tm, PLE_DIM), lambda i: (i, 0)),
            pl.BlockSpec((1, D_MODEL), lambda i: (0, 0)),
            pl.BlockSpec((D_MODEL, D_MODEL), lambda i: (0, 0)),
            pl.BlockSpec((PLE_DIM, D_MODEL), lambda i: (0, 0)),
        ],
        out_specs=pl.BlockSpec((tm, D_MODEL), lambda i: (i, 0)),
        out_shape=jax.ShapeDtypeStruct((t, D_MODEL), F32),
        compiler_params=pltpu.CompilerParams(
            dimension_semantics=("arbitrary",), vmem_limit_bytes=VMEM_LIMIT),
        name="ple",
    )(x, p, ln3, wg, wp)


MOE_ROW_TILE = 256
META_E, META_G, META_RANK = 0, 2, 4


def _router_kernel(x_ref, ln_ref, rwh_ref, rwl_ref, rb_ref, tri_ref, meta_ref, cnt_ref, carry_ref):
    i = pl.program_id(0)
    lane = lax.broadcasted_iota(jnp.int32, (1, LANES), 1)
    lanef = lane.astype(F32)

    @pl.when(i == 0)
    def _():
        carry_ref[...] = jnp.zeros(carry_ref.shape, F32)

    i1, i2, g1, g2 = _route(_rms(x_ref[...], ln_ref[...]), rwh_ref[...], rwl_ref[...], rb_ref[...])
    onehot = jnp.where((lanef == i1) | (lanef == i2), 1.0, 0.0)
    before = _dot(tri_ref[...], onehot.astype(BF16)) + carry_ref[...]
    r1 = jnp.sum(jnp.where(lanef == i1, before, 0.0), axis=1, keepdims=True)
    r2 = jnp.sum(jnp.where(lanef == i2, before, 0.0), axis=1, keepdims=True)
    carry_ref[...] += jnp.sum(onehot, axis=0, keepdims=True)
    rec = jnp.zeros((x_ref.shape[0], LANES), F32)
    for k, val in ((META_E, i1), (META_E + 1, i2), (META_G, g1), (META_G + 1, g2),
                   (META_RANK, r1), (META_RANK + 1, r2)):
        rec = jnp.where(lane == k, val, rec)
    meta_ref[...] = rec
    cnt_ref[...] = jnp.broadcast_to(carry_ref[...], cnt_ref.shape)


def _router(x, ln2, rw_hi, rw_lo, rb, *, tm):
    t = x.shape[0]
    r = lax.broadcasted_iota(jnp.int32, (tm, tm), 0)
    c = lax.broadcasted_iota(jnp.int32, (tm, tm), 1)
    tri = (c < r).astype(BF16)
    return pl.pallas_call(
        _router_kernel,
        grid=(t // tm,),
        in_specs=[
            pl.BlockSpec((tm, D_MODEL), lambda i: (i, 0)),
            pl.BlockSpec((1, D_MODEL), lambda i: (0, 0)),
            pl.BlockSpec((D_MODEL, LANES), lambda i: (0, 0)),
            pl.BlockSpec((D_MODEL, LANES), lambda i: (0, 0)),
            pl.BlockSpec((1, LANES), lambda i: (0, 0)),
            pl.BlockSpec((tm, tm), lambda i: (0, 0)),
        ],
        out_specs=[
            pl.BlockSpec((tm, LANES), lambda i: (i, 0)),
            pl.BlockSpec((8, LANES), lambda i: (0, 0)),
        ],
        out_shape=[
            jax.ShapeDtypeStruct((t, LANES), F32),
            jax.ShapeDtypeStruct((8, LANES), F32),
        ],
        scratch_shapes=[pltpu.VMEM((1, LANES), F32)],
        compiler_params=pltpu.CompilerParams(
            dimension_semantics=("arbitrary",), vmem_limit_bytes=VMEM_LIMIT),
        name="moe_router",
    )(x, ln2, rw_hi, rw_lo, rb, tri)


def _row_copy(src_ref, src_row, dst_ref, dst_row, sem):
    return pltpu.make_async_copy(src_ref.at[pl.ds(src_row, 1), :], dst_ref.at[pl.ds(dst_row, 1), :], sem)


def _scatter_kernel(slot_ref, x_ref, hs_in_ref, hs_ref, sem, *, n_tok):
    del hs_in_ref
    tm = x_ref.shape[0]
    base = pl.program_id(0) * tm

    def start(r, carry):
        for k in range(2):
            _row_copy(x_ref, r, hs_ref, slot_ref[k * n_tok + base + r], sem).start()
        return carry

    lax.fori_loop(0, tm, start, 0)
    for k in range(2):
        pltpu.make_async_copy(x_ref, hs_ref.at[pl.ds(0, tm), :], sem).wait()


def _scatter_rows(slots, x, hs_zero, *, tm):
    t = x.shape[0]
    grid_spec = pltpu.PrefetchScalarGridSpec(
        num_scalar_prefetch=1,
        grid=(t // tm,),
        in_specs=[
            pl.BlockSpec((tm, D_MODEL), lambda i, s: (i, 0)),
            pl.BlockSpec(memory_space=pl.ANY),
        ],
        out_specs=pl.BlockSpec(memory_space=pl.ANY),
        scratch_shapes=[pltpu.SemaphoreType.DMA(())],
    )
    return pl.pallas_call(
        functools.partial(_scatter_kernel, n_tok=t),
        grid_spec=grid_spec,
        out_shape=jax.ShapeDtypeStruct(hs_zero.shape, F32),
        input_output_aliases={2: 0},
        compiler_params=pltpu.CompilerParams(
            dimension_semantics=("arbitrary",), vmem_limit_bytes=VMEM_LIMIT),
        name="moe_scatter",
    )(slots, x, hs_zero)


def _expert_kernel(te_ref, nu_ref, hs_ref, ln_ref, w1_ref, w3_ref, w2_ref, y_ref):
    i = pl.program_id(0)

    @pl.when(i < nu_ref[0])
    def _():
        h = _rms(hs_ref[...], ln_ref[...]).astype(BF16)
        up = _dot(h, w1_ref[...].astype(BF16))
        hid = (up * _sigmoid(up)) * _dot(h, w3_ref[...].astype(BF16))
        y_ref[...] = _dot(hid.astype(BF16), w2_ref[...].astype(BF16))

    @pl.when(i >= nu_ref[0])
    def _():
        y_ref[...] = jnp.zeros(y_ref.shape, F32)


def _expert_mlp(tile_expert, n_used, hs, ln2, w1, w3, w2):
    n_tiles = tile_expert.shape[0]
    grid_spec = pltpu.PrefetchScalarGridSpec(
        num_scalar_prefetch=2,
        grid=(n_tiles,),
        in_specs=[
            pl.BlockSpec((MOE_ROW_TILE, D_MODEL), lambda i, te, nu: (i, 0)),
            pl.BlockSpec((1, D_MODEL), lambda i, te, nu: (0, 0)),
            pl.BlockSpec((None, D_MODEL, D_EXPERT), lambda i, te, nu: (te[i], 0, 0)),
            pl.BlockSpec((None, D_MODEL, D_EXPERT), lambda i, te, nu: (te[i], 0, 0)),
            pl.BlockSpec((None, D_EXPERT, D_MODEL), lambda i, te, nu: (te[i], 0, 0)),
        ],
        out_specs=pl.BlockSpec((MOE_ROW_TILE, D_MODEL), lambda i, te, nu: (i, 0)),
    )
    return pl.pallas_call(
        _expert_kernel,
        grid_spec=grid_spec,
        out_shape=jax.ShapeDtypeStruct(hs.shape, F32),
        compiler_params=pltpu.CompilerParams(
            dimension_semantics=("arbitrary",), vmem_limit_bytes=VMEM_LIMIT),
        name="moe_experts",
    )(tile_expert, n_used, hs, ln2, w1, w3, w2)


def _combine_ple_kernel(slot_ref, x_ref, meta_ref, p_ref, ln_ref, wg_ref, wp_ref, y_ref, o_ref,
                        ybuf_ref, sem, *, n_tok):
    i = pl.program_id(0)
    tm = x_ref.shape[0]

    def gather(tile, buf):
        def start(r, carry):
            for k in range(2):
                _row_copy(y_ref, slot_ref[k * n_tok + tile * tm + r], ybuf_ref.at[buf, k], r,
                          sem.at[buf]).start()
            return carry
        lax.fori_loop(0, tm, start, 0)

    @pl.when(i == 0)
    def _():
        gather(0, 0)

    @pl.when(i + 1 < pl.num_programs(0))
    def _():
        gather(i + 1, (i + 1) % 2)

    cur = i % 2
    for k in range(2):
        pltpu.make_async_copy(y_ref.at[pl.ds(0, tm), :], ybuf_ref.at[cur, k], sem.at[cur]).wait()
    meta = meta_ref[...]
    x2 = (x_ref[...] + meta[:, META_G:META_G + 1] * ybuf_ref[cur, 0]
          + meta[:, META_G + 1:META_G + 2] * ybuf_ref[cur, 1])
    o_ref[...] = _ple_update(x2, p_ref[...], ln_ref[...], wg_ref[...], wp_ref[...])


def _combine_ple(slots, x, meta, p, ln3, wg, wp, y, *, tm):
    t = x.shape[0]
    grid_spec = pltpu.PrefetchScalarGridSpec(
        num_scalar_prefetch=1,
        grid=(t // tm,),
        in_specs=[
            pl.BlockSpec((tm, D_MODEL), lambda i, s: (i, 0)),
            pl.BlockSpec((tm, LANES), lambda i, s: (i, 0)),
            pl.BlockSpec((tm, PLE_DIM), lambda i, s: (i, 0)),
            pl.BlockSpec((1, D_MODEL), lambda i, s: (0, 0)),
            pl.BlockSpec((D_MODEL, D_MODEL), lambda i, s: (0, 0)),
            pl.BlockSpec((PLE_DIM, D_MODEL), lambda i, s: (0, 0)),
            pl.BlockSpec(memory_space=pl.ANY),
        ],
        out_specs=pl.BlockSpec((tm, D_MODEL), lambda i, s: (i, 0)),
        scratch_shapes=[
            pltpu.VMEM((2, 2, tm, D_MODEL), F32),
            pltpu.SemaphoreType.DMA((2,)),
        ],
    )
    return pl.pallas_call(
        functools.partial(_combine_ple_kernel, n_tok=t),
        grid_spec=grid_spec,
        out_shape=jax.ShapeDtypeStruct((t, D_MODEL), F32),
        compiler_params=pltpu.CompilerParams(
            dimension_semantics=("arbitrary",), vmem_limit_bytes=VMEM_LIMIT),
        name="moe_combine_ple",
    )(slots, x, meta, p, ln3, wg, wp, y)


def _sparse_moe_ple(x, p, w):
    t = x.shape[0]
    rt = MOE_ROW_TILE
    meta, cnt = _router(x, w["ln2"], w["rw_hi"], w["rw_lo"], w["rb"], tm=256)
    counts = cnt[0, :N_EXPERTS].astype(jnp.int32)
    padded = ((counts + rt - 1) // rt) * rt
    ends = jnp.cumsum(padded)
    offs = ends - padded
    eid = meta[:, META_E:META_E + 2].astype(jnp.int32)
    rank = meta[:, META_RANK:META_RANK + 2].astype(jnp.int32)
    base = jnp.sum(jnp.where(eid[:, :, None] == jnp.arange(N_EXPERTS), offs, 0), axis=-1)
    slots = jnp.transpose(base + rank).reshape(2 * t)
    n_tiles = (2 * t + N_EXPERTS * (rt - 1)) // rt
    tile_start = jnp.arange(n_tiles, dtype=jnp.int32) * rt
    tile_expert = jnp.minimum(jnp.sum((tile_start[:, None] >= ends[None, :]).astype(jnp.int32), axis=1),
                              N_EXPERTS - 1)
    n_used = (ends[N_EXPERTS - 1] // rt).reshape(1)
    hs = _scatter_rows(slots, x, jnp.zeros((n_tiles * rt, D_MODEL), F32), tm=256)
    y = _expert_mlp(tile_expert, n_used, hs, w["ln2"], w["w1"], w["w3"], w["w2"])
    return _combine_ple(slots, x, meta, p, w["ln3"], w["wg"], w["wp"], y, tm=256)


def _rope_tables(pos):
    half = HEAD_DIM // 2
    inv = ROPE_THETA ** (-jnp.arange(half, dtype=F32) / half)
    ang = pos.astype(F32)[:, None] * inv[None, :]
    cos, sin = jnp.cos(ang), jnp.sin(ang)
    return (jnp.concatenate([cos, cos, cos, cos], axis=1),
            jnp.concatenate([-sin, sin, -sin, sin], axis=1))


def _block_diag(w, per):
    n, r, _ = w.shape
    eye = jnp.eye(per, dtype=w.dtype)
    wg = w.reshape(n // per, per, r, r)
    return jnp.einsum("gpij,pq->gpiqj", wg, eye).reshape(n // per, per * r, per * r)


def _layer_weights(ln1, w_in, q_norm, k_norm, conv_w, conv_b, w_a, b_a, w_x, b_x, lam, w_br_rnn,
                   w_br_attn, w_out, ln2, w_rg, b_rg, w_re, b_re, w1, w3, w2, ln3, w_ple_gate,
                   w_ple_proj):
    o_q = 2 * D_RNN
    o_k = o_q + ATTN_W
    o_v = o_k + KV_W
    o_qi = o_v + KV_W
    o_ki = o_qi + N_IDX_HEADS * IDX_DIM
    o_wi = o_ki + IDX_DIM
    o_gr = o_wi + N_IDX_HEADS
    o_ga = o_gr + D_MODEL
    assert o_ki == COL_NR_END
    w_gates = w_in[:, o_gr:o_ga + D_MODEL]
    w_small = jnp.concatenate(
        [w_in[:, o_ki:o_gr], jnp.zeros((D_MODEL, LANES - IDX_DIM - N_IDX_HEADS), F32)], axis=1)
    ones = lambda n: jnp.ones((n,), F32)
    zeros = lambda n: jnp.zeros((n,), F32)
    n_gate = 2 * D_MODEL
    gain = jnp.concatenate([ones(o_q), jnp.tile(q_norm, N_HEADS), jnp.tile(k_norm, N_KV_HEADS),
                            ones(KV_W + N_IDX_HEADS * IDX_DIM + n_gate)])
    norm_on = jnp.concatenate([zeros(o_q), ones(ATTN_W + KV_W), zeros(KV_W + N_IDX_HEADS * IDX_DIM + n_gate)])
    rope_on = jnp.concatenate([zeros(o_q), ones(ATTN_W + KV_W), zeros(KV_W), ones(N_IDX_HEADS * IDX_DIM),
                               zeros(n_gate)])
    post = jnp.concatenate([ones(o_q), jnp.full((ATTN_W,), QK_SCALE, F32),
                            ones(2 * KV_W + N_IDX_HEADS * IDX_DIM + n_gate)])
    colctl = jnp.concatenate([jnp.stack([gain, norm_on, rope_on, post]), jnp.zeros((4, N_MAIN), F32)], axis=0)
    tn = 512
    head_of = jnp.arange(tn) // HEAD_DIM
    bd = (head_of[:, None] == head_of[None, :]).astype(BF16)
    rw = jnp.concatenate([w_re, w_rg, jnp.zeros((D_MODEL, LANES - N_EXPERTS - N_GROUPS), F32)], axis=1)
    rw_hi = rw.astype(BF16)
    rw_lo = (rw - rw_hi.astype(F32)).astype(BF16)
    rb = jnp.concatenate([b_re, b_rg, jnp.zeros((LANES - N_EXPERTS - N_GROUPS,), F32)])[None, :]
    return dict(
        ln1=ln1[None, :], w_in=w_in, w_gates=w_gates, w_small=w_small, colctl=colctl, bd=bd,
        cw=conv_w, cb=conv_b[None, :],
        wa_bd=_block_diag(w_a, 4).astype(BF16), ba=b_a[None, :],
        wx_bd=_block_diag(w_x, 4).astype(BF16), bx=b_x[None, :], lam=lam[None, :],
        wr=w_br_rnn.astype(BF16), wa=w_br_attn.astype(BF16), wo=w_out.astype(BF16),
        ln2=ln2[None, :], rw_hi=rw_hi, rw_lo=rw_lo, rb=rb,
        w1=w1, w3=w3, w2=w2,
        ln3=ln3[None, :], wg=w_ple_gate.astype(BF16), wp=w_ple_proj.astype(BF16),
    )


def _tail(x, rnn, attn, proj, p, w):
    t = x.shape[0]
    x1 = _merge(x, rnn, attn, proj, w["wr"], w["wa"], w["wo"], tm=min(t, 256))
    if 2 * t >= N_EXPERTS * MOE_ROW_TILE:
        return _sparse_moe_ple(x1, p, w)
    x2 = _moe(x1, w["ln2"], w["rw_hi"], w["rw_lo"], w["rb"], w["w1"], w["w3"], w["w2"], tm=min(t, 512))
    return _ple(x2, p, w["ln3"], w["wg"], w["wp"], tm=min(t, 512))


def _prompt_layer(x, p, w):
    bp, tp, _ = x.shape
    xt = x.reshape(bp * tp, D_MODEL)
    cs, sn = _rope_tables(jnp.arange(tp, dtype=jnp.int32))
    proj, small, kvb, kib = _inproj(xt, w["ln1"], w["w_in"], w["w_gates"], w["w_small"], w["colctl"],
                                    cs, sn, w["bd"], tm=min(tp, 1024))
    conv0 = jnp.zeros((bp, 8, D_RNN), F32)
    h0 = jnp.zeros((bp, 1, D_RNN), F32)
    rnn, h_last = _rglru(proj, conv0, h0, w["cw"], w["cb"], w["wa_bd"], w["ba"], w["wx_bd"], w["bx"],
                         w["lam"], n_seq=bp, tt=min(tp, 256))
    attn = _prompt_attention(proj, small, kvb, kib, n_batch=bp, seq=tp)
    y = _tail(xt, rnn, attn, proj, p.reshape(bp * tp, PLE_DIM), w)
    o_k = 2 * D_RNN + ATTN_W
    k = proj[:, o_k:o_k + KV_W].reshape(bp, tp, N_KV_HEADS, HEAD_DIM)
    v = proj[:, o_k + KV_W:o_k + 2 * KV_W].reshape(bp, tp, N_KV_HEADS, HEAD_DIM)
    ki = small[:, :IDX_DIM].reshape(bp, tp, IDX_DIM)
    conv_new = proj.reshape(bp, tp, N_MAIN)[:, tp - (CONV_W - 1):, :D_RNN]
    return y.reshape(bp, tp, D_MODEL), (k, v, ki, conv_new, h_last.reshape(bp, D_RNN))


def _sample_layer(x, p, cache_k, cache_v, cache_kidx, state_conv, state_h, page_table, w):
    bs, ts, _ = x.shape
    n_pages = page_table.shape[1]
    past = n_pages * PAGE_SIZE
    xt = x.reshape(bs * ts, D_MODEL)
    cs, sn = _rope_tables(past + jnp.tile(jnp.arange(ts, dtype=jnp.int32), bs))
    proj, small, _, _ = _inproj(xt, w["ln1"], w["w_in"], w["w_gates"], w["w_small"], w["colctl"], cs, sn,
                                w["bd"], tm=bs * ts)
    conv0 = jnp.concatenate([jnp.zeros((bs, 8 - (CONV_W - 1), D_RNN), F32), state_conv], axis=1)
    rnn, h_last = _rglru(proj, conv0, state_h[:, None, :], w["cw"], w["cb"], w["wa_bd"], w["ba"],
                         w["wx_bd"], w["bx"], w["lam"], n_seq=bs, tt=ts)
    o_q = 2 * D_RNN
    o_qi = o_q + ATTN_W + 2 * KV_W
    qi = proj[:, o_qi:o_qi + N_IDX_HEADS * IDX_DIM].reshape(bs, ts, N_IDX_HEADS, IDX_DIM)
    qst = jnp.transpose(qi, (0, 2, 1, 3)).reshape(bs, N_IDX_HEADS * ts, IDX_DIM).astype(BF16)
    wi = small[:, IDX_DIM:IDX_DIM + N_IDX_HEADS].reshape(bs, ts, N_IDX_HEADS)
    wcol = jnp.transpose(wi, (0, 2, 1)).reshape(bs, N_IDX_HEADS * ts, 1)
    n_pool = cache_k.shape[0]
    kidx_t = jnp.transpose(cache_kidx, (0, 2, 1))
    k_t = jnp.transpose(cache_k, (0, 2, 3, 1)).reshape(n_pool, KV_W, PAGE_SIZE)
    v_t = jnp.transpose(cache_v, (0, 2, 3, 1)).reshape(n_pool, KV_W, PAGE_SIZE)
    scores = _sample_select(page_table, qst, wcol, small, kidx_t, n_new=ts)
    bias = _select_bias(scores.reshape(bs * ts, -1), n_new=ts, past=past).reshape(scores.shape)
    q = proj[:, o_q:o_q + ATTN_W].reshape(bs, ts, N_KV_HEADS, N_HEADS // N_KV_HEADS, HEAD_DIM)
    eye = jnp.eye(N_KV_HEADS, dtype=F32)
    qbd = jnp.einsum("btgjd,gk->bgjtkd", q, eye).reshape(bs, N_HEADS * ts, KV_W).astype(BF16)
    att = _sample_attend(page_table, qbd, bias, proj, k_t, v_t, n_new=ts)
    att = att.reshape(bs, N_KV_HEADS, N_HEADS // N_KV_HEADS, ts, N_KV_HEADS, HEAD_DIM)
    att = jnp.stack([att[:, g, :, :, g, :] for g in range(N_KV_HEADS)], axis=1)
    attn = jnp.transpose(att, (0, 3, 1, 2, 4)).reshape(bs * ts, ATTN_W)
    y = _tail(xt, rnn, attn, proj, p.reshape(bs * ts, PLE_DIM), w)
    o_k = o_q + ATTN_W
    k = proj[:, o_k:o_k + KV_W].reshape(bs, ts, N_KV_HEADS, HEAD_DIM)
    v = proj[:, o_k + KV_W:o_k + 2 * KV_W].reshape(bs, ts, N_KV_HEADS, HEAD_DIM)
    ki = small[:, :IDX_DIM].reshape(bs, ts, IDX_DIM)
    conv_new = proj.reshape(bs, ts, N_MAIN)[:, ts - (CONV_W - 1):, :D_RNN]
    return y.reshape(bs, ts, D_MODEL), (k, v, ki, conv_new, h_last.reshape(bs, D_RNN))


def kernel(x_prompt, x_sample, p_prompt, p_sample, cache_k, cache_v, cache_kidx, state_conv, state_h,
           page_table, ln1, w_in, q_norm, k_norm, conv_w, conv_b, w_a, b_a, w_x, b_x, lam, w_br_rnn,
           w_br_attn, w_out, ln2, w_rg, b_rg, w_re, b_re, w1, w3, w2, ln3, w_ple_gate, w_ple_proj):
    weights = (ln1, w_in, q_norm, k_norm, conv_w, conv_b, w_a, b_a, w_x, b_x, lam, w_br_rnn, w_br_attn,
               w_out, ln2, w_rg, b_rg, w_re, b_re, w1, w3, w2, ln3, w_ple_gate, w_ple_proj)
    depth = ln1.shape[0]
    yp, ys = x_prompt, x_sample
    st_p, st_s = [], []
    for i in range(depth):
        w = _layer_weights(*[wt[i] for wt in weights])
        yp, sp = _prompt_layer(yp, p_prompt[i], w)
        ys, ss = _sample_layer(ys, p_sample[i], cache_k[i], cache_v[i], cache_kidx[i], state_conv[i],
                               state_h[i], page_table, w)
        st_p.append(sp)
        st_s.append(ss)
    stack = lambda sts, j: jnp.stack([s[j] for s in sts])
    return (yp, ys, stack(st_p, 0), stack(st_p, 1), stack(st_p, 2), stack(st_p, 3), stack(st_p, 4),
            stack(st_s, 0), stack(st_s, 1), stack(st_s, 2), stack(st_s, 3), stack(st_s, 4))
```

```python
import functools

import jax
import jax.numpy as jnp
import numpy as np
from jax import lax
from jax.experimental import pallas as pl
from jax.experimental.pallas import tpu as pltpu

F32 = jnp.float32
BF16 = jnp.bfloat16

D_MODEL = 2048
HEAD_DIM = 64
N_HEADS = 16
N_KV_HEADS = 4
ATTN_W = N_HEADS * HEAD_DIM
KV_W = N_KV_HEADS * HEAD_DIM
N_IDX_HEADS = 8
IDX_DIM = 64
TOPK_MAX = 256
ROPE_THETA = 10000.0
D_RNN = 1024
N_RNN_BLOCKS = 16
RNN_BLOCK = 64
CONV_W = 4
LRU_C = 8.0
N_GROUPS = 4
EXPERTS_PER_GROUP = 8
N_EXPERTS = 32
D_EXPERT = 256
PLE_DIM = 256
PAGE_SIZE = 128
EPS = 1e-6

LANES = 128
N_MAIN = 8192
COL_RAW_END = 2 * D_RNN
COL_NR_END = 4096
IDX_SCALE = (IDX_DIM ** -0.5) * (N_IDX_HEADS ** -0.5)
QK_SCALE = HEAD_DIM ** -0.5
F32_MIN = float(np.finfo(np.float32).min)
INT_MIN = -2147483648
KEY_NEG_INF = INT_MIN + 0x7FFFFF
VMEM_LIMIT = 56 * 1024 * 1024


def _dot(a, b):
    return jnp.dot(a, b, preferred_element_type=F32)


def _dot_nt(a, b):
    return lax.dot_general(a, b, (((1,), (1,)), ((), ())), preferred_element_type=F32)


def _sigmoid(x):
    return 1.0 / (1.0 + jnp.exp(-x))


def _rms(x, g):
    return x * lax.rsqrt(jnp.mean(x * x, axis=-1, keepdims=True) + EPS) * g


def _split_bf16(x):
    hi = x.astype(BF16)
    lo = (x - hi.astype(F32)).astype(BF16)
    return hi, lo


def _rope_chunks(y, c, s):
    lane = lax.broadcasted_iota(jnp.int32, (1, LANES), 1)
    first_half = (lane % HEAD_DIM) < (HEAD_DIM // 2)
    outs = []
    for k in range(y.shape[1] // LANES):
        yc = y[:, k * LANES:(k + 1) * LANES]
        partner = jnp.where(first_half, pltpu.roll(yc, LANES - HEAD_DIM // 2, 1),
                            pltpu.roll(yc, HEAD_DIM // 2, 1))
        outs.append(yc * c + partner * s)
    return outs[0] if len(outs) == 1 else jnp.concatenate(outs, axis=1)


def _inproj_kernel(x_ref, ln_ref, wlo_ref, whi_ref, ws_ref, ctl_ref, cs_ref, sn_ref, bd_ref,
                   o_ref, os_ref, kvb_ref, kib_ref, h_ref, *, tn):
    j = pl.program_id(1)

    @pl.when(j == 0)
    def _():
        hb = _rms(x_ref[...], ln_ref[...]).astype(BF16)
        h_ref[...] = hb
        ys = _dot_nt(hb, ws_ref[...].astype(BF16))
        lane = lax.broadcasted_iota(jnp.int32, (1, LANES), 1)
        roped = _rope_chunks(ys, cs_ref[...], sn_ref[...])
        os_ref[...] = jnp.where(lane < IDX_DIM, roped, ys)
        kib_ref[...] = jnp.where(lane < IDX_DIM, roped, 0.0).astype(BF16)

    @pl.when(j < COL_NR_END // tn)
    def _():
        o_ref[...] = _dot_nt(h_ref[...], wlo_ref[...].astype(BF16))

    @pl.when(j >= COL_NR_END // tn)
    def _():
        o_ref[...] = _dot_nt(h_ref[...], whi_ref[...].astype(BF16))

    @pl.when((j >= COL_RAW_END // tn) & (j < COL_NR_END // tn))
    def _():
        y = o_ref[...]
        ctl = ctl_ref[...]
        gain, norm_on, rope_on, post = ctl[0:1], ctl[1:2], ctl[2:3], ctl[3:4]
        hi, lo = _split_bf16(y * y)
        ss = _dot(hi, bd_ref[...]) + _dot(lo, bd_ref[...])
        yn = jnp.where(norm_on > 0.0, y * lax.rsqrt(ss * (1.0 / HEAD_DIM) + EPS) * gain, y)
        yr = jnp.where(rope_on > 0.0, _rope_chunks(yn, cs_ref[...], sn_ref[...]), yn)
        o_ref[...] = yr * post

    @pl.when(j == (COL_RAW_END + ATTN_W) // tn)
    def _():
        kvb_ref[...] = o_ref[...].astype(BF16)


def _inproj(x, ln1, w_t, w_small_t, colctl, cs, sn, bd, *, tm, tn=512):
    t = x.shape[0]
    assert tn == 2 * KV_W
    grid = (t // tm, N_MAIN // tn)
    n_lo = COL_NR_END // tn
    n_rope = cs.shape[0] // tm
    gate_row0 = COL_NR_END + IDX_DIM + N_IDX_HEADS
    return pl.pallas_call(
        functools.partial(_inproj_kernel, tn=tn),
        grid=grid,
        in_specs=[
            pl.BlockSpec((tm, D_MODEL), lambda i, j: (i, 0)),
            pl.BlockSpec((1, D_MODEL), lambda i, j: (0, 0)),
            pl.BlockSpec((tn, D_MODEL), lambda i, j: (jnp.minimum(j, n_lo - 1), 0)),
            pl.BlockSpec((pl.Element(tn), pl.Element(D_MODEL)),
                         lambda i, j: (pl.multiple_of(gate_row0 + jnp.maximum(j - n_lo, 0) * tn, 8), 0)),
            pl.BlockSpec((LANES, D_MODEL), lambda i, j: (0, 0)),
            pl.BlockSpec((8, tn), lambda i, j: (0, j)),
            pl.BlockSpec((tm, LANES), lambda i, j: (i % n_rope, 0)),
            pl.BlockSpec((tm, LANES), lambda i, j: (i % n_rope, 0)),
            pl.BlockSpec((tn, tn), lambda i, j: (0, 0)),
        ],
        out_specs=[
            pl.BlockSpec((tm, tn), lambda i, j: (i, j)),
            pl.BlockSpec((tm, LANES), lambda i, j: (i, 0)),
            pl.BlockSpec((tm, 2 * KV_W), lambda i, j: (i, 0)),
            pl.BlockSpec((tm, LANES), lambda i, j: (i, 0)),
        ],
        out_shape=[
            jax.ShapeDtypeStruct((t, N_MAIN), F32),
            jax.ShapeDtypeStruct((t, LANES), F32),
            jax.ShapeDtypeStruct((t, 2 * KV_W), BF16),
            jax.ShapeDtypeStruct((t, LANES), BF16),
        ],
        scratch_shapes=[pltpu.VMEM((tm, D_MODEL), BF16)],
        compiler_params=pltpu.CompilerParams(
            dimension_semantics=("arbitrary", "arbitrary"), vmem_limit_bytes=VMEM_LIMIT),
        name="inproj",
    )(x, ln1, w_t, w_t, w_small_t, colctl, cs, sn, bd)


def _rglru_kernel(x_ref, g_ref, c0_ref, h0_ref, cw_ref, cb_ref, wa_ref, ba_ref, wx_ref, bx_ref,
                  lam_ref, o_ref, hl_ref, xs_ref, a_ref, b_ref, hc_ref):
    t = pl.program_id(1)
    tt = x_ref.shape[0]

    @pl.when(t == 0)
    def _():
        xs_ref[0:8, :] = c0_ref[...]
        hc_ref[...] = h0_ref[...]

    xs_ref[8:8 + tt, :] = x_ref[...]
    cw = cw_ref[...]
    taps = (xs_ref[5:5 + tt, :] * cw[0:1] + xs_ref[6:6 + tt, :] * cw[1:2]
            + xs_ref[7:7 + tt, :] * cw[2:3] + xs_ref[8:8 + tt, :] * cw[3:4])
    xc = cb_ref[...] + taps
    xs_ref[0:8, :] = xs_ref[tt:tt + 8, :]

    xcb = xc.astype(BF16)
    ra, ri = [], []
    for c in range(wa_ref.shape[0]):
        blk = xcb[:, c * 256:(c + 1) * 256]
        ra.append(_dot(blk, wa_ref[c]))
        ri.append(_dot(blk, wx_ref[c]))
    r = _sigmoid(jnp.concatenate(ra, axis=1) + ba_ref[...])
    ig = _sigmoid(jnp.concatenate(ri, axis=1) + bx_ref[...])
    nlam = -lam_ref[...]
    softplus = jnp.maximum(nlam, 0.0) + jnp.log1p(jnp.exp(-jnp.abs(nlam)))
    log_a = (-LRU_C) * r * softplus
    a = jnp.exp(log_a)
    u = jnp.sqrt(jnp.tanh(-log_a) * (a * a + 1.0)) * (ig * xc)

    n8 = tt // 8
    a3 = a.reshape(n8, 8, D_RNN)
    b3 = u.reshape(n8, 8, D_RNN)
    sub = lax.broadcasted_iota(jnp.int32, (1, 8, 1), 1)
    for s in (1, 2, 4):
        a_prev = pltpu.roll(a3, s, 1)
        b_prev = pltpu.roll(b3, s, 1)
        m = sub >= s
        b3 = jnp.where(m, a3 * b_prev + b3, b3)
        a3 = jnp.where(m, a3 * a_prev, a3)
    a_ref[...] = a3.reshape(tt, D_RNN)
    b_ref[...] = b3.reshape(tt, D_RNN)

    def chain(k, carry):
        i0 = pl.multiple_of(k * 8, 8)
        h8 = a_ref[pl.ds(i0, 8), :] * carry + b_ref[pl.ds(i0, 8), :]
        b_ref[pl.ds(i0, 8), :] = h8
        return h8[7:8, :]

    carry = lax.fori_loop(0, n8, chain, hc_ref[...])
    hc_ref[...] = carry
    g = g_ref[...]
    gelu = 0.5 * g * (1.0 + jnp.tanh(0.7978845608028654 * (g + 0.044715 * (g * g * g))))
    o_ref[...] = b_ref[...] * gelu

    @pl.when(t == pl.num_programs(1) - 1)
    def _():
        hl_ref[...] = carry


def _rglru(proj, conv0, h0, cw, cb, wa_bd, ba, wx_bd, bx, lam, *, n_seq, tt):
    t_total = proj.shape[0]
    nt = t_total // (n_seq * tt)
    full = lambda shape: pl.BlockSpec(shape, lambda b, t: (0,) * len(shape))
    return pl.pallas_call(
        _rglru_kernel,
        grid=(n_seq, nt),
        in_specs=[
            pl.BlockSpec((tt, D_RNN), lambda b, t: (b * nt + t, 0)),
            pl.BlockSpec((tt, D_RNN), lambda b, t: (b * nt + t, 1)),
            pl.BlockSpec((None, 8, D_RNN), lambda b, t: (b, 0, 0)),
            pl.BlockSpec((None, 1, D_RNN), lambda b, t: (b, 0, 0)),
            full((CONV_W, D_RNN)), full((1, D_RNN)),
            full(wa_bd.shape), full((1, D_RNN)),
            full(wx_bd.shape), full((1, D_RNN)),
            full((1, D_RNN)),
        ],
        out_specs=[
            pl.BlockSpec((tt, D_RNN), lambda b, t: (b * nt + t, 0)),
            pl.BlockSpec((None, 1, D_RNN), lambda b, t: (b, 0, 0)),
        ],
        out_shape=[
            jax.ShapeDtypeStruct((t_total, D_RNN), F32),
            jax.ShapeDtypeStruct((n_seq, 1, D_RNN), F32),
        ],
        scratch_shapes=[
            pltpu.VMEM((tt + 8, D_RNN), F32),
            pltpu.VMEM((tt, D_RNN), F32),
            pltpu.VMEM((tt, D_RNN), F32),
            pltpu.VMEM((1, D_RNN), F32),
        ],
        compiler_params=pltpu.CompilerParams(
            dimension_semantics=("arbitrary", "arbitrary"), vmem_limit_bytes=VMEM_LIMIT),
        name="rglru",
    )(proj, proj, conv0, h0, cw, cb, wa_bd, ba, wx_bd, bx, lam)


def _select_topk(s, kk):
    rows, n = s.shape
    kkf = float(kk)

    def key_to_f32(w):
        k = w ^ INT_MIN
        bits = jnp.where(k >= 0, k, k ^ 0x7FFFFFFF)
        return k, lax.bitcast_convert_type(bits, F32)

    def vbody(it, w):
        cand_w = w | jnp.left_shift(jnp.int32(1), 31 - it)
        cand_k, cand_f = key_to_f32(cand_w)
        cnt = jnp.sum(jnp.where(s >= cand_f, 1.0, 0.0), axis=1, keepdims=True)
        ok = (cnt >= kkf) | (cand_k < KEY_NEG_INF)
        return jnp.where(ok, cand_w, w)

    w = lax.fori_loop(0, 32, vbody, jnp.zeros((rows, 1), jnp.int32))
    _, thr = key_to_f32(w)
    gt = s > thr
    eq = s == thr
    need = kkf - jnp.sum(jnp.where(gt, 1.0, 0.0), axis=1, keepdims=True)
    col = lax.broadcasted_iota(jnp.int32, (1, n), 1)
    nbits = int(n).bit_length()

    def jbody(it, jmax):
        cand = jmax | jnp.left_shift(jnp.int32(1), nbits - 1 - it)
        cnt = jnp.sum(jnp.where(eq & (col < cand), 1.0, 0.0), axis=1, keepdims=True)
        return jnp.where(cnt <= need, cand, jmax)

    n_ge = jnp.sum(jnp.where(s >= thr, 1.0, 0.0), axis=1, keepdims=True)
    jmax = lax.cond(
        jnp.max(n_ge) > kkf,
        lambda: lax.fori_loop(0, nbits, jbody, jnp.zeros((rows, 1), jnp.int32)),
        lambda: jnp.full((rows, 1), (1 << nbits) - 1, jnp.int32))
    return gt | (eq & (col < jmax))


def _pattn_kernel(q_ref, qi_ref, sm_ref, k_ref, v_ref, ki_ref, o_ref, s_ref, *, i0, n_keys, kc, topk):
    i = pl.program_id(0)
    n_batch, tq = q_ref.shape[0], q_ref.shape[1]
    lane = lax.broadcasted_iota(jnp.int32, (1, LANES), 1)
    qpos = (i0 + i) * tq + lax.broadcasted_iota(jnp.int32, (tq, 1), 0)

    def score(b, carry):
        sm = sm_ref[b]
        qi = qi_ref[b]
        qrows, wrows = [], []
        for h in range(N_IDX_HEADS):
            blk = qi[:, (h // 2) * LANES:(h // 2 + 1) * LANES]
            if h % 2 == 1:
                blk = pltpu.roll(blk, IDX_DIM, 1)
            qrows.append(jnp.where(lane < IDX_DIM, blk, 0.0))
            wrows.append(sm[:, IDX_DIM + h:IDX_DIM + h + 1])
        qst = jnp.concatenate(qrows, axis=0).astype(BF16)
        wst = jnp.concatenate(wrows, axis=0) * IDX_SCALE
        r0 = pl.multiple_of(b * tq, tq)
        for c in range(n_keys // kc):
            s = jnp.maximum(_dot_nt(qst, ki_ref[b, c * kc:(c + 1) * kc, :]), 0.0) * wst
            sc = s[0:tq]
            for h in range(1, N_IDX_HEADS):
                sc = sc + s[h * tq:(h + 1) * tq]
            col = c * kc + lax.broadcasted_iota(jnp.int32, (1, kc), 1)
            s_ref[pl.ds(r0, tq), c * kc:(c + 1) * kc] = jnp.where(col <= qpos, sc, F32_MIN)
        return carry

    lax.fori_loop(0, n_batch, score, 0)

    sel = _select_topk(s_ref[...], topk)
    colf = lax.broadcasted_iota(jnp.int32, (1, n_keys), 1)
    qpos_all = jnp.concatenate([qpos] * n_batch, axis=0)
    s_ref[...] = jnp.where(sel & (colf <= qpos_all), 0.0, -jnp.inf)

    def attend(b, carry):
        _attend_tile(q_ref[b], s_ref[pl.ds(pl.multiple_of(b * tq, tq), tq), :],
                     k_ref[b, 0:n_keys, :], v_ref[b, 0:n_keys, :], o_ref.at[b])
        return carry

    lax.fori_loop(0, n_batch, attend, 0)


def _attend_tile(q, bias, kb, vb, o_ref):
    tq = q.shape[0]
    lane = lax.broadcasted_iota(jnp.int32, (1, LANES), 1)
    bias4 = jnp.concatenate([bias] * 4, axis=0)
    outs = [None] * N_HEADS
    for g in range(N_KV_HEADS):
        lo = (g % 2) * HEAD_DIM
        keep = (lane >= lo) & (lane < lo + HEAD_DIM)
        rows = []
        for j in range(4):
            h = 4 * g + j
            blk = q[:, (h // 2) * LANES:(h // 2 + 1) * LANES]
            if h % 2 != g % 2:
                blk = pltpu.roll(blk, HEAD_DIM, 1)
            piece = jnp.where(keep, blk, 0.0)
            zero = jnp.zeros_like(piece)
            rows.append(jnp.concatenate([piece, zero] if g < 2 else [zero, piece], axis=1))
        qbd = jnp.concatenate(rows, axis=0).astype(BF16)
        logits = _dot_nt(qbd, kb) + bias4
        m = jnp.max(logits, axis=1, keepdims=True)
        p = jnp.exp(logits - m)
        denom = jnp.sum(p, axis=1, keepdims=True)
        acc = _dot(p.astype(BF16), vb) / denom
        for j in range(4):
            outs[4 * g + j] = acc[j * tq:(j + 1) * tq, (g // 2) * LANES:(g // 2 + 1) * LANES]
    for c in range(N_HEADS // 2):
        g = (2 * c) // 4
        even, odd = outs[2 * c], outs[2 * c + 1]
        if g % 2 == 1:
            even = pltpu.roll(even, HEAD_DIM, 1)
        else:
            odd = pltpu.roll(odd, HEAD_DIM, 1)
        o_ref[:, c * LANES:(c + 1) * LANES] = jnp.where(lane < HEAD_DIM, even, odd).astype(BF16)


def _prompt_attention_part(proj3, small3, kvb, kib, *, i0, n_tiles, tq):
    n_batch, seq, _ = proj3.shape
    n_keys = (i0 + n_tiles) * tq
    kc = next(c for c in (512, 256, 128) if n_keys % c == 0)
    return pl.pallas_call(
        functools.partial(_pattn_kernel, i0=i0, n_keys=n_keys, kc=kc, topk=min(TOPK_MAX, seq // 4)),
        grid=(n_tiles,),
        in_specs=[
            pl.BlockSpec((n_batch, tq, ATTN_W), lambda i: (0, i0 + i, 2)),
            pl.BlockSpec((n_batch, tq, N_IDX_HEADS * IDX_DIM), lambda i: (0, i0 + i, 7)),
            pl.BlockSpec((n_batch, tq, LANES), lambda i: (0, i0 + i, 0)),
            pl.BlockSpec((n_batch, seq, KV_W), lambda i: (0, 0, 0)),
            pl.BlockSpec((n_batch, seq, KV_W), lambda i: (0, 0, 1)),
            pl.BlockSpec((n_batch, seq, LANES), lambda i: (0, 0, 0)),
        ],
        out_specs=pl.BlockSpec((n_batch, tq, ATTN_W), lambda i: (0, i, 0)),
        out_shape=jax.ShapeDtypeStruct((n_batch, n_tiles * tq, ATTN_W), BF16),
        scratch_shapes=[pltpu.VMEM((n_batch * tq, n_keys), F32)],
        compiler_params=pltpu.CompilerParams(
            dimension_semantics=("arbitrary",), vmem_limit_bytes=VMEM_LIMIT),
        name=f"prompt_attention_{i0}",
    )(proj3, proj3, small3, kvb, kvb, kib)


def _prompt_attention(proj, small, kvb, kib, *, n_batch, seq, tq=128, tiles_per_part=2):
    nq = seq // tq
    proj3 = proj.reshape(n_batch, seq, N_MAIN)
    small3 = small.reshape(n_batch, seq, LANES)
    kvb3 = kvb.reshape(n_batch, seq, 2 * KV_W)
    kib3 = kib.reshape(n_batch, seq, LANES)
    parts = [
        _prompt_attention_part(proj3, small3, kvb3, kib3, i0=i0,
                               n_tiles=min(tiles_per_part, nq - i0), tq=tq)
        for i0 in range(0, nq, tiles_per_part)
    ]
    return jnp.concatenate(parts, axis=1).reshape(n_batch * seq, ATTN_W)


SELECT_PAGES_PER_STEP = 32
ATTEND_PAGES_PER_STEP = 32
SUB_PAGES = 8


def _sidx_kernel(pt_ref, qst_ref, w_ref, sm_ref, *refs, n_chunks, n_new, ps):
    pages = refs[:ps]
    o_ref, s_ref = refs[ps], refs[ps + 1]
    c = pl.program_id(1)
    qst = qst_ref[...]
    w = w_ref[...] * IDX_SCALE

    def head_sum(s):
        s = jnp.maximum(s, 0.0) * w
        out = s[0:n_new]
        for h in range(1, N_IDX_HEADS):
            out = out + s[h * n_new:(h + 1) * n_new]
        return out

    for r0 in range(0, ps, SUB_PAGES):
        kt = jnp.concatenate([pages[r0 + r][...] for r in range(SUB_PAGES)], axis=1).astype(BF16)
        part = head_sum(_dot(qst, kt))
        for r in range(SUB_PAGES):
            s_ref[ps * c + r0 + r] = part[:, r * PAGE_SIZE:(r + 1) * PAGE_SIZE]

    @pl.when(c == n_chunks - 1)
    def _():
        n_past_blocks = n_chunks * ps
        past = n_past_blocks * PAGE_SIZE
        k_new = sm_ref[...][:, 0:IDX_DIM]
        kp = jnp.concatenate([k_new, jnp.zeros((PAGE_SIZE - n_new, IDX_DIM), F32)], axis=0)
        lane = lax.broadcasted_iota(jnp.int32, (n_new, LANES), 1)
        trow = lax.broadcasted_iota(jnp.int32, (n_new, LANES), 0)
        s_new = head_sum(_dot_nt(qst, kp.astype(BF16)))
        s_ref[n_past_blocks] = jnp.where(lane < n_new, jnp.where(lane <= trow, s_new, F32_MIN), -jnp.inf)
        o_ref[...] = jnp.concatenate([s_ref[k] for k in range(n_past_blocks + 1)], axis=1)


def _select_bias_kernel(s_ref, o_ref, *, n_new, past, topk):
    s = s_ref[...]
    rows, n_all = s.shape
    sel = _select_topk(s, topk)
    col = lax.broadcasted_iota(jnp.int32, (1, n_all), 1)
    tq = lax.broadcasted_iota(jnp.int32, (rows, 1), 0) % n_new
    o_ref[...] = jnp.where(sel & ((col - past) <= tq), 0.0, -jnp.inf)


def _select_bias(scores, *, n_new, past, rows_per_step=128):
    rows, n_all = scores.shape
    rows_per_step = min(rows_per_step, rows)
    return pl.pallas_call(
        functools.partial(_select_bias_kernel, n_new=n_new, past=past,
                          topk=min(TOPK_MAX, (past + n_new) // 4)),
        grid=(rows // rows_per_step,),
        in_specs=[pl.BlockSpec((rows_per_step, n_all), lambda i: (i, 0))],
        out_specs=pl.BlockSpec((rows_per_step, n_all), lambda i: (i, 0)),
        out_shape=jax.ShapeDtypeStruct((rows, n_all), F32),
        compiler_params=pltpu.CompilerParams(
            dimension_semantics=("arbitrary",), vmem_limit_bytes=VMEM_LIMIT),
        name="sample_select_bias",
    )(scores)


def _sample_select(page_table, qst, wcol, small, cache_kidx_t, *, n_new):
    n_seq, n_pages = page_table.shape
    ps = min(SELECT_PAGES_PER_STEP, n_pages)
    n_chunks = n_pages // ps
    n_all = n_pages * PAGE_SIZE + LANES
    page_specs = [
        pl.BlockSpec((None, IDX_DIM, PAGE_SIZE), lambda b, c, pt, r=r: (pt[b, ps * c + r], 0, 0))
        for r in range(ps)
    ]
    rows = N_IDX_HEADS * n_new
    grid_spec = pltpu.PrefetchScalarGridSpec(
        num_scalar_prefetch=1,
        grid=(n_seq, n_chunks),
        in_specs=[
            pl.BlockSpec((None, rows, IDX_DIM), lambda b, c, pt: (b, 0, 0)),
            pl.BlockSpec((None, rows, 1), lambda b, c, pt: (b, 0, 0)),
            pl.BlockSpec((n_new, LANES), lambda b, c, pt: (b, 0)),
        ] + page_specs,
        out_specs=pl.BlockSpec((None, n_new, n_all), lambda b, c, pt: (b, 0, 0)),
        scratch_shapes=[pltpu.VMEM((n_pages + 1, n_new, LANES), F32)],
    )
    return pl.pallas_call(
        functools.partial(_sidx_kernel, n_chunks=n_chunks, n_new=n_new, ps=ps),
        grid_spec=grid_spec,
        out_shape=jax.ShapeDtypeStruct((n_seq, n_new, n_all), F32),
        compiler_params=pltpu.CompilerParams(
            dimension_semantics=("arbitrary", "arbitrary"), vmem_limit_bytes=VMEM_LIMIT),
        name="sample_select",
    )(page_table, qst, wcol, small, *([cache_kidx_t] * ps))


def _sattn_kernel(pt_ref, q_ref, bias_ref, biasn_ref, kn_ref, vn_ref, *refs, n_chunks, n_new, ps):
    kpages = refs[:ps]
    vpages = refs[ps:2 * ps]
    o_ref, m_ref, l_ref, acc_ref = refs[2 * ps:]
    c = pl.program_id(1)
    rows = q_ref.shape[0]
    reps = rows // n_new

    @pl.when(c == 0)
    def _():
        m_ref[...] = jnp.full(m_ref.shape, -1e30, F32)
        l_ref[...] = jnp.zeros(l_ref.shape, F32)
        acc_ref[...] = jnp.zeros(acc_ref.shape, F32)

    def update(logits, bias, pv):
        logits = logits + jnp.concatenate([bias] * reps, axis=0)
        m_old = m_ref[...]
        m_new = jnp.maximum(m_old, jnp.max(logits, axis=1, keepdims=True))
        alpha = jnp.exp(m_old - m_new)
        p = jnp.exp(logits - m_new)
        l_ref[...] = alpha * l_ref[...] + jnp.sum(p, axis=1, keepdims=True)
        acc_ref[...] = alpha * acc_ref[...] + pv(p.astype(BF16))
        m_ref[...] = m_new

    sub_keys = SUB_PAGES * PAGE_SIZE
    logits, vts = [], []
    for r0 in range(0, ps, SUB_PAGES):
        kt = jnp.concatenate([kpages[r0 + r][...] for r in range(SUB_PAGES)], axis=1).astype(BF16)
        vts.append(jnp.concatenate([vpages[r0 + r][...] for r in range(SUB_PAGES)], axis=1).astype(BF16))
        logits.append(_dot(q_ref[...], kt))

    def pv(p):
        acc = _dot_nt(p[:, 0:sub_keys], vts[0])
        for n in range(1, len(vts)):
            acc = acc + _dot_nt(p[:, n * sub_keys:(n + 1) * sub_keys], vts[n])
        return acc

    update(jnp.concatenate(logits, axis=1), bias_ref[...], pv)

    @pl.when(c == n_chunks - 1)
    def _():
        pad = jnp.zeros((PAGE_SIZE - n_new, KV_W), F32)
        kn = jnp.concatenate([kn_ref[...], pad], axis=0).astype(BF16)
        vn = jnp.concatenate([vn_ref[...], pad], axis=0).astype(BF16)
        update(_dot_nt(q_ref[...], kn), biasn_ref[...], lambda p: _dot(p, vn))
        o_ref[...] = acc_ref[...] / l_ref[...]


def _sample_attend(page_table, qbd, bias, proj, cache_k_t, cache_v_t, *, n_new):
    n_seq, n_pages = page_table.shape
    ps = min(ATTEND_PAGES_PER_STEP, n_pages)
    n_chunks = n_pages // ps
    rows = qbd.shape[1]
    chunk_keys = ps * PAGE_SIZE
    page_specs = [
        pl.BlockSpec((None, KV_W, PAGE_SIZE), lambda b, c, pt, r=r: (pt[b, ps * c + r], 0, 0))
        for r in range(ps)
    ]
    grid_spec = pltpu.PrefetchScalarGridSpec(
        num_scalar_prefetch=1,
        grid=(n_seq, n_chunks),
        in_specs=[
            pl.BlockSpec((None, rows, KV_W), lambda b, c, pt: (b, 0, 0)),
            pl.BlockSpec((None, n_new, chunk_keys), lambda b, c, pt: (b, 0, c)),
            pl.BlockSpec((None, n_new, LANES), lambda b, c, pt: (b, 0, n_pages)),
            pl.BlockSpec((n_new, KV_W), lambda b, c, pt: (b, 12)),
            pl.BlockSpec((n_new, KV_W), lambda b, c, pt: (b, 13)),
        ] + page_specs + page_specs,
        out_specs=pl.BlockSpec((None, rows, KV_W), lambda b, c, pt: (b, 0, 0)),
        scratch_shapes=[
            pltpu.VMEM((rows, 1), F32),
            pltpu.VMEM((rows, 1), F32),
            pltpu.VMEM((rows, KV_W), F32),
        ],
    )
    return pl.pallas_call(
        functools.partial(_sattn_kernel, n_chunks=n_chunks, n_new=n_new, ps=ps),
        grid_spec=grid_spec,
        out_shape=jax.ShapeDtypeStruct((n_seq, rows, KV_W), F32),
        compiler_params=pltpu.CompilerParams(
            dimension_semantics=("arbitrary", "arbitrary"), vmem_limit_bytes=VMEM_LIMIT),
        name="sample_attend",
    )(page_table, qbd, bias, bias, proj, proj, *([cache_k_t] * ps), *([cache_v_t] * ps))


def _merge_kernel(x_ref, rnn_ref, att_ref, gr_ref, ga_ref, wr_ref, wa_ref, wo_ref, o_ref):
    mixed = (_sigmoid(gr_ref[...]) * _dot(rnn_ref[...].astype(BF16), wr_ref[...])
             + _sigmoid(ga_ref[...]) * _dot(att_ref[...].astype(BF16), wa_ref[...]))
    o_ref[...] = x_ref[...] + _dot(mixed.astype(BF16), wo_ref[...])


def _merge(x, rnn, attn, proj, wr, wa, wo, *, tm):
    t = x.shape[0]
    return pl.pallas_call(
        _merge_kernel,
        grid=(t // tm,),
        in_specs=[
            pl.BlockSpec((tm, D_MODEL), lambda i: (i, 0)),
            pl.BlockSpec((tm, D_RNN), lambda i: (i, 0)),
            pl.BlockSpec((tm, ATTN_W), lambda i: (i, 0)),
            pl.BlockSpec((tm, D_MODEL), lambda i: (i, 2)),
            pl.BlockSpec((tm, D_MODEL), lambda i: (i, 3)),
            pl.BlockSpec((D_RNN, D_MODEL), lambda i: (0, 0)),
            pl.BlockSpec((ATTN_W, D_MODEL), lambda i: (0, 0)),
            pl.BlockSpec((D_MODEL, D_MODEL), lambda i: (0, 0)),
        ],
        out_specs=pl.BlockSpec((tm, D_MODEL), lambda i: (i, 0)),
        out_shape=jax.ShapeDtypeStruct((t, D_MODEL), F32),
        compiler_params=pltpu.CompilerParams(
            dimension_semantics=("arbitrary",), vmem_limit_bytes=VMEM_LIMIT),
        name="merge",
    )(x, rnn, attn, proj, proj, wr, wa, wo)


def _route(h, rwh, rwl, rb):
    lane = lax.broadcasted_iota(jnp.int32, (1, LANES), 1)
    lanef = lane.astype(F32)
    hh, hl = _split_bf16(h)
    lg = (_dot(hh, rwh) + _dot(hl, rwh) + _dot(hh, rwl)) + rb
    is_g = (lane >= N_EXPERTS) & (lane < N_EXPERTS + N_GROUPS)
    gl = jnp.where(is_g, lg, -jnp.inf)
    gmax = jnp.max(gl, axis=1, keepdims=True)
    gprob = 1.0 / jnp.sum(jnp.exp(gl - gmax), axis=1, keepdims=True)
    gsel = jnp.min(jnp.where(is_g & (lg == gmax), lanef - N_EXPERTS, 1e9), axis=1, keepdims=True)
    in_grp = (lane < N_EXPERTS) & (jnp.floor(lanef * (1.0 / EXPERTS_PER_GROUP)) == gsel)
    v1 = jnp.where(in_grp, lg, -jnp.inf)
    t1 = jnp.max(v1, axis=1, keepdims=True)
    i1 = jnp.min(jnp.where(in_grp & (lg == t1), lanef, 1e9), axis=1, keepdims=True)
    rest = in_grp & (lanef != i1)
    v2 = jnp.where(rest, lg, -jnp.inf)
    t2 = jnp.max(v2, axis=1, keepdims=True)
    i2 = jnp.min(jnp.where(rest & (lg == t2), lanef, 1e9), axis=1, keepdims=True)
    d = jnp.exp(t2 - t1)
    return i1, i2, gprob / (1.0 + d), gprob * d / (1.0 + d)


def _moe_kernel(x_ref, ln_ref, rwh_ref, rwl_ref, rb_ref, w1_ref, w3_ref, w2_ref, o_ref, h_ref, gate_ref):
    e = pl.program_id(1)
    lane = lax.broadcasted_iota(jnp.int32, (1, LANES), 1)

    @pl.when(e == 0)
    def _():
        h = _rms(x_ref[...], ln_ref[...])
        h_ref[...] = h.astype(BF16)
        i1, i2, g1, g2 = _route(h, rwh_ref[...], rwl_ref[...], rb_ref[...])
        lanef = lane.astype(F32)
        gate_ref[...] = jnp.where(lanef == i1, g1, 0.0) + jnp.where(lanef == i2, g2, 0.0)

    ge = jnp.sum(jnp.where(lane == e, gate_ref[...], 0.0), axis=1, keepdims=True)
    up = _dot(h_ref[...], w1_ref[...].astype(BF16))
    hid = (up * _sigmoid(up)) * _dot(h_ref[...], w3_ref[...].astype(BF16))
    contrib = _dot((hid * ge).astype(BF16), w2_ref[...].astype(BF16))

    @pl.when(e == 0)
    def _():
        o_ref[...] = x_ref[...] + contrib

    @pl.when(e > 0)
    def _():
        o_ref[...] += contrib


def _moe(x, ln2, rw_hi, rw_lo, rb, w1, w3, w2, *, tm):
    t = x.shape[0]
    return pl.pallas_call(
        _moe_kernel,
        grid=(t // tm, N_EXPERTS),
        in_specs=[
            pl.BlockSpec((tm, D_MODEL), lambda i, e: (i, 0)),
            pl.BlockSpec((1, D_MODEL), lambda i, e: (0, 0)),
            pl.BlockSpec((D_MODEL, LANES), lambda i, e: (0, 0)),
            pl.BlockSpec((D_MODEL, LANES), lambda i, e: (0, 0)),
            pl.BlockSpec((1, LANES), lambda i, e: (0, 0)),
            pl.BlockSpec((None, D_MODEL, D_EXPERT), lambda i, e: (e, 0, 0)),
            pl.BlockSpec((None, D_MODEL, D_EXPERT), lambda i, e: (e, 0, 0)),
            pl.BlockSpec((None, D_EXPERT, D_MODEL), lambda i, e: (e, 0, 0)),
        ],
        out_specs=pl.BlockSpec((tm, D_MODEL), lambda i, e: (i, 0)),
        out_shape=jax.ShapeDtypeStruct((t, D_MODEL), F32),
        scratch_shapes=[pltpu.VMEM((tm, D_MODEL), BF16), pltpu.VMEM((tm, LANES), F32)],
        compiler_params=pltpu.CompilerParams(
            dimension_semantics=("arbitrary", "arbitrary"), vmem_limit_bytes=VMEM_LIMIT),
        name="moe",
    )(x, ln2, rw_hi, rw_lo, rb, w1, w3, w2)


def _ple_update(x2, p, ln, wg, wp):
    gate = _sigmoid(_dot(_rms(x2, ln).astype(BF16), wg))
    return x2 + gate * _dot(p.astype(BF16), wp)


def _ple_kernel(x_ref, p_ref, ln_ref, wg_ref, wp_ref, o_ref):
    o_ref[...] = _ple_update(x_ref[...], p_ref[...], ln_ref[...], wg_ref[...], wp_ref[...])


def _ple(x, p, ln3, wg, wp, *, tm):
    t = x.shape[0]
    return pl.pallas_call(
        _ple_kernel,
        grid=(t // tm,),
        in_specs=[
            pl.BlockSpec((tm, D_MODEL), lambda i: (i, 0)),
            pl.BlockSpec((tm, PLE_DIM), lambda i: (i, 0)),
            pl.BlockSpec((1, D_MODEL), lambda i: (0, 0)),
            pl.BlockSpec((D_MODEL, D_MODEL), lambda i: (0, 0)),
            pl.BlockSpec((PLE_DIM, D_MODEL), lambda i: (0, 0)),
        ],
        out_specs=pl.BlockSpec((tm, D_MODEL), lambda i: (i, 0)),
        out_shape=jax.ShapeDtypeStruct((t, D_MODEL), F32),
        compiler_params=pltpu.CompilerParams(
            dimension_semantics=("arbitrary",), vmem_limit_bytes=VMEM_LIMIT),
        name="ple",
    )(x, p, ln3, wg, wp)


MOE_ROW_TILE = 256
META_E, META_G, META_RANK = 0, 2, 4


def _router_kernel(x_ref, ln_ref, rwh_ref, rwl_ref, rb_ref, tri_ref, meta_ref, cnt_ref, carry_ref):
    i = pl.program_id(0)
    lane = lax.broadcasted_iota(jnp.int32, (1, LANES), 1)
    lanef = lane.astype(F32)

    @pl.when(i == 0)
    def _():
        carry_ref[...] = jnp.zeros(carry_ref.shape, F32)

    i1, i2, g1, g2 = _route(_rms(x_ref[...], ln_ref[...]), rwh_ref[...], rwl_ref[...], rb_ref[...])
    onehot = jnp.where((lanef == i1) | (lanef == i2), 1.0, 0.0)
    before = _dot(tri_ref[...], onehot.astype(BF16)) + carry_ref[...]
    r1 = jnp.sum(jnp.where(lanef == i1, before, 0.0), axis=1, keepdims=True)
    r2 = jnp.sum(jnp.where(lanef == i2, before, 0.0), axis=1, keepdims=True)
    carry_ref[...] += jnp.sum(onehot, axis=0, keepdims=True)
    rec = jnp.zeros((x_ref.shape[0], LANES), F32)
    for k, val in ((META_E, i1), (META_E + 1, i2), (META_G, g1), (META_G + 1, g2),
                   (META_RANK, r1), (META_RANK + 1, r2)):
        rec = jnp.where(lane == k, val, rec)
    meta_ref[...] = rec
    cnt_ref[...] = jnp.broadcast_to(carry_ref[...], cnt_ref.shape)


def _router(x, ln2, rw_hi, rw_lo, rb, *, tm):
    t = x.shape[0]
    r = lax.broadcasted_iota(jnp.int32, (tm, tm), 0)
    c = lax.broadcasted_iota(jnp.int32, (tm, tm), 1)
    tri = (c < r).astype(BF16)
    return pl.pallas_call(
        _router_kernel,
        grid=(t // tm,),
        in_specs=[
            pl.BlockSpec((tm, D_MODEL), lambda i: (i, 0)),
            pl.BlockSpec((1, D_MODEL), lambda i: (0, 0)),
            pl.BlockSpec((D_MODEL, LANES), lambda i: (0, 0)),
            pl.BlockSpec((D_MODEL, LANES), lambda i: (0, 0)),
            pl.BlockSpec((1, LANES), lambda i: (0, 0)),
            pl.BlockSpec((tm, tm), lambda i: (0, 0)),
        ],
        out_specs=[
            pl.BlockSpec((tm, LANES), lambda i: (i, 0)),
            pl.BlockSpec((8, LANES), lambda i: (0, 0)),
        ],
        out_shape=[
            jax.ShapeDtypeStruct((t, LANES), F32),
            jax.ShapeDtypeStruct((8, LANES), F32),
        ],
        scratch_shapes=[pltpu.VMEM((1, LANES), F32)],
        compiler_params=pltpu.CompilerParams(
            dimension_semantics=("arbitrary",), vmem_limit_bytes=VMEM_LIMIT),
        name="moe_router",
    )(x, ln2, rw_hi, rw_lo, rb, tri)


def _row_copy(src_ref, src_row, dst_ref, dst_row, sem):
    return pltpu.make_async_copy(src_ref.at[pl.ds(src_row, 1), :], dst_ref.at[pl.ds(dst_row, 1), :], sem)


def _scatter_kernel(slot_ref, x_ref, hs_in_ref, hs_ref, sem, *, n_tok):
    del hs_in_ref
    tm = x_ref.shape[0]
    base = pl.program_id(0) * tm

    def start(r, carry):
        for k in range(2):
            _row_copy(x_ref, r, hs_ref, slot_ref[k * n_tok + base + r], sem).start()
        return carry

    lax.fori_loop(0, tm, start, 0)
    for k in range(2):
        pltpu.make_async_copy(x_ref, hs_ref.at[pl.ds(0, tm), :], sem).wait()


def _scatter_rows(slots, x, hs_zero, *, tm):
    t = x.shape[0]
    grid_spec = pltpu.PrefetchScalarGridSpec(
        num_scalar_prefetch=1,
        grid=(t // tm,),
        in_specs=[
            pl.BlockSpec((tm, D_MODEL), lambda i, s: (i, 0)),
            pl.BlockSpec(memory_space=pl.ANY),
        ],
        out_specs=pl.BlockSpec(memory_space=pl.ANY),
        scratch_shapes=[pltpu.SemaphoreType.DMA(())],
    )
    return pl.pallas_call(
        functools.partial(_scatter_kernel, n_tok=t),
        grid_spec=grid_spec,
        out_shape=jax.ShapeDtypeStruct(hs_zero.shape, F32),
        input_output_aliases={2: 0},
        compiler_params=pltpu.CompilerParams(
            dimension_semantics=("arbitrary",), vmem_limit_bytes=VMEM_LIMIT),
        name="moe_scatter",
    )(slots, x, hs_zero)


def _expert_kernel(te_ref, nu_ref, hs_ref, ln_ref, w1_ref, w3_ref, w2_ref, y_ref):
    i = pl.program_id(0)

    @pl.when(i < nu_ref[0])
    def _():
        h = _rms(hs_ref[...], ln_ref[...]).astype(BF16)
        up = _dot(h, w1_ref[...].astype(BF16))
        hid = (up * _sigmoid(up)) * _dot(h, w3_ref[...].astype(BF16))
        y_ref[...] = _dot(hid.astype(BF16), w2_ref[...].astype(BF16))

    @pl.when(i >= nu_ref[0])
    def _():
        y_ref[...] = jnp.zeros(y_ref.shape, F32)


def _expert_mlp(tile_expert, n_used, hs, ln2, w1, w3, w2):
    n_tiles = tile_expert.shape[0]
    grid_spec = pltpu.PrefetchScalarGridSpec(
        num_scalar_prefetch=2,
        grid=(n_tiles,),
        in_specs=[
            pl.BlockSpec((MOE_ROW_TILE, D_MODEL), lambda i, te, nu: (i, 0)),
            pl.BlockSpec((1, D_MODEL), lambda i, te, nu: (0, 0)),
            pl.BlockSpec((None, D_MODEL, D_EXPERT), lambda i, te, nu: (te[i], 0, 0)),
            pl.BlockSpec((None, D_MODEL, D_EXPERT), lambda i, te, nu: (te[i], 0, 0)),
            pl.BlockSpec((None, D_EXPERT, D_MODEL), lambda i, te, nu: (te[i], 0, 0)),
        ],
        out_specs=pl.BlockSpec((MOE_ROW_TILE, D_MODEL), lambda i, te, nu: (i, 0)),
    )
    return pl.pallas_call(
        _expert_kernel,
        grid_spec=grid_spec,
        out_shape=jax.ShapeDtypeStruct(hs.shape, F32),
        compiler_params=pltpu.CompilerParams(
            dimension_semantics=("arbitrary",), vmem_limit_bytes=VMEM_LIMIT),
        name="moe_experts",
    )(tile_expert, n_used, hs, ln2, w1, w3, w2)


def _combine_ple_kernel(slot_ref, x_ref, meta_ref, p_ref, ln_ref, wg_ref, wp_ref, y_ref, o_ref,
                        ybuf_ref, sem, *, n_tok):
    i = pl.program_id(0)
    tm = x_ref.shape[0]

    def gather(tile, buf):
        def start(r, carry):
            for k in range(2):
                _row_copy(y_ref, slot_ref[k * n_tok + tile * tm + r], ybuf_ref.at[buf, k], r,
                          sem.at[buf]).start()
            return carry
        lax.fori_loop(0, tm, start, 0)

    @pl.when(i == 0)
    def _():
        gather(0, 0)

    @pl.when(i + 1 < pl.num_programs(0))
    def _():
        gather(i + 1, (i + 1) % 2)

    cur = i % 2
    for k in range(2):
        pltpu.make_async_copy(y_ref.at[pl.ds(0, tm), :], ybuf_ref.at[cur, k], sem.at[cur]).wait()
    meta = meta_ref[...]
    x2 = (x_ref[...] + meta[:, META_G:META_G + 1] * ybuf_ref[cur, 0]
          + meta[:, META_G + 1:META_G + 2] * ybuf_ref[cur, 1])
    o_ref[...] = _ple_update(x2, p_ref[...], ln_ref[...], wg_ref[...], wp_ref[...])


def _combine_ple(slots, x, meta, p, ln3, wg, wp, y, *, tm):
    t = x.shape[0]
    grid_spec = pltpu.PrefetchScalarGridSpec(
        num_scalar_prefetch=1,
        grid=(t // tm,),
        in_specs=[
            pl.BlockSpec((tm, D_MODEL), lambda i, s: (i, 0)),
            pl.BlockSpec((tm, LANES), lambda i, s: (i, 0)),
            pl.BlockSpec((tm, PLE_DIM), lambda i, s: (i, 0)),
            pl.BlockSpec((1, D_MODEL), lambda i, s: (0, 0)),
            pl.BlockSpec((D_MODEL, D_MODEL), lambda i, s: (0, 0)),
            pl.BlockSpec((PLE_DIM, D_MODEL), lambda i, s: (0, 0)),
            pl.BlockSpec(memory_space=pl.ANY),
        ],
        out_specs=pl.BlockSpec((tm, D_MODEL), lambda i, s: (i, 0)),
        scratch_shapes=[
            pltpu.VMEM((2, 2, tm, D_MODEL), F32),
            pltpu.SemaphoreType.DMA((2,)),
        ],
    )
    return pl.pallas_call(
        functools.partial(_combine_ple_kernel, n_tok=t),
        grid_spec=grid_spec,
        out_shape=jax.ShapeDtypeStruct((t, D_MODEL), F32),
        compiler_params=pltpu.CompilerParams(
            dimension_semantics=("arbitrary",), vmem_limit_bytes=VMEM_LIMIT),
        name="moe_combine_ple",
    )(slots, x, meta, p, ln3, wg, wp, y)


def _sparse_moe_ple(x, p, w):
    t = x.shape[0]
    rt = MOE_ROW_TILE
    meta, cnt = _router(x, w["ln2"], w["rw_hi"], w["rw_lo"], w["rb"], tm=256)
    counts = cnt[0, :N_EXPERTS].astype(jnp.int32)
    padded = ((counts + rt - 1) // rt) * rt
    ends = jnp.cumsum(padded)
    offs = ends - padded
    eid = meta[:, META_E:META_E + 2].astype(jnp.int32)
    rank = meta[:, META_RANK:META_RANK + 2].astype(jnp.int32)
    base = jnp.sum(jnp.where(eid[:, :, None] == jnp.arange(N_EXPERTS), offs, 0), axis=-1)
    slots = jnp.transpose(base + rank).reshape(2 * t)
    n_tiles = (2 * t + N_EXPERTS * (rt - 1)) // rt
    tile_start = jnp.arange(n_tiles, dtype=jnp.int32) * rt
    tile_expert = jnp.minimum(jnp.sum((tile_start[:, None] >= ends[None, :]).astype(jnp.int32), axis=1),
                              N_EXPERTS - 1)
    n_used = (ends[N_EXPERTS - 1] // rt).reshape(1)
    hs = _scatter_rows(slots, x, jnp.zeros((n_tiles * rt, D_MODEL), F32), tm=256)
    y = _expert_mlp(tile_expert, n_used, hs, w["ln2"], w["w1"], w["w3"], w["w2"])
    return _combine_ple(slots, x, meta, p, w["ln3"], w["wg"], w["wp"], y, tm=256)


def _rope_tables(pos):
    half = HEAD_DIM // 2
    inv = ROPE_THETA ** (-jnp.arange(half, dtype=F32) / half)
    ang = pos.astype(F32)[:, None] * inv[None, :]
    cos, sin = jnp.cos(ang), jnp.sin(ang)
    return (jnp.concatenate([cos, cos, cos, cos], axis=1),
            jnp.concatenate([-sin, sin, -sin, sin], axis=1))


def _block_diag(w, per):
    n, r, _ = w.shape
    eye = jnp.eye(per, dtype=w.dtype)
    wg = w.reshape(n // per, per, r, r)
    return jnp.einsum("gpij,pq->gpiqj", wg, eye).reshape(n // per, per * r, per * r)


def _layer_weights(ln1, w_in, q_norm, k_norm, conv_w, conv_b, w_a, b_a, w_x, b_x, lam, w_br_rnn,
                   w_br_attn, w_out, ln2, w_rg, b_rg, w_re, b_re, w1, w3, w2, ln3, w_ple_gate,
                   w_ple_proj):
    o_q = 2 * D_RNN
    o_k = o_q + ATTN_W
    o_v = o_k + KV_W
    o_qi = o_v + KV_W
    o_ki = o_qi + N_IDX_HEADS * IDX_DIM
    o_wi = o_ki + IDX_DIM
    o_gr = o_wi + N_IDX_HEADS
    o_ga = o_gr + D_MODEL
    assert o_ki == COL_NR_END
    w_t = jnp.transpose(w_in)
    w_small_t = jnp.concatenate(
        [w_t[o_ki:o_gr], jnp.zeros((LANES - IDX_DIM - N_IDX_HEADS, D_MODEL), F32)], axis=0)
    ones = lambda n: jnp.ones((n,), F32)
    zeros = lambda n: jnp.zeros((n,), F32)
    n_gate = 2 * D_MODEL
    gain = jnp.concatenate([ones(o_q), jnp.tile(q_norm, N_HEADS), jnp.tile(k_norm, N_KV_HEADS),
                            ones(KV_W + N_IDX_HEADS * IDX_DIM + n_gate)])
    norm_on = jnp.concatenate([zeros(o_q), ones(ATTN_W + KV_W), zeros(KV_W + N_IDX_HEADS * IDX_DIM + n_gate)])
    rope_on = jnp.concatenate([zeros(o_q), ones(ATTN_W + KV_W), zeros(KV_W), ones(N_IDX_HEADS * IDX_DIM),
                               zeros(n_gate)])
    post = jnp.concatenate([ones(o_q), jnp.full((ATTN_W,), QK_SCALE, F32),
                            ones(2 * KV_W + N_IDX_HEADS * IDX_DIM + n_gate)])
    colctl = jnp.concatenate([jnp.stack([gain, norm_on, rope_on, post]), jnp.zeros((4, N_MAIN), F32)], axis=0)
    tn = 512
    head_of = jnp.arange(tn) // HEAD_DIM
    bd = (head_of[:, None] == head_of[None, :]).astype(BF16)
    rw = jnp.concatenate([w_re, w_rg, jnp.zeros((D_MODEL, LANES - N_EXPERTS - N_GROUPS), F32)], axis=1)
    rw_hi = rw.astype(BF16)
    rw_lo = (rw - rw_hi.astype(F32)).astype(BF16)
    rb = jnp.concatenate([b_re, b_rg, jnp.zeros((LANES - N_EXPERTS - N_GROUPS,), F32)])[None, :]
    return dict(
        ln1=ln1[None, :], w_t=w_t, w_small_t=w_small_t, colctl=colctl, bd=bd,
        cw=conv_w, cb=conv_b[None, :],
        wa_bd=_block_diag(w_a, 4).astype(BF16), ba=b_a[None, :],
        wx_bd=_block_diag(w_x, 4).astype(BF16), bx=b_x[None, :], lam=lam[None, :],
        wr=w_br_rnn.astype(BF16), wa=w_br_attn.astype(BF16), wo=w_out.astype(BF16),
        ln2=ln2[None, :], rw_hi=rw_hi, rw_lo=rw_lo, rb=rb,
        w1=w1, w3=w3, w2=w2,
        ln3=ln3[None, :], wg=w_ple_gate.astype(BF16), wp=w_ple_proj.astype(BF16),
    )


def _tail(x, rnn, attn, proj, p, w):
    t = x.shape[0]
    x1 = _merge(x, rnn, attn, proj, w["wr"], w["wa"], w["wo"], tm=min(t, 256))
    if 2 * t >= N_EXPERTS * MOE_ROW_TILE:
        return _sparse_moe_ple(x1, p, w)
    x2 = _moe(x1, w["ln2"], w["rw_hi"], w["rw_lo"], w["rb"], w["w1"], w["w3"], w["w2"], tm=min(t, 512))
    return _ple(x2, p, w["ln3"], w["wg"], w["wp"], tm=min(t, 512))


def _prompt_layer(x, p, w):
    bp, tp, _ = x.shape
    xt = x.reshape(bp * tp, D_MODEL)
    cs, sn = _rope_tables(jnp.arange(tp, dtype=jnp.int32))
    proj, small, kvb, kib = _inproj(xt, w["ln1"], w["w_t"], w["w_small_t"], w["colctl"],
                                    cs, sn, w["bd"], tm=min(tp, 1024))
    conv0 = jnp.zeros((bp, 8, D_RNN), F32)
    h0 = jnp.zeros((bp, 1, D_RNN), F32)
    rnn, h_last = _rglru(proj, conv0, h0, w["cw"], w["cb"], w["wa_bd"], w["ba"], w["wx_bd"], w["bx"],
                         w["lam"], n_seq=bp, tt=min(tp, 256))
    attn = _prompt_attention(proj, small, kvb, kib, n_batch=bp, seq=tp)
    y = _tail(xt, rnn, attn, proj, p.reshape(bp * tp, PLE_DIM), w)
    o_k = 2 * D_RNN + ATTN_W
    k = proj[:, o_k:o_k + KV_W].reshape(bp, tp, N_KV_HEADS, HEAD_DIM)
    v = proj[:, o_k + KV_W:o_k + 2 * KV_W].reshape(bp, tp, N_KV_HEADS, HEAD_DIM)
    ki = small[:, :IDX_DIM].reshape(bp, tp, IDX_DIM)
    conv_new = proj.reshape(bp, tp, N_MAIN)[:, tp - (CONV_W - 1):, :D_RNN]
    return y.reshape(bp, tp, D_MODEL), (k, v, ki, conv_new, h_last.reshape(bp, D_RNN))


def _sample_layer(x, p, cache_k, cache_v, cache_kidx, state_conv, state_h, page_table, w):
    bs, ts, _ = x.shape
    n_pages = page_table.shape[1]
    past = n_pages * PAGE_SIZE
    xt = x.reshape(bs * ts, D_MODEL)
    cs, sn = _rope_tables(past + jnp.tile(jnp.arange(ts, dtype=jnp.int32), bs))
    proj, small, _, _ = _inproj(xt, w["ln1"], w["w_t"], w["w_small_t"], w["colctl"], cs, sn,
                                w["bd"], tm=bs * ts)
    conv0 = jnp.concatenate([jnp.zeros((bs, 8 - (CONV_W - 1), D_RNN), F32), state_conv], axis=1)
    rnn, h_last = _rglru(proj, conv0, state_h[:, None, :], w["cw"], w["cb"], w["wa_bd"], w["ba"],
                         w["wx_bd"], w["bx"], w["lam"], n_seq=bs, tt=ts)
    o_q = 2 * D_RNN
    o_qi = o_q + ATTN_W + 2 * KV_W
    qi = proj[:, o_qi:o_qi + N_IDX_HEADS * IDX_DIM].reshape(bs, ts, N_IDX_HEADS, IDX_DIM)
    qst = jnp.transpose(qi, (0, 2, 1, 3)).reshape(bs, N_IDX_HEADS * ts, IDX_DIM).astype(BF16)
    wi = small[:, IDX_DIM:IDX_DIM + N_IDX_HEADS].reshape(bs, ts, N_IDX_HEADS)
    wcol = jnp.transpose(wi, (0, 2, 1)).reshape(bs, N_IDX_HEADS * ts, 1)
    n_pool = cache_k.shape[0]
    kidx_t = jnp.transpose(cache_kidx, (0, 2, 1))
    k_t = jnp.transpose(cache_k, (0, 2, 3, 1)).reshape(n_pool, KV_W, PAGE_SIZE)
    v_t = jnp.transpose(cache_v, (0, 2, 3, 1)).reshape(n_pool, KV_W, PAGE_SIZE)
    scores = _sample_select(page_table, qst, wcol, small, kidx_t, n_new=ts)
    bias = _select_bias(scores.reshape(bs * ts, -1), n_new=ts, past=past).reshape(scores.shape)
    q = proj[:, o_q:o_q + ATTN_W].reshape(bs, ts, N_KV_HEADS, N_HEADS // N_KV_HEADS, HEAD_DIM)
    eye = jnp.eye(N_KV_HEADS, dtype=F32)
    qbd = jnp.einsum("btgjd,gk->bgjtkd", q, eye).reshape(bs, N_HEADS * ts, KV_W).astype(BF16)
    att = _sample_attend(page_table, qbd, bias, proj, k_t, v_t, n_new=ts)
    att = att.reshape(bs, N_KV_HEADS, N_HEADS // N_KV_HEADS, ts, N_KV_HEADS, HEAD_DIM)
    att = jnp.stack([att[:, g, :, :, g, :] for g in range(N_KV_HEADS)], axis=1)
    attn = jnp.transpose(att, (0, 3, 1, 2, 4)).reshape(bs * ts, ATTN_W)
    y = _tail(xt, rnn, attn, proj, p.reshape(bs * ts, PLE_DIM), w)
    o_k = o_q + ATTN_W
    k = proj[:, o_k:o_k + KV_W].reshape(bs, ts, N_KV_HEADS, HEAD_DIM)
    v = proj[:, o_k + KV_W:o_k + 2 * KV_W].reshape(bs, ts, N_KV_HEADS, HEAD_DIM)
    ki = small[:, :IDX_DIM].reshape(bs, ts, IDX_DIM)
    conv_new = proj.reshape(bs, ts, N_MAIN)[:, ts - (CONV_W - 1):, :D_RNN]
    return y.reshape(bs, ts, D_MODEL), (k, v, ki, conv_new, h_last.reshape(bs, D_RNN))


def kernel(x_prompt, x_sample, p_prompt, p_sample, cache_k, cache_v, cache_kidx, state_conv, state_h,
           page_table, ln1, w_in, q_norm, k_norm, conv_w, conv_b, w_a, b_a, w_x, b_x, lam, w_br_rnn,
           w_br_attn, w_out, ln2, w_rg, b_rg, w_re, b_re, w1, w3, w2, ln3, w_ple_gate, w_ple_proj):
    weights = (ln1, w_in, q_norm, k_norm, conv_w, conv_b, w_a, b_a, w_x, b_x, lam, w_br_rnn, w_br_attn,
               w_out, ln2, w_rg, b_rg, w_re, b_re, w1, w3, w2, ln3, w_ple_gate, w_ple_proj)
    depth = ln1.shape[0]
    yp, ys = x_prompt, x_sample
    st_p, st_s = [], []
    for i in range(depth):
        w = _layer_weights(*[wt[i] for wt in weights])
        yp, sp = _prompt_layer(yp, p_prompt[i], w)
        ys, ss = _sample_layer(ys, p_sample[i], cache_k[i], cache_v[i], cache_kidx[i], state_conv[i],
                               state_h[i], page_table, w)
        st_p.append(sp)
        st_s.append(ss)
    stack = lambda sts, j: jnp.stack([s[j] for s in sts])
    return (yp, ys, stack(st_p, 0), stack(st_p, 1), stack(st_p, 2), stack(st_p, 3), stack(st_p, 4),
            stack(st_s, 0), stack(st_s, 1), stack(st_s, 2), stack(st_s, 3), stack(st_s, 4))
```

```python
import functools
import math

import jax
import jax.numpy as jnp
import numpy as np
from jax import lax
from jax.experimental import pallas as pl
from jax.experimental.pallas import tpu as pltpu

F32 = jnp.float32
BF16 = jnp.bfloat16

D_MODEL = 2048
HEAD_DIM = 64
N_HEADS = 16
N_KV_HEADS = 4
ATTN_W = N_HEADS * HEAD_DIM
KV_W = N_KV_HEADS * HEAD_DIM
N_IDX_HEADS = 8
IDX_DIM = 64
TOPK_MAX = 256
ROPE_THETA = 10000.0
D_RNN = 1024
N_RNN_BLOCKS = 16
RNN_BLOCK = 64
CONV_W = 4
LRU_C = 8.0
N_GROUPS = 4
EXPERTS_PER_GROUP = 8
N_EXPERTS = 32
D_EXPERT = 256
PLE_DIM = 256
PAGE_SIZE = 128
EPS = 1e-6

LANES = 128
N_MAIN = 8192
COL_RAW_END = 2 * D_RNN
COL_NR_END = 4096
IDX_SCALE = (IDX_DIM ** -0.5) * (N_IDX_HEADS ** -0.5)
QK_SCALE = HEAD_DIM ** -0.5
F32_MIN = float(np.finfo(np.float32).min)
INT_MIN = -2147483648
KEY_NEG_INF = INT_MIN + 0x7FFFFF
VMEM_LIMIT = 56 * 1024 * 1024


def _dot(a, b):
    return jnp.dot(a, b, preferred_element_type=F32)


def _dot_nt(a, b):
    return lax.dot_general(a, b, (((1,), (1,)), ((), ())), preferred_element_type=F32)


def _sigmoid(x):
    return 1.0 / (1.0 + jnp.exp(-x))


def _rms(x, g):
    return x * lax.rsqrt(jnp.mean(x * x, axis=-1, keepdims=True) + EPS) * g


def _split_bf16(x):
    hi = x.astype(BF16)
    lo = (x - hi.astype(F32)).astype(BF16)
    return hi, lo


def _rope_chunks(y, c, s):
    lane = lax.broadcasted_iota(jnp.int32, (1, LANES), 1)
    first_half = (lane % HEAD_DIM) < (HEAD_DIM // 2)
    outs = []
    for k in range(y.shape[1] // LANES):
        yc = y[:, k * LANES:(k + 1) * LANES]
        partner = jnp.where(first_half, pltpu.roll(yc, LANES - HEAD_DIM // 2, 1),
                            pltpu.roll(yc, HEAD_DIM // 2, 1))
        outs.append(yc * c + partner * s)
    return outs[0] if len(outs) == 1 else jnp.concatenate(outs, axis=1)


def _inproj_kernel(x_ref, ln_ref, wlo_ref, whi_ref, ws_ref, ctl_ref, cs_ref, sn_ref, bd_ref,
                   o_ref, os_ref, kvb_ref, kib_ref, h_ref, *, tn):
    j = pl.program_id(1)

    @pl.when(j == 0)
    def _():
        hb = _rms(x_ref[...], ln_ref[...]).astype(BF16)
        h_ref[...] = hb
        ys = _dot_nt(hb, ws_ref[...].astype(BF16))
        lane = lax.broadcasted_iota(jnp.int32, (1, LANES), 1)
        roped = _rope_chunks(ys, cs_ref[...], sn_ref[...])
        os_ref[...] = jnp.where(lane < IDX_DIM, roped, ys)
        kib_ref[...] = jnp.where(lane < IDX_DIM, roped, 0.0).astype(BF16)

    @pl.when(j < COL_NR_END // tn)
    def _():
        o_ref[...] = _dot_nt(h_ref[...], wlo_ref[...].astype(BF16))

    @pl.when(j >= COL_NR_END // tn)
    def _():
        o_ref[...] = _dot_nt(h_ref[...], whi_ref[...].astype(BF16))

    @pl.when((j >= COL_RAW_END // tn) & (j < COL_NR_END // tn))
    def _():
        y = o_ref[...]
        ctl = ctl_ref[...]
        gain, norm_on, rope_on, post = ctl[0:1], ctl[1:2], ctl[2:3], ctl[3:4]
        hi, lo = _split_bf16(y * y)
        ss = _dot(hi, bd_ref[...]) + _dot(lo, bd_ref[...])
        yn = jnp.where(norm_on > 0.0, y * lax.rsqrt(ss * (1.0 / HEAD_DIM) + EPS) * gain, y)
        yr = jnp.where(rope_on > 0.0, _rope_chunks(yn, cs_ref[...], sn_ref[...]), yn)
        o_ref[...] = yr * post

    @pl.when(j == (COL_RAW_END + ATTN_W) // tn)
    def _():
        kvb_ref[...] = o_ref[...].astype(BF16)


def _inproj(x, ln1, w_t, w_small_t, colctl, cs, sn, bd, *, tm, tn=512):
    t = x.shape[0]
    assert tn == 2 * KV_W
    grid = (t // tm, N_MAIN // tn)
    n_lo = COL_NR_END // tn
    n_rope = cs.shape[0] // tm
    gate_row0 = COL_NR_END + IDX_DIM + N_IDX_HEADS
    return pl.pallas_call(
        functools.partial(_inproj_kernel, tn=tn),
        grid=grid,
        in_specs=[
            pl.BlockSpec((tm, D_MODEL), lambda i, j: (i, 0)),
            pl.BlockSpec((1, D_MODEL), lambda i, j: (0, 0)),
            pl.BlockSpec((tn, D_MODEL), lambda i, j: (jnp.minimum(j, n_lo - 1), 0)),
            pl.BlockSpec((pl.Element(tn), pl.Element(D_MODEL)),
                         lambda i, j: (pl.multiple_of(gate_row0 + jnp.maximum(j - n_lo, 0) * tn, 8), 0)),
            pl.BlockSpec((LANES, D_MODEL), lambda i, j: (0, 0)),
            pl.BlockSpec((8, tn), lambda i, j: (0, j)),
            pl.BlockSpec((tm, LANES), lambda i, j: (i % n_rope, 0)),
            pl.BlockSpec((tm, LANES), lambda i, j: (i % n_rope, 0)),
            pl.BlockSpec((tn, tn), lambda i, j: (0, 0)),
        ],
        out_specs=[
            pl.BlockSpec((tm, tn), lambda i, j: (i, j)),
            pl.BlockSpec((tm, LANES), lambda i, j: (i, 0)),
            pl.BlockSpec((tm, 2 * KV_W), lambda i, j: (i, 0)),
            pl.BlockSpec((tm, LANES), lambda i, j: (i, 0)),
        ],
        out_shape=[
            jax.ShapeDtypeStruct((t, N_MAIN), F32),
            jax.ShapeDtypeStruct((t, LANES), F32),
            jax.ShapeDtypeStruct((t, 2 * KV_W), BF16),
            jax.ShapeDtypeStruct((t, LANES), BF16),
        ],
        scratch_shapes=[pltpu.VMEM((tm, D_MODEL), BF16)],
        compiler_params=pltpu.CompilerParams(
            dimension_semantics=("arbitrary", "arbitrary"), vmem_limit_bytes=VMEM_LIMIT),
        name="inproj",
    )(x, ln1, w_t, w_t, w_small_t, colctl, cs, sn, bd)


def _rglru_kernel(x_ref, g_ref, c0_ref, h0_ref, cw_ref, cb_ref, wa_ref, ba_ref, wx_ref, bx_ref,
                  lam_ref, o_ref, hl_ref, xs_ref, a_ref, b_ref, hc_ref):
    t = pl.program_id(1)
    tt = x_ref.shape[0]

    @pl.when(t == 0)
    def _():
        xs_ref[0:8, :] = c0_ref[...]
        hc_ref[...] = h0_ref[...]

    xs_ref[8:8 + tt, :] = x_ref[...]
    cw = cw_ref[...]
    taps = (xs_ref[5:5 + tt, :] * cw[0:1] + xs_ref[6:6 + tt, :] * cw[1:2]
            + xs_ref[7:7 + tt, :] * cw[2:3] + xs_ref[8:8 + tt, :] * cw[3:4])
    xc = cb_ref[...] + taps
    xs_ref[0:8, :] = xs_ref[tt:tt + 8, :]

    xcb = xc.astype(BF16)
    ra, ri = [], []
    for c in range(wa_ref.shape[0]):
        blk = xcb[:, c * 256:(c + 1) * 256]
        ra.append(_dot(blk, wa_ref[c]))
        ri.append(_dot(blk, wx_ref[c]))
    r = _sigmoid(jnp.concatenate(ra, axis=1) + ba_ref[...])
    ig = _sigmoid(jnp.concatenate(ri, axis=1) + bx_ref[...])
    nlam = -lam_ref[...]
    softplus = jnp.maximum(nlam, 0.0) + jnp.log1p(jnp.exp(-jnp.abs(nlam)))
    log_a = (-LRU_C) * r * softplus
    a = jnp.exp(log_a)
    u = jnp.sqrt(jnp.tanh(-log_a) * (a * a + 1.0)) * (ig * xc)

    n8 = tt // 8
    a3 = a.reshape(n8, 8, D_RNN)
    b3 = u.reshape(n8, 8, D_RNN)
    sub = lax.broadcasted_iota(jnp.int32, (1, 8, 1), 1)
    for s in (1, 2, 4):
        a_prev = pltpu.roll(a3, s, 1)
        b_prev = pltpu.roll(b3, s, 1)
        m = sub >= s
        b3 = jnp.where(m, a3 * b_prev + b3, b3)
        a3 = jnp.where(m, a3 * a_prev, a3)
    a_ref[...] = a3.reshape(tt, D_RNN)
    b_ref[...] = b3.reshape(tt, D_RNN)

    def chain(k, carry):
        i0 = pl.multiple_of(k * 8, 8)
        h8 = a_ref[pl.ds(i0, 8), :] * carry + b_ref[pl.ds(i0, 8), :]
        b_ref[pl.ds(i0, 8), :] = h8
        return h8[7:8, :]

    carry = lax.fori_loop(0, n8, chain, hc_ref[...])
    hc_ref[...] = carry
    g = g_ref[...]
    gelu = 0.5 * g * (1.0 + jnp.tanh(0.7978845608028654 * (g + 0.044715 * (g * g * g))))
    o_ref[...] = b_ref[...] * gelu

    @pl.when(t == pl.num_programs(1) - 1)
    def _():
        hl_ref[...] = carry


def _rglru(proj, conv0, h0, cw, cb, wa_bd, ba, wx_bd, bx, lam, *, n_seq, tt):
    t_total = proj.shape[0]
    nt = t_total // (n_seq * tt)
    full = lambda shape: pl.BlockSpec(shape, lambda b, t: (0,) * len(shape))
    return pl.pallas_call(
        _rglru_kernel,
        grid=(n_seq, nt),
        in_specs=[
            pl.BlockSpec((tt, D_RNN), lambda b, t: (b * nt + t, 0)),
            pl.BlockSpec((tt, D_RNN), lambda b, t: (b * nt + t, 1)),
            pl.BlockSpec((None, 8, D_RNN), lambda b, t: (b, 0, 0)),
            pl.BlockSpec((None, 1, D_RNN), lambda b, t: (b, 0, 0)),
            full((CONV_W, D_RNN)), full((1, D_RNN)),
            full(wa_bd.shape), full((1, D_RNN)),
            full(wx_bd.shape), full((1, D_RNN)),
            full((1, D_RNN)),
        ],
        out_specs=[
            pl.BlockSpec((tt, D_RNN), lambda b, t: (b * nt + t, 0)),
            pl.BlockSpec((None, 1, D_RNN), lambda b, t: (b, 0, 0)),
        ],
        out_shape=[
            jax.ShapeDtypeStruct((t_total, D_RNN), F32),
            jax.ShapeDtypeStruct((n_seq, 1, D_RNN), F32),
        ],
        scratch_shapes=[
            pltpu.VMEM((tt + 8, D_RNN), F32),
            pltpu.VMEM((tt, D_RNN), F32),
            pltpu.VMEM((tt, D_RNN), F32),
            pltpu.VMEM((1, D_RNN), F32),
        ],
        compiler_params=pltpu.CompilerParams(
            dimension_semantics=("arbitrary", "arbitrary"), vmem_limit_bytes=VMEM_LIMIT),
        name="rglru",
    )(proj, proj, conv0, h0, cw, cb, wa_bd, ba, wx_bd, bx, lam)


def _select_topk(s, kk):
    rows, n = s.shape
    kkf = float(kk)

    def key_to_f32(w):
        k = w ^ INT_MIN
        bits = jnp.where(k >= 0, k, k ^ 0x7FFFFFFF)
        return k, lax.bitcast_convert_type(bits, F32)

    def vbody(it, w):
        cand_w = w | jnp.left_shift(jnp.int32(1), 31 - it)
        cand_k, cand_f = key_to_f32(cand_w)
        cnt = jnp.sum(jnp.where(s >= cand_f, 1.0, 0.0), axis=1, keepdims=True)
        ok = (cnt >= kkf) | (cand_k < KEY_NEG_INF)
        return jnp.where(ok, cand_w, w)

    w = lax.fori_loop(0, 32, vbody, jnp.zeros((rows, 1), jnp.int32))
    _, thr = key_to_f32(w)
    gt = s > thr
    eq = s == thr
    need = kkf - jnp.sum(jnp.where(gt, 1.0, 0.0), axis=1, keepdims=True)
    col = lax.broadcasted_iota(jnp.int32, (1, n), 1)
    nbits = int(n).bit_length()

    def jbody(it, jmax):
        cand = jmax | jnp.left_shift(jnp.int32(1), nbits - 1 - it)
        cnt = jnp.sum(jnp.where(eq & (col < cand), 1.0, 0.0), axis=1, keepdims=True)
        return jnp.where(cnt <= need, cand, jmax)

    n_ge = jnp.sum(jnp.where(s >= thr, 1.0, 0.0), axis=1, keepdims=True)
    jmax = lax.cond(
        jnp.max(n_ge) > kkf,
        lambda: lax.fori_loop(0, nbits, jbody, jnp.zeros((rows, 1), jnp.int32)),
        lambda: jnp.full((rows, 1), (1 << nbits) - 1, jnp.int32))
    return gt | (eq & (col < jmax))


def _pattn_kernel(q_ref, qi_ref, sm_ref, k_ref, v_ref, ki_ref, o_ref, s_ref, *, i0, n_keys, kc, topk):
    i = pl.program_id(0)
    n_batch, tq = q_ref.shape[0], q_ref.shape[1]
    lane = lax.broadcasted_iota(jnp.int32, (1, LANES), 1)
    qpos = (i0 + i) * tq + lax.broadcasted_iota(jnp.int32, (tq, 1), 0)

    def score(b, carry):
        sm = sm_ref[b]
        qi = qi_ref[b]
        qrows, wrows = [], []
        for h in range(N_IDX_HEADS):
            blk = qi[:, (h // 2) * LANES:(h // 2 + 1) * LANES]
            if h % 2 == 1:
                blk = pltpu.roll(blk, IDX_DIM, 1)
            qrows.append(jnp.where(lane < IDX_DIM, blk, 0.0))
            wrows.append(sm[:, IDX_DIM + h:IDX_DIM + h + 1])
        qst = jnp.concatenate(qrows, axis=0).astype(BF16)
        wst = jnp.concatenate(wrows, axis=0) * IDX_SCALE
        r0 = pl.multiple_of(b * tq, tq)
        for c in range(n_keys // kc):
            s = jnp.maximum(_dot_nt(qst, ki_ref[b, c * kc:(c + 1) * kc, :]), 0.0) * wst
            sc = s[0:tq]
            for h in range(1, N_IDX_HEADS):
                sc = sc + s[h * tq:(h + 1) * tq]
            col = c * kc + lax.broadcasted_iota(jnp.int32, (1, kc), 1)
            s_ref[pl.ds(r0, tq), c * kc:(c + 1) * kc] = jnp.where(col <= qpos, sc, F32_MIN)
        return carry

    lax.fori_loop(0, n_batch, score, 0)

    sel = _select_topk(s_ref[...], topk)
    colf = lax.broadcasted_iota(jnp.int32, (1, n_keys), 1)
    qpos_all = jnp.concatenate([qpos] * n_batch, axis=0)
    s_ref[...] = jnp.where(sel & (colf <= qpos_all), 0.0, -jnp.inf)

    def attend(b, carry):
        _attend_tile(q_ref[b], s_ref[pl.ds(pl.multiple_of(b * tq, tq), tq), :],
                     k_ref[b, 0:n_keys, :], v_ref[b, 0:n_keys, :], o_ref.at[b])
        return carry

    lax.fori_loop(0, n_batch, attend, 0)


def _attend_tile(q, bias, kb, vb, o_ref):
    tq = q.shape[0]
    lane = lax.broadcasted_iota(jnp.int32, (1, LANES), 1)
    bias4 = jnp.concatenate([bias] * 4, axis=0)
    outs = [None] * N_HEADS
    for g in range(N_KV_HEADS):
        lo = (g % 2) * HEAD_DIM
        keep = (lane >= lo) & (lane < lo + HEAD_DIM)
        rows = []
        for j in range(4):
            h = 4 * g + j
            blk = q[:, (h // 2) * LANES:(h // 2 + 1) * LANES]
            if h % 2 != g % 2:
                blk = pltpu.roll(blk, HEAD_DIM, 1)
            piece = jnp.where(keep, blk, 0.0)
            zero = jnp.zeros_like(piece)
            rows.append(jnp.concatenate([piece, zero] if g < 2 else [zero, piece], axis=1))
        qbd = jnp.concatenate(rows, axis=0).astype(BF16)
        logits = _dot_nt(qbd, kb) + bias4
        m = jnp.max(logits, axis=1, keepdims=True)
        p = jnp.exp(logits - m)
        denom = jnp.sum(p, axis=1, keepdims=True)
        acc = _dot(p.astype(BF16), vb) / denom
        for j in range(4):
            outs[4 * g + j] = acc[j * tq:(j + 1) * tq, (g // 2) * LANES:(g // 2 + 1) * LANES]
    for c in range(N_HEADS // 2):
        g = (2 * c) // 4
        even, odd = outs[2 * c], outs[2 * c + 1]
        if g % 2 == 1:
            even = pltpu.roll(even, HEAD_DIM, 1)
        else:
            odd = pltpu.roll(odd, HEAD_DIM, 1)
        o_ref[:, c * LANES:(c + 1) * LANES] = jnp.where(lane < HEAD_DIM, even, odd).astype(BF16)


def _prompt_attention_part(proj3, small3, kvb, kib, *, i0, n_tiles, tq):
    n_batch, seq, _ = proj3.shape
    n_keys = (i0 + n_tiles) * tq
    kc = next(c for c in (512, 256, 128) if n_keys % c == 0)
    return pl.pallas_call(
        functools.partial(_pattn_kernel, i0=i0, n_keys=n_keys, kc=kc, topk=min(TOPK_MAX, seq // 4)),
        grid=(n_tiles,),
        in_specs=[
            pl.BlockSpec((n_batch, tq, ATTN_W), lambda i: (0, i0 + i, 2)),
            pl.BlockSpec((n_batch, tq, N_IDX_HEADS * IDX_DIM), lambda i: (0, i0 + i, 7)),
            pl.BlockSpec((n_batch, tq, LANES), lambda i: (0, i0 + i, 0)),
            pl.BlockSpec((n_batch, seq, KV_W), lambda i: (0, 0, 0)),
            pl.BlockSpec((n_batch, seq, KV_W), lambda i: (0, 0, 1)),
            pl.BlockSpec((n_batch, seq, LANES), lambda i: (0, 0, 0)),
        ],
        out_specs=pl.BlockSpec((n_batch, tq, ATTN_W), lambda i: (0, i, 0)),
        out_shape=jax.ShapeDtypeStruct((n_batch, n_tiles * tq, ATTN_W), BF16),
        scratch_shapes=[pltpu.VMEM((n_batch * tq, n_keys), F32)],
        compiler_params=pltpu.CompilerParams(
            dimension_semantics=("arbitrary",), vmem_limit_bytes=VMEM_LIMIT),
        name=f"prompt_attention_{i0}",
    )(proj3, proj3, small3, kvb, kvb, kib)


def _prompt_attention(proj, small, kvb, kib, *, n_batch, seq, tq=128, tiles_per_part=2):
    nq = seq // tq
    proj3 = proj.reshape(n_batch, seq, N_MAIN)
    small3 = small.reshape(n_batch, seq, LANES)
    kvb3 = kvb.reshape(n_batch, seq, 2 * KV_W)
    kib3 = kib.reshape(n_batch, seq, LANES)
    parts = [
        _prompt_attention_part(proj3, small3, kvb3, kib3, i0=i0,
                               n_tiles=min(tiles_per_part, nq - i0), tq=tq)
        for i0 in range(0, nq, tiles_per_part)
    ]
    return jnp.concatenate(parts, axis=1).reshape(n_batch * seq, ATTN_W)


SELECT_PAGES_PER_STEP = 32
ATTEND_PAGES_PER_STEP = 32
SUB_PAGES = 8


def _sidx_kernel(pt_ref, qst_ref, w_ref, sm_ref, *refs, n_chunks, n_new, ps):
    pages = refs[:ps]
    o_ref, s_ref = refs[ps], refs[ps + 1]
    c = pl.program_id(1)
    qst = qst_ref[...]
    w = w_ref[...] * IDX_SCALE

    def head_sum(s):
        s = jnp.maximum(s, 0.0) * w
        out = s[0:n_new]
        for h in range(1, N_IDX_HEADS):
            out = out + s[h * n_new:(h + 1) * n_new]
        return out

    for r0 in range(0, ps, SUB_PAGES):
        kt = jnp.concatenate([pages[r0 + r][...] for r in range(SUB_PAGES)], axis=1).astype(BF16)
        part = head_sum(_dot(qst, kt))
        for r in range(SUB_PAGES):
            s_ref[ps * c + r0 + r] = part[:, r * PAGE_SIZE:(r + 1) * PAGE_SIZE]

    @pl.when(c == n_chunks - 1)
    def _():
        n_past_blocks = n_chunks * ps
        past = n_past_blocks * PAGE_SIZE
        k_new = sm_ref[...][:, 0:IDX_DIM]
        kp = jnp.concatenate([k_new, jnp.zeros((PAGE_SIZE - n_new, IDX_DIM), F32)], axis=0)
        lane = lax.broadcasted_iota(jnp.int32, (n_new, LANES), 1)
        trow = lax.broadcasted_iota(jnp.int32, (n_new, LANES), 0)
        s_new = head_sum(_dot_nt(qst, kp.astype(BF16)))
        s_ref[n_past_blocks] = jnp.where(lane < n_new, jnp.where(lane <= trow, s_new, F32_MIN), -jnp.inf)
        o_ref[...] = jnp.concatenate([s_ref[k] for k in range(n_past_blocks + 1)], axis=1)


def _select_bias_kernel(s_ref, o_ref, *, n_new, past, topk):
    s = s_ref[...]
    rows, n_all = s.shape
    sel = _select_topk(s, topk)
    col = lax.broadcasted_iota(jnp.int32, (1, n_all), 1)
    tq = lax.broadcasted_iota(jnp.int32, (rows, 1), 0) % n_new
    o_ref[...] = jnp.where(sel & ((col - past) <= tq), 0.0, -jnp.inf)


def _select_bias(scores, *, n_new, past, rows_per_step=128):
    rows, n_all = scores.shape
    rows_per_step = min(rows_per_step, rows)
    return pl.pallas_call(
        functools.partial(_select_bias_kernel, n_new=n_new, past=past,
                          topk=min(TOPK_MAX, (past + n_new) // 4)),
        grid=(rows // rows_per_step,),
        in_specs=[pl.BlockSpec((rows_per_step, n_all), lambda i: (i, 0))],
        out_specs=pl.BlockSpec((rows_per_step, n_all), lambda i: (i, 0)),
        out_shape=jax.ShapeDtypeStruct((rows, n_all), F32),
        compiler_params=pltpu.CompilerParams(
            dimension_semantics=("arbitrary",), vmem_limit_bytes=VMEM_LIMIT),
        name="sample_select_bias",
    )(scores)


def _sample_select(page_table, qst, wcol, small, cache_kidx_t, *, n_new):
    n_seq, n_pages = page_table.shape
    ps = min(SELECT_PAGES_PER_STEP, n_pages)
    n_chunks = n_pages // ps
    n_all = n_pages * PAGE_SIZE + LANES
    page_specs = [
        pl.BlockSpec((None, IDX_DIM, PAGE_SIZE), lambda b, c, pt, r=r: (pt[b, ps * c + r], 0, 0))
        for r in range(ps)
    ]
    rows = N_IDX_HEADS * n_new
    grid_spec = pltpu.PrefetchScalarGridSpec(
        num_scalar_prefetch=1,
        grid=(n_seq, n_chunks),
        in_specs=[
            pl.BlockSpec((None, rows, IDX_DIM), lambda b, c, pt: (b, 0, 0)),
            pl.BlockSpec((None, rows, 1), lambda b, c, pt: (b, 0, 0)),
            pl.BlockSpec((n_new, LANES), lambda b, c, pt: (b, 0)),
        ] + page_specs,
        out_specs=pl.BlockSpec((None, n_new, n_all), lambda b, c, pt: (b, 0, 0)),
        scratch_shapes=[pltpu.VMEM((n_pages + 1, n_new, LANES), F32)],
    )
    return pl.pallas_call(
        functools.partial(_sidx_kernel, n_chunks=n_chunks, n_new=n_new, ps=ps),
        grid_spec=grid_spec,
        out_shape=jax.ShapeDtypeStruct((n_seq, n_new, n_all), F32),
        compiler_params=pltpu.CompilerParams(
            dimension_semantics=("arbitrary", "arbitrary"), vmem_limit_bytes=VMEM_LIMIT),
        name="sample_select",
    )(page_table, qst, wcol, small, *([cache_kidx_t] * ps))


def _sattn_kernel(pt_ref, q_ref, bias_ref, biasn_ref, kn_ref, vn_ref, k_hbm, v_hbm, o_ref,
                  m_ref, l_ref, acc_ref, kbuf_ref, vbuf_ref, ksem, vsem, *, n_chunks, n_new, ps):
    b = pl.program_id(0)
    c = pl.program_id(1)
    rows = q_ref.shape[0]
    reps = rows // n_new
    step = b * n_chunks + c
    n_steps = pl.num_programs(0) * n_chunks

    def fetch(s, buf):
        sb = s // n_chunks
        sc = s % n_chunks

        def start(r, carry):
            page = pt_ref[sb, sc * ps + r]
            pltpu.make_async_copy(k_hbm.at[page], kbuf_ref.at[buf, r], ksem.at[buf]).start()
            pltpu.make_async_copy(v_hbm.at[page], vbuf_ref.at[buf, r], vsem.at[buf]).start()
            return carry

        lax.fori_loop(0, ps, start, 0)

    @pl.when(step == 0)
    def _():
        fetch(0, 0)

    @pl.when(step + 1 < n_steps)
    def _():
        fetch(step + 1, (step + 1) % 2)

    cur = step % 2
    pltpu.make_async_copy(k_hbm.at[pl.ds(0, ps)], kbuf_ref.at[cur], ksem.at[cur]).wait()
    pltpu.make_async_copy(v_hbm.at[pl.ds(0, ps)], vbuf_ref.at[cur], vsem.at[cur]).wait()
    kpages = [kbuf_ref.at[cur, r] for r in range(ps)]
    vpages = [vbuf_ref.at[cur, r] for r in range(ps)]

    @pl.when(c == 0)
    def _():
        m_ref[...] = jnp.full(m_ref.shape, -1e30, F32)
        l_ref[...] = jnp.zeros(l_ref.shape, F32)
        acc_ref[...] = jnp.zeros(acc_ref.shape, F32)

    def update(logits, bias, pv):
        logits = logits + jnp.concatenate([bias] * reps, axis=0)
        m_old = m_ref[...]
        m_new = jnp.maximum(m_old, jnp.max(logits, axis=1, keepdims=True))
        alpha = jnp.exp(m_old - m_new)
        p = jnp.exp(logits - m_new)
        l_ref[...] = alpha * l_ref[...] + jnp.sum(p, axis=1, keepdims=True)
        acc_ref[...] = alpha * acc_ref[...] + pv(p.astype(BF16))
        m_ref[...] = m_new

    sub_keys = SUB_PAGES * PAGE_SIZE
    logits, vts = [], []
    for r0 in range(0, ps, SUB_PAGES):
        kt = jnp.concatenate([kpages[r0 + r][...] for r in range(SUB_PAGES)], axis=1).astype(BF16)
        vts.append(jnp.concatenate([vpages[r0 + r][...] for r in range(SUB_PAGES)], axis=1).astype(BF16))
        logits.append(_dot(q_ref[...], kt))

    def pv(p):
        acc = _dot_nt(p[:, 0:sub_keys], vts[0])
        for n in range(1, len(vts)):
            acc = acc + _dot_nt(p[:, n * sub_keys:(n + 1) * sub_keys], vts[n])
        return acc

    update(jnp.concatenate(logits, axis=1), bias_ref[...], pv)

    @pl.when(c == n_chunks - 1)
    def _():
        pad = jnp.zeros((PAGE_SIZE - n_new, KV_W), F32)
        kn = jnp.concatenate([kn_ref[...], pad], axis=0).astype(BF16)
        vn = jnp.concatenate([vn_ref[...], pad], axis=0).astype(BF16)
        update(_dot_nt(q_ref[...], kn), biasn_ref[...], lambda p: _dot(p, vn))
        o_ref[...] = acc_ref[...] / l_ref[...]


def _sample_attend(page_table, qbd, bias, proj, cache_k_t, cache_v_t, *, n_new):
    n_seq, n_pages = page_table.shape
    ps = min(ATTEND_PAGES_PER_STEP, n_pages)
    n_chunks = n_pages // ps
    rows = qbd.shape[1]
    chunk_keys = ps * PAGE_SIZE
    grid_spec = pltpu.PrefetchScalarGridSpec(
        num_scalar_prefetch=1,
        grid=(n_seq, n_chunks),
        in_specs=[
            pl.BlockSpec((None, rows, KV_W), lambda b, c, pt: (b, 0, 0)),
            pl.BlockSpec((None, n_new, chunk_keys), lambda b, c, pt: (b, 0, c)),
            pl.BlockSpec((None, n_new, LANES), lambda b, c, pt: (b, 0, n_pages)),
            pl.BlockSpec((n_new, KV_W), lambda b, c, pt: (b, 12)),
            pl.BlockSpec((n_new, KV_W), lambda b, c, pt: (b, 13)),
            pl.BlockSpec(memory_space=pl.ANY),
            pl.BlockSpec(memory_space=pl.ANY),
        ],
        out_specs=pl.BlockSpec((None, rows, KV_W), lambda b, c, pt: (b, 0, 0)),
        scratch_shapes=[
            pltpu.VMEM((rows, 1), F32),
            pltpu.VMEM((rows, 1), F32),
            pltpu.VMEM((rows, KV_W), F32),
            pltpu.VMEM((2, ps, KV_W, PAGE_SIZE), F32),
            pltpu.VMEM((2, ps, KV_W, PAGE_SIZE), F32),
            pltpu.SemaphoreType.DMA((2,)),
            pltpu.SemaphoreType.DMA((2,)),
        ],
    )
    return pl.pallas_call(
        functools.partial(_sattn_kernel, n_chunks=n_chunks, n_new=n_new, ps=ps),
        grid_spec=grid_spec,
        out_shape=jax.ShapeDtypeStruct((n_seq, rows, KV_W), F32),
        compiler_params=pltpu.CompilerParams(
            dimension_semantics=("arbitrary", "arbitrary"), vmem_limit_bytes=VMEM_LIMIT),
        name="sample_attend",
    )(page_table, qbd, bias, bias, proj, proj, cache_k_t, cache_v_t)


def _merge_kernel(x_ref, rnn_ref, att_ref, gr_ref, ga_ref, wr_ref, wa_ref, wo_ref, o_ref, *, n_valid):
    i = pl.program_id(0)

    @pl.when(i < n_valid)
    def _():
        mixed = (_sigmoid(gr_ref[...]) * _dot(rnn_ref[...].astype(BF16), wr_ref[...])
                 + _sigmoid(ga_ref[...]) * _dot(att_ref[...].astype(BF16), wa_ref[...]))
        o_ref[...] = x_ref[...] + _dot(mixed.astype(BF16), wo_ref[...])

    @pl.when(i >= n_valid)
    def _():
        o_ref[...] = jnp.zeros(o_ref.shape, F32)


def _merge_into_kernel(x_ref, rnn_ref, att_ref, gr_ref, ga_ref, wr_ref, wa_ref, wo_ref, dst_ref, o_ref,
                       *, n_valid):
    del dst_ref
    _merge_kernel(x_ref, rnn_ref, att_ref, gr_ref, ga_ref, wr_ref, wa_ref, wo_ref, o_ref, n_valid=n_valid)


def _merge(x, rnn, attn, proj, wr, wa, wo, *, tm, out_rows=None, row0=0, into=None):
    t = x.shape[0]
    out_rows = t if out_rows is None else out_rows
    blk0 = row0 // tm
    n_valid = t // tm
    n_steps = n_valid if into is not None else out_rows // tm
    row = lambda i: jnp.minimum(i, n_valid - 1)
    in_specs = [
        pl.BlockSpec((tm, D_MODEL), lambda i: (row(i), 0)),
        pl.BlockSpec((tm, D_RNN), lambda i: (row(i), 0)),
        pl.BlockSpec((tm, ATTN_W), lambda i: (row(i), 0)),
        pl.BlockSpec((tm, D_MODEL), lambda i: (row(i), 2)),
        pl.BlockSpec((tm, D_MODEL), lambda i: (row(i), 3)),
        pl.BlockSpec((D_RNN, D_MODEL), lambda i: (0, 0)),
        pl.BlockSpec((ATTN_W, D_MODEL), lambda i: (0, 0)),
        pl.BlockSpec((D_MODEL, D_MODEL), lambda i: (0, 0)),
    ]
    args = (x, rnn, attn, proj, proj, wr, wa, wo)
    if into is not None:
        in_specs.append(pl.BlockSpec(memory_space=pl.ANY))
        args = args + (into,)
    return pl.pallas_call(
        functools.partial(_merge_kernel if into is None else _merge_into_kernel, n_valid=n_valid),
        grid=(n_steps,),
        in_specs=in_specs,
        out_specs=pl.BlockSpec((tm, D_MODEL), lambda i: (blk0 + i, 0)),
        out_shape=jax.ShapeDtypeStruct((out_rows, D_MODEL), F32),
        input_output_aliases={} if into is None else {len(args) - 1: 0},
        compiler_params=pltpu.CompilerParams(
            dimension_semantics=("arbitrary",), vmem_limit_bytes=VMEM_LIMIT),
        name="merge",
    )(*args)


def _route(h, rwh, rwl, rb):
    lane = lax.broadcasted_iota(jnp.int32, (1, LANES), 1)
    lanef = lane.astype(F32)
    hh, hl = _split_bf16(h)
    lg = (_dot(hh, rwh) + _dot(hl, rwh) + _dot(hh, rwl)) + rb
    is_g = (lane >= N_EXPERTS) & (lane < N_EXPERTS + N_GROUPS)
    gl = jnp.where(is_g, lg, -jnp.inf)
    gmax = jnp.max(gl, axis=1, keepdims=True)
    gprob = 1.0 / jnp.sum(jnp.exp(gl - gmax), axis=1, keepdims=True)
    gsel = jnp.min(jnp.where(is_g & (lg == gmax), lanef - N_EXPERTS, 1e9), axis=1, keepdims=True)
    in_grp = (lane < N_EXPERTS) & (jnp.floor(lanef * (1.0 / EXPERTS_PER_GROUP)) == gsel)
    v1 = jnp.where(in_grp, lg, -jnp.inf)
    t1 = jnp.max(v1, axis=1, keepdims=True)
    i1 = jnp.min(jnp.where(in_grp & (lg == t1), lanef, 1e9), axis=1, keepdims=True)
    rest = in_grp & (lanef != i1)
    v2 = jnp.where(rest, lg, -jnp.inf)
    t2 = jnp.max(v2, axis=1, keepdims=True)
    i2 = jnp.min(jnp.where(rest & (lg == t2), lanef, 1e9), axis=1, keepdims=True)
    d = jnp.exp(t2 - t1)
    return i1, i2, gprob / (1.0 + d), gprob * d / (1.0 + d)


def _moe_kernel(x_ref, ln_ref, rwh_ref, rwl_ref, rb_ref, w1_ref, w3_ref, w2_ref, o_ref, h_ref, gate_ref):
    e = pl.program_id(1)
    lane = lax.broadcasted_iota(jnp.int32, (1, LANES), 1)

    @pl.when(e == 0)
    def _():
        h = _rms(x_ref[...], ln_ref[...])
        h_ref[...] = h.astype(BF16)
        i1, i2, g1, g2 = _route(h, rwh_ref[...], rwl_ref[...], rb_ref[...])
        lanef = lane.astype(F32)
        gate_ref[...] = jnp.where(lanef == i1, g1, 0.0) + jnp.where(lanef == i2, g2, 0.0)

    ge = jnp.sum(jnp.where(lane == e, gate_ref[...], 0.0), axis=1, keepdims=True)
    up = _dot(h_ref[...], w1_ref[...].astype(BF16))
    hid = (up * _sigmoid(up)) * _dot(h_ref[...], w3_ref[...].astype(BF16))
    contrib = _dot((hid * ge).astype(BF16), w2_ref[...].astype(BF16))

    @pl.when(e == 0)
    def _():
        o_ref[...] = x_ref[...] + contrib

    @pl.when(e > 0)
    def _():
        o_ref[...] += contrib


def _moe(x, ln2, rw_hi, rw_lo, rb, w1, w3, w2, *, tm):
    t = x.shape[0]
    return pl.pallas_call(
        _moe_kernel,
        grid=(t // tm, N_EXPERTS),
        in_specs=[
            pl.BlockSpec((tm, D_MODEL), lambda i, e: (i, 0)),
            pl.BlockSpec((1, D_MODEL), lambda i, e: (0, 0)),
            pl.BlockSpec((D_MODEL, LANES), lambda i, e: (0, 0)),
            pl.BlockSpec((D_MODEL, LANES), lambda i, e: (0, 0)),
            pl.BlockSpec((1, LANES), lambda i, e: (0, 0)),
            pl.BlockSpec((None, D_MODEL, D_EXPERT), lambda i, e: (e, 0, 0)),
            pl.BlockSpec((None, D_MODEL, D_EXPERT), lambda i, e: (e, 0, 0)),
            pl.BlockSpec((None, D_EXPERT, D_MODEL), lambda i, e: (e, 0, 0)),
        ],
        out_specs=pl.BlockSpec((tm, D_MODEL), lambda i, e: (i, 0)),
        out_shape=jax.ShapeDtypeStruct((t, D_MODEL), F32),
        scratch_shapes=[pltpu.VMEM((tm, D_MODEL), BF16), pltpu.VMEM((tm, LANES), F32)],
        compiler_params=pltpu.CompilerParams(
            dimension_semantics=("arbitrary", "arbitrary"), vmem_limit_bytes=VMEM_LIMIT),
        name="moe",
    )(x, ln2, rw_hi, rw_lo, rb, w1, w3, w2)


def _ple_update(x2, p, ln, wg, wp):
    gate = _sigmoid(_dot(_rms(x2, ln).astype(BF16), wg))
    return x2 + gate * _dot(p.astype(BF16), wp)


def _ple_kernel(x_ref, p_ref, ln_ref, wg_ref, wp_ref, o_ref):
    o_ref[...] = _ple_update(x_ref[...], p_ref[...], ln_ref[...], wg_ref[...], wp_ref[...])


def _ple(x, p, ln3, wg, wp, *, tm):
    t = x.shape[0]
    return pl.pallas_call(
        _ple_kernel,
        grid=(t // tm,),
        in_specs=[
            pl.BlockSpec((tm, D_MODEL), lambda i: (i, 0)),
            pl.BlockSpec((tm, PLE_DIM), lambda i: (i, 0)),
            pl.BlockSpec((1, D_MODEL), lambda i: (0, 0)),
            pl.BlockSpec((D_MODEL, D_MODEL), lambda i: (0, 0)),
            pl.BlockSpec((PLE_DIM, D_MODEL), lambda i: (0, 0)),
        ],
        out_specs=pl.BlockSpec((tm, D_MODEL), lambda i: (i, 0)),
        out_shape=jax.ShapeDtypeStruct((t, D_MODEL), F32),
        compiler_params=pltpu.CompilerParams(
            dimension_semantics=("arbitrary",), vmem_limit_bytes=VMEM_LIMIT),
        name="ple",
    )(x, p, ln3, wg, wp)


MOE_ROW_TILE = 256
META_E, META_G, META_RANK = 0, 2, 4


def _router_kernel(x_ref, ln_ref, rwh_ref, rwl_ref, rb_ref, tri_ref, meta_ref, cnt_ref, carry_ref):
    i = pl.program_id(0)
    lane = lax.broadcasted_iota(jnp.int32, (1, LANES), 1)
    lanef = lane.astype(F32)

    @pl.when(i == 0)
    def _():
        carry_ref[...] = jnp.zeros(carry_ref.shape, F32)

    i1, i2, g1, g2 = _route(_rms(x_ref[...], ln_ref[...]), rwh_ref[...], rwl_ref[...], rb_ref[...])
    onehot = jnp.where((lanef == i1) | (lanef == i2), 1.0, 0.0)
    before = _dot(tri_ref[...], onehot.astype(BF16)) + carry_ref[...]
    r1 = jnp.sum(jnp.where(lanef == i1, before, 0.0), axis=1, keepdims=True)
    r2 = jnp.sum(jnp.where(lanef == i2, before, 0.0), axis=1, keepdims=True)
    carry_ref[...] += jnp.sum(onehot, axis=0, keepdims=True)
    rec = jnp.zeros((x_ref.shape[0], LANES), F32)
    for k, val in ((META_E, i1), (META_E + 1, i2), (META_G, g1), (META_G + 1, g2),
                   (META_RANK, r1), (META_RANK + 1, r2)):
        rec = jnp.where(lane == k, val, rec)
    meta_ref[...] = rec
    cnt_ref[...] = jnp.broadcast_to(carry_ref[...], cnt_ref.shape)


def _router(x, ln2, rw_hi, rw_lo, rb, *, tm):
    t = x.shape[0]
    r = lax.broadcasted_iota(jnp.int32, (tm, tm), 0)
    c = lax.broadcasted_iota(jnp.int32, (tm, tm), 1)
    tri = (c < r).astype(BF16)
    return pl.pallas_call(
        _router_kernel,
        grid=(t // tm,),
        in_specs=[
            pl.BlockSpec((tm, D_MODEL), lambda i: (i, 0)),
            pl.BlockSpec((1, D_MODEL), lambda i: (0, 0)),
            pl.BlockSpec((D_MODEL, LANES), lambda i: (0, 0)),
            pl.BlockSpec((D_MODEL, LANES), lambda i: (0, 0)),
            pl.BlockSpec((1, LANES), lambda i: (0, 0)),
            pl.BlockSpec((tm, tm), lambda i: (0, 0)),
        ],
        out_specs=[
            pl.BlockSpec((tm, LANES), lambda i: (i, 0)),
            pl.BlockSpec((8, LANES), lambda i: (0, 0)),
        ],
        out_shape=[
            jax.ShapeDtypeStruct((t, LANES), F32),
            jax.ShapeDtypeStruct((8, LANES), F32),
        ],
        scratch_shapes=[pltpu.VMEM((1, LANES), F32)],
        compiler_params=pltpu.CompilerParams(
            dimension_semantics=("arbitrary",), vmem_limit_bytes=VMEM_LIMIT),
        name="moe_router",
    )(x, ln2, rw_hi, rw_lo, rb, tri)


def _row_copy(src_ref, src_row, dst_ref, dst_row, sem):
    return pltpu.make_async_copy(src_ref.at[pl.ds(src_row, 1), :], dst_ref.at[pl.ds(dst_row, 1), :], sem)


def _scatter_kernel(slot_ref, x_ref, hs_in_ref, hs_ref, sem, *, n_tok):
    del hs_in_ref
    tm = x_ref.shape[0]
    base = pl.program_id(0) * tm

    def start(r, carry):
        for k in range(2):
            _row_copy(x_ref, r, hs_ref, slot_ref[k * n_tok + base + r], sem).start()
        return carry

    lax.fori_loop(0, tm, start, 0)
    for k in range(2):
        pltpu.make_async_copy(x_ref, hs_ref.at[pl.ds(0, tm), :], sem).wait()


def _scatter_rows(slots, x, hs_zero, *, tm):
    t = x.shape[0]
    grid_spec = pltpu.PrefetchScalarGridSpec(
        num_scalar_prefetch=1,
        grid=(t // tm,),
        in_specs=[
            pl.BlockSpec((tm, D_MODEL), lambda i, s: (i, 0)),
            pl.BlockSpec(memory_space=pl.ANY),
        ],
        out_specs=pl.BlockSpec(memory_space=pl.ANY),
        scratch_shapes=[pltpu.SemaphoreType.DMA(())],
    )
    return pl.pallas_call(
        functools.partial(_scatter_kernel, n_tok=t),
        grid_spec=grid_spec,
        out_shape=jax.ShapeDtypeStruct(hs_zero.shape, F32),
        input_output_aliases={2: 0},
        compiler_params=pltpu.CompilerParams(
            dimension_semantics=("arbitrary",), vmem_limit_bytes=VMEM_LIMIT),
        name="moe_scatter",
    )(slots, x, hs_zero)


def _expert_kernel(te_ref, nu_ref, hs_ref, ln_ref, w1_ref, w3_ref, w2_ref, y_ref):
    i = pl.program_id(0)

    @pl.when(i < nu_ref[0])
    def _():
        h = _rms(hs_ref[...], ln_ref[...]).astype(BF16)
        up = _dot(h, w1_ref[...].astype(BF16))
        hid = (up * _sigmoid(up)) * _dot(h, w3_ref[...].astype(BF16))
        y_ref[...] = _dot(hid.astype(BF16), w2_ref[...].astype(BF16))

    @pl.when(i >= nu_ref[0])
    def _():
        y_ref[...] = jnp.zeros(y_ref.shape, F32)


def _expert_mlp(tile_expert, n_used, hs, ln2, w1, w3, w2):
    n_tiles = tile_expert.shape[0]
    grid_spec = pltpu.PrefetchScalarGridSpec(
        num_scalar_prefetch=2,
        grid=(n_tiles,),
        in_specs=[
            pl.BlockSpec((MOE_ROW_TILE, D_MODEL), lambda i, te, nu: (i, 0)),
            pl.BlockSpec((1, D_MODEL), lambda i, te, nu: (0, 0)),
            pl.BlockSpec((None, D_MODEL, D_EXPERT), lambda i, te, nu: (te[i], 0, 0)),
            pl.BlockSpec((None, D_MODEL, D_EXPERT), lambda i, te, nu: (te[i], 0, 0)),
            pl.BlockSpec((None, D_EXPERT, D_MODEL), lambda i, te, nu: (te[i], 0, 0)),
        ],
        out_specs=pl.BlockSpec((MOE_ROW_TILE, D_MODEL), lambda i, te, nu: (i, 0)),
    )
    return pl.pallas_call(
        _expert_kernel,
        grid_spec=grid_spec,
        out_shape=jax.ShapeDtypeStruct(hs.shape, F32),
        compiler_params=pltpu.CompilerParams(
            dimension_semantics=("arbitrary",), vmem_limit_bytes=VMEM_LIMIT),
        name="moe_experts",
    )(tile_expert, n_used, hs, ln2, w1, w3, w2)


def _combine_ple_kernel(slot_ref, x_ref, meta_ref, p_ref, ln_ref, wg_ref, wp_ref, y_ref, o_ref, o2_ref,
                        ybuf_ref, sem, *, n_tok, n_first):
    i = pl.program_id(0)
    tm = x_ref.shape[0]

    def gather(tile, buf):
        def start(r, carry):
            for k in range(2):
                _row_copy(y_ref, slot_ref[k * n_tok + tile * tm + r], ybuf_ref.at[buf, k], r,
                          sem.at[buf]).start()
            return carry
        lax.fori_loop(0, tm, start, 0)

    @pl.when(i == 0)
    def _():
        gather(0, 0)

    @pl.when(i + 1 < pl.num_programs(0))
    def _():
        gather(i + 1, (i + 1) % 2)

    cur = i % 2
    for k in range(2):
        pltpu.make_async_copy(y_ref.at[pl.ds(0, tm), :], ybuf_ref.at[cur, k], sem.at[cur]).wait()
    meta = meta_ref[...]
    x2 = (x_ref[...] + meta[:, META_G:META_G + 1] * ybuf_ref[cur, 0]
          + meta[:, META_G + 1:META_G + 2] * ybuf_ref[cur, 1])
    out = _ple_update(x2, p_ref[...], ln_ref[...], wg_ref[...], wp_ref[...])

    @pl.when(i < n_first)
    def _():
        o_ref[...] = out

    @pl.when(i >= n_first)
    def _():
        o2_ref[...] = out


def _combine_ple(slots, x, meta, p, ln3, wg, wp, y, *, tm, t_first):
    t = x.shape[0]
    n_first = t_first // tm
    grid_spec = pltpu.PrefetchScalarGridSpec(
        num_scalar_prefetch=1,
        grid=(t // tm,),
        in_specs=[
            pl.BlockSpec((tm, D_MODEL), lambda i, s: (i, 0)),
            pl.BlockSpec((tm, LANES), lambda i, s: (i, 0)),
            pl.BlockSpec((tm, PLE_DIM), lambda i, s: (i, 0)),
            pl.BlockSpec((1, D_MODEL), lambda i, s: (0, 0)),
            pl.BlockSpec((D_MODEL, D_MODEL), lambda i, s: (0, 0)),
            pl.BlockSpec((PLE_DIM, D_MODEL), lambda i, s: (0, 0)),
            pl.BlockSpec(memory_space=pl.ANY),
        ],
        out_specs=[
            pl.BlockSpec((tm, D_MODEL), lambda i, s: (jnp.minimum(i, n_first - 1), 0)),
            pl.BlockSpec((tm, D_MODEL), lambda i, s: (jnp.maximum(i - n_first, 0), 0)),
        ],
        scratch_shapes=[
            pltpu.VMEM((2, 2, tm, D_MODEL), F32),
            pltpu.SemaphoreType.DMA((2,)),
        ],
    )
    return pl.pallas_call(
        functools.partial(_combine_ple_kernel, n_tok=t, n_first=n_first),
        grid_spec=grid_spec,
        out_shape=[jax.ShapeDtypeStruct((t_first, D_MODEL), F32),
                   jax.ShapeDtypeStruct((t - t_first, D_MODEL), F32)],
        compiler_params=pltpu.CompilerParams(
            dimension_semantics=("arbitrary",), vmem_limit_bytes=VMEM_LIMIT),
        name="moe_combine_ple",
    )(slots, x, meta, p, ln3, wg, wp, y)


def _sparse_moe_ple(x, p, w, *, tm, t_first):
    t = x.shape[0]
    rt = MOE_ROW_TILE
    meta, cnt = _router(x, w["ln2"], w["rw_hi"], w["rw_lo"], w["rb"], tm=tm)
    counts = cnt[0, :N_EXPERTS].astype(jnp.int32)
    padded = ((counts + rt - 1) // rt) * rt
    ends = jnp.cumsum(padded)
    offs = ends - padded
    eid = meta[:, META_E:META_E + 2].astype(jnp.int32)
    rank = meta[:, META_RANK:META_RANK + 2].astype(jnp.int32)
    base = jnp.sum(jnp.where(eid[:, :, None] == jnp.arange(N_EXPERTS), offs, 0), axis=-1)
    slots = jnp.transpose(base + rank).reshape(2 * t)
    n_tiles = (2 * t + N_EXPERTS * (rt - 1)) // rt
    tile_start = jnp.arange(n_tiles, dtype=jnp.int32) * rt
    tile_expert = jnp.minimum(jnp.sum((tile_start[:, None] >= ends[None, :]).astype(jnp.int32), axis=1),
                              N_EXPERTS - 1)
    n_used = (ends[N_EXPERTS - 1] // rt).reshape(1)
    hs = _scatter_rows(slots, x, jnp.zeros((n_tiles * rt, D_MODEL), F32), tm=tm)
    y = _expert_mlp(tile_expert, n_used, hs, w["ln2"], w["w1"], w["w3"], w["w2"])
    return _combine_ple(slots, x, meta, p, w["ln3"], w["wg"], w["wp"], y, tm=tm, t_first=t_first)


def _rope_tables(pos):
    half = HEAD_DIM // 2
    inv = ROPE_THETA ** (-jnp.arange(half, dtype=F32) / half)
    ang = pos.astype(F32)[:, None] * inv[None, :]
    cos, sin = jnp.cos(ang), jnp.sin(ang)
    return (jnp.concatenate([cos, cos, cos, cos], axis=1),
            jnp.concatenate([-sin, sin, -sin, sin], axis=1))


def _block_diag(w, per):
    n, r, _ = w.shape
    eye = jnp.eye(per, dtype=w.dtype)
    wg = w.reshape(n // per, per, r, r)
    return jnp.einsum("gpij,pq->gpiqj", wg, eye).reshape(n // per, per * r, per * r)


def _layer_weights(ln1, w_in, q_norm, k_norm, conv_w, conv_b, w_a, b_a, w_x, b_x, lam, w_br_rnn,
                   w_br_attn, w_out, ln2, w_rg, b_rg, w_re, b_re, w1, w3, w2, ln3, w_ple_gate,
                   w_ple_proj):
    o_q = 2 * D_RNN
    o_k = o_q + ATTN_W
    o_v = o_k + KV_W
    o_qi = o_v + KV_W
    o_ki = o_qi + N_IDX_HEADS * IDX_DIM
    o_wi = o_ki + IDX_DIM
    o_gr = o_wi + N_IDX_HEADS
    o_ga = o_gr + D_MODEL
    assert o_ki == COL_NR_END
    w_t = jnp.transpose(w_in)
    w_small_t = jnp.concatenate(
        [w_t[o_ki:o_gr], jnp.zeros((LANES - IDX_DIM - N_IDX_HEADS, D_MODEL), F32)], axis=0)
    ones = lambda n: jnp.ones((n,), F32)
    zeros = lambda n: jnp.zeros((n,), F32)
    n_gate = 2 * D_MODEL
    gain = jnp.concatenate([ones(o_q), jnp.tile(q_norm, N_HEADS), jnp.tile(k_norm, N_KV_HEADS),
                            ones(KV_W + N_IDX_HEADS * IDX_DIM + n_gate)])
    norm_on = jnp.concatenate([zeros(o_q), ones(ATTN_W + KV_W), zeros(KV_W + N_IDX_HEADS * IDX_DIM + n_gate)])
    rope_on = jnp.concatenate([zeros(o_q), ones(ATTN_W + KV_W), zeros(KV_W), ones(N_IDX_HEADS * IDX_DIM),
                               zeros(n_gate)])
    post = jnp.concatenate([ones(o_q), jnp.full((ATTN_W,), QK_SCALE, F32),
                            ones(2 * KV_W + N_IDX_HEADS * IDX_DIM + n_gate)])
    colctl = jnp.concatenate([jnp.stack([gain, norm_on, rope_on, post]), jnp.zeros((4, N_MAIN), F32)], axis=0)
    tn = 512
    head_of = jnp.arange(tn) // HEAD_DIM
    bd = (head_of[:, None] == head_of[None, :]).astype(BF16)
    rw = jnp.concatenate([w_re, w_rg, jnp.zeros((D_MODEL, LANES - N_EXPERTS - N_GROUPS), F32)], axis=1)
    rw_hi = rw.astype(BF16)
    rw_lo = (rw - rw_hi.astype(F32)).astype(BF16)
    rb = jnp.concatenate([b_re, b_rg, jnp.zeros((LANES - N_EXPERTS - N_GROUPS,), F32)])[None, :]
    return dict(
        ln1=ln1[None, :], w_t=w_t, w_small_t=w_small_t, colctl=colctl, bd=bd,
        cw=conv_w, cb=conv_b[None, :],
        wa_bd=_block_diag(w_a, 4).astype(BF16), ba=b_a[None, :],
        wx_bd=_block_diag(w_x, 4).astype(BF16), bx=b_x[None, :], lam=lam[None, :],
        wr=w_br_rnn.astype(BF16), wa=w_br_attn.astype(BF16), wo=w_out.astype(BF16),
        ln2=ln2[None, :], rw_hi=rw_hi, rw_lo=rw_lo, rb=rb,
        w1=w1, w3=w3, w2=w2,
        ln3=ln3[None, :], wg=w_ple_gate.astype(BF16), wp=w_ple_proj.astype(BF16),
    )


def _tail_dense(branches, p, w):
    x, rnn, attn, proj = branches
    t = x.shape[0]
    x1 = _merge(x, rnn, attn, proj, w["wr"], w["wa"], w["wo"], tm=min(t, 256))
    x2 = _moe(x1, w["ln2"], w["rw_hi"], w["rw_lo"], w["rb"], w["w1"], w["w3"], w["w2"], tm=min(t, 512))
    return _ple(x2, p, w["ln3"], w["wg"], w["wp"], tm=min(t, 512))


def _tail(branches_a, p_a, branches_b, p_b, w):
    ta, tb = branches_a[0].shape[0], branches_b[0].shape[0]
    if 2 * (ta + tb) < N_EXPERTS * MOE_ROW_TILE:
        return _tail_dense(branches_a, p_a, w), _tail_dense(branches_b, p_b, w)
    tm = math.gcd(256, ta, tb)
    mw = (w["wr"], w["wa"], w["wo"])
    x1 = _merge(*branches_a, *mw, tm=tm, out_rows=ta + tb)
    x1 = _merge(*branches_b, *mw, tm=tm, out_rows=ta + tb, row0=ta, into=x1)
    return _sparse_moe_ple(x1, jnp.concatenate([p_a, p_b], axis=0), w, tm=tm, t_first=ta)


def _prompt_layer(x, w):
    bp, tp, _ = x.shape
    xt = x.reshape(bp * tp, D_MODEL)
    cs, sn = _rope_tables(jnp.arange(tp, dtype=jnp.int32))
    proj, small, kvb, kib = _inproj(xt, w["ln1"], w["w_t"], w["w_small_t"], w["colctl"],
                                    cs, sn, w["bd"], tm=min(tp, 1024))
    conv0 = jnp.zeros((bp, 8, D_RNN), F32)
    h0 = jnp.zeros((bp, 1, D_RNN), F32)
    rnn, h_last = _rglru(proj, conv0, h0, w["cw"], w["cb"], w["wa_bd"], w["ba"], w["wx_bd"], w["bx"],
                         w["lam"], n_seq=bp, tt=min(tp, 256))
    attn = _prompt_attention(proj, small, kvb, kib, n_batch=bp, seq=tp)
    o_k = 2 * D_RNN + ATTN_W
    k = proj[:, o_k:o_k + KV_W].reshape(bp, tp, N_KV_HEADS, HEAD_DIM)
    v = proj[:, o_k + KV_W:o_k + 2 * KV_W].reshape(bp, tp, N_KV_HEADS, HEAD_DIM)
    ki = small[:, :IDX_DIM].reshape(bp, tp, IDX_DIM)
    conv_new = proj.reshape(bp, tp, N_MAIN)[:, tp - (CONV_W - 1):, :D_RNN]
    return (xt, rnn, attn, proj), (k, v, ki, conv_new, h_last.reshape(bp, D_RNN))


def _sample_layer(x, cache_k, cache_v, cache_kidx, state_conv, state_h, page_table, w):
    bs, ts, _ = x.shape
    n_pages = page_table.shape[1]
    past = n_pages * PAGE_SIZE
    xt = x.reshape(bs * ts, D_MODEL)
    cs, sn = _rope_tables(past + jnp.tile(jnp.arange(ts, dtype=jnp.int32), bs))
    proj, small, _, _ = _inproj(xt, w["ln1"], w["w_t"], w["w_small_t"], w["colctl"], cs, sn,
                                w["bd"], tm=bs * ts)
    conv0 = jnp.concatenate([jnp.zeros((bs, 8 - (CONV_W - 1), D_RNN), F32), state_conv], axis=1)
    rnn, h_last = _rglru(proj, conv0, state_h[:, None, :], w["cw"], w["cb"], w["wa_bd"], w["ba"],
                         w["wx_bd"], w["bx"], w["lam"], n_seq=bs, tt=ts)
    o_q = 2 * D_RNN
    o_qi = o_q + ATTN_W + 2 * KV_W
    qi = proj[:, o_qi:o_qi + N_IDX_HEADS * IDX_DIM].reshape(bs, ts, N_IDX_HEADS, IDX_DIM)
    qst = jnp.transpose(qi, (0, 2, 1, 3)).reshape(bs, N_IDX_HEADS * ts, IDX_DIM).astype(BF16)
    wi = small[:, IDX_DIM:IDX_DIM + N_IDX_HEADS].reshape(bs, ts, N_IDX_HEADS)
    wcol = jnp.transpose(wi, (0, 2, 1)).reshape(bs, N_IDX_HEADS * ts, 1)
    n_pool = cache_k.shape[0]
    kidx_t = jnp.transpose(cache_kidx, (0, 2, 1))
    k_t = jnp.transpose(cache_k, (0, 2, 3, 1)).reshape(n_pool, KV_W, PAGE_SIZE)
    v_t = jnp.transpose(cache_v, (0, 2, 3, 1)).reshape(n_pool, KV_W, PAGE_SIZE)
    scores = _sample_select(page_table, qst, wcol, small, kidx_t, n_new=ts)
    bias = _select_bias(scores.reshape(bs * ts, -1), n_new=ts, past=past).reshape(scores.shape)
    q = proj[:, o_q:o_q + ATTN_W].reshape(bs, ts, N_KV_HEADS, N_HEADS // N_KV_HEADS, HEAD_DIM)
    eye = jnp.eye(N_KV_HEADS, dtype=F32)
    qbd = jnp.einsum("btgjd,gk->bgjtkd", q, eye).reshape(bs, N_HEADS * ts, KV_W).astype(BF16)
    att = _sample_attend(page_table, qbd, bias, proj, k_t, v_t, n_new=ts)
    att = att.reshape(bs, N_KV_HEADS, N_HEADS // N_KV_HEADS, ts, N_KV_HEADS, HEAD_DIM)
    att = jnp.stack([att[:, g, :, :, g, :] for g in range(N_KV_HEADS)], axis=1)
    attn = jnp.transpose(att, (0, 3, 1, 2, 4)).reshape(bs * ts, ATTN_W)
    o_k = o_q + ATTN_W
    k = proj[:, o_k:o_k + KV_W].reshape(bs, ts, N_KV_HEADS, HEAD_DIM)
    v = proj[:, o_k + KV_W:o_k + 2 * KV_W].reshape(bs, ts, N_KV_HEADS, HEAD_DIM)
    ki = small[:, :IDX_DIM].reshape(bs, ts, IDX_DIM)
    conv_new = proj.reshape(bs, ts, N_MAIN)[:, ts - (CONV_W - 1):, :D_RNN]
    return (xt, rnn, attn, proj), (k, v, ki, conv_new, h_last.reshape(bs, D_RNN))


def kernel(x_prompt, x_sample, p_prompt, p_sample, cache_k, cache_v, cache_kidx, state_conv, state_h,
           page_table, ln1, w_in, q_norm, k_norm, conv_w, conv_b, w_a, b_a, w_x, b_x, lam, w_br_rnn,
           w_br_attn, w_out, ln2, w_rg, b_rg, w_re, b_re, w1, w3, w2, ln3, w_ple_gate, w_ple_proj):
    weights = (ln1, w_in, q_norm, k_norm, conv_w, conv_b, w_a, b_a, w_x, b_x, lam, w_br_rnn, w_br_attn,
               w_out, ln2, w_rg, b_rg, w_re, b_re, w1, w3, w2, ln3, w_ple_gate, w_ple_proj)
    depth = ln1.shape[0]
    yp, ys = x_prompt, x_sample
    st_p, st_s = [], []
    for i in range(depth):
        w = _layer_weights(*[wt[i] for wt in weights])
        br_p, sp = _prompt_layer(yp, w)
        br_s, ss = _sample_layer(ys, cache_k[i], cache_v[i], cache_kidx[i], state_conv[i], state_h[i],
                                 page_table, w)
        out_p, out_s = _tail(br_p, p_prompt[i].reshape(-1, PLE_DIM), br_s,
                             p_sample[i].reshape(-1, PLE_DIM), w)
        yp, ys = out_p.reshape(yp.shape), out_s.reshape(ys.shape)
        st_p.append(sp)
        st_s.append(ss)
    stack = lambda sts, j: jnp.stack([s[j] for s in sts])
    return (yp, ys, stack(st_p, 0), stack(st_p, 1), stack(st_p, 2), stack(st_p, 3), stack(st_p, 4),
            stack(st_s, 0), stack(st_s, 1), stack(st_s, 2), stack(st_s, 3), stack(st_s, 4))
```

```python
import functools
import math

import jax
import jax.numpy as jnp
import numpy as np
from jax import lax
from jax.experimental import pallas as pl
from jax.experimental.pallas import tpu as pltpu

F32 = jnp.float32
BF16 = jnp.bfloat16

D_MODEL = 2048
HEAD_DIM = 64
N_HEADS = 16
N_KV_HEADS = 4
ATTN_W = N_HEADS * HEAD_DIM
KV_W = N_KV_HEADS * HEAD_DIM
N_IDX_HEADS = 8
IDX_DIM = 64
TOPK_MAX = 256
ROPE_THETA = 10000.0
D_RNN = 1024
N_RNN_BLOCKS = 16
RNN_BLOCK = 64
CONV_W = 4
LRU_C = 8.0
N_GROUPS = 4
EXPERTS_PER_GROUP = 8
N_EXPERTS = 32
D_EXPERT = 256
PLE_DIM = 256
PAGE_SIZE = 128
EPS = 1e-6

LANES = 128
N_MAIN = 8192
COL_RAW_END = 2 * D_RNN
COL_NR_END = 4096
IDX_SCALE = (IDX_DIM ** -0.5) * (N_IDX_HEADS ** -0.5)
QK_SCALE = HEAD_DIM ** -0.5
F32_MIN = float(np.finfo(np.float32).min)
INT_MIN = -2147483648
KEY_NEG_INF = INT_MIN + 0x7FFFFF
VMEM_LIMIT = 56 * 1024 * 1024


def _dot(a, b):
    return jnp.dot(a, b, preferred_element_type=F32)


def _dot_nt(a, b):
    return lax.dot_general(a, b, (((1,), (1,)), ((), ())), preferred_element_type=F32)


def _sigmoid(x):
    return 1.0 / (1.0 + jnp.exp(-x))


def _rms(x, g):
    return x * lax.rsqrt(jnp.mean(x * x, axis=-1, keepdims=True) + EPS) * g


def _split_bf16(x):
    hi = x.astype(BF16)
    lo = (x - hi.astype(F32)).astype(BF16)
    return hi, lo


def _rope_chunks(y, c, s):
    lane = lax.broadcasted_iota(jnp.int32, (1, LANES), 1)
    first_half = (lane % HEAD_DIM) < (HEAD_DIM // 2)
    outs = []
    for k in range(y.shape[1] // LANES):
        yc = y[:, k * LANES:(k + 1) * LANES]
        partner = jnp.where(first_half, pltpu.roll(yc, LANES - HEAD_DIM // 2, 1),
                            pltpu.roll(yc, HEAD_DIM // 2, 1))
        outs.append(yc * c + partner * s)
    return outs[0] if len(outs) == 1 else jnp.concatenate(outs, axis=1)


def _inproj_kernel(x_ref, ln_ref, wlo_ref, whi_ref, ws_ref, ctl_ref, cs_ref, sn_ref, bd_ref,
                   o_ref, os_ref, kvb_ref, kib_ref, h_ref, *, tn):
    j = pl.program_id(1)

    @pl.when(j == 0)
    def _():
        hb = _rms(x_ref[...], ln_ref[...]).astype(BF16)
        h_ref[...] = hb
        ys = _dot_nt(hb, ws_ref[...].astype(BF16))
        lane = lax.broadcasted_iota(jnp.int32, (1, LANES), 1)
        roped = _rope_chunks(ys, cs_ref[...], sn_ref[...])
        os_ref[...] = jnp.where(lane < IDX_DIM, roped, ys)
        kib_ref[...] = jnp.where(lane < IDX_DIM, roped, 0.0).astype(BF16)

    @pl.when(j < COL_NR_END // tn)
    def _():
        o_ref[...] = _dot_nt(h_ref[...], wlo_ref[...].astype(BF16))

    @pl.when(j >= COL_NR_END // tn)
    def _():
        o_ref[...] = _dot_nt(h_ref[...], whi_ref[...].astype(BF16))

    @pl.when((j >= COL_RAW_END // tn) & (j < COL_NR_END // tn))
    def _():
        y = o_ref[...]
        ctl = ctl_ref[...]
        gain, norm_on, rope_on, post = ctl[0:1], ctl[1:2], ctl[2:3], ctl[3:4]
        hi, lo = _split_bf16(y * y)
        ss = _dot(hi, bd_ref[...]) + _dot(lo, bd_ref[...])
        yn = jnp.where(norm_on > 0.0, y * lax.rsqrt(ss * (1.0 / HEAD_DIM) + EPS) * gain, y)
        yr = jnp.where(rope_on > 0.0, _rope_chunks(yn, cs_ref[...], sn_ref[...]), yn)
        o_ref[...] = yr * post

    @pl.when(j == (COL_RAW_END + ATTN_W) // tn)
    def _():
        kvb_ref[...] = o_ref[...].astype(BF16)


def _inproj(x, ln1, w_t, w_small_t, colctl, cs, sn, bd, *, tm, tn=512):
    t = x.shape[0]
    assert tn == 2 * KV_W
    grid = (t // tm, N_MAIN // tn)
    n_lo = COL_NR_END // tn
    n_rope = cs.shape[0] // tm
    gate_row0 = COL_NR_END + IDX_DIM + N_IDX_HEADS
    return pl.pallas_call(
        functools.partial(_inproj_kernel, tn=tn),
        grid=grid,
        in_specs=[
            pl.BlockSpec((tm, D_MODEL), lambda i, j: (i, 0)),
            pl.BlockSpec((1, D_MODEL), lambda i, j: (0, 0)),
            pl.BlockSpec((tn, D_MODEL), lambda i, j: (jnp.minimum(j, n_lo - 1), 0)),
            pl.BlockSpec((pl.Element(tn), pl.Element(D_MODEL)),
                         lambda i, j: (pl.multiple_of(gate_row0 + jnp.maximum(j - n_lo, 0) * tn, 8), 0)),
            pl.BlockSpec((LANES, D_MODEL), lambda i, j: (0, 0)),
            pl.BlockSpec((8, tn), lambda i, j: (0, j)),
            pl.BlockSpec((tm, LANES), lambda i, j: (i % n_rope, 0)),
            pl.BlockSpec((tm, LANES), lambda i, j: (i % n_rope, 0)),
            pl.BlockSpec((tn, tn), lambda i, j: (0, 0)),
        ],
        out_specs=[
            pl.BlockSpec((tm, tn), lambda i, j: (i, j)),
            pl.BlockSpec((tm, LANES), lambda i, j: (i, 0)),
            pl.BlockSpec((tm, 2 * KV_W), lambda i, j: (i, 0)),
            pl.BlockSpec((tm, LANES), lambda i, j: (i, 0)),
        ],
        out_shape=[
            jax.ShapeDtypeStruct((t, N_MAIN), F32),
            jax.ShapeDtypeStruct((t, LANES), F32),
            jax.ShapeDtypeStruct((t, 2 * KV_W), BF16),
            jax.ShapeDtypeStruct((t, LANES), BF16),
        ],
        scratch_shapes=[pltpu.VMEM((tm, D_MODEL), BF16)],
        compiler_params=pltpu.CompilerParams(
            dimension_semantics=("arbitrary", "arbitrary"), vmem_limit_bytes=VMEM_LIMIT),
        name="inproj",
    )(x, ln1, w_t, w_t, w_small_t, colctl, cs, sn, bd)


def _rglru_kernel(x_ref, g_ref, c0_ref, h0_ref, cw_ref, cb_ref, wa_ref, ba_ref, wx_ref, bx_ref,
                  lam_ref, o_ref, hl_ref, xs_ref, a_ref, b_ref, hc_ref):
    t = pl.program_id(1)
    tt = x_ref.shape[0]

    @pl.when(t == 0)
    def _():
        xs_ref[0:8, :] = c0_ref[...]
        hc_ref[...] = h0_ref[...]

    xs_ref[8:8 + tt, :] = x_ref[...]
    cw = cw_ref[...]
    taps = (xs_ref[5:5 + tt, :] * cw[0:1] + xs_ref[6:6 + tt, :] * cw[1:2]
            + xs_ref[7:7 + tt, :] * cw[2:3] + xs_ref[8:8 + tt, :] * cw[3:4])
    xc = cb_ref[...] + taps
    xs_ref[0:8, :] = xs_ref[tt:tt + 8, :]

    xcb = xc.astype(BF16)
    ra, ri = [], []
    for c in range(wa_ref.shape[0]):
        blk = xcb[:, c * 256:(c + 1) * 256]
        ra.append(_dot(blk, wa_ref[c]))
        ri.append(_dot(blk, wx_ref[c]))
    r = _sigmoid(jnp.concatenate(ra, axis=1) + ba_ref[...])
    ig = _sigmoid(jnp.concatenate(ri, axis=1) + bx_ref[...])
    nlam = -lam_ref[...]
    softplus = jnp.maximum(nlam, 0.0) + jnp.log1p(jnp.exp(-jnp.abs(nlam)))
    log_a = (-LRU_C) * r * softplus
    a = jnp.exp(log_a)
    u = jnp.sqrt(jnp.tanh(-log_a) * (a * a + 1.0)) * (ig * xc)

    n8 = tt // 8
    a3 = a.reshape(n8, 8, D_RNN)
    b3 = u.reshape(n8, 8, D_RNN)
    sub = lax.broadcasted_iota(jnp.int32, (1, 8, 1), 1)
    for s in (1, 2, 4):
        a_prev = pltpu.roll(a3, s, 1)
        b_prev = pltpu.roll(b3, s, 1)
        m = sub >= s
        b3 = jnp.where(m, a3 * b_prev + b3, b3)
        a3 = jnp.where(m, a3 * a_prev, a3)
    a_ref[...] = a3.reshape(tt, D_RNN)
    b_ref[...] = b3.reshape(tt, D_RNN)

    def chain(k, carry):
        i0 = pl.multiple_of(k * 8, 8)
        h8 = a_ref[pl.ds(i0, 8), :] * carry + b_ref[pl.ds(i0, 8), :]
        b_ref[pl.ds(i0, 8), :] = h8
        return h8[7:8, :]

    carry = lax.fori_loop(0, n8, chain, hc_ref[...])
    hc_ref[...] = carry
    g = g_ref[...]
    gelu = 0.5 * g * (1.0 + jnp.tanh(0.7978845608028654 * (g + 0.044715 * (g * g * g))))
    o_ref[...] = b_ref[...] * gelu

    @pl.when(t == pl.num_programs(1) - 1)
    def _():
        hl_ref[...] = carry


def _rglru(proj, conv0, h0, cw, cb, wa_bd, ba, wx_bd, bx, lam, *, n_seq, tt):
    t_total = proj.shape[0]
    nt = t_total // (n_seq * tt)
    full = lambda shape: pl.BlockSpec(shape, lambda b, t: (0,) * len(shape))
    return pl.pallas_call(
        _rglru_kernel,
        grid=(n_seq, nt),
        in_specs=[
            pl.BlockSpec((tt, D_RNN), lambda b, t: (b * nt + t, 0)),
            pl.BlockSpec((tt, D_RNN), lambda b, t: (b * nt + t, 1)),
            pl.BlockSpec((None, 8, D_RNN), lambda b, t: (b, 0, 0)),
            pl.BlockSpec((None, 1, D_RNN), lambda b, t: (b, 0, 0)),
            full((CONV_W, D_RNN)), full((1, D_RNN)),
            full(wa_bd.shape), full((1, D_RNN)),
            full(wx_bd.shape), full((1, D_RNN)),
            full((1, D_RNN)),
        ],
        out_specs=[
            pl.BlockSpec((tt, D_RNN), lambda b, t: (b * nt + t, 0)),
            pl.BlockSpec((None, 1, D_RNN), lambda b, t: (b, 0, 0)),
        ],
        out_shape=[
            jax.ShapeDtypeStruct((t_total, D_RNN), F32),
            jax.ShapeDtypeStruct((n_seq, 1, D_RNN), F32),
        ],
        scratch_shapes=[
            pltpu.VMEM((tt + 8, D_RNN), F32),
            pltpu.VMEM((tt, D_RNN), F32),
            pltpu.VMEM((tt, D_RNN), F32),
            pltpu.VMEM((1, D_RNN), F32),
        ],
        compiler_params=pltpu.CompilerParams(
            dimension_semantics=("arbitrary", "arbitrary"), vmem_limit_bytes=VMEM_LIMIT),
        name="rglru",
    )(proj, proj, conv0, h0, cw, cb, wa_bd, ba, wx_bd, bx, lam)


def _select_topk(s, kk):
    rows, n = s.shape
    kkf = float(kk)

    def key_to_f32(w):
        k = w ^ INT_MIN
        bits = jnp.where(k >= 0, k, k ^ 0x7FFFFFFF)
        return k, lax.bitcast_convert_type(bits, F32)

    def vbody(it, w):
        cand_w = w | jnp.left_shift(jnp.int32(1), 31 - it)
        cand_k, cand_f = key_to_f32(cand_w)
        cnt = jnp.sum(jnp.where(s >= cand_f, 1.0, 0.0), axis=1, keepdims=True)
        ok = (cnt >= kkf) | (cand_k < KEY_NEG_INF)
        return jnp.where(ok, cand_w, w)

    w = lax.fori_loop(0, 32, vbody, jnp.zeros((rows, 1), jnp.int32))
    _, thr = key_to_f32(w)
    gt = s > thr
    eq = s == thr
    need = kkf - jnp.sum(jnp.where(gt, 1.0, 0.0), axis=1, keepdims=True)
    col = lax.broadcasted_iota(jnp.int32, (1, n), 1)
    nbits = int(n).bit_length()

    def jbody(it, jmax):
        cand = jmax | jnp.left_shift(jnp.int32(1), nbits - 1 - it)
        cnt = jnp.sum(jnp.where(eq & (col < cand), 1.0, 0.0), axis=1, keepdims=True)
        return jnp.where(cnt <= need, cand, jmax)

    n_ge = jnp.sum(jnp.where(s >= thr, 1.0, 0.0), axis=1, keepdims=True)
    jmax = lax.cond(
        jnp.max(n_ge) > kkf,
        lambda: lax.fori_loop(0, nbits, jbody, jnp.zeros((rows, 1), jnp.int32)),
        lambda: jnp.full((rows, 1), (1 << nbits) - 1, jnp.int32))
    return gt | (eq & (col < jmax))


def _pattn_kernel(q_ref, qi_ref, sm_ref, k_ref, v_ref, ki_ref, o_ref, s_ref, *, i0, n_keys, kc, topk):
    i = pl.program_id(0)
    n_batch, tq = q_ref.shape[0], q_ref.shape[1]
    lane = lax.broadcasted_iota(jnp.int32, (1, LANES), 1)
    qpos = (i0 + i) * tq + lax.broadcasted_iota(jnp.int32, (tq, 1), 0)

    def score(b, carry):
        sm = sm_ref[b]
        qi = qi_ref[b]
        qrows, wrows = [], []
        for h in range(N_IDX_HEADS):
            blk = qi[:, (h // 2) * LANES:(h // 2 + 1) * LANES]
            if h % 2 == 1:
                blk = pltpu.roll(blk, IDX_DIM, 1)
            qrows.append(jnp.where(lane < IDX_DIM, blk, 0.0))
            wrows.append(sm[:, IDX_DIM + h:IDX_DIM + h + 1])
        qst = jnp.concatenate(qrows, axis=0).astype(BF16)
        wst = jnp.concatenate(wrows, axis=0) * IDX_SCALE
        r0 = pl.multiple_of(b * tq, tq)
        for c in range(n_keys // kc):
            s = jnp.maximum(_dot_nt(qst, ki_ref[b, c * kc:(c + 1) * kc, :]), 0.0) * wst
            sc = s[0:tq]
            for h in range(1, N_IDX_HEADS):
                sc = sc + s[h * tq:(h + 1) * tq]
            col = c * kc + lax.broadcasted_iota(jnp.int32, (1, kc), 1)
            s_ref[pl.ds(r0, tq), c * kc:(c + 1) * kc] = jnp.where(col <= qpos, sc, F32_MIN)
        return carry

    lax.fori_loop(0, n_batch, score, 0)

    sel = _select_topk(s_ref[...], topk)
    colf = lax.broadcasted_iota(jnp.int32, (1, n_keys), 1)
    qpos_all = jnp.concatenate([qpos] * n_batch, axis=0)
    s_ref[...] = jnp.where(sel & (colf <= qpos_all), 0.0, -jnp.inf)

    def attend(b, carry):
        _attend_tile(q_ref[b], s_ref[pl.ds(pl.multiple_of(b * tq, tq), tq), :],
                     k_ref[b, 0:n_keys, :], v_ref[b, 0:n_keys, :], o_ref.at[b])
        return carry

    lax.fori_loop(0, n_batch, attend, 0)


def _attend_tile(q, bias, kb, vb, o_ref):
    tq = q.shape[0]
    lane = lax.broadcasted_iota(jnp.int32, (1, LANES), 1)
    bias4 = jnp.concatenate([bias] * 4, axis=0)
    outs = [None] * N_HEADS
    for g in range(N_KV_HEADS):
        lo = (g % 2) * HEAD_DIM
        keep = (lane >= lo) & (lane < lo + HEAD_DIM)
        rows = []
        for j in range(4):
            h = 4 * g + j
            blk = q[:, (h // 2) * LANES:(h // 2 + 1) * LANES]
            if h % 2 != g % 2:
                blk = pltpu.roll(blk, HEAD_DIM, 1)
            piece = jnp.where(keep, blk, 0.0)
            zero = jnp.zeros_like(piece)
            rows.append(jnp.concatenate([piece, zero] if g < 2 else [zero, piece], axis=1))
        qbd = jnp.concatenate(rows, axis=0).astype(BF16)
        logits = _dot_nt(qbd, kb) + bias4
        m = jnp.max(logits, axis=1, keepdims=True)
        p = jnp.exp(logits - m)
        denom = jnp.sum(p, axis=1, keepdims=True)
        acc = _dot(p.astype(BF16), vb) / denom
        for j in range(4):
            outs[4 * g + j] = acc[j * tq:(j + 1) * tq, (g // 2) * LANES:(g // 2 + 1) * LANES]
    for c in range(N_HEADS // 2):
        g = (2 * c) // 4
        even, odd = outs[2 * c], outs[2 * c + 1]
        if g % 2 == 1:
            even = pltpu.roll(even, HEAD_DIM, 1)
        else:
            odd = pltpu.roll(odd, HEAD_DIM, 1)
        o_ref[:, c * LANES:(c + 1) * LANES] = jnp.where(lane < HEAD_DIM, even, odd).astype(BF16)


def _prompt_attention_part(proj3, small3, kvb, kib, *, i0, n_tiles, tq):
    n_batch, seq, _ = proj3.shape
    n_keys = (i0 + n_tiles) * tq
    kc = next(c for c in (512, 256, 128) if n_keys % c == 0)
    return pl.pallas_call(
        functools.partial(_pattn_kernel, i0=i0, n_keys=n_keys, kc=kc, topk=min(TOPK_MAX, seq // 4)),
        grid=(n_tiles,),
        in_specs=[
            pl.BlockSpec((n_batch, tq, ATTN_W), lambda i: (0, i0 + i, 2)),
            pl.BlockSpec((n_batch, tq, N_IDX_HEADS * IDX_DIM), lambda i: (0, i0 + i, 7)),
            pl.BlockSpec((n_batch, tq, LANES), lambda i: (0, i0 + i, 0)),
            pl.BlockSpec((n_batch, seq, KV_W), lambda i: (0, 0, 0)),
            pl.BlockSpec((n_batch, seq, KV_W), lambda i: (0, 0, 1)),
            pl.BlockSpec((n_batch, seq, LANES), lambda i: (0, 0, 0)),
        ],
        out_specs=pl.BlockSpec((n_batch, tq, ATTN_W), lambda i: (0, i, 0)),
        out_shape=jax.ShapeDtypeStruct((n_batch, n_tiles * tq, ATTN_W), BF16),
        scratch_shapes=[pltpu.VMEM((n_batch * tq, n_keys), F32)],
        compiler_params=pltpu.CompilerParams(
            dimension_semantics=("arbitrary",), vmem_limit_bytes=VMEM_LIMIT),
        name=f"prompt_attention_{i0}",
    )(proj3, proj3, small3, kvb, kvb, kib)


def _prompt_attention(proj, small, kvb, kib, *, n_batch, seq, tq=128, tiles_per_part=2):
    nq = seq // tq
    proj3 = proj.reshape(n_batch, seq, N_MAIN)
    small3 = small.reshape(n_batch, seq, LANES)
    kvb3 = kvb.reshape(n_batch, seq, 2 * KV_W)
    kib3 = kib.reshape(n_batch, seq, LANES)
    parts = [
        _prompt_attention_part(proj3, small3, kvb3, kib3, i0=i0,
                               n_tiles=min(tiles_per_part, nq - i0), tq=tq)
        for i0 in range(0, nq, tiles_per_part)
    ]
    return jnp.concatenate(parts, axis=1).reshape(n_batch * seq, ATTN_W)


SELECT_PAGES_PER_STEP = 64
ATTEND_PAGES_PER_STEP = 32
SUB_PAGES = 8


def _sidx_kernel(pt_ref, qst_ref, w_ref, sm_ref, kidx_hbm, o_ref, s_ref, kbuf_ref, ksem,
                 *, n_chunks, n_new, ps):
    c = pl.program_id(1)
    step = pl.program_id(0) * n_chunks + c
    n_steps = pl.num_programs(0) * n_chunks

    def fetch(s, buf):
        sb = s // n_chunks
        sc = s % n_chunks

        def start(r, carry):
            pltpu.make_async_copy(kidx_hbm.at[pt_ref[sb, sc * ps + r]], kbuf_ref.at[buf, r],
                                  ksem.at[buf]).start()
            return carry

        lax.fori_loop(0, ps, start, 0)

    @pl.when(step == 0)
    def _():
        fetch(0, 0)

    @pl.when(step + 1 < n_steps)
    def _():
        fetch(step + 1, (step + 1) % 2)

    cur = step % 2
    pltpu.make_async_copy(kidx_hbm.at[pl.ds(0, ps)], kbuf_ref.at[cur], ksem.at[cur]).wait()
    pages = [kbuf_ref.at[cur, r] for r in range(ps)]
    qst = qst_ref[...]
    w = w_ref[...] * IDX_SCALE

    def head_sum(s):
        s = jnp.maximum(s, 0.0) * w
        out = s[0:n_new]
        for h in range(1, N_IDX_HEADS):
            out = out + s[h * n_new:(h + 1) * n_new]
        return out

    for r0 in range(0, ps, SUB_PAGES):
        kt = jnp.concatenate([pages[r0 + r][...] for r in range(SUB_PAGES)], axis=1).astype(BF16)
        part = head_sum(_dot(qst, kt))
        for r in range(SUB_PAGES):
            s_ref[ps * c + r0 + r] = part[:, r * PAGE_SIZE:(r + 1) * PAGE_SIZE]

    @pl.when(c == n_chunks - 1)
    def _():
        n_past_blocks = n_chunks * ps
        past = n_past_blocks * PAGE_SIZE
        k_new = sm_ref[...][:, 0:IDX_DIM]
        kp = jnp.concatenate([k_new, jnp.zeros((PAGE_SIZE - n_new, IDX_DIM), F32)], axis=0)
        lane = lax.broadcasted_iota(jnp.int32, (n_new, LANES), 1)
        trow = lax.broadcasted_iota(jnp.int32, (n_new, LANES), 0)
        s_new = head_sum(_dot_nt(qst, kp.astype(BF16)))
        s_ref[n_past_blocks] = jnp.where(lane < n_new, jnp.where(lane <= trow, s_new, F32_MIN), -jnp.inf)
        o_ref[...] = jnp.concatenate([s_ref[k] for k in range(n_past_blocks + 1)], axis=1)


def _select_bias_kernel(s_ref, o_ref, *, n_new, past, topk):
    s = s_ref[...]
    rows, n_all = s.shape
    sel = _select_topk(s, topk)
    col = lax.broadcasted_iota(jnp.int32, (1, n_all), 1)
    tq = lax.broadcasted_iota(jnp.int32, (rows, 1), 0) % n_new
    o_ref[...] = jnp.where(sel & ((col - past) <= tq), 0.0, -jnp.inf)


def _select_bias(scores, *, n_new, past, rows_per_step=128):
    rows, n_all = scores.shape
    rows_per_step = min(rows_per_step, rows)
    return pl.pallas_call(
        functools.partial(_select_bias_kernel, n_new=n_new, past=past,
                          topk=min(TOPK_MAX, (past + n_new) // 4)),
        grid=(rows // rows_per_step,),
        in_specs=[pl.BlockSpec((rows_per_step, n_all), lambda i: (i, 0))],
        out_specs=pl.BlockSpec((rows_per_step, n_all), lambda i: (i, 0)),
        out_shape=jax.ShapeDtypeStruct((rows, n_all), F32),
        compiler_params=pltpu.CompilerParams(
            dimension_semantics=("arbitrary",), vmem_limit_bytes=VMEM_LIMIT),
        name="sample_select_bias",
    )(scores)


def _sample_select(page_table, qst, wcol, small, cache_kidx_t, *, n_new):
    n_seq, n_pages = page_table.shape
    ps = min(SELECT_PAGES_PER_STEP, n_pages)
    n_chunks = n_pages // ps
    n_all = n_pages * PAGE_SIZE + LANES
    rows = N_IDX_HEADS * n_new
    grid_spec = pltpu.PrefetchScalarGridSpec(
        num_scalar_prefetch=1,
        grid=(n_seq, n_chunks),
        in_specs=[
            pl.BlockSpec((None, rows, IDX_DIM), lambda b, c, pt: (b, 0, 0)),
            pl.BlockSpec((None, rows, 1), lambda b, c, pt: (b, 0, 0)),
            pl.BlockSpec((n_new, LANES), lambda b, c, pt: (b, 0)),
            pl.BlockSpec(memory_space=pl.ANY),
        ],
        out_specs=pl.BlockSpec((None, n_new, n_all), lambda b, c, pt: (b, 0, 0)),
        scratch_shapes=[
            pltpu.VMEM((n_pages + 1, n_new, LANES), F32),
            pltpu.VMEM((2, ps, IDX_DIM, PAGE_SIZE), F32),
            pltpu.SemaphoreType.DMA((2,)),
        ],
    )
    return pl.pallas_call(
        functools.partial(_sidx_kernel, n_chunks=n_chunks, n_new=n_new, ps=ps),
        grid_spec=grid_spec,
        out_shape=jax.ShapeDtypeStruct((n_seq, n_new, n_all), F32),
        compiler_params=pltpu.CompilerParams(
            dimension_semantics=("arbitrary", "arbitrary"), vmem_limit_bytes=VMEM_LIMIT),
        name="sample_select",
    )(page_table, qst, wcol, small, cache_kidx_t)


def _sattn_kernel(pt_ref, q_ref, bias_ref, biasn_ref, kn_ref, vn_ref, k_hbm, v_hbm, o_ref,
                  m_ref, l_ref, acc_ref, kbuf_ref, vbuf_ref, ksem, vsem, *, n_chunks, n_new, ps):
    b = pl.program_id(0)
    c = pl.program_id(1)
    rows = q_ref.shape[0]
    reps = rows // n_new
    step = b * n_chunks + c
    n_steps = pl.num_programs(0) * n_chunks

    def fetch(s, buf):
        sb = s // n_chunks
        sc = s % n_chunks

        def start(r, carry):
            page = pt_ref[sb, sc * ps + r]
            pltpu.make_async_copy(k_hbm.at[page], kbuf_ref.at[buf, r], ksem.at[buf]).start()
            pltpu.make_async_copy(v_hbm.at[page], vbuf_ref.at[buf, r], vsem.at[buf]).start()
            return carry

        lax.fori_loop(0, ps, start, 0)

    @pl.when(step == 0)
    def _():
        fetch(0, 0)

    @pl.when(step + 1 < n_steps)
    def _():
        fetch(step + 1, (step + 1) % 2)

    cur = step % 2
    pltpu.make_async_copy(k_hbm.at[pl.ds(0, ps)], kbuf_ref.at[cur], ksem.at[cur]).wait()
    pltpu.make_async_copy(v_hbm.at[pl.ds(0, ps)], vbuf_ref.at[cur], vsem.at[cur]).wait()
    kpages = [kbuf_ref.at[cur, r] for r in range(ps)]
    vpages = [vbuf_ref.at[cur, r] for r in range(ps)]

    @pl.when(c == 0)
    def _():
        m_ref[...] = jnp.full(m_ref.shape, -1e30, F32)
        l_ref[...] = jnp.zeros(l_ref.shape, F32)
        acc_ref[...] = jnp.zeros(acc_ref.shape, F32)

    def update(logits, bias, pv):
        logits = logits + jnp.concatenate([bias] * reps, axis=0)
        m_old = m_ref[...]
        m_new = jnp.maximum(m_old, jnp.max(logits, axis=1, keepdims=True))
        alpha = jnp.exp(m_old - m_new)
        p = jnp.exp(logits - m_new)
        l_ref[...] = alpha * l_ref[...] + jnp.sum(p, axis=1, keepdims=True)
        acc_ref[...] = alpha * acc_ref[...] + pv(p.astype(BF16))
        m_ref[...] = m_new

    sub_keys = SUB_PAGES * PAGE_SIZE
    logits, vts = [], []
    for r0 in range(0, ps, SUB_PAGES):
        kt = jnp.concatenate([kpages[r0 + r][...] for r in range(SUB_PAGES)], axis=1).astype(BF16)
        vts.append(jnp.concatenate([vpages[r0 + r][...] for r in range(SUB_PAGES)], axis=1).astype(BF16))
        logits.append(_dot(q_ref[...], kt))

    def pv(p):
        acc = _dot_nt(p[:, 0:sub_keys], vts[0])
        for n in range(1, len(vts)):
            acc = acc + _dot_nt(p[:, n * sub_keys:(n + 1) * sub_keys], vts[n])
        return acc

    update(jnp.concatenate(logits, axis=1), bias_ref[...], pv)

    @pl.when(c == n_chunks - 1)
    def _():
        pad = jnp.zeros((PAGE_SIZE - n_new, KV_W), F32)
        kn = jnp.concatenate([kn_ref[...], pad], axis=0).astype(BF16)
        vn = jnp.concatenate([vn_ref[...], pad], axis=0).astype(BF16)
        update(_dot_nt(q_ref[...], kn), biasn_ref[...], lambda p: _dot(p, vn))
        o_ref[...] = acc_ref[...] / l_ref[...]


def _sample_attend(page_table, qbd, bias, proj, cache_k_t, cache_v_t, *, n_new):
    n_seq, n_pages = page_table.shape
    ps = min(ATTEND_PAGES_PER_STEP, n_pages)
    n_chunks = n_pages // ps
    rows = qbd.shape[1]
    chunk_keys = ps * PAGE_SIZE
    grid_spec = pltpu.PrefetchScalarGridSpec(
        num_scalar_prefetch=1,
        grid=(n_seq, n_chunks),
        in_specs=[
            pl.BlockSpec((None, rows, KV_W), lambda b, c, pt: (b, 0, 0)),
            pl.BlockSpec((None, n_new, chunk_keys), lambda b, c, pt: (b, 0, c)),
            pl.BlockSpec((None, n_new, LANES), lambda b, c, pt: (b, 0, n_pages)),
            pl.BlockSpec((n_new, KV_W), lambda b, c, pt: (b, 12)),
            pl.BlockSpec((n_new, KV_W), lambda b, c, pt: (b, 13)),
            pl.BlockSpec(memory_space=pl.ANY),
            pl.BlockSpec(memory_space=pl.ANY),
        ],
        out_specs=pl.BlockSpec((None, rows, KV_W), lambda b, c, pt: (b, 0, 0)),
        scratch_shapes=[
            pltpu.VMEM((rows, 1), F32),
            pltpu.VMEM((rows, 1), F32),
            pltpu.VMEM((rows, KV_W), F32),
            pltpu.VMEM((2, ps, KV_W, PAGE_SIZE), F32),
            pltpu.VMEM((2, ps, KV_W, PAGE_SIZE), F32),
            pltpu.SemaphoreType.DMA((2,)),
            pltpu.SemaphoreType.DMA((2,)),
        ],
    )
    return pl.pallas_call(
        functools.partial(_sattn_kernel, n_chunks=n_chunks, n_new=n_new, ps=ps),
        grid_spec=grid_spec,
        out_shape=jax.ShapeDtypeStruct((n_seq, rows, KV_W), F32),
        compiler_params=pltpu.CompilerParams(
            dimension_semantics=("arbitrary", "arbitrary"), vmem_limit_bytes=VMEM_LIMIT),
        name="sample_attend",
    )(page_table, qbd, bias, bias, proj, proj, cache_k_t, cache_v_t)


def _merge_kernel(x_ref, rnn_ref, att_ref, gr_ref, ga_ref, wr_ref, wa_ref, wo_ref, o_ref, *, n_valid):
    i = pl.program_id(0)

    @pl.when(i < n_valid)
    def _():
        mixed = (_sigmoid(gr_ref[...]) * _dot(rnn_ref[...].astype(BF16), wr_ref[...])
                 + _sigmoid(ga_ref[...]) * _dot(att_ref[...].astype(BF16), wa_ref[...]))
        o_ref[...] = x_ref[...] + _dot(mixed.astype(BF16), wo_ref[...])

    @pl.when(i >= n_valid)
    def _():
        o_ref[...] = jnp.zeros(o_ref.shape, F32)


def _merge_into_kernel(x_ref, rnn_ref, att_ref, gr_ref, ga_ref, wr_ref, wa_ref, wo_ref, dst_ref, o_ref,
                       *, n_valid):
    del dst_ref
    _merge_kernel(x_ref, rnn_ref, att_ref, gr_ref, ga_ref, wr_ref, wa_ref, wo_ref, o_ref, n_valid=n_valid)


def _merge(x, rnn, attn, proj, wr, wa, wo, *, tm, out_rows=None, row0=0, into=None):
    t = x.shape[0]
    out_rows = t if out_rows is None else out_rows
    blk0 = row0 // tm
    n_valid = t // tm
    n_steps = n_valid if into is not None else out_rows // tm
    row = lambda i: jnp.minimum(i, n_valid - 1)
    in_specs = [
        pl.BlockSpec((tm, D_MODEL), lambda i: (row(i), 0)),
        pl.BlockSpec((tm, D_RNN), lambda i: (row(i), 0)),
        pl.BlockSpec((tm, ATTN_W), lambda i: (row(i), 0)),
        pl.BlockSpec((tm, D_MODEL), lambda i: (row(i), 2)),
        pl.BlockSpec((tm, D_MODEL), lambda i: (row(i), 3)),
        pl.BlockSpec((D_RNN, D_MODEL), lambda i: (0, 0)),
        pl.BlockSpec((ATTN_W, D_MODEL), lambda i: (0, 0)),
        pl.BlockSpec((D_MODEL, D_MODEL), lambda i: (0, 0)),
    ]
    args = (x, rnn, attn, proj, proj, wr, wa, wo)
    if into is not None:
        in_specs.append(pl.BlockSpec(memory_space=pl.ANY))
        args = args + (into,)
    return pl.pallas_call(
        functools.partial(_merge_kernel if into is None else _merge_into_kernel, n_valid=n_valid),
        grid=(n_steps,),
        in_specs=in_specs,
        out_specs=pl.BlockSpec((tm, D_MODEL), lambda i: (blk0 + i, 0)),
        out_shape=jax.ShapeDtypeStruct((out_rows, D_MODEL), F32),
        input_output_aliases={} if into is None else {len(args) - 1: 0},
        compiler_params=pltpu.CompilerParams(
            dimension_semantics=("arbitrary",), vmem_limit_bytes=VMEM_LIMIT),
        name="merge",
    )(*args)


def _route(h, rwh, rwl, rb):
    lane = lax.broadcasted_iota(jnp.int32, (1, LANES), 1)
    lanef = lane.astype(F32)
    hh, hl = _split_bf16(h)
    lg = (_dot(hh, rwh) + _dot(hl, rwh) + _dot(hh, rwl)) + rb
    is_g = (lane >= N_EXPERTS) & (lane < N_EXPERTS + N_GROUPS)
    gl = jnp.where(is_g, lg, -jnp.inf)
    gmax = jnp.max(gl, axis=1, keepdims=True)
    gprob = 1.0 / jnp.sum(jnp.exp(gl - gmax), axis=1, keepdims=True)
    gsel = jnp.min(jnp.where(is_g & (lg == gmax), lanef - N_EXPERTS, 1e9), axis=1, keepdims=True)
    in_grp = (lane < N_EXPERTS) & (jnp.floor(lanef * (1.0 / EXPERTS_PER_GROUP)) == gsel)
    v1 = jnp.where(in_grp, lg, -jnp.inf)
    t1 = jnp.max(v1, axis=1, keepdims=True)
    i1 = jnp.min(jnp.where(in_grp & (lg == t1), lanef, 1e9), axis=1, keepdims=True)
    rest = in_grp & (lanef != i1)
    v2 = jnp.where(rest, lg, -jnp.inf)
    t2 = jnp.max(v2, axis=1, keepdims=True)
    i2 = jnp.min(jnp.where(rest & (lg == t2), lanef, 1e9), axis=1, keepdims=True)
    d = jnp.exp(t2 - t1)
    return i1, i2, gprob / (1.0 + d), gprob * d / (1.0 + d)


def _moe_kernel(x_ref, ln_ref, rwh_ref, rwl_ref, rb_ref, w1_ref, w3_ref, w2_ref, o_ref, h_ref, gate_ref):
    e = pl.program_id(1)
    lane = lax.broadcasted_iota(jnp.int32, (1, LANES), 1)

    @pl.when(e == 0)
    def _():
        h = _rms(x_ref[...], ln_ref[...])
        h_ref[...] = h.astype(BF16)
        i1, i2, g1, g2 = _route(h, rwh_ref[...], rwl_ref[...], rb_ref[...])
        lanef = lane.astype(F32)
        gate_ref[...] = jnp.where(lanef == i1, g1, 0.0) + jnp.where(lanef == i2, g2, 0.0)

    ge = jnp.sum(jnp.where(lane == e, gate_ref[...], 0.0), axis=1, keepdims=True)
    up = _dot(h_ref[...], w1_ref[...].astype(BF16))
    hid = (up * _sigmoid(up)) * _dot(h_ref[...], w3_ref[...].astype(BF16))
    contrib = _dot((hid * ge).astype(BF16), w2_ref[...].astype(BF16))

    @pl.when(e == 0)
    def _():
        o_ref[...] = x_ref[...] + contrib

    @pl.when(e > 0)
    def _():
        o_ref[...] += contrib


def _moe(x, ln2, rw_hi, rw_lo, rb, w1, w3, w2, *, tm):
    t = x.shape[0]
    return pl.pallas_call(
        _moe_kernel,
        grid=(t // tm, N_EXPERTS),
        in_specs=[
            pl.BlockSpec((tm, D_MODEL), lambda i, e: (i, 0)),
            pl.BlockSpec((1, D_MODEL), lambda i, e: (0, 0)),
            pl.BlockSpec((D_MODEL, LANES), lambda i, e: (0, 0)),
            pl.BlockSpec((D_MODEL, LANES), lambda i, e: (0, 0)),
            pl.BlockSpec((1, LANES), lambda i, e: (0, 0)),
            pl.BlockSpec((None, D_MODEL, D_EXPERT), lambda i, e: (e, 0, 0)),
            pl.BlockSpec((None, D_MODEL, D_EXPERT), lambda i, e: (e, 0, 0)),
            pl.BlockSpec((None, D_EXPERT, D_MODEL), lambda i, e: (e, 0, 0)),
        ],
        out_specs=pl.BlockSpec((tm, D_MODEL), lambda i, e: (i, 0)),
        out_shape=jax.ShapeDtypeStruct((t, D_MODEL), F32),
        scratch_shapes=[pltpu.VMEM((tm, D_MODEL), BF16), pltpu.VMEM((tm, LANES), F32)],
        compiler_params=pltpu.CompilerParams(
            dimension_semantics=("arbitrary", "arbitrary"), vmem_limit_bytes=VMEM_LIMIT),
        name="moe",
    )(x, ln2, rw_hi, rw_lo, rb, w1, w3, w2)


def _ple_update(x2, p, ln, wg, wp):
    gate = _sigmoid(_dot(_rms(x2, ln).astype(BF16), wg))
    return x2 + gate * _dot(p.astype(BF16), wp)


def _ple_kernel(x_ref, p_ref, ln_ref, wg_ref, wp_ref, o_ref):
    o_ref[...] = _ple_update(x_ref[...], p_ref[...], ln_ref[...], wg_ref[...], wp_ref[...])


def _ple(x, p, ln3, wg, wp, *, tm):
    t = x.shape[0]
    return pl.pallas_call(
        _ple_kernel,
        grid=(t // tm,),
        in_specs=[
            pl.BlockSpec((tm, D_MODEL), lambda i: (i, 0)),
            pl.BlockSpec((tm, PLE_DIM), lambda i: (i, 0)),
            pl.BlockSpec((1, D_MODEL), lambda i: (0, 0)),
            pl.BlockSpec((D_MODEL, D_MODEL), lambda i: (0, 0)),
            pl.BlockSpec((PLE_DIM, D_MODEL), lambda i: (0, 0)),
        ],
        out_specs=pl.BlockSpec((tm, D_MODEL), lambda i: (i, 0)),
        out_shape=jax.ShapeDtypeStruct((t, D_MODEL), F32),
        compiler_params=pltpu.CompilerParams(
            dimension_semantics=("arbitrary",), vmem_limit_bytes=VMEM_LIMIT),
        name="ple",
    )(x, p, ln3, wg, wp)


MOE_ROW_TILE = 256
META_E, META_G, META_RANK = 0, 2, 4


def _router_kernel(x_ref, ln_ref, rwh_ref, rwl_ref, rb_ref, tri_ref, meta_ref, cnt_ref, carry_ref):
    i = pl.program_id(0)
    lane = lax.broadcasted_iota(jnp.int32, (1, LANES), 1)
    lanef = lane.astype(F32)

    @pl.when(i == 0)
    def _():
        carry_ref[...] = jnp.zeros(carry_ref.shape, F32)

    i1, i2, g1, g2 = _route(_rms(x_ref[...], ln_ref[...]), rwh_ref[...], rwl_ref[...], rb_ref[...])
    onehot = jnp.where((lanef == i1) | (lanef == i2), 1.0, 0.0)
    before = _dot(tri_ref[...], onehot.astype(BF16)) + carry_ref[...]
    r1 = jnp.sum(jnp.where(lanef == i1, before, 0.0), axis=1, keepdims=True)
    r2 = jnp.sum(jnp.where(lanef == i2, before, 0.0), axis=1, keepdims=True)
    carry_ref[...] += jnp.sum(onehot, axis=0, keepdims=True)
    rec = jnp.zeros((x_ref.shape[0], LANES), F32)
    for k, val in ((META_E, i1), (META_E + 1, i2), (META_G, g1), (META_G + 1, g2),
                   (META_RANK, r1), (META_RANK + 1, r2)):
        rec = jnp.where(lane == k, val, rec)
    meta_ref[...] = rec
    cnt_ref[...] = jnp.broadcast_to(carry_ref[...], cnt_ref.shape)


def _router(x, ln2, rw_hi, rw_lo, rb, *, tm):
    t = x.shape[0]
    r = lax.broadcasted_iota(jnp.int32, (tm, tm), 0)
    c = lax.broadcasted_iota(jnp.int32, (tm, tm), 1)
    tri = (c < r).astype(BF16)
    return pl.pallas_call(
        _router_kernel,
        grid=(t // tm,),
        in_specs=[
            pl.BlockSpec((tm, D_MODEL), lambda i: (i, 0)),
            pl.BlockSpec((1, D_MODEL), lambda i: (0, 0)),
            pl.BlockSpec((D_MODEL, LANES), lambda i: (0, 0)),
            pl.BlockSpec((D_MODEL, LANES), lambda i: (0, 0)),
            pl.BlockSpec((1, LANES), lambda i: (0, 0)),
            pl.BlockSpec((tm, tm), lambda i: (0, 0)),
        ],
        out_specs=[
            pl.BlockSpec((tm, LANES), lambda i: (i, 0)),
            pl.BlockSpec((8, LANES), lambda i: (0, 0)),
        ],
        out_shape=[
            jax.ShapeDtypeStruct((t, LANES), F32),
            jax.ShapeDtypeStruct((8, LANES), F32),
        ],
        scratch_shapes=[pltpu.VMEM((1, LANES), F32)],
        compiler_params=pltpu.CompilerParams(
            dimension_semantics=("arbitrary",), vmem_limit_bytes=VMEM_LIMIT),
        name="moe_router",
    )(x, ln2, rw_hi, rw_lo, rb, tri)


def _row_copy(src_ref, src_row, dst_ref, dst_row, sem):
    return pltpu.make_async_copy(src_ref.at[pl.ds(src_row, 1), :], dst_ref.at[pl.ds(dst_row, 1), :], sem)


def _scatter_kernel(slot_ref, pad0_ref, npad_ref, nu_ref, x_ref, hs_ref, zero_ref, sem, zsem,
                    *, n_tok, n_tiles):
    tm = x_ref.shape[0]
    base = pl.program_id(0) * tm
    rt = zero_ref.shape[0]
    pieces = [1 << k for k in range(rt.bit_length() - 2, 2, -1)]

    def zero_fill(go):
        def per_expert(e, carry):
            first = pad0_ref[e]
            first8 = (first + 7) & -8
            for k in range(7):
                @pl.when(first + k < jnp.minimum(first8, first + npad_ref[e]))
                def _():
                    go(pltpu.make_async_copy(zero_ref.at[pl.ds(0, 1), :],
                                             hs_ref.at[pl.ds(first + k, 1), :], zsem))
            n = jnp.maximum(first + npad_ref[e] - first8, 0)
            for bit in pieces:
                @pl.when((n & bit) != 0)
                def _():
                    row = pl.multiple_of(first8 + (n & (-2 * bit)), 8)
                    go(pltpu.make_async_copy(zero_ref.at[pl.ds(0, bit), :],
                                             hs_ref.at[pl.ds(row, bit), :], zsem))
            return carry

        def per_tile(tile, carry):
            go(pltpu.make_async_copy(zero_ref, hs_ref.at[pl.ds(pl.multiple_of(tile * rt, rt), rt), :],
                                     zsem))
            return carry

        lax.fori_loop(0, N_EXPERTS, per_expert, 0)
        lax.fori_loop(nu_ref[0], n_tiles, per_tile, 0)

    @pl.when(pl.program_id(0) == 0)
    def _():
        zero_ref[...] = jnp.zeros(zero_ref.shape, F32)
        zero_fill(lambda cp: cp.start())
        zero_fill(lambda cp: cp.wait())

    def start(r, carry):
        for k in range(2):
            _row_copy(x_ref, r, hs_ref, slot_ref[k * n_tok + base + r], sem).start()
        return carry

    lax.fori_loop(0, tm, start, 0)
    for k in range(2):
        pltpu.make_async_copy(x_ref, hs_ref.at[pl.ds(0, tm), :], sem).wait()


def _scatter_rows(slots, pad0, npad, n_used, x, *, tm, n_tiles):
    t = x.shape[0]
    grid_spec = pltpu.PrefetchScalarGridSpec(
        num_scalar_prefetch=4,
        grid=(t // tm,),
        in_specs=[pl.BlockSpec((tm, D_MODEL), lambda i, *_: (i, 0))],
        out_specs=pl.BlockSpec(memory_space=pl.ANY),
        scratch_shapes=[
            pltpu.VMEM((MOE_ROW_TILE, D_MODEL), F32),
            pltpu.SemaphoreType.DMA(()),
            pltpu.SemaphoreType.DMA(()),
        ],
    )
    return pl.pallas_call(
        functools.partial(_scatter_kernel, n_tok=t, n_tiles=n_tiles),
        grid_spec=grid_spec,
        out_shape=jax.ShapeDtypeStruct((n_tiles * MOE_ROW_TILE, D_MODEL), F32),
        compiler_params=pltpu.CompilerParams(
            dimension_semantics=("arbitrary",), vmem_limit_bytes=VMEM_LIMIT),
        name="moe_scatter",
    )(slots, pad0, npad, n_used, x)


def _expert_kernel(te_ref, nu_ref, hs_ref, ln_ref, w1_ref, w3_ref, w2_ref, y_ref):
    i = pl.program_id(0)

    @pl.when(i < nu_ref[0])
    def _():
        h = _rms(hs_ref[...], ln_ref[...]).astype(BF16)
        up = _dot(h, w1_ref[...].astype(BF16))
        hid = (up * _sigmoid(up)) * _dot(h, w3_ref[...].astype(BF16))
        y_ref[...] = _dot(hid.astype(BF16), w2_ref[...].astype(BF16))

    @pl.when(i >= nu_ref[0])
    def _():
        y_ref[...] = jnp.zeros(y_ref.shape, F32)


def _expert_mlp(tile_expert, n_used, hs, ln2, w1, w3, w2):
    n_tiles = tile_expert.shape[0]
    grid_spec = pltpu.PrefetchScalarGridSpec(
        num_scalar_prefetch=2,
        grid=(n_tiles,),
        in_specs=[
            pl.BlockSpec((MOE_ROW_TILE, D_MODEL), lambda i, te, nu: (i, 0)),
            pl.BlockSpec((1, D_MODEL), lambda i, te, nu: (0, 0)),
            pl.BlockSpec((None, D_MODEL, D_EXPERT), lambda i, te, nu: (te[i], 0, 0)),
            pl.BlockSpec((None, D_MODEL, D_EXPERT), lambda i, te, nu: (te[i], 0, 0)),
            pl.BlockSpec((None, D_EXPERT, D_MODEL), lambda i, te, nu: (te[i], 0, 0)),
        ],
        out_specs=pl.BlockSpec((MOE_ROW_TILE, D_MODEL), lambda i, te, nu: (i, 0)),
    )
    return pl.pallas_call(
        _expert_kernel,
        grid_spec=grid_spec,
        out_shape=jax.ShapeDtypeStruct(hs.shape, F32),
        compiler_params=pltpu.CompilerParams(
            dimension_semantics=("arbitrary",), vmem_limit_bytes=VMEM_LIMIT),
        name="moe_experts",
    )(tile_expert, n_used, hs, ln2, w1, w3, w2)


def _combine_ple_kernel(slot_ref, x_ref, meta_ref, p_ref, ln_ref, wg_ref, wp_ref, y_ref, o_ref, o2_ref,
                        ybuf_ref, sem, *, n_tok, n_first):
    i = pl.program_id(0)
    tm = x_ref.shape[0]

    def gather(tile, buf):
        def start(r, carry):
            for k in range(2):
                _row_copy(y_ref, slot_ref[k * n_tok + tile * tm + r], ybuf_ref.at[buf, k], r,
                          sem.at[buf]).start()
            return carry
        lax.fori_loop(0, tm, start, 0)

    @pl.when(i == 0)
    def _():
        gather(0, 0)

    @pl.when(i + 1 < pl.num_programs(0))
    def _():
        gather(i + 1, (i + 1) % 2)

    cur = i % 2
    for k in range(2):
        pltpu.make_async_copy(y_ref.at[pl.ds(0, tm), :], ybuf_ref.at[cur, k], sem.at[cur]).wait()
    meta = meta_ref[...]
    x2 = (x_ref[...] + meta[:, META_G:META_G + 1] * ybuf_ref[cur, 0]
          + meta[:, META_G + 1:META_G + 2] * ybuf_ref[cur, 1])
    out = _ple_update(x2, p_ref[...], ln_ref[...], wg_ref[...], wp_ref[...])

    @pl.when(i < n_first)
    def _():
        o_ref[...] = out

    @pl.when(i >= n_first)
    def _():
        o2_ref[...] = out


def _combine_ple(slots, x, meta, p, ln3, wg, wp, y, *, tm, t_first):
    t = x.shape[0]
    n_first = t_first // tm
    grid_spec = pltpu.PrefetchScalarGridSpec(
        num_scalar_prefetch=1,
        grid=(t // tm,),
        in_specs=[
            pl.BlockSpec((tm, D_MODEL), lambda i, s: (i, 0)),
            pl.BlockSpec((tm, LANES), lambda i, s: (i, 0)),
            pl.BlockSpec((tm, PLE_DIM), lambda i, s: (i, 0)),
            pl.BlockSpec((1, D_MODEL), lambda i, s: (0, 0)),
            pl.BlockSpec((D_MODEL, D_MODEL), lambda i, s: (0, 0)),
            pl.BlockSpec((PLE_DIM, D_MODEL), lambda i, s: (0, 0)),
            pl.BlockSpec(memory_space=pl.ANY),
        ],
        out_specs=[
            pl.BlockSpec((tm, D_MODEL), lambda i, s: (jnp.minimum(i, n_first - 1), 0)),
            pl.BlockSpec((tm, D_MODEL), lambda i, s: (jnp.maximum(i - n_first, 0), 0)),
        ],
        scratch_shapes=[
            pltpu.VMEM((2, 2, tm, D_MODEL), F32),
            pltpu.SemaphoreType.DMA((2,)),
        ],
    )
    return pl.pallas_call(
        functools.partial(_combine_ple_kernel, n_tok=t, n_first=n_first),
        grid_spec=grid_spec,
        out_shape=[jax.ShapeDtypeStruct((t_first, D_MODEL), F32),
                   jax.ShapeDtypeStruct((t - t_first, D_MODEL), F32)],
        compiler_params=pltpu.CompilerParams(
            dimension_semantics=("arbitrary",), vmem_limit_bytes=VMEM_LIMIT),
        name="moe_combine_ple",
    )(slots, x, meta, p, ln3, wg, wp, y)


def _sparse_moe_ple(x, p, w, *, tm, t_first):
    t = x.shape[0]
    rt = MOE_ROW_TILE
    meta, cnt = _router(x, w["ln2"], w["rw_hi"], w["rw_lo"], w["rb"], tm=tm)
    counts = cnt[0, :N_EXPERTS].astype(jnp.int32)
    padded = ((counts + rt - 1) // rt) * rt
    ends = jnp.cumsum(padded)
    offs = ends - padded
    eid = meta[:, META_E:META_E + 2].astype(jnp.int32)
    rank = meta[:, META_RANK:META_RANK + 2].astype(jnp.int32)
    base = jnp.sum(jnp.where(eid[:, :, None] == jnp.arange(N_EXPERTS), offs, 0), axis=-1)
    slots = jnp.transpose(base + rank).reshape(2 * t)
    n_tiles = (2 * t + N_EXPERTS * (rt - 1)) // rt
    tile_start = jnp.arange(n_tiles, dtype=jnp.int32) * rt
    tile_expert = jnp.minimum(jnp.sum((tile_start[:, None] >= ends[None, :]).astype(jnp.int32), axis=1),
                              N_EXPERTS - 1)
    n_used = (ends[N_EXPERTS - 1] // rt).reshape(1)
    hs = _scatter_rows(slots, offs + counts, padded - counts, n_used, x, tm=tm, n_tiles=n_tiles)
    y = _expert_mlp(tile_expert, n_used, hs, w["ln2"], w["w1"], w["w3"], w["w2"])
    return _combine_ple(slots, x, meta, p, w["ln3"], w["wg"], w["wp"], y, tm=tm, t_first=t_first)


def _rope_tables(pos):
    half = HEAD_DIM // 2
    inv = ROPE_THETA ** (-jnp.arange(half, dtype=F32) / half)
    ang = pos.astype(F32)[:, None] * inv[None, :]
    cos, sin = jnp.cos(ang), jnp.sin(ang)
    return (jnp.concatenate([cos, cos, cos, cos], axis=1),
            jnp.concatenate([-sin, sin, -sin, sin], axis=1))


def _block_diag(w, per):
    n, r, _ = w.shape
    eye = jnp.eye(per, dtype=w.dtype)
    wg = w.reshape(n // per, per, r, r)
    return jnp.einsum("gpij,pq->gpiqj", wg, eye).reshape(n // per, per * r, per * r)


def _layer_weights(ln1, w_in, q_norm, k_norm, conv_w, conv_b, w_a, b_a, w_x, b_x, lam, w_br_rnn,
                   w_br_attn, w_out, ln2, w_rg, b_rg, w_re, b_re, w1, w3, w2, ln3, w_ple_gate,
                   w_ple_proj):
    o_q = 2 * D_RNN
    o_k = o_q + ATTN_W
    o_v = o_k + KV_W
    o_qi = o_v + KV_W
    o_ki = o_qi + N_IDX_HEADS * IDX_DIM
    o_wi = o_ki + IDX_DIM
    o_gr = o_wi + N_IDX_HEADS
    o_ga = o_gr + D_MODEL
    assert o_ki == COL_NR_END
    w_t = jnp.transpose(w_in)
    w_small_t = jnp.concatenate(
        [w_t[o_ki:o_gr], jnp.zeros((LANES - IDX_DIM - N_IDX_HEADS, D_MODEL), F32)], axis=0)
    ones = lambda n: jnp.ones((n,), F32)
    zeros = lambda n: jnp.zeros((n,), F32)
    n_gate = 2 * D_MODEL
    gain = jnp.concatenate([ones(o_q), jnp.tile(q_norm, N_HEADS), jnp.tile(k_norm, N_KV_HEADS),
                            ones(KV_W + N_IDX_HEADS * IDX_DIM + n_gate)])
    norm_on = jnp.concatenate([zeros(o_q), ones(ATTN_W + KV_W), zeros(KV_W + N_IDX_HEADS * IDX_DIM + n_gate)])
    rope_on = jnp.concatenate([zeros(o_q), ones(ATTN_W + KV_W), zeros(KV_W), ones(N_IDX_HEADS * IDX_DIM),
                               zeros(n_gate)])
    post = jnp.concatenate([ones(o_q), jnp.full((ATTN_W,), QK_SCALE, F32),
                            ones(2 * KV_W + N_IDX_HEADS * IDX_DIM + n_gate)])
    colctl = jnp.concatenate([jnp.stack([gain, norm_on, rope_on, post]), jnp.zeros((4, N_MAIN), F32)], axis=0)
    tn = 512
    head_of = jnp.arange(tn) // HEAD_DIM
    bd = (head_of[:, None] == head_of[None, :]).astype(BF16)
    rw = jnp.concatenate([w_re, w_rg, jnp.zeros((D_MODEL, LANES - N_EXPERTS - N_GROUPS), F32)], axis=1)
    rw_hi = rw.astype(BF16)
    rw_lo = (rw - rw_hi.astype(F32)).astype(BF16)
    rb = jnp.concatenate([b_re, b_rg, jnp.zeros((LANES - N_EXPERTS - N_GROUPS,), F32)])[None, :]
    return dict(
        ln1=ln1[None, :], w_t=w_t, w_small_t=w_small_t, colctl=colctl, bd=bd,
        cw=conv_w, cb=conv_b[None, :],
        wa_bd=_block_diag(w_a, 4).astype(BF16), ba=b_a[None, :],
        wx_bd=_block_diag(w_x, 4).astype(BF16), bx=b_x[None, :], lam=lam[None, :],
        wr=w_br_rnn.astype(BF16), wa=w_br_attn.astype(BF16), wo=w_out.astype(BF16),
        ln2=ln2[None, :], rw_hi=rw_hi, rw_lo=rw_lo, rb=rb,
        w1=w1, w3=w3, w2=w2,
        ln3=ln3[None, :], wg=w_ple_gate.astype(BF16), wp=w_ple_proj.astype(BF16),
    )


def _tail_dense(branches, p, w):
    x, rnn, attn, proj = branches
    t = x.shape[0]
    x1 = _merge(x, rnn, attn, proj, w["wr"], w["wa"], w["wo"], tm=min(t, 256))
    x2 = _moe(x1, w["ln2"], w["rw_hi"], w["rw_lo"], w["rb"], w["w1"], w["w3"], w["w2"], tm=min(t, 512))
    return _ple(x2, p, w["ln3"], w["wg"], w["wp"], tm=min(t, 512))


def _tail(branches_a, p_a, branches_b, p_b, w):
    ta, tb = branches_a[0].shape[0], branches_b[0].shape[0]
    if 2 * (ta + tb) < N_EXPERTS * MOE_ROW_TILE:
        return _tail_dense(branches_a, p_a, w), _tail_dense(branches_b, p_b, w)
    tm = math.gcd(256, ta, tb)
    mw = (w["wr"], w["wa"], w["wo"])
    x1 = _merge(*branches_a, *mw, tm=tm, out_rows=ta + tb)
    x1 = _merge(*branches_b, *mw, tm=tm, out_rows=ta + tb, row0=ta, into=x1)
    return _sparse_moe_ple(x1, jnp.concatenate([p_a, p_b], axis=0), w, tm=tm, t_first=ta)


def _prompt_layer(x, w):
    bp, tp, _ = x.shape
    xt = x.reshape(bp * tp, D_MODEL)
    cs, sn = _rope_tables(jnp.arange(tp, dtype=jnp.int32))
    proj, small, kvb, kib = _inproj(xt, w["ln1"], w["w_t"], w["w_small_t"], w["colctl"],
                                    cs, sn, w["bd"], tm=min(tp, 1024))
    conv0 = jnp.zeros((bp, 8, D_RNN), F32)
    h0 = jnp.zeros((bp, 1, D_RNN), F32)
    rnn, h_last = _rglru(proj, conv0, h0, w["cw"], w["cb"], w["wa_bd"], w["ba"], w["wx_bd"], w["bx"],
                         w["lam"], n_seq=bp, tt=min(tp, 256))
    attn = _prompt_attention(proj, small, kvb, kib, n_batch=bp, seq=tp)
    o_k = 2 * D_RNN + ATTN_W
    k = proj[:, o_k:o_k + KV_W].reshape(bp, tp, N_KV_HEADS, HEAD_DIM)
    v = proj[:, o_k + KV_W:o_k + 2 * KV_W].reshape(bp, tp, N_KV_HEADS, HEAD_DIM)
    ki = small[:, :IDX_DIM].reshape(bp, tp, IDX_DIM)
    conv_new = proj.reshape(bp, tp, N_MAIN)[:, tp - (CONV_W - 1):, :D_RNN]
    return (xt, rnn, attn, proj), (k, v, ki, conv_new, h_last.reshape(bp, D_RNN))


def _sample_layer(x, cache_k, cache_v, cache_kidx, state_conv, state_h, page_table, w):
    bs, ts, _ = x.shape
    n_pages = page_table.shape[1]
    past = n_pages * PAGE_SIZE
    xt = x.reshape(bs * ts, D_MODEL)
    cs, sn = _rope_tables(past + jnp.tile(jnp.arange(ts, dtype=jnp.int32), bs))
    proj, small, _, _ = _inproj(xt, w["ln1"], w["w_t"], w["w_small_t"], w["colctl"], cs, sn,
                                w["bd"], tm=bs * ts)
    conv0 = jnp.concatenate([jnp.zeros((bs, 8 - (CONV_W - 1), D_RNN), F32), state_conv], axis=1)
    rnn, h_last = _rglru(proj, conv0, state_h[:, None, :], w["cw"], w["cb"], w["wa_bd"], w["ba"],
                         w["wx_bd"], w["bx"], w["lam"], n_seq=bs, tt=ts)
    o_q = 2 * D_RNN
    o_qi = o_q + ATTN_W + 2 * KV_W
    qi = proj[:, o_qi:o_qi + N_IDX_HEADS * IDX_DIM].reshape(bs, ts, N_IDX_HEADS, IDX_DIM)
    qst = jnp.transpose(qi, (0, 2, 1, 3)).reshape(bs, N_IDX_HEADS * ts, IDX_DIM).astype(BF16)
    wi = small[:, IDX_DIM:IDX_DIM + N_IDX_HEADS].reshape(bs, ts, N_IDX_HEADS)
    wcol = jnp.transpose(wi, (0, 2, 1)).reshape(bs, N_IDX_HEADS * ts, 1)
    n_pool = cache_k.shape[0]
    kidx_t = jnp.transpose(cache_kidx, (0, 2, 1))
    k_t = jnp.transpose(cache_k, (0, 2, 3, 1)).reshape(n_pool, KV_W, PAGE_SIZE)
    v_t = jnp.transpose(cache_v, (0, 2, 3, 1)).reshape(n_pool, KV_W, PAGE_SIZE)
    scores = _sample_select(page_table, qst, wcol, small, kidx_t, n_new=ts)
    bias = _select_bias(scores.reshape(bs * ts, -1), n_new=ts, past=past).reshape(scores.shape)
    q = proj[:, o_q:o_q + ATTN_W].reshape(bs, ts, N_KV_HEADS, N_HEADS // N_KV_HEADS, HEAD_DIM)
    eye = jnp.eye(N_KV_HEADS, dtype=F32)
    qbd = jnp.einsum("btgjd,gk->bgjtkd", q, eye).reshape(bs, N_HEADS * ts, KV_W).astype(BF16)
    att = _sample_attend(page_table, qbd, bias, proj, k_t, v_t, n_new=ts)
    att = att.reshape(bs, N_KV_HEADS, N_HEADS // N_KV_HEADS, ts, N_KV_HEADS, HEAD_DIM)
    att = jnp.stack([att[:, g, :, :, g, :] for g in range(N_KV_HEADS)], axis=1)
    attn = jnp.transpose(att, (0, 3, 1, 2, 4)).reshape(bs * ts, ATTN_W)
    o_k = o_q + ATTN_W
    k = proj[:, o_k:o_k + KV_W].reshape(bs, ts, N_KV_HEADS, HEAD_DIM)
    v = proj[:, o_k + KV_W:o_k + 2 * KV_W].reshape(bs, ts, N_KV_HEADS, HEAD_DIM)
    ki = small[:, :IDX_DIM].reshape(bs, ts, IDX_DIM)
    conv_new = proj.reshape(bs, ts, N_MAIN)[:, ts - (CONV_W - 1):, :D_RNN]
    return (xt, rnn, attn, proj), (k, v, ki, conv_new, h_last.reshape(bs, D_RNN))


def kernel(x_prompt, x_sample, p_prompt, p_sample, cache_k, cache_v, cache_kidx, state_conv, state_h,
           page_table, ln1, w_in, q_norm, k_norm, conv_w, conv_b, w_a, b_a, w_x, b_x, lam, w_br_rnn,
           w_br_attn, w_out, ln2, w_rg, b_rg, w_re, b_re, w1, w3, w2, ln3, w_ple_gate, w_ple_proj):
    weights = (ln1, w_in, q_norm, k_norm, conv_w, conv_b, w_a, b_a, w_x, b_x, lam, w_br_rnn, w_br_attn,
               w_out, ln2, w_rg, b_rg, w_re, b_re, w1, w3, w2, ln3, w_ple_gate, w_ple_proj)
    depth = ln1.shape[0]
    yp, ys = x_prompt, x_sample
    st_p, st_s = [], []
    for i in range(depth):
        w = _layer_weights(*[wt[i] for wt in weights])
        br_p, sp = _prompt_layer(yp, w)
        br_s, ss = _sample_layer(ys, cache_k[i], cache_v[i], cache_kidx[i], state_conv[i], state_h[i],
                                 page_table, w)
        out_p, out_s = _tail(br_p, p_prompt[i].reshape(-1, PLE_DIM), br_s,
                             p_sample[i].reshape(-1, PLE_DIM), w)
        yp, ys = out_p.reshape(yp.shape), out_s.reshape(ys.shape)
        st_p.append(sp)
        st_s.append(ss)
    stack = lambda sts, j: jnp.stack([s[j] for s in sts])
    return (yp, ys, stack(st_p, 0), stack(st_p, 1), stack(st_p, 2), stack(st_p, 3), stack(st_p, 4),
            stack(st_s, 0), stack(st_s, 1), stack(st_s, 2), stack(st_s, 3), stack(st_s, 4))
```

```python
import functools
import math

import jax
import jax.numpy as jnp
import numpy as np
from jax import lax
from jax.experimental import pallas as pl
from jax.experimental.pallas import tpu as pltpu

F32 = jnp.float32
BF16 = jnp.bfloat16

D_MODEL = 2048
HEAD_DIM = 64
N_HEADS = 16
N_KV_HEADS = 4
ATTN_W = N_HEADS * HEAD_DIM
KV_W = N_KV_HEADS * HEAD_DIM
N_IDX_HEADS = 8
IDX_DIM = 64
TOPK_MAX = 256
ROPE_THETA = 10000.0
D_RNN = 1024
N_RNN_BLOCKS = 16
RNN_BLOCK = 64
CONV_W = 4
LRU_C = 8.0
N_GROUPS = 4
EXPERTS_PER_GROUP = 8
N_EXPERTS = 32
D_EXPERT = 256
PLE_DIM = 256
PAGE_SIZE = 128
EPS = 1e-6

LANES = 128
N_MAIN = 8192
COL_RAW_END = 2 * D_RNN
COL_NR_END = 4096
IDX_SCALE = (IDX_DIM ** -0.5) * (N_IDX_HEADS ** -0.5)
QK_SCALE = HEAD_DIM ** -0.5
F32_MIN = float(np.finfo(np.float32).min)
INT_MIN = -2147483648
KEY_NEG_INF = INT_MIN + 0x7FFFFF
VMEM_LIMIT = 56 * 1024 * 1024


def _dot(a, b):
    return jnp.dot(a, b, preferred_element_type=F32)


def _dot_nt(a, b):
    return lax.dot_general(a, b, (((1,), (1,)), ((), ())), preferred_element_type=F32)


def _sigmoid(x):
    return 1.0 / (1.0 + jnp.exp(-x))


def _rms(x, g):
    return x * lax.rsqrt(jnp.mean(x * x, axis=-1, keepdims=True) + EPS) * g


def _split_bf16(x):
    hi = x.astype(BF16)
    lo = (x - hi.astype(F32)).astype(BF16)
    return hi, lo


def _rope_chunks(y, c, s):
    lane = lax.broadcasted_iota(jnp.int32, (1, LANES), 1)
    first_half = (lane % HEAD_DIM) < (HEAD_DIM // 2)
    outs = []
    for k in range(y.shape[1] // LANES):
        yc = y[:, k * LANES:(k + 1) * LANES]
        partner = jnp.where(first_half, pltpu.roll(yc, LANES - HEAD_DIM // 2, 1),
                            pltpu.roll(yc, HEAD_DIM // 2, 1))
        outs.append(yc * c + partner * s)
    return outs[0] if len(outs) == 1 else jnp.concatenate(outs, axis=1)


def _inproj_kernel(x_ref, ln_ref, wlo_ref, whi_ref, ws_ref, ctl_ref, cs_ref, sn_ref, bd_ref,
                   o_ref, os_ref, kvb_ref, kib_ref, h_ref, *, tn):
    j = pl.program_id(1)

    @pl.when(j == 0)
    def _():
        hb = _rms(x_ref[...], ln_ref[...]).astype(BF16)
        h_ref[...] = hb
        ys = _dot_nt(hb, ws_ref[...].astype(BF16))
        lane = lax.broadcasted_iota(jnp.int32, (1, LANES), 1)
        roped = _rope_chunks(ys, cs_ref[...], sn_ref[...])
        os_ref[...] = jnp.where(lane < IDX_DIM, roped, ys)
        kib_ref[...] = jnp.where(lane < IDX_DIM, roped, 0.0).astype(BF16)

    @pl.when(j < COL_NR_END // tn)
    def _():
        o_ref[...] = _dot_nt(h_ref[...], wlo_ref[...].astype(BF16))

    @pl.when(j >= COL_NR_END // tn)
    def _():
        o_ref[...] = _dot_nt(h_ref[...], whi_ref[...].astype(BF16))

    j_idx = (COL_NR_END - N_IDX_HEADS * IDX_DIM) // tn

    @pl.when((j >= COL_RAW_END // tn) & (j < j_idx))
    def _():
        y = o_ref[...]
        ctl = ctl_ref[...]
        gain, norm_on, rope_on, post = ctl[0:1], ctl[1:2], ctl[2:3], ctl[3:4]
        hi, lo = _split_bf16(y * y)
        ss = _dot(hi, bd_ref[...]) + _dot(lo, bd_ref[...])
        yn = jnp.where(norm_on > 0.0, y * lax.rsqrt(ss * (1.0 / HEAD_DIM) + EPS) * gain, y)
        yr = jnp.where(rope_on > 0.0, _rope_chunks(yn, cs_ref[...], sn_ref[...]), yn)
        o_ref[...] = yr * post

    @pl.when(j == j_idx)
    def _():
        o_ref[...] = _rope_chunks(o_ref[...], cs_ref[...], sn_ref[...])

    @pl.when(j == (COL_RAW_END + ATTN_W) // tn)
    def _():
        kvb_ref[...] = o_ref[...].astype(BF16)


def _inproj(x, ln1, w_t, w_small_t, colctl, cs, sn, bd, *, tm, tn=512):
    t = x.shape[0]
    assert tn == 2 * KV_W
    grid = (t // tm, N_MAIN // tn)
    n_lo = COL_NR_END // tn
    n_rope = cs.shape[0] // tm
    gate_row0 = COL_NR_END + IDX_DIM + N_IDX_HEADS
    return pl.pallas_call(
        functools.partial(_inproj_kernel, tn=tn),
        grid=grid,
        in_specs=[
            pl.BlockSpec((tm, D_MODEL), lambda i, j: (i, 0)),
            pl.BlockSpec((1, D_MODEL), lambda i, j: (0, 0)),
            pl.BlockSpec((tn, D_MODEL), lambda i, j: (jnp.minimum(j, n_lo - 1), 0)),
            pl.BlockSpec((pl.Element(tn), pl.Element(D_MODEL)),
                         lambda i, j: (pl.multiple_of(gate_row0 + jnp.maximum(j - n_lo, 0) * tn, 8), 0)),
            pl.BlockSpec((LANES, D_MODEL), lambda i, j: (0, 0)),
            pl.BlockSpec((8, tn), lambda i, j: (0, j)),
            pl.BlockSpec((tm, LANES), lambda i, j: (i % n_rope, 0)),
            pl.BlockSpec((tm, LANES), lambda i, j: (i % n_rope, 0)),
            pl.BlockSpec((tn, tn), lambda i, j: (0, 0)),
        ],
        out_specs=[
            pl.BlockSpec((tm, tn), lambda i, j: (i, j)),
            pl.BlockSpec((tm, LANES), lambda i, j: (i, 0)),
            pl.BlockSpec((tm, 2 * KV_W), lambda i, j: (i, 0)),
            pl.BlockSpec((tm, LANES), lambda i, j: (i, 0)),
        ],
        out_shape=[
            jax.ShapeDtypeStruct((t, N_MAIN), F32),
            jax.ShapeDtypeStruct((t, LANES), F32),
            jax.ShapeDtypeStruct((t, 2 * KV_W), BF16),
            jax.ShapeDtypeStruct((t, LANES), BF16),
        ],
        scratch_shapes=[pltpu.VMEM((tm, D_MODEL), BF16)],
        compiler_params=pltpu.CompilerParams(
            dimension_semantics=("arbitrary", "arbitrary"), vmem_limit_bytes=VMEM_LIMIT),
        name="inproj",
    )(x, ln1, w_t, w_t, w_small_t, colctl, cs, sn, bd)


def _rglru_kernel(x_ref, g_ref, c0_ref, h0_ref, cw_ref, cb_ref, wa_ref, ba_ref, wx_ref, bx_ref,
                  lam_ref, o_ref, hl_ref, xs_ref, a_ref, b_ref, hc_ref):
    t = pl.program_id(1)
    tt = x_ref.shape[0]

    @pl.when(t == 0)
    def _():
        xs_ref[0:8, :] = c0_ref[...]
        hc_ref[...] = h0_ref[...]

    xs_ref[8:8 + tt, :] = x_ref[...]
    cw = cw_ref[...]
    taps = (xs_ref[5:5 + tt, :] * cw[0:1] + xs_ref[6:6 + tt, :] * cw[1:2]
            + xs_ref[7:7 + tt, :] * cw[2:3] + xs_ref[8:8 + tt, :] * cw[3:4])
    xc = cb_ref[...] + taps
    xs_ref[0:8, :] = xs_ref[tt:tt + 8, :]

    xcb = xc.astype(BF16)
    ra, ri = [], []
    for c in range(wa_ref.shape[0]):
        blk = xcb[:, c * 256:(c + 1) * 256]
        ra.append(_dot(blk, wa_ref[c]))
        ri.append(_dot(blk, wx_ref[c]))
    r = _sigmoid(jnp.concatenate(ra, axis=1) + ba_ref[...])
    ig = _sigmoid(jnp.concatenate(ri, axis=1) + bx_ref[...])
    nlam = -lam_ref[...]
    softplus = jnp.maximum(nlam, 0.0) + jnp.log1p(jnp.exp(-jnp.abs(nlam)))
    log_a = (-LRU_C) * r * softplus
    a = jnp.exp(log_a)
    u = jnp.sqrt(jnp.tanh(-log_a) * (a * a + 1.0)) * (ig * xc)

    n8 = tt // 8
    a3 = a.reshape(n8, 8, D_RNN)
    b3 = u.reshape(n8, 8, D_RNN)
    sub = lax.broadcasted_iota(jnp.int32, (1, 8, 1), 1)
    for s in (1, 2, 4):
        a_prev = pltpu.roll(a3, s, 1)
        b_prev = pltpu.roll(b3, s, 1)
        m = sub >= s
        b3 = jnp.where(m, a3 * b_prev + b3, b3)
        a3 = jnp.where(m, a3 * a_prev, a3)
    a_ref[...] = a3.reshape(tt, D_RNN)
    b_ref[...] = b3.reshape(tt, D_RNN)

    def chain(k, carry):
        i0 = pl.multiple_of(k * 8, 8)
        h8 = a_ref[pl.ds(i0, 8), :] * carry + b_ref[pl.ds(i0, 8), :]
        b_ref[pl.ds(i0, 8), :] = h8
        return h8[7:8, :]

    carry = lax.fori_loop(0, n8, chain, hc_ref[...])
    hc_ref[...] = carry
    g = g_ref[...]
    gelu = 0.5 * g * (1.0 + jnp.tanh(0.7978845608028654 * (g + 0.044715 * (g * g * g))))
    o_ref[...] = b_ref[...] * gelu

    @pl.when(t == pl.num_programs(1) - 1)
    def _():
        hl_ref[...] = carry


def _rglru(proj, conv0, h0, cw, cb, wa_bd, ba, wx_bd, bx, lam, *, n_seq, tt):
    t_total = proj.shape[0]
    nt = t_total // (n_seq * tt)
    full = lambda shape: pl.BlockSpec(shape, lambda b, t: (0,) * len(shape))
    return pl.pallas_call(
        _rglru_kernel,
        grid=(n_seq, nt),
        in_specs=[
            pl.BlockSpec((tt, D_RNN), lambda b, t: (b * nt + t, 0)),
            pl.BlockSpec((tt, D_RNN), lambda b, t: (b * nt + t, 1)),
            pl.BlockSpec((None, 8, D_RNN), lambda b, t: (b, 0, 0)),
            pl.BlockSpec((None, 1, D_RNN), lambda b, t: (b, 0, 0)),
            full((CONV_W, D_RNN)), full((1, D_RNN)),
            full(wa_bd.shape), full((1, D_RNN)),
            full(wx_bd.shape), full((1, D_RNN)),
            full((1, D_RNN)),
        ],
        out_specs=[
            pl.BlockSpec((tt, D_RNN), lambda b, t: (b * nt + t, 0)),
            pl.BlockSpec((None, 1, D_RNN), lambda b, t: (b, 0, 0)),
        ],
        out_shape=[
            jax.ShapeDtypeStruct((t_total, D_RNN), F32),
            jax.ShapeDtypeStruct((n_seq, 1, D_RNN), F32),
        ],
        scratch_shapes=[
            pltpu.VMEM((tt + 8, D_RNN), F32),
            pltpu.VMEM((tt, D_RNN), F32),
            pltpu.VMEM((tt, D_RNN), F32),
            pltpu.VMEM((1, D_RNN), F32),
        ],
        compiler_params=pltpu.CompilerParams(
            dimension_semantics=("arbitrary", "arbitrary"), vmem_limit_bytes=VMEM_LIMIT),
        name="rglru",
    )(proj, proj, conv0, h0, cw, cb, wa_bd, ba, wx_bd, bx, lam)


def _select_topk(s, kk):
    rows, n = s.shape
    kkf = float(kk)

    def key_to_f32(w):
        k = w ^ INT_MIN
        bits = jnp.where(k >= 0, k, k ^ 0x7FFFFFFF)
        return k, lax.bitcast_convert_type(bits, F32)

    def vbody(it, w):
        cand_w = w | jnp.left_shift(jnp.int32(1), 31 - it)
        cand_k, cand_f = key_to_f32(cand_w)
        cnt = jnp.sum(jnp.where(s >= cand_f, 1.0, 0.0), axis=1, keepdims=True)
        ok = (cnt >= kkf) | (cand_k < KEY_NEG_INF)
        return jnp.where(ok, cand_w, w)

    w = lax.fori_loop(0, 32, vbody, jnp.zeros((rows, 1), jnp.int32))
    _, thr = key_to_f32(w)
    gt = s > thr
    eq = s == thr
    need = kkf - jnp.sum(jnp.where(gt, 1.0, 0.0), axis=1, keepdims=True)
    col = lax.broadcasted_iota(jnp.int32, (1, n), 1)
    nbits = int(n).bit_length()

    def jbody(it, jmax):
        cand = jmax | jnp.left_shift(jnp.int32(1), nbits - 1 - it)
        cnt = jnp.sum(jnp.where(eq & (col < cand), 1.0, 0.0), axis=1, keepdims=True)
        return jnp.where(cnt <= need, cand, jmax)

    n_ge = jnp.sum(jnp.where(s >= thr, 1.0, 0.0), axis=1, keepdims=True)
    jmax = lax.cond(
        jnp.max(n_ge) > kkf,
        lambda: lax.fori_loop(0, nbits, jbody, jnp.zeros((rows, 1), jnp.int32)),
        lambda: jnp.full((rows, 1), (1 << nbits) - 1, jnp.int32))
    return gt | (eq & (col < jmax))


def _pattn_kernel(q_ref, qi_ref, sm_ref, k_ref, v_ref, ki_ref, o_ref, s_ref, *, i0, n_keys, kc, topk):
    i = pl.program_id(0)
    n_batch, tq = q_ref.shape[0], q_ref.shape[1]
    lane = lax.broadcasted_iota(jnp.int32, (1, LANES), 1)
    qpos = (i0 + i) * tq + lax.broadcasted_iota(jnp.int32, (tq, 1), 0)

    def score(b, carry):
        sm = sm_ref[b]
        qi = qi_ref[b]
        qrows, wrows = [], []
        for h in range(N_IDX_HEADS):
            blk = qi[:, (h // 2) * LANES:(h // 2 + 1) * LANES]
            if h % 2 == 1:
                blk = pltpu.roll(blk, IDX_DIM, 1)
            qrows.append(jnp.where(lane < IDX_DIM, blk, 0.0))
            wrows.append(sm[:, IDX_DIM + h:IDX_DIM + h + 1])
        qst = jnp.concatenate(qrows, axis=0).astype(BF16)
        wst = jnp.concatenate(wrows, axis=0) * IDX_SCALE
        r0 = pl.multiple_of(b * tq, tq)
        for c in range(n_keys // kc):
            s = jnp.maximum(_dot_nt(qst, ki_ref[b, c * kc:(c + 1) * kc, :]), 0.0) * wst
            sc = s[0:tq]
            for h in range(1, N_IDX_HEADS):
                sc = sc + s[h * tq:(h + 1) * tq]
            col = c * kc + lax.broadcasted_iota(jnp.int32, (1, kc), 1)
            s_ref[pl.ds(r0, tq), c * kc:(c + 1) * kc] = jnp.where(col <= qpos, sc, F32_MIN)
        return carry

    lax.fori_loop(0, n_batch, score, 0)

    sel = _select_topk(s_ref[...], topk)
    colf = lax.broadcasted_iota(jnp.int32, (1, n_keys), 1)
    qpos_all = jnp.concatenate([qpos] * n_batch, axis=0)
    s_ref[...] = jnp.where(sel & (colf <= qpos_all), 0.0, -jnp.inf)

    def attend(b, carry):
        _attend_tile(q_ref[b], s_ref[pl.ds(pl.multiple_of(b * tq, tq), tq), :],
                     k_ref[b, 0:n_keys, :], v_ref[b, 0:n_keys, :], o_ref.at[b])
        return carry

    lax.fori_loop(0, n_batch, attend, 0)


def _attend_tile(q, bias, kb, vb, o_ref):
    tq = q.shape[0]
    lane = lax.broadcasted_iota(jnp.int32, (1, LANES), 1)
    bias4 = jnp.concatenate([bias] * 4, axis=0)
    outs = [None] * N_HEADS
    for g in range(N_KV_HEADS):
        lo = (g % 2) * HEAD_DIM
        keep = (lane >= lo) & (lane < lo + HEAD_DIM)
        rows = []
        for j in range(4):
            h = 4 * g + j
            blk = q[:, (h // 2) * LANES:(h // 2 + 1) * LANES]
            if h % 2 != g % 2:
                blk = pltpu.roll(blk, HEAD_DIM, 1)
            piece = jnp.where(keep, blk, 0.0)
            zero = jnp.zeros_like(piece)
            rows.append(jnp.concatenate([piece, zero] if g < 2 else [zero, piece], axis=1))
        qbd = jnp.concatenate(rows, axis=0).astype(BF16)
        logits = _dot_nt(qbd, kb) + bias4
        m = jnp.max(logits, axis=1, keepdims=True)
        p = jnp.exp(logits - m)
        denom = jnp.sum(p, axis=1, keepdims=True)
        acc = _dot(p.astype(BF16), vb) / denom
        for j in range(4):
            outs[4 * g + j] = acc[j * tq:(j + 1) * tq, (g // 2) * LANES:(g // 2 + 1) * LANES]
    for c in range(N_HEADS // 2):
        g = (2 * c) // 4
        even, odd = outs[2 * c], outs[2 * c + 1]
        if g % 2 == 1:
            even = pltpu.roll(even, HEAD_DIM, 1)
        else:
            odd = pltpu.roll(odd, HEAD_DIM, 1)
        o_ref[:, c * LANES:(c + 1) * LANES] = jnp.where(lane < HEAD_DIM, even, odd).astype(BF16)


def _prompt_attention_part(proj3, small3, kvb, kib, *, i0, n_tiles, tq):
    n_batch, seq, _ = proj3.shape
    n_keys = (i0 + n_tiles) * tq
    kc = next(c for c in (512, 256, 128) if n_keys % c == 0)
    return pl.pallas_call(
        functools.partial(_pattn_kernel, i0=i0, n_keys=n_keys, kc=kc, topk=min(TOPK_MAX, seq // 4)),
        grid=(n_tiles,),
        in_specs=[
            pl.BlockSpec((n_batch, tq, ATTN_W), lambda i: (0, i0 + i, 2)),
            pl.BlockSpec((n_batch, tq, N_IDX_HEADS * IDX_DIM), lambda i: (0, i0 + i, 7)),
            pl.BlockSpec((n_batch, tq, LANES), lambda i: (0, i0 + i, 0)),
            pl.BlockSpec((n_batch, seq, KV_W), lambda i: (0, 0, 0)),
            pl.BlockSpec((n_batch, seq, KV_W), lambda i: (0, 0, 1)),
            pl.BlockSpec((n_batch, seq, LANES), lambda i: (0, 0, 0)),
        ],
        out_specs=pl.BlockSpec((n_batch, tq, ATTN_W), lambda i: (0, i, 0)),
        out_shape=jax.ShapeDtypeStruct((n_batch, n_tiles * tq, ATTN_W), BF16),
        scratch_shapes=[pltpu.VMEM((n_batch * tq, n_keys), F32)],
        compiler_params=pltpu.CompilerParams(
            dimension_semantics=("arbitrary",), vmem_limit_bytes=VMEM_LIMIT),
        name=f"prompt_attention_{i0}",
    )(proj3, proj3, small3, kvb, kvb, kib)


def _prompt_attention(proj, small, kvb, kib, *, n_batch, seq, tq=128, tiles_per_part=2):
    nq = seq // tq
    proj3 = proj.reshape(n_batch, seq, N_MAIN)
    small3 = small.reshape(n_batch, seq, LANES)
    kvb3 = kvb.reshape(n_batch, seq, 2 * KV_W)
    kib3 = kib.reshape(n_batch, seq, LANES)
    parts = [
        _prompt_attention_part(proj3, small3, kvb3, kib3, i0=i0,
                               n_tiles=min(tiles_per_part, nq - i0), tq=tq)
        for i0 in range(0, nq, tiles_per_part)
    ]
    return jnp.concatenate(parts, axis=1).reshape(n_batch * seq, ATTN_W)


SELECT_PAGES_PER_STEP = 64
ATTEND_PAGES_PER_STEP = 64
SUB_PAGES = 8


def _sidx_kernel(pt_ref, qst_ref, w_ref, sm_ref, kidx_hbm, o_ref, s_ref, kbuf_ref, ksem,
                 *, n_chunks, n_new, ps):
    c = pl.program_id(1)
    step = pl.program_id(0) * n_chunks + c
    n_steps = pl.num_programs(0) * n_chunks

    def fetch(s, buf):
        sb = s // n_chunks
        sc = s % n_chunks

        def start(r, carry):
            pltpu.make_async_copy(kidx_hbm.at[pt_ref[sb, sc * ps + r]], kbuf_ref.at[buf, r],
                                  ksem.at[buf]).start()
            return carry

        lax.fori_loop(0, ps, start, 0)

    @pl.when(step == 0)
    def _():
        fetch(0, 0)

    @pl.when(step + 1 < n_steps)
    def _():
        fetch(step + 1, (step + 1) % 2)

    cur = step % 2
    pltpu.make_async_copy(kidx_hbm.at[pl.ds(0, ps)], kbuf_ref.at[cur], ksem.at[cur]).wait()
    pages = [kbuf_ref.at[cur, r] for r in range(ps)]
    qst = qst_ref[...]
    w = w_ref[...] * IDX_SCALE

    def head_sum(s):
        s = jnp.maximum(s, 0.0) * w
        out = s[0:n_new]
        for h in range(1, N_IDX_HEADS):
            out = out + s[h * n_new:(h + 1) * n_new]
        return out

    for r0 in range(0, ps, SUB_PAGES):
        kt = jnp.concatenate([pages[r0 + r][...] for r in range(SUB_PAGES)], axis=1).astype(BF16)
        part = head_sum(_dot(qst, kt))
        for r in range(SUB_PAGES):
            s_ref[ps * c + r0 + r] = part[:, r * PAGE_SIZE:(r + 1) * PAGE_SIZE]

    @pl.when(c == n_chunks - 1)
    def _():
        n_past_blocks = n_chunks * ps
        past = n_past_blocks * PAGE_SIZE
        k_new = sm_ref[...][:, 0:IDX_DIM]
        kp = jnp.concatenate([k_new, jnp.zeros((PAGE_SIZE - n_new, IDX_DIM), F32)], axis=0)
        lane = lax.broadcasted_iota(jnp.int32, (n_new, LANES), 1)
        trow = lax.broadcasted_iota(jnp.int32, (n_new, LANES), 0)
        s_new = head_sum(_dot_nt(qst, kp.astype(BF16)))
        s_ref[n_past_blocks] = jnp.where(lane < n_new, jnp.where(lane <= trow, s_new, F32_MIN), -jnp.inf)
        o_ref[...] = jnp.concatenate([s_ref[k] for k in range(n_past_blocks + 1)], axis=1)


def _select_bias_kernel(s_ref, o_ref, *, n_new, past, topk):
    s = s_ref[...]
    rows, n_all = s.shape
    sel = _select_topk(s, topk)
    col = lax.broadcasted_iota(jnp.int32, (1, n_all), 1)
    tq = lax.broadcasted_iota(jnp.int32, (rows, 1), 0) % n_new
    o_ref[...] = jnp.where(sel & ((col - past) <= tq), 0.0, -jnp.inf)


def _select_bias(scores, *, n_new, past, rows_per_step=128):
    rows, n_all = scores.shape
    rows_per_step = min(rows_per_step, rows)
    return pl.pallas_call(
        functools.partial(_select_bias_kernel, n_new=n_new, past=past,
                          topk=min(TOPK_MAX, (past + n_new) // 4)),
        grid=(rows // rows_per_step,),
        in_specs=[pl.BlockSpec((rows_per_step, n_all), lambda i: (i, 0))],
        out_specs=pl.BlockSpec((rows_per_step, n_all), lambda i: (i, 0)),
        out_shape=jax.ShapeDtypeStruct((rows, n_all), F32),
        compiler_params=pltpu.CompilerParams(
            dimension_semantics=("arbitrary",), vmem_limit_bytes=VMEM_LIMIT),
        name="sample_select_bias",
    )(scores)


def _sample_select(page_table, qst, wcol, small, cache_kidx_t, *, n_new):
    n_seq, n_pages = page_table.shape
    ps = min(SELECT_PAGES_PER_STEP, n_pages)
    n_chunks = n_pages // ps
    n_all = n_pages * PAGE_SIZE + LANES
    rows = N_IDX_HEADS * n_new
    grid_spec = pltpu.PrefetchScalarGridSpec(
        num_scalar_prefetch=1,
        grid=(n_seq, n_chunks),
        in_specs=[
            pl.BlockSpec((None, rows, IDX_DIM), lambda b, c, pt: (b, 0, 0)),
            pl.BlockSpec((None, rows, 1), lambda b, c, pt: (b, 0, 0)),
            pl.BlockSpec((n_new, LANES), lambda b, c, pt: (b, 0)),
            pl.BlockSpec(memory_space=pl.ANY),
        ],
        out_specs=pl.BlockSpec((None, n_new, n_all), lambda b, c, pt: (b, 0, 0)),
        scratch_shapes=[
            pltpu.VMEM((n_pages + 1, n_new, LANES), F32),
            pltpu.VMEM((2, ps, IDX_DIM, PAGE_SIZE), F32),
            pltpu.SemaphoreType.DMA((2,)),
        ],
    )
    return pl.pallas_call(
        functools.partial(_sidx_kernel, n_chunks=n_chunks, n_new=n_new, ps=ps),
        grid_spec=grid_spec,
        out_shape=jax.ShapeDtypeStruct((n_seq, n_new, n_all), F32),
        compiler_params=pltpu.CompilerParams(
            dimension_semantics=("arbitrary", "arbitrary"), vmem_limit_bytes=VMEM_LIMIT),
        name="sample_select",
    )(page_table, qst, wcol, small, cache_kidx_t)


def _sattn_kernel(pt_ref, q_ref, bias_ref, biasn_ref, kn_ref, vn_ref, k_hbm, v_hbm, o_ref,
                  m_ref, l_ref, acc_ref, kbuf_ref, vbuf_ref, ksem, vsem, *, n_chunks, n_new, ps):
    b = pl.program_id(0)
    c = pl.program_id(1)
    rows = q_ref.shape[0]
    reps = rows // n_new
    step = b * n_chunks + c
    n_steps = pl.num_programs(0) * n_chunks

    def fetch(s, buf):
        sb = s // n_chunks
        sc = s % n_chunks

        def start(r, carry):
            page = pt_ref[sb, sc * ps + r]
            pltpu.make_async_copy(k_hbm.at[page], kbuf_ref.at[buf, r], ksem.at[buf]).start()
            pltpu.make_async_copy(v_hbm.at[page], vbuf_ref.at[buf, r], vsem.at[buf]).start()
            return carry

        lax.fori_loop(0, ps, start, 0)

    @pl.when(step == 0)
    def _():
        fetch(0, 0)

    @pl.when(step + 1 < n_steps)
    def _():
        fetch(step + 1, (step + 1) % 2)

    cur = step % 2
    pltpu.make_async_copy(k_hbm.at[pl.ds(0, ps)], kbuf_ref.at[cur], ksem.at[cur]).wait()
    pltpu.make_async_copy(v_hbm.at[pl.ds(0, ps)], vbuf_ref.at[cur], vsem.at[cur]).wait()
    kpages = [kbuf_ref.at[cur, r] for r in range(ps)]
    vpages = [vbuf_ref.at[cur, r] for r in range(ps)]

    @pl.when(c == 0)
    def _():
        m_ref[...] = jnp.full(m_ref.shape, -1e30, F32)
        l_ref[...] = jnp.zeros(l_ref.shape, F32)
        acc_ref[...] = jnp.zeros(acc_ref.shape, F32)

    def update(logits, bias, pv):
        logits = logits + jnp.concatenate([bias] * reps, axis=0)
        m_old = m_ref[...]
        m_new = jnp.maximum(m_old, jnp.max(logits, axis=1, keepdims=True))
        alpha = jnp.exp(m_old - m_new)
        p = jnp.exp(logits - m_new)
        l_ref[...] = alpha * l_ref[...] + jnp.sum(p, axis=1, keepdims=True)
        acc_ref[...] = alpha * acc_ref[...] + pv(p.astype(BF16))
        m_ref[...] = m_new

    sub_keys = SUB_PAGES * PAGE_SIZE
    logits, vts = [], []
    for r0 in range(0, ps, SUB_PAGES):
        kt = jnp.concatenate([kpages[r0 + r][...] for r in range(SUB_PAGES)], axis=1).astype(BF16)
        vts.append(jnp.concatenate([vpages[r0 + r][...] for r in range(SUB_PAGES)], axis=1).astype(BF16))
        logits.append(_dot(q_ref[...], kt))

    def pv(p):
        acc = _dot_nt(p[:, 0:sub_keys], vts[0])
        for n in range(1, len(vts)):
            acc = acc + _dot_nt(p[:, n * sub_keys:(n + 1) * sub_keys], vts[n])
        return acc

    update(jnp.concatenate(logits, axis=1), bias_ref[...], pv)

    @pl.when(c == n_chunks - 1)
    def _():
        pad = jnp.zeros((PAGE_SIZE - n_new, KV_W), F32)
        kn = jnp.concatenate([kn_ref[...], pad], axis=0).astype(BF16)
        vn = jnp.concatenate([vn_ref[...], pad], axis=0).astype(BF16)
        update(_dot_nt(q_ref[...], kn), biasn_ref[...], lambda p: _dot(p, vn))
        o_ref[...] = acc_ref[...] / l_ref[...]


def _sample_attend(page_table, qbd, bias, proj, cache_k_t, cache_v_t, *, n_new):
    n_seq, n_pages = page_table.shape
    ps = min(ATTEND_PAGES_PER_STEP, n_pages)
    n_chunks = n_pages // ps
    rows = qbd.shape[1]
    chunk_keys = ps * PAGE_SIZE
    grid_spec = pltpu.PrefetchScalarGridSpec(
        num_scalar_prefetch=1,
        grid=(n_seq, n_chunks),
        in_specs=[
            pl.BlockSpec((None, rows, KV_W), lambda b, c, pt: (b, 0, 0)),
            pl.BlockSpec((None, n_new, chunk_keys), lambda b, c, pt: (b, 0, c)),
            pl.BlockSpec((None, n_new, LANES), lambda b, c, pt: (b, 0, n_pages)),
            pl.BlockSpec((n_new, KV_W), lambda b, c, pt: (b, 12)),
            pl.BlockSpec((n_new, KV_W), lambda b, c, pt: (b, 13)),
            pl.BlockSpec(memory_space=pl.ANY),
            pl.BlockSpec(memory_space=pl.ANY),
        ],
        out_specs=pl.BlockSpec((None, rows, KV_W), lambda b, c, pt: (b, 0, 0)),
        scratch_shapes=[
            pltpu.VMEM((rows, 1), F32),
            pltpu.VMEM((rows, 1), F32),
            pltpu.VMEM((rows, KV_W), F32),
            pltpu.VMEM((2, ps, KV_W, PAGE_SIZE), F32),
            pltpu.VMEM((2, ps, KV_W, PAGE_SIZE), F32),
            pltpu.SemaphoreType.DMA((2,)),
            pltpu.SemaphoreType.DMA((2,)),
        ],
    )
    return pl.pallas_call(
        functools.partial(_sattn_kernel, n_chunks=n_chunks, n_new=n_new, ps=ps),
        grid_spec=grid_spec,
        out_shape=jax.ShapeDtypeStruct((n_seq, rows, KV_W), F32),
        compiler_params=pltpu.CompilerParams(
            dimension_semantics=("arbitrary", "arbitrary"), vmem_limit_bytes=VMEM_LIMIT),
        name="sample_attend",
    )(page_table, qbd, bias, bias, proj, proj, cache_k_t, cache_v_t)


def _merge_kernel(x_ref, rnn_ref, att_ref, gr_ref, ga_ref, wr_ref, wa_ref, wo_ref, o_ref, *, n_valid):
    i = pl.program_id(0)

    @pl.when(i < n_valid)
    def _():
        mixed = (_sigmoid(gr_ref[...]) * _dot(rnn_ref[...].astype(BF16), wr_ref[...])
                 + _sigmoid(ga_ref[...]) * _dot(att_ref[...].astype(BF16), wa_ref[...]))
        o_ref[...] = x_ref[...] + _dot(mixed.astype(BF16), wo_ref[...])

    @pl.when(i >= n_valid)
    def _():
        o_ref[...] = jnp.zeros(o_ref.shape, F32)


def _merge_into_kernel(x_ref, rnn_ref, att_ref, gr_ref, ga_ref, wr_ref, wa_ref, wo_ref, dst_ref, o_ref,
                       *, n_valid):
    del dst_ref
    _merge_kernel(x_ref, rnn_ref, att_ref, gr_ref, ga_ref, wr_ref, wa_ref, wo_ref, o_ref, n_valid=n_valid)


def _merge(x, rnn, attn, proj, wr, wa, wo, *, tm, out_rows=None, row0=0, into=None):
    t = x.shape[0]
    out_rows = t if out_rows is None else out_rows
    blk0 = row0 // tm
    n_valid = t // tm
    n_steps = n_valid if into is not None else out_rows // tm
    row = lambda i: jnp.minimum(i, n_valid - 1)
    in_specs = [
        pl.BlockSpec((tm, D_MODEL), lambda i: (row(i), 0)),
        pl.BlockSpec((tm, D_RNN), lambda i: (row(i), 0)),
        pl.BlockSpec((tm, ATTN_W), lambda i: (row(i), 0)),
        pl.BlockSpec((tm, D_MODEL), lambda i: (row(i), 2)),
        pl.BlockSpec((tm, D_MODEL), lambda i: (row(i), 3)),
        pl.BlockSpec((D_RNN, D_MODEL), lambda i: (0, 0)),
        pl.BlockSpec((ATTN_W, D_MODEL), lambda i: (0, 0)),
        pl.BlockSpec((D_MODEL, D_MODEL), lambda i: (0, 0)),
    ]
    args = (x, rnn, attn, proj, proj, wr, wa, wo)
    if into is not None:
        in_specs.append(pl.BlockSpec(memory_space=pl.ANY))
        args = args + (into,)
    return pl.pallas_call(
        functools.partial(_merge_kernel if into is None else _merge_into_kernel, n_valid=n_valid),
        grid=(n_steps,),
        in_specs=in_specs,
        out_specs=pl.BlockSpec((tm, D_MODEL), lambda i: (blk0 + i, 0)),
        out_shape=jax.ShapeDtypeStruct((out_rows, D_MODEL), F32),
        input_output_aliases={} if into is None else {len(args) - 1: 0},
        compiler_params=pltpu.CompilerParams(
            dimension_semantics=("arbitrary",), vmem_limit_bytes=VMEM_LIMIT),
        name="merge",
    )(*args)


def _route(h, rwh, rwl, rb):
    lane = lax.broadcasted_iota(jnp.int32, (1, LANES), 1)
    lanef = lane.astype(F32)
    hh, hl = _split_bf16(h)
    lg = (_dot(hh, rwh) + _dot(hl, rwh) + _dot(hh, rwl)) + rb
    is_g = (lane >= N_EXPERTS) & (lane < N_EXPERTS + N_GROUPS)
    gl = jnp.where(is_g, lg, -jnp.inf)
    gmax = jnp.max(gl, axis=1, keepdims=True)
    gprob = 1.0 / jnp.sum(jnp.exp(gl - gmax), axis=1, keepdims=True)
    gsel = jnp.min(jnp.where(is_g & (lg == gmax), lanef - N_EXPERTS, 1e9), axis=1, keepdims=True)
    in_grp = (lane < N_EXPERTS) & (jnp.floor(lanef * (1.0 / EXPERTS_PER_GROUP)) == gsel)
    v1 = jnp.where(in_grp, lg, -jnp.inf)
    t1 = jnp.max(v1, axis=1, keepdims=True)
    i1 = jnp.min(jnp.where(in_grp & (lg == t1), lanef, 1e9), axis=1, keepdims=True)
    rest = in_grp & (lanef != i1)
    v2 = jnp.where(rest, lg, -jnp.inf)
    t2 = jnp.max(v2, axis=1, keepdims=True)
    i2 = jnp.min(jnp.where(rest & (lg == t2), lanef, 1e9), axis=1, keepdims=True)
    d = jnp.exp(t2 - t1)
    return i1, i2, gprob / (1.0 + d), gprob * d / (1.0 + d)


def _moe_kernel(x_ref, ln_ref, rwh_ref, rwl_ref, rb_ref, w1_ref, w3_ref, w2_ref, o_ref, h_ref, gate_ref):
    e = pl.program_id(1)
    lane = lax.broadcasted_iota(jnp.int32, (1, LANES), 1)

    @pl.when(e == 0)
    def _():
        h = _rms(x_ref[...], ln_ref[...])
        h_ref[...] = h.astype(BF16)
        i1, i2, g1, g2 = _route(h, rwh_ref[...], rwl_ref[...], rb_ref[...])
        lanef = lane.astype(F32)
        gate_ref[...] = jnp.where(lanef == i1, g1, 0.0) + jnp.where(lanef == i2, g2, 0.0)

    ge = jnp.sum(jnp.where(lane == e, gate_ref[...], 0.0), axis=1, keepdims=True)
    up = _dot(h_ref[...], w1_ref[...].astype(BF16))
    hid = (up * _sigmoid(up)) * _dot(h_ref[...], w3_ref[...].astype(BF16))
    contrib = _dot((hid * ge).astype(BF16), w2_ref[...].astype(BF16))

    @pl.when(e == 0)
    def _():
        o_ref[...] = x_ref[...] + contrib

    @pl.when(e > 0)
    def _():
        o_ref[...] += contrib


def _moe(x, ln2, rw_hi, rw_lo, rb, w1, w3, w2, *, tm):
    t = x.shape[0]
    return pl.pallas_call(
        _moe_kernel,
        grid=(t // tm, N_EXPERTS),
        in_specs=[
            pl.BlockSpec((tm, D_MODEL), lambda i, e: (i, 0)),
            pl.BlockSpec((1, D_MODEL), lambda i, e: (0, 0)),
            pl.BlockSpec((D_MODEL, LANES), lambda i, e: (0, 0)),
            pl.BlockSpec((D_MODEL, LANES), lambda i, e: (0, 0)),
            pl.BlockSpec((1, LANES), lambda i, e: (0, 0)),
            pl.BlockSpec((None, D_MODEL, D_EXPERT), lambda i, e: (e, 0, 0)),
            pl.BlockSpec((None, D_MODEL, D_EXPERT), lambda i, e: (e, 0, 0)),
            pl.BlockSpec((None, D_EXPERT, D_MODEL), lambda i, e: (e, 0, 0)),
        ],
        out_specs=pl.BlockSpec((tm, D_MODEL), lambda i, e: (i, 0)),
        out_shape=jax.ShapeDtypeStruct((t, D_MODEL), F32),
        scratch_shapes=[pltpu.VMEM((tm, D_MODEL), BF16), pltpu.VMEM((tm, LANES), F32)],
        compiler_params=pltpu.CompilerParams(
            dimension_semantics=("arbitrary", "arbitrary"), vmem_limit_bytes=VMEM_LIMIT),
        name="moe",
    )(x, ln2, rw_hi, rw_lo, rb, w1, w3, w2)


def _ple_update(x2, p, ln, wg, wp):
    gate = _sigmoid(_dot(_rms(x2, ln).astype(BF16), wg))
    return x2 + gate * _dot(p.astype(BF16), wp)


def _ple_kernel(x_ref, p_ref, ln_ref, wg_ref, wp_ref, o_ref):
    o_ref[...] = _ple_update(x_ref[...], p_ref[...], ln_ref[...], wg_ref[...], wp_ref[...])


def _ple(x, p, ln3, wg, wp, *, tm):
    t = x.shape[0]
    return pl.pallas_call(
        _ple_kernel,
        grid=(t // tm,),
        in_specs=[
            pl.BlockSpec((tm, D_MODEL), lambda i: (i, 0)),
            pl.BlockSpec((tm, PLE_DIM), lambda i: (i, 0)),
            pl.BlockSpec((1, D_MODEL), lambda i: (0, 0)),
            pl.BlockSpec((D_MODEL, D_MODEL), lambda i: (0, 0)),
            pl.BlockSpec((PLE_DIM, D_MODEL), lambda i: (0, 0)),
        ],
        out_specs=pl.BlockSpec((tm, D_MODEL), lambda i: (i, 0)),
        out_shape=jax.ShapeDtypeStruct((t, D_MODEL), F32),
        compiler_params=pltpu.CompilerParams(
            dimension_semantics=("arbitrary",), vmem_limit_bytes=VMEM_LIMIT),
        name="ple",
    )(x, p, ln3, wg, wp)


MOE_ROW_TILE = 256
META_E, META_G, META_RANK = 0, 2, 4


def _router_kernel(x_ref, ln_ref, rwh_ref, rwl_ref, rb_ref, tri_ref, meta_ref, cnt_ref, carry_ref):
    i = pl.program_id(0)
    lane = lax.broadcasted_iota(jnp.int32, (1, LANES), 1)
    lanef = lane.astype(F32)

    @pl.when(i == 0)
    def _():
        carry_ref[...] = jnp.zeros(carry_ref.shape, F32)

    i1, i2, g1, g2 = _route(_rms(x_ref[...], ln_ref[...]), rwh_ref[...], rwl_ref[...], rb_ref[...])
    onehot = jnp.where((lanef == i1) | (lanef == i2), 1.0, 0.0)
    before = _dot(tri_ref[...], onehot.astype(BF16)) + carry_ref[...]
    r1 = jnp.sum(jnp.where(lanef == i1, before, 0.0), axis=1, keepdims=True)
    r2 = jnp.sum(jnp.where(lanef == i2, before, 0.0), axis=1, keepdims=True)
    carry_ref[...] += jnp.sum(onehot, axis=0, keepdims=True)
    rec = jnp.zeros((x_ref.shape[0], LANES), F32)
    for k, val in ((META_E, i1), (META_E + 1, i2), (META_G, g1), (META_G + 1, g2),
                   (META_RANK, r1), (META_RANK + 1, r2)):
        rec = jnp.where(lane == k, val, rec)
    meta_ref[...] = rec
    cnt_ref[...] = jnp.broadcast_to(carry_ref[...], cnt_ref.shape)


def _router(x, ln2, rw_hi, rw_lo, rb, *, tm):
    t = x.shape[0]
    r = lax.broadcasted_iota(jnp.int32, (tm, tm), 0)
    c = lax.broadcasted_iota(jnp.int32, (tm, tm), 1)
    tri = (c < r).astype(BF16)
    return pl.pallas_call(
        _router_kernel,
        grid=(t // tm,),
        in_specs=[
            pl.BlockSpec((tm, D_MODEL), lambda i: (i, 0)),
            pl.BlockSpec((1, D_MODEL), lambda i: (0, 0)),
            pl.BlockSpec((D_MODEL, LANES), lambda i: (0, 0)),
            pl.BlockSpec((D_MODEL, LANES), lambda i: (0, 0)),
            pl.BlockSpec((1, LANES), lambda i: (0, 0)),
            pl.BlockSpec((tm, tm), lambda i: (0, 0)),
        ],
        out_specs=[
            pl.BlockSpec((tm, LANES), lambda i: (i, 0)),
            pl.BlockSpec((8, LANES), lambda i: (0, 0)),
        ],
        out_shape=[
            jax.ShapeDtypeStruct((t, LANES), F32),
            jax.ShapeDtypeStruct((8, LANES), F32),
        ],
        scratch_shapes=[pltpu.VMEM((1, LANES), F32)],
        compiler_params=pltpu.CompilerParams(
            dimension_semantics=("arbitrary",), vmem_limit_bytes=VMEM_LIMIT),
        name="moe_router",
    )(x, ln2, rw_hi, rw_lo, rb, tri)


def _row_copy(src_ref, src_row, dst_ref, dst_row, sem):
    return pltpu.make_async_copy(src_ref.at[pl.ds(src_row, 1), :], dst_ref.at[pl.ds(dst_row, 1), :], sem)


def _scatter_kernel(slot_ref, pad0_ref, npad_ref, nu_ref, x_ref, hs_ref, zero_ref, sem, zsem,
                    *, n_tok, n_tiles):
    tm = x_ref.shape[0]
    base = pl.program_id(0) * tm
    rt = zero_ref.shape[0]
    pieces = [1 << k for k in range(rt.bit_length() - 2, 2, -1)]

    def zero_fill(go):
        def per_expert(e, carry):
            first = pad0_ref[e]
            first8 = (first + 7) & -8
            for k in range(7):
                @pl.when(first + k < jnp.minimum(first8, first + npad_ref[e]))
                def _():
                    go(pltpu.make_async_copy(zero_ref.at[pl.ds(0, 1), :],
                                             hs_ref.at[pl.ds(first + k, 1), :], zsem))
            n = jnp.maximum(first + npad_ref[e] - first8, 0)
            for bit in pieces:
                @pl.when((n & bit) != 0)
                def _():
                    row = pl.multiple_of(first8 + (n & (-2 * bit)), 8)
                    go(pltpu.make_async_copy(zero_ref.at[pl.ds(0, bit), :],
                                             hs_ref.at[pl.ds(row, bit), :], zsem))
            return carry

        def per_tile(tile, carry):
            go(pltpu.make_async_copy(zero_ref, hs_ref.at[pl.ds(pl.multiple_of(tile * rt, rt), rt), :],
                                     zsem))
            return carry

        lax.fori_loop(0, N_EXPERTS, per_expert, 0)
        lax.fori_loop(nu_ref[0], n_tiles, per_tile, 0)

    @pl.when(pl.program_id(0) == 0)
    def _():
        zero_ref[...] = jnp.zeros(zero_ref.shape, F32)
        zero_fill(lambda cp: cp.start())
        zero_fill(lambda cp: cp.wait())

    def start(r, carry):
        for k in range(2):
            _row_copy(x_ref, r, hs_ref, slot_ref[k * n_tok + base + r], sem).start()
        return carry

    lax.fori_loop(0, tm, start, 0)
    for k in range(2):
        pltpu.make_async_copy(x_ref, hs_ref.at[pl.ds(0, tm), :], sem).wait()


def _scatter_rows(slots, pad0, npad, n_used, x, *, tm, n_tiles):
    t = x.shape[0]
    grid_spec = pltpu.PrefetchScalarGridSpec(
        num_scalar_prefetch=4,
        grid=(t // tm,),
        in_specs=[pl.BlockSpec((tm, D_MODEL), lambda i, *_: (i, 0))],
        out_specs=pl.BlockSpec(memory_space=pl.ANY),
        scratch_shapes=[
            pltpu.VMEM((MOE_ROW_TILE, D_MODEL), F32),
            pltpu.SemaphoreType.DMA(()),
            pltpu.SemaphoreType.DMA(()),
        ],
    )
    return pl.pallas_call(
        functools.partial(_scatter_kernel, n_tok=t, n_tiles=n_tiles),
        grid_spec=grid_spec,
        out_shape=jax.ShapeDtypeStruct((n_tiles * MOE_ROW_TILE, D_MODEL), F32),
        compiler_params=pltpu.CompilerParams(
            dimension_semantics=("arbitrary",), vmem_limit_bytes=VMEM_LIMIT),
        name="moe_scatter",
    )(slots, pad0, npad, n_used, x)


def _expert_kernel(te_ref, nu_ref, hs_ref, ln_ref, w1_ref, w3_ref, w2_ref, y_ref):
    i = pl.program_id(0)

    @pl.when(i < nu_ref[0])
    def _():
        h = _rms(hs_ref[...], ln_ref[...]).astype(BF16)
        up = _dot(h, w1_ref[...].astype(BF16))
        hid = (up * _sigmoid(up)) * _dot(h, w3_ref[...].astype(BF16))
        y_ref[...] = _dot(hid.astype(BF16), w2_ref[...].astype(BF16))

    @pl.when(i >= nu_ref[0])
    def _():
        y_ref[...] = jnp.zeros(y_ref.shape, F32)


def _expert_mlp(tile_expert, n_used, hs, ln2, w1, w3, w2):
    n_tiles = tile_expert.shape[0]
    grid_spec = pltpu.PrefetchScalarGridSpec(
        num_scalar_prefetch=2,
        grid=(n_tiles,),
        in_specs=[
            pl.BlockSpec((MOE_ROW_TILE, D_MODEL), lambda i, te, nu: (i, 0)),
            pl.BlockSpec((1, D_MODEL), lambda i, te, nu: (0, 0)),
            pl.BlockSpec((None, D_MODEL, D_EXPERT), lambda i, te, nu: (te[i], 0, 0)),
            pl.BlockSpec((None, D_MODEL, D_EXPERT), lambda i, te, nu: (te[i], 0, 0)),
            pl.BlockSpec((None, D_EXPERT, D_MODEL), lambda i, te, nu: (te[i], 0, 0)),
        ],
        out_specs=pl.BlockSpec((MOE_ROW_TILE, D_MODEL), lambda i, te, nu: (i, 0)),
    )
    return pl.pallas_call(
        _expert_kernel,
        grid_spec=grid_spec,
        out_shape=jax.ShapeDtypeStruct(hs.shape, F32),
        compiler_params=pltpu.CompilerParams(
            dimension_semantics=("arbitrary",), vmem_limit_bytes=VMEM_LIMIT),
        name="moe_experts",
    )(tile_expert, n_used, hs, ln2, w1, w3, w2)


def _combine_ple_kernel(slot_ref, x_ref, meta_ref, p_ref, ln_ref, wg_ref, wp_ref, y_ref, o_ref, o2_ref,
                        ybuf_ref, sem, *, n_tok, n_first):
    i = pl.program_id(0)
    tm = x_ref.shape[0]

    def gather(tile, buf):
        def start(r, carry):
            for k in range(2):
                _row_copy(y_ref, slot_ref[k * n_tok + tile * tm + r], ybuf_ref.at[buf, k], r,
                          sem.at[buf]).start()
            return carry
        lax.fori_loop(0, tm, start, 0)

    @pl.when(i == 0)
    def _():
        gather(0, 0)

    @pl.when(i + 1 < pl.num_programs(0))
    def _():
        gather(i + 1, (i + 1) % 2)

    cur = i % 2
    for k in range(2):
        pltpu.make_async_copy(y_ref.at[pl.ds(0, tm), :], ybuf_ref.at[cur, k], sem.at[cur]).wait()
    meta = meta_ref[...]
    x2 = (x_ref[...] + meta[:, META_G:META_G + 1] * ybuf_ref[cur, 0]
          + meta[:, META_G + 1:META_G + 2] * ybuf_ref[cur, 1])
    out = _ple_update(x2, p_ref[...], ln_ref[...], wg_ref[...], wp_ref[...])

    @pl.when(i < n_first)
    def _():
        o_ref[...] = out

    @pl.when(i >= n_first)
    def _():
        o2_ref[...] = out


def _combine_ple(slots, x, meta, p, ln3, wg, wp, y, *, tm, t_first):
    t = x.shape[0]
    n_first = t_first // tm
    grid_spec = pltpu.PrefetchScalarGridSpec(
        num_scalar_prefetch=1,
        grid=(t // tm,),
        in_specs=[
            pl.BlockSpec((tm, D_MODEL), lambda i, s: (i, 0)),
            pl.BlockSpec((tm, LANES), lambda i, s: (i, 0)),
            pl.BlockSpec((tm, PLE_DIM), lambda i, s: (i, 0)),
            pl.BlockSpec((1, D_MODEL), lambda i, s: (0, 0)),
            pl.BlockSpec((D_MODEL, D_MODEL), lambda i, s: (0, 0)),
            pl.BlockSpec((PLE_DIM, D_MODEL), lambda i, s: (0, 0)),
            pl.BlockSpec(memory_space=pl.ANY),
        ],
        out_specs=[
            pl.BlockSpec((tm, D_MODEL), lambda i, s: (jnp.minimum(i, n_first - 1), 0)),
            pl.BlockSpec((tm, D_MODEL), lambda i, s: (jnp.maximum(i - n_first, 0), 0)),
        ],
        scratch_shapes=[
            pltpu.VMEM((2, 2, tm, D_MODEL), F32),
            pltpu.SemaphoreType.DMA((2,)),
        ],
    )
    return pl.pallas_call(
        functools.partial(_combine_ple_kernel, n_tok=t, n_first=n_first),
        grid_spec=grid_spec,
        out_shape=[jax.ShapeDtypeStruct((t_first, D_MODEL), F32),
                   jax.ShapeDtypeStruct((t - t_first, D_MODEL), F32)],
        compiler_params=pltpu.CompilerParams(
            dimension_semantics=("arbitrary",), vmem_limit_bytes=VMEM_LIMIT),
        name="moe_combine_ple",
    )(slots, x, meta, p, ln3, wg, wp, y)


def _sparse_moe_ple(x, p, w, *, tm, t_first):
    t = x.shape[0]
    rt = MOE_ROW_TILE
    meta, cnt = _router(x, w["ln2"], w["rw_hi"], w["rw_lo"], w["rb"], tm=tm)
    counts = cnt[0, :N_EXPERTS].astype(jnp.int32)
    padded = ((counts + rt - 1) // rt) * rt
    ends = jnp.cumsum(padded)
    offs = ends - padded
    eid = meta[:, META_E:META_E + 2].astype(jnp.int32)
    rank = meta[:, META_RANK:META_RANK + 2].astype(jnp.int32)
    base = jnp.sum(jnp.where(eid[:, :, None] == jnp.arange(N_EXPERTS), offs, 0), axis=-1)
    slots = jnp.transpose(base + rank).reshape(2 * t)
    n_tiles = (2 * t + N_EXPERTS * (rt - 1)) // rt
    tile_start = jnp.arange(n_tiles, dtype=jnp.int32) * rt
    tile_expert = jnp.minimum(jnp.sum((tile_start[:, None] >= ends[None, :]).astype(jnp.int32), axis=1),
                              N_EXPERTS - 1)
    n_used = (ends[N_EXPERTS - 1] // rt).reshape(1)
    hs = _scatter_rows(slots, offs + counts, padded - counts, n_used, x, tm=tm, n_tiles=n_tiles)
    y = _expert_mlp(tile_expert, n_used, hs, w["ln2"], w["w1"], w["w3"], w["w2"])
    return _combine_ple(slots, x, meta, p, w["ln3"], w["wg"], w["wp"], y, tm=tm, t_first=t_first)


def _rope_tables(pos):
    half = HEAD_DIM // 2
    inv = ROPE_THETA ** (-jnp.arange(half, dtype=F32) / half)
    ang = pos.astype(F32)[:, None] * inv[None, :]
    cos, sin = jnp.cos(ang), jnp.sin(ang)
    return (jnp.concatenate([cos, cos, cos, cos], axis=1),
            jnp.concatenate([-sin, sin, -sin, sin], axis=1))


def _block_diag(w, per):
    n, r, _ = w.shape
    eye = jnp.eye(per, dtype=w.dtype)
    wg = w.reshape(n // per, per, r, r)
    return jnp.einsum("gpij,pq->gpiqj", wg, eye).reshape(n // per, per * r, per * r)


def _layer_weights(ln1, w_in, q_norm, k_norm, conv_w, conv_b, w_a, b_a, w_x, b_x, lam, w_br_rnn,
                   w_br_attn, w_out, ln2, w_rg, b_rg, w_re, b_re, w1, w3, w2, ln3, w_ple_gate,
                   w_ple_proj):
    o_q = 2 * D_RNN
    o_k = o_q + ATTN_W
    o_v = o_k + KV_W
    o_qi = o_v + KV_W
    o_ki = o_qi + N_IDX_HEADS * IDX_DIM
    o_wi = o_ki + IDX_DIM
    o_gr = o_wi + N_IDX_HEADS
    o_ga = o_gr + D_MODEL
    assert o_ki == COL_NR_END
    w_t = jnp.transpose(w_in)
    w_small_t = jnp.concatenate(
        [w_t[o_ki:o_gr], jnp.zeros((LANES - IDX_DIM - N_IDX_HEADS, D_MODEL), F32)], axis=0)
    ones = lambda n: jnp.ones((n,), F32)
    zeros = lambda n: jnp.zeros((n,), F32)
    n_gate = 2 * D_MODEL
    gain = jnp.concatenate([ones(o_q), jnp.tile(q_norm, N_HEADS), jnp.tile(k_norm, N_KV_HEADS),
                            ones(KV_W + N_IDX_HEADS * IDX_DIM + n_gate)])
    norm_on = jnp.concatenate([zeros(o_q), ones(ATTN_W + KV_W), zeros(KV_W + N_IDX_HEADS * IDX_DIM + n_gate)])
    rope_on = jnp.concatenate([zeros(o_q), ones(ATTN_W + KV_W), zeros(KV_W), ones(N_IDX_HEADS * IDX_DIM),
                               zeros(n_gate)])
    post = jnp.concatenate([ones(o_q), jnp.full((ATTN_W,), QK_SCALE, F32),
                            ones(2 * KV_W + N_IDX_HEADS * IDX_DIM + n_gate)])
    colctl = jnp.concatenate([jnp.stack([gain, norm_on, rope_on, post]), jnp.zeros((4, N_MAIN), F32)], axis=0)
    tn = 512
    head_of = jnp.arange(tn) // HEAD_DIM
    bd = (head_of[:, None] == head_of[None, :]).astype(BF16)
    rw = jnp.concatenate([w_re, w_rg, jnp.zeros((D_MODEL, LANES - N_EXPERTS - N_GROUPS), F32)], axis=1)
    rw_hi = rw.astype(BF16)
    rw_lo = (rw - rw_hi.astype(F32)).astype(BF16)
    rb = jnp.concatenate([b_re, b_rg, jnp.zeros((LANES - N_EXPERTS - N_GROUPS,), F32)])[None, :]
    return dict(
        ln1=ln1[None, :], w_t=w_t, w_small_t=w_small_t, colctl=colctl, bd=bd,
        cw=conv_w, cb=conv_b[None, :],
        wa_bd=_block_diag(w_a, 4).astype(BF16), ba=b_a[None, :],
        wx_bd=_block_diag(w_x, 4).astype(BF16), bx=b_x[None, :], lam=lam[None, :],
        wr=w_br_rnn.astype(BF16), wa=w_br_attn.astype(BF16), wo=w_out.astype(BF16),
        ln2=ln2[None, :], rw_hi=rw_hi, rw_lo=rw_lo, rb=rb,
        w1=w1, w3=w3, w2=w2,
        ln3=ln3[None, :], wg=w_ple_gate.astype(BF16), wp=w_ple_proj.astype(BF16),
    )


def _tail_dense(branches, p, w):
    x, rnn, attn, proj = branches
    t = x.shape[0]
    x1 = _merge(x, rnn, attn, proj, w["wr"], w["wa"], w["wo"], tm=min(t, 256))
    x2 = _moe(x1, w["ln2"], w["rw_hi"], w["rw_lo"], w["rb"], w["w1"], w["w3"], w["w2"], tm=min(t, 512))
    return _ple(x2, p, w["ln3"], w["wg"], w["wp"], tm=min(t, 512))


def _tail(branches_a, p_a, branches_b, p_b, w):
    ta, tb = branches_a[0].shape[0], branches_b[0].shape[0]
    if 2 * (ta + tb) < N_EXPERTS * MOE_ROW_TILE:
        return _tail_dense(branches_a, p_a, w), _tail_dense(branches_b, p_b, w)
    tm = math.gcd(256, ta, tb)
    mw = (w["wr"], w["wa"], w["wo"])
    x1 = _merge(*branches_a, *mw, tm=tm, out_rows=ta + tb)
    x1 = _merge(*branches_b, *mw, tm=tm, out_rows=ta + tb, row0=ta, into=x1)
    return _sparse_moe_ple(x1, jnp.concatenate([p_a, p_b], axis=0), w, tm=tm, t_first=ta)


def _prompt_layer(x, w):
    bp, tp, _ = x.shape
    xt = x.reshape(bp * tp, D_MODEL)
    cs, sn = _rope_tables(jnp.arange(tp, dtype=jnp.int32))
    proj, small, kvb, kib = _inproj(xt, w["ln1"], w["w_t"], w["w_small_t"], w["colctl"],
                                    cs, sn, w["bd"], tm=min(tp, 1024))
    conv0 = jnp.zeros((bp, 8, D_RNN), F32)
    h0 = jnp.zeros((bp, 1, D_RNN), F32)
    rnn, h_last = _rglru(proj, conv0, h0, w["cw"], w["cb"], w["wa_bd"], w["ba"], w["wx_bd"], w["bx"],
                         w["lam"], n_seq=bp, tt=min(tp, 256))
    attn = _prompt_attention(proj, small, kvb, kib, n_batch=bp, seq=tp)
    o_k = 2 * D_RNN + ATTN_W
    k = proj[:, o_k:o_k + KV_W].reshape(bp, tp, N_KV_HEADS, HEAD_DIM)
    v = proj[:, o_k + KV_W:o_k + 2 * KV_W].reshape(bp, tp, N_KV_HEADS, HEAD_DIM)
    ki = small[:, :IDX_DIM].reshape(bp, tp, IDX_DIM)
    conv_new = proj.reshape(bp, tp, N_MAIN)[:, tp - (CONV_W - 1):, :D_RNN]
    return (xt, rnn, attn, proj), (k, v, ki, conv_new, h_last.reshape(bp, D_RNN))


def _sample_layer(x, cache_k, cache_v, cache_kidx, state_conv, state_h, page_table, w):
    bs, ts, _ = x.shape
    n_pages = page_table.shape[1]
    past = n_pages * PAGE_SIZE
    xt = x.reshape(bs * ts, D_MODEL)
    cs, sn = _rope_tables(past + jnp.tile(jnp.arange(ts, dtype=jnp.int32), bs))
    proj, small, _, _ = _inproj(xt, w["ln1"], w["w_t"], w["w_small_t"], w["colctl"], cs, sn,
                                w["bd"], tm=bs * ts)
    conv0 = jnp.concatenate([jnp.zeros((bs, 8 - (CONV_W - 1), D_RNN), F32), state_conv], axis=1)
    rnn, h_last = _rglru(proj, conv0, state_h[:, None, :], w["cw"], w["cb"], w["wa_bd"], w["ba"],
                         w["wx_bd"], w["bx"], w["lam"], n_seq=bs, tt=ts)
    o_q = 2 * D_RNN
    o_qi = o_q + ATTN_W + 2 * KV_W
    qi = proj[:, o_qi:o_qi + N_IDX_HEADS * IDX_DIM].reshape(bs, ts, N_IDX_HEADS, IDX_DIM)
    qst = jnp.transpose(qi, (0, 2, 1, 3)).reshape(bs, N_IDX_HEADS * ts, IDX_DIM).astype(BF16)
    wi = small[:, IDX_DIM:IDX_DIM + N_IDX_HEADS].reshape(bs, ts, N_IDX_HEADS)
    wcol = jnp.transpose(wi, (0, 2, 1)).reshape(bs, N_IDX_HEADS * ts, 1)
    n_pool = cache_k.shape[0]
    kidx_t = jnp.transpose(cache_kidx, (0, 2, 1))
    k_t = jnp.transpose(cache_k, (0, 2, 3, 1)).reshape(n_pool, KV_W, PAGE_SIZE)
    v_t = jnp.transpose(cache_v, (0, 2, 3, 1)).reshape(n_pool, KV_W, PAGE_SIZE)
    scores = _sample_select(page_table, qst, wcol, small, kidx_t, n_new=ts)
    bias = _select_bias(scores.reshape(bs * ts, -1), n_new=ts, past=past).reshape(scores.shape)
    q = proj[:, o_q:o_q + ATTN_W].reshape(bs, ts, N_KV_HEADS, N_HEADS // N_KV_HEADS, HEAD_DIM)
    eye = jnp.eye(N_KV_HEADS, dtype=F32)
    qbd = jnp.einsum("btgjd,gk->bgjtkd", q, eye).reshape(bs, N_HEADS * ts, KV_W).astype(BF16)
    att = _sample_attend(page_table, qbd, bias, proj, k_t, v_t, n_new=ts)
    att = att.reshape(bs, N_KV_HEADS, N_HEADS // N_KV_HEADS, ts, N_KV_HEADS, HEAD_DIM)
    att = jnp.stack([att[:, g, :, :, g, :] for g in range(N_KV_HEADS)], axis=1)
    attn = jnp.transpose(att, (0, 3, 1, 2, 4)).reshape(bs * ts, ATTN_W)
    o_k = o_q + ATTN_W
    k = proj[:, o_k:o_k + KV_W].reshape(bs, ts, N_KV_HEADS, HEAD_DIM)
    v = proj[:, o_k + KV_W:o_k + 2 * KV_W].reshape(bs, ts, N_KV_HEADS, HEAD_DIM)
    ki = small[:, :IDX_DIM].reshape(bs, ts, IDX_DIM)
    conv_new = proj.reshape(bs, ts, N_MAIN)[:, ts - (CONV_W - 1):, :D_RNN]
    return (xt, rnn, attn, proj), (k, v, ki, conv_new, h_last.reshape(bs, D_RNN))


def kernel(x_prompt, x_sample, p_prompt, p_sample, cache_k, cache_v, cache_kidx, state_conv, state_h,
           page_table, ln1, w_in, q_norm, k_norm, conv_w, conv_b, w_a, b_a, w_x, b_x, lam, w_br_rnn,
           w_br_attn, w_out, ln2, w_rg, b_rg, w_re, b_re, w1, w3, w2, ln3, w_ple_gate, w_ple_proj):
    weights = (ln1, w_in, q_norm, k_norm, conv_w, conv_b, w_a, b_a, w_x, b_x, lam, w_br_rnn, w_br_attn,
               w_out, ln2, w_rg, b_rg, w_re, b_re, w1, w3, w2, ln3, w_ple_gate, w_ple_proj)
    depth = ln1.shape[0]
    yp, ys = x_prompt, x_sample
    st_p, st_s = [], []
    for i in range(depth):
        w = _layer_weights(*[wt[i] for wt in weights])
        br_p, sp = _prompt_layer(yp, w)
        br_s, ss = _sample_layer(ys, cache_k[i], cache_v[i], cache_kidx[i], state_conv[i], state_h[i],
                                 page_table, w)
        out_p, out_s = _tail(br_p, p_prompt[i].reshape(-1, PLE_DIM), br_s,
                             p_sample[i].reshape(-1, PLE_DIM), w)
        yp, ys = out_p.reshape(yp.shape), out_s.reshape(ys.shape)
        st_p.append(sp)
        st_s.append(ss)
    stack = lambda sts, j: jnp.stack([s[j] for s in sts])
    return (yp, ys, stack(st_p, 0), stack(st_p, 1), stack(st_p, 2), stack(st_p, 3), stack(st_p, 4),
            stack(st_s, 0), stack(st_s, 1), stack(st_s, 2), stack(st_s, 3), stack(st_s, 4))
```

```python
import functools
import math

import jax
import jax.numpy as jnp
import numpy as np
from jax import lax
from jax.experimental import pallas as pl
from jax.experimental.pallas import tpu as pltpu

F32 = jnp.float32
BF16 = jnp.bfloat16

D_MODEL = 2048
HEAD_DIM = 64
N_HEADS = 16
N_KV_HEADS = 4
ATTN_W = N_HEADS * HEAD_DIM
KV_W = N_KV_HEADS * HEAD_DIM
N_IDX_HEADS = 8
IDX_DIM = 64
TOPK_MAX = 256
ROPE_THETA = 10000.0
D_RNN = 1024
N_RNN_BLOCKS = 16
RNN_BLOCK = 64
CONV_W = 4
LRU_C = 8.0
N_GROUPS = 4
EXPERTS_PER_GROUP = 8
N_EXPERTS = 32
D_EXPERT = 256
PLE_DIM = 256
PAGE_SIZE = 128
EPS = 1e-6

LANES = 128
N_MAIN = 8192
COL_RAW_END = 2 * D_RNN
COL_NR_END = 4096
IDX_SCALE = (IDX_DIM ** -0.5) * (N_IDX_HEADS ** -0.5)
QK_SCALE = HEAD_DIM ** -0.5
F32_MIN = float(np.finfo(np.float32).min)
INT_MIN = -2147483648
KEY_NEG_INF = INT_MIN + 0x7FFFFF
VMEM_LIMIT = 56 * 1024 * 1024


def _dot(a, b):
    return jnp.dot(a, b, preferred_element_type=F32)


def _dot_nt(a, b):
    return lax.dot_general(a, b, (((1,), (1,)), ((), ())), preferred_element_type=F32)


def _sigmoid(x):
    return 1.0 / (1.0 + jnp.exp(-x))


def _rms(x, g):
    return x * lax.rsqrt(jnp.mean(x * x, axis=-1, keepdims=True) + EPS) * g


def _split_bf16(x):
    hi = x.astype(BF16)
    lo = (x - hi.astype(F32)).astype(BF16)
    return hi, lo


def _rope_chunks(y, c, s):
    lane = lax.broadcasted_iota(jnp.int32, (1, LANES), 1)
    first_half = (lane % HEAD_DIM) < (HEAD_DIM // 2)
    outs = []
    for k in range(y.shape[1] // LANES):
        yc = y[:, k * LANES:(k + 1) * LANES]
        partner = jnp.where(first_half, pltpu.roll(yc, LANES - HEAD_DIM // 2, 1),
                            pltpu.roll(yc, HEAD_DIM // 2, 1))
        outs.append(yc * c + partner * s)
    return outs[0] if len(outs) == 1 else jnp.concatenate(outs, axis=1)


def _inproj_kernel(x_ref, ln_ref, wlo_ref, whi_ref, ws_ref, ctl_ref, cs_ref, sn_ref, bd_ref,
                   o_ref, os_ref, kvb_ref, kib_ref, h_ref, *, tn):
    j = pl.program_id(1)

    @pl.when(j == 0)
    def _():
        hb = _rms(x_ref[...], ln_ref[...]).astype(BF16)
        h_ref[...] = hb
        ys = _dot_nt(hb, ws_ref[...].astype(BF16))
        lane = lax.broadcasted_iota(jnp.int32, (1, LANES), 1)
        roped = _rope_chunks(ys, cs_ref[...], sn_ref[...])
        os_ref[...] = jnp.where(lane < IDX_DIM, roped, ys)
        kib_ref[...] = jnp.where(lane < IDX_DIM, roped, 0.0).astype(BF16)

    @pl.when(j < COL_NR_END // tn)
    def _():
        o_ref[...] = _dot_nt(h_ref[...], wlo_ref[...].astype(BF16))

    @pl.when(j >= COL_NR_END // tn)
    def _():
        o_ref[...] = _dot_nt(h_ref[...], whi_ref[...].astype(BF16))

    j_idx = (COL_NR_END - N_IDX_HEADS * IDX_DIM) // tn

    @pl.when((j >= COL_RAW_END // tn) & (j < j_idx))
    def _():
        y = o_ref[...]
        ctl = ctl_ref[...]
        gain, norm_on, rope_on, post = ctl[0:1], ctl[1:2], ctl[2:3], ctl[3:4]
        hi, lo = _split_bf16(y * y)
        ss = _dot(hi, bd_ref[...]) + _dot(lo, bd_ref[...])
        yn = jnp.where(norm_on > 0.0, y * lax.rsqrt(ss * (1.0 / HEAD_DIM) + EPS) * gain, y)
        yr = jnp.where(rope_on > 0.0, _rope_chunks(yn, cs_ref[...], sn_ref[...]), yn)
        o_ref[...] = yr * post

    @pl.when(j == j_idx)
    def _():
        o_ref[...] = _rope_chunks(o_ref[...], cs_ref[...], sn_ref[...])

    @pl.when(j == (COL_RAW_END + ATTN_W) // tn)
    def _():
        kvb_ref[...] = o_ref[...].astype(BF16)


def _inproj(x, ln1, w_t, w_small_t, colctl, cs, sn, bd, *, tm, tn=512):
    t = x.shape[0]
    assert tn == 2 * KV_W
    grid = (t // tm, N_MAIN // tn)
    n_lo = COL_NR_END // tn
    n_rope = cs.shape[0] // tm
    gate_row0 = COL_NR_END + IDX_DIM + N_IDX_HEADS
    return pl.pallas_call(
        functools.partial(_inproj_kernel, tn=tn),
        grid=grid,
        in_specs=[
            pl.BlockSpec((tm, D_MODEL), lambda i, j: (i, 0)),
            pl.BlockSpec((1, D_MODEL), lambda i, j: (0, 0)),
            pl.BlockSpec((tn, D_MODEL), lambda i, j: (jnp.minimum(j, n_lo - 1), 0)),
            pl.BlockSpec((pl.Element(tn), pl.Element(D_MODEL)),
                         lambda i, j: (pl.multiple_of(gate_row0 + jnp.maximum(j - n_lo, 0) * tn, 8), 0)),
            pl.BlockSpec((LANES, D_MODEL), lambda i, j: (0, 0)),
            pl.BlockSpec((8, tn), lambda i, j: (0, j)),
            pl.BlockSpec((tm, LANES), lambda i, j: (i % n_rope, 0)),
            pl.BlockSpec((tm, LANES), lambda i, j: (i % n_rope, 0)),
            pl.BlockSpec((tn, tn), lambda i, j: (0, 0)),
        ],
        out_specs=[
            pl.BlockSpec((tm, tn), lambda i, j: (i, j)),
            pl.BlockSpec((tm, LANES), lambda i, j: (i, 0)),
            pl.BlockSpec((tm, 2 * KV_W), lambda i, j: (i, 0)),
            pl.BlockSpec((tm, LANES), lambda i, j: (i, 0)),
        ],
        out_shape=[
            jax.ShapeDtypeStruct((t, N_MAIN), F32),
            jax.ShapeDtypeStruct((t, LANES), F32),
            jax.ShapeDtypeStruct((t, 2 * KV_W), BF16),
            jax.ShapeDtypeStruct((t, LANES), BF16),
        ],
        scratch_shapes=[pltpu.VMEM((tm, D_MODEL), BF16)],
        compiler_params=pltpu.CompilerParams(
            dimension_semantics=("arbitrary", "arbitrary"), vmem_limit_bytes=VMEM_LIMIT),
        name="inproj",
    )(x, ln1, w_t, w_t, w_small_t, colctl, cs, sn, bd)


def _rglru_kernel(x_ref, g_ref, c0_ref, h0_ref, cw_ref, cb_ref, wa_ref, ba_ref, wx_ref, bx_ref,
                  lam_ref, o_ref, hl_ref, xs_ref, a_ref, b_ref, hc_ref):
    t = pl.program_id(1)
    tt = x_ref.shape[0]

    @pl.when(t == 0)
    def _():
        xs_ref[0:8, :] = c0_ref[...]
        hc_ref[...] = h0_ref[...]

    xs_ref[8:8 + tt, :] = x_ref[...]
    cw = cw_ref[...]
    taps = (xs_ref[5:5 + tt, :] * cw[0:1] + xs_ref[6:6 + tt, :] * cw[1:2]
            + xs_ref[7:7 + tt, :] * cw[2:3] + xs_ref[8:8 + tt, :] * cw[3:4])
    xc = cb_ref[...] + taps
    xs_ref[0:8, :] = xs_ref[tt:tt + 8, :]

    xcb = xc.astype(BF16)
    ra, ri = [], []
    for c in range(wa_ref.shape[0]):
        blk = xcb[:, c * 256:(c + 1) * 256]
        ra.append(_dot(blk, wa_ref[c]))
        ri.append(_dot(blk, wx_ref[c]))
    r = _sigmoid(jnp.concatenate(ra, axis=1) + ba_ref[...])
    ig = _sigmoid(jnp.concatenate(ri, axis=1) + bx_ref[...])
    nlam = -lam_ref[...]
    softplus = jnp.maximum(nlam, 0.0) + jnp.log1p(jnp.exp(-jnp.abs(nlam)))
    log_a = (-LRU_C) * r * softplus
    a = jnp.exp(log_a)
    u = jnp.sqrt(jnp.tanh(-log_a) * (a * a + 1.0)) * (ig * xc)

    n8 = tt // 8
    a3 = a.reshape(n8, 8, D_RNN)
    b3 = u.reshape(n8, 8, D_RNN)
    sub = lax.broadcasted_iota(jnp.int32, (1, 8, 1), 1)
    for s in (1, 2, 4):
        a_prev = pltpu.roll(a3, s, 1)
        b_prev = pltpu.roll(b3, s, 1)
        m = sub >= s
        b3 = jnp.where(m, a3 * b_prev + b3, b3)
        a3 = jnp.where(m, a3 * a_prev, a3)
    a_ref[...] = a3.reshape(tt, D_RNN)
    b_ref[...] = b3.reshape(tt, D_RNN)

    def chain(k, carry):
        i0 = pl.multiple_of(k * 8, 8)
        h8 = a_ref[pl.ds(i0, 8), :] * carry + b_ref[pl.ds(i0, 8), :]
        b_ref[pl.ds(i0, 8), :] = h8
        return h8[7:8, :]

    carry = lax.fori_loop(0, n8, chain, hc_ref[...])
    hc_ref[...] = carry
    g = g_ref[...]
    gelu = 0.5 * g * (1.0 + jnp.tanh(0.7978845608028654 * (g + 0.044715 * (g * g * g))))
    o_ref[...] = b_ref[...] * gelu

    @pl.when(t == pl.num_programs(1) - 1)
    def _():
        hl_ref[...] = carry


def _rglru(proj, conv0, h0, cw, cb, wa_bd, ba, wx_bd, bx, lam, *, n_seq, tt):
    t_total = proj.shape[0]
    nt = t_total // (n_seq * tt)
    full = lambda shape: pl.BlockSpec(shape, lambda b, t: (0,) * len(shape))
    return pl.pallas_call(
        _rglru_kernel,
        grid=(n_seq, nt),
        in_specs=[
            pl.BlockSpec((tt, D_RNN), lambda b, t: (b * nt + t, 0)),
            pl.BlockSpec((tt, D_RNN), lambda b, t: (b * nt + t, 1)),
            pl.BlockSpec((None, 8, D_RNN), lambda b, t: (b, 0, 0)),
            pl.BlockSpec((None, 1, D_RNN), lambda b, t: (b, 0, 0)),
            full((CONV_W, D_RNN)), full((1, D_RNN)),
            full(wa_bd.shape), full((1, D_RNN)),
            full(wx_bd.shape), full((1, D_RNN)),
            full((1, D_RNN)),
        ],
        out_specs=[
            pl.BlockSpec((tt, D_RNN), lambda b, t: (b * nt + t, 0)),
            pl.BlockSpec((None, 1, D_RNN), lambda b, t: (b, 0, 0)),
        ],
        out_shape=[
            jax.ShapeDtypeStruct((t_total, D_RNN), F32),
            jax.ShapeDtypeStruct((n_seq, 1, D_RNN), F32),
        ],
        scratch_shapes=[
            pltpu.VMEM((tt + 8, D_RNN), F32),
            pltpu.VMEM((tt, D_RNN), F32),
            pltpu.VMEM((tt, D_RNN), F32),
            pltpu.VMEM((1, D_RNN), F32),
        ],
        compiler_params=pltpu.CompilerParams(
            dimension_semantics=("arbitrary", "arbitrary"), vmem_limit_bytes=VMEM_LIMIT),
        name="rglru",
    )(proj, proj, conv0, h0, cw, cb, wa_bd, ba, wx_bd, bx, lam)


def _select_topk(s, kk):
    rows, n = s.shape
    kkf = float(kk)

    def key_to_f32(w):
        k = w ^ INT_MIN
        bits = jnp.where(k >= 0, k, k ^ 0x7FFFFFFF)
        return k, lax.bitcast_convert_type(bits, F32)

    def vbody(it, w):
        cand_w = w | jnp.left_shift(jnp.int32(1), 31 - it)
        cand_k, cand_f = key_to_f32(cand_w)
        cnt = jnp.sum(jnp.where(s >= cand_f, 1.0, 0.0), axis=1, keepdims=True)
        ok = (cnt >= kkf) | (cand_k < KEY_NEG_INF)
        return jnp.where(ok, cand_w, w)

    w = lax.fori_loop(0, 32, vbody, jnp.zeros((rows, 1), jnp.int32))
    _, thr = key_to_f32(w)
    gt = s > thr
    eq = s == thr
    need = kkf - jnp.sum(jnp.where(gt, 1.0, 0.0), axis=1, keepdims=True)
    col = lax.broadcasted_iota(jnp.int32, (1, n), 1)
    nbits = int(n).bit_length()

    def jbody(it, jmax):
        cand = jmax | jnp.left_shift(jnp.int32(1), nbits - 1 - it)
        cnt = jnp.sum(jnp.where(eq & (col < cand), 1.0, 0.0), axis=1, keepdims=True)
        return jnp.where(cnt <= need, cand, jmax)

    n_ge = jnp.sum(jnp.where(s >= thr, 1.0, 0.0), axis=1, keepdims=True)
    jmax = lax.cond(
        jnp.max(n_ge) > kkf,
        lambda: lax.fori_loop(0, nbits, jbody, jnp.zeros((rows, 1), jnp.int32)),
        lambda: jnp.full((rows, 1), (1 << nbits) - 1, jnp.int32))
    return gt | (eq & (col < jmax))


def _pattn_kernel(q_ref, qi_ref, sm_ref, k_ref, v_ref, ki_ref, o_ref, s_ref, *, i0, n_keys, kc, topk):
    i = pl.program_id(0)
    n_batch, tq = q_ref.shape[0], q_ref.shape[1]
    lane = lax.broadcasted_iota(jnp.int32, (1, LANES), 1)
    qpos = (i0 + i) * tq + lax.broadcasted_iota(jnp.int32, (tq, 1), 0)

    def score(b, carry):
        sm = sm_ref[b]
        qi = qi_ref[b]
        qrows, wrows = [], []
        for h in range(N_IDX_HEADS):
            blk = qi[:, (h // 2) * LANES:(h // 2 + 1) * LANES]
            if h % 2 == 1:
                blk = pltpu.roll(blk, IDX_DIM, 1)
            qrows.append(jnp.where(lane < IDX_DIM, blk, 0.0))
            wrows.append(sm[:, IDX_DIM + h:IDX_DIM + h + 1])
        qst = jnp.concatenate(qrows, axis=0).astype(BF16)
        wst = jnp.concatenate(wrows, axis=0) * IDX_SCALE
        r0 = pl.multiple_of(b * tq, tq)
        for c in range(n_keys // kc):
            s = jnp.maximum(_dot_nt(qst, ki_ref[b, c * kc:(c + 1) * kc, :]), 0.0) * wst
            sc = s[0:tq]
            for h in range(1, N_IDX_HEADS):
                sc = sc + s[h * tq:(h + 1) * tq]
            col = c * kc + lax.broadcasted_iota(jnp.int32, (1, kc), 1)
            s_ref[pl.ds(r0, tq), c * kc:(c + 1) * kc] = jnp.where(col <= qpos, sc, F32_MIN)
        return carry

    lax.fori_loop(0, n_batch, score, 0)

    sel = _select_topk(s_ref[...], topk)
    colf = lax.broadcasted_iota(jnp.int32, (1, n_keys), 1)
    qpos_all = jnp.concatenate([qpos] * n_batch, axis=0)
    s_ref[...] = jnp.where(sel & (colf <= qpos_all), 0.0, -jnp.inf)

    def attend(b, carry):
        _attend_tile(q_ref[b], s_ref[pl.ds(pl.multiple_of(b * tq, tq), tq), :],
                     k_ref[b, 0:n_keys, :], v_ref[b, 0:n_keys, :], o_ref.at[b])
        return carry

    lax.fori_loop(0, n_batch, attend, 0)


def _attend_tile(q, bias, kb, vb, o_ref):
    tq = q.shape[0]
    lane = lax.broadcasted_iota(jnp.int32, (1, LANES), 1)
    bias4 = jnp.concatenate([bias] * 4, axis=0)
    outs = [None] * N_HEADS
    for g in range(N_KV_HEADS):
        lo = (g % 2) * HEAD_DIM
        keep = (lane >= lo) & (lane < lo + HEAD_DIM)
        rows = []
        for j in range(4):
            h = 4 * g + j
            blk = q[:, (h // 2) * LANES:(h // 2 + 1) * LANES]
            if h % 2 != g % 2:
                blk = pltpu.roll(blk, HEAD_DIM, 1)
            piece = jnp.where(keep, blk, 0.0)
            zero = jnp.zeros_like(piece)
            rows.append(jnp.concatenate([piece, zero] if g < 2 else [zero, piece], axis=1))
        qbd = jnp.concatenate(rows, axis=0).astype(BF16)
        logits = _dot_nt(qbd, kb) + bias4
        m = jnp.max(logits, axis=1, keepdims=True)
        p = jnp.exp(logits - m)
        denom = jnp.sum(p, axis=1, keepdims=True)
        acc = _dot(p.astype(BF16), vb) / denom
        for j in range(4):
            outs[4 * g + j] = acc[j * tq:(j + 1) * tq, (g // 2) * LANES:(g // 2 + 1) * LANES]
    for c in range(N_HEADS // 2):
        g = (2 * c) // 4
        even, odd = outs[2 * c], outs[2 * c + 1]
        if g % 2 == 1:
            even = pltpu.roll(even, HEAD_DIM, 1)
        else:
            odd = pltpu.roll(odd, HEAD_DIM, 1)
        o_ref[:, c * LANES:(c + 1) * LANES] = jnp.where(lane < HEAD_DIM, even, odd).astype(BF16)


def _prompt_attention_part(proj3, small3, kvb, kib, *, i0, n_tiles, tq):
    n_batch, seq, _ = proj3.shape
    n_keys = (i0 + n_tiles) * tq
    kc = next(c for c in (512, 256, 128) if n_keys % c == 0)
    return pl.pallas_call(
        functools.partial(_pattn_kernel, i0=i0, n_keys=n_keys, kc=kc, topk=min(TOPK_MAX, seq // 4)),
        grid=(n_tiles,),
        in_specs=[
            pl.BlockSpec((n_batch, tq, ATTN_W), lambda i: (0, i0 + i, 2)),
            pl.BlockSpec((n_batch, tq, N_IDX_HEADS * IDX_DIM), lambda i: (0, i0 + i, 7)),
            pl.BlockSpec((n_batch, tq, LANES), lambda i: (0, i0 + i, 0)),
            pl.BlockSpec((n_batch, seq, KV_W), lambda i: (0, 0, 0)),
            pl.BlockSpec((n_batch, seq, KV_W), lambda i: (0, 0, 1)),
            pl.BlockSpec((n_batch, seq, LANES), lambda i: (0, 0, 0)),
        ],
        out_specs=pl.BlockSpec((n_batch, tq, ATTN_W), lambda i: (0, i, 0)),
        out_shape=jax.ShapeDtypeStruct((n_batch, n_tiles * tq, ATTN_W), BF16),
        scratch_shapes=[pltpu.VMEM((n_batch * tq, n_keys), F32)],
        compiler_params=pltpu.CompilerParams(
            dimension_semantics=("arbitrary",), vmem_limit_bytes=VMEM_LIMIT),
        name=f"prompt_attention_{i0}",
    )(proj3, proj3, small3, kvb, kvb, kib)


def _prompt_attention(proj, small, kvb, kib, *, n_batch, seq, tq=128, tiles_per_part=2):
    nq = seq // tq
    proj3 = proj.reshape(n_batch, seq, N_MAIN)
    small3 = small.reshape(n_batch, seq, LANES)
    kvb3 = kvb.reshape(n_batch, seq, 2 * KV_W)
    kib3 = kib.reshape(n_batch, seq, LANES)
    parts = [
        _prompt_attention_part(proj3, small3, kvb3, kib3, i0=i0,
                               n_tiles=min(tiles_per_part, nq - i0), tq=tq)
        for i0 in range(0, nq, tiles_per_part)
    ]
    return jnp.concatenate(parts, axis=1).reshape(n_batch * seq, ATTN_W)


SELECT_PAGES_PER_STEP = 64
ATTEND_PAGES_PER_STEP = 64
SUB_PAGES = 8


def _sidx_kernel(pt_ref, qst_ref, w_ref, sm_ref, kidx_hbm, o_ref, s_ref, kbuf_ref, ksem,
                 *, n_chunks, n_new, ps):
    c = pl.program_id(1)
    step = pl.program_id(0) * n_chunks + c
    n_steps = pl.num_programs(0) * n_chunks

    def fetch(s, buf):
        sb = s // n_chunks
        sc = s % n_chunks

        def start(r, carry):
            pltpu.make_async_copy(kidx_hbm.at[pt_ref[sb, sc * ps + r]], kbuf_ref.at[buf, r],
                                  ksem.at[buf]).start()
            return carry

        lax.fori_loop(0, ps, start, 0)

    @pl.when(step == 0)
    def _():
        fetch(0, 0)

    @pl.when(step + 1 < n_steps)
    def _():
        fetch(step + 1, (step + 1) % 2)

    cur = step % 2
    pltpu.make_async_copy(kidx_hbm.at[pl.ds(0, ps)], kbuf_ref.at[cur], ksem.at[cur]).wait()
    pages = [kbuf_ref.at[cur, r] for r in range(ps)]
    qst = qst_ref[...]
    w = w_ref[...] * IDX_SCALE

    def head_sum(s):
        s = jnp.maximum(s, 0.0) * w
        out = s[0:n_new]
        for h in range(1, N_IDX_HEADS):
            out = out + s[h * n_new:(h + 1) * n_new]
        return out

    for r0 in range(0, ps, SUB_PAGES):
        kt = jnp.concatenate([pages[r0 + r][...] for r in range(SUB_PAGES)], axis=1).astype(BF16)
        part = head_sum(_dot(qst, kt))
        for r in range(SUB_PAGES):
            s_ref[ps * c + r0 + r] = part[:, r * PAGE_SIZE:(r + 1) * PAGE_SIZE]

    @pl.when(c == n_chunks - 1)
    def _():
        n_past_blocks = n_chunks * ps
        past = n_past_blocks * PAGE_SIZE
        k_new = sm_ref[...][:, 0:IDX_DIM]
        kp = jnp.concatenate([k_new, jnp.zeros((PAGE_SIZE - n_new, IDX_DIM), F32)], axis=0)
        lane = lax.broadcasted_iota(jnp.int32, (n_new, LANES), 1)
        trow = lax.broadcasted_iota(jnp.int32, (n_new, LANES), 0)
        s_new = head_sum(_dot_nt(qst, kp.astype(BF16)))
        s_ref[n_past_blocks] = jnp.where(lane < n_new, jnp.where(lane <= trow, s_new, F32_MIN), -jnp.inf)
        o_ref[...] = jnp.concatenate([s_ref[k] for k in range(n_past_blocks + 1)], axis=1)


def _select_bias_kernel(s_ref, o_ref, *, n_new, past, topk):
    s = s_ref[...]
    rows, n_all = s.shape
    sel = _select_topk(s, topk)
    col = lax.broadcasted_iota(jnp.int32, (1, n_all), 1)
    tq = lax.broadcasted_iota(jnp.int32, (rows, 1), 0) % n_new
    o_ref[...] = jnp.where(sel & ((col - past) <= tq), 0.0, -jnp.inf)


def _select_bias(scores, *, n_new, past, rows_per_step=128):
    rows, n_all = scores.shape
    rows_per_step = min(rows_per_step, rows)
    return pl.pallas_call(
        functools.partial(_select_bias_kernel, n_new=n_new, past=past,
                          topk=min(TOPK_MAX, (past + n_new) // 4)),
        grid=(rows // rows_per_step,),
        in_specs=[pl.BlockSpec((rows_per_step, n_all), lambda i: (i, 0))],
        out_specs=pl.BlockSpec((rows_per_step, n_all), lambda i: (i, 0)),
        out_shape=jax.ShapeDtypeStruct((rows, n_all), F32),
        compiler_params=pltpu.CompilerParams(
            dimension_semantics=("arbitrary",), vmem_limit_bytes=VMEM_LIMIT),
        name="sample_select_bias",
    )(scores)


def _sample_select(page_table, qst, wcol, small, cache_kidx_t, *, n_new):
    n_seq, n_pages = page_table.shape
    ps = min(SELECT_PAGES_PER_STEP, n_pages)
    n_chunks = n_pages // ps
    n_all = n_pages * PAGE_SIZE + LANES
    rows = N_IDX_HEADS * n_new
    grid_spec = pltpu.PrefetchScalarGridSpec(
        num_scalar_prefetch=1,
        grid=(n_seq, n_chunks),
        in_specs=[
            pl.BlockSpec((None, rows, IDX_DIM), lambda b, c, pt: (b, 0, 0)),
            pl.BlockSpec((None, rows, 1), lambda b, c, pt: (b, 0, 0)),
            pl.BlockSpec((n_new, LANES), lambda b, c, pt: (b, 0)),
            pl.BlockSpec(memory_space=pl.ANY),
        ],
        out_specs=pl.BlockSpec((None, n_new, n_all), lambda b, c, pt: (b, 0, 0)),
        scratch_shapes=[
            pltpu.VMEM((n_pages + 1, n_new, LANES), F32),
            pltpu.VMEM((2, ps, IDX_DIM, PAGE_SIZE), F32),
            pltpu.SemaphoreType.DMA((2,)),
        ],
    )
    return pl.pallas_call(
        functools.partial(_sidx_kernel, n_chunks=n_chunks, n_new=n_new, ps=ps),
        grid_spec=grid_spec,
        out_shape=jax.ShapeDtypeStruct((n_seq, n_new, n_all), F32),
        compiler_params=pltpu.CompilerParams(
            dimension_semantics=("arbitrary", "arbitrary"), vmem_limit_bytes=VMEM_LIMIT),
        name="sample_select",
    )(page_table, qst, wcol, small, cache_kidx_t)


def _sattn_kernel(pt_ref, q_ref, bias_ref, biasn_ref, kn_ref, vn_ref, k_hbm, v_hbm, o_ref,
                  m_ref, l_ref, acc_ref, kbuf_ref, vbuf_ref, ksem, vsem, *, n_chunks, n_new, ps):
    b = pl.program_id(0)
    c = pl.program_id(1)
    rows = q_ref.shape[0]
    reps = rows // n_new
    step = b * n_chunks + c
    n_steps = pl.num_programs(0) * n_chunks

    def fetch(s, buf):
        sb = s // n_chunks
        sc = s % n_chunks

        def start(r, carry):
            page = pt_ref[sb, sc * ps + r]
            pltpu.make_async_copy(k_hbm.at[page], kbuf_ref.at[buf, r], ksem.at[buf]).start()
            pltpu.make_async_copy(v_hbm.at[page], vbuf_ref.at[buf, r], vsem.at[buf]).start()
            return carry

        lax.fori_loop(0, ps, start, 0)

    @pl.when(step == 0)
    def _():
        fetch(0, 0)

    @pl.when(step + 1 < n_steps)
    def _():
        fetch(step + 1, (step + 1) % 2)

    cur = step % 2
    pltpu.make_async_copy(k_hbm.at[pl.ds(0, ps)], kbuf_ref.at[cur], ksem.at[cur]).wait()
    pltpu.make_async_copy(v_hbm.at[pl.ds(0, ps)], vbuf_ref.at[cur], vsem.at[cur]).wait()
    kpages = [kbuf_ref.at[cur, r] for r in range(ps)]
    vpages = [vbuf_ref.at[cur, r] for r in range(ps)]

    @pl.when(c == 0)
    def _():
        m_ref[...] = jnp.full(m_ref.shape, -1e30, F32)
        l_ref[...] = jnp.zeros(l_ref.shape, F32)
        acc_ref[...] = jnp.zeros(acc_ref.shape, F32)

    def update(logits, bias, pv):
        logits = logits + jnp.concatenate([bias] * reps, axis=0)
        m_old = m_ref[...]
        m_new = jnp.maximum(m_old, jnp.max(logits, axis=1, keepdims=True))
        alpha = jnp.exp(m_old - m_new)
        p = jnp.exp(logits - m_new)
        l_ref[...] = alpha * l_ref[...] + jnp.sum(p, axis=1, keepdims=True)
        acc_ref[...] = alpha * acc_ref[...] + pv(p.astype(BF16))
        m_ref[...] = m_new

    sub_keys = SUB_PAGES * PAGE_SIZE
    logits, vts = [], []
    for r0 in range(0, ps, SUB_PAGES):
        kt = jnp.concatenate([kpages[r0 + r][...] for r in range(SUB_PAGES)], axis=1).astype(BF16)
        vts.append(jnp.concatenate([vpages[r0 + r][...] for r in range(SUB_PAGES)], axis=1).astype(BF16))
        logits.append(_dot(q_ref[...], kt))

    def pv(p):
        acc = _dot_nt(p[:, 0:sub_keys], vts[0])
        for n in range(1, len(vts)):
            acc = acc + _dot_nt(p[:, n * sub_keys:(n + 1) * sub_keys], vts[n])
        return acc

    update(jnp.concatenate(logits, axis=1), bias_ref[...], pv)

    @pl.when(c == n_chunks - 1)
    def _():
        pad = jnp.zeros((PAGE_SIZE - n_new, KV_W), F32)
        kn = jnp.concatenate([kn_ref[...], pad], axis=0).astype(BF16)
        vn = jnp.concatenate([vn_ref[...], pad], axis=0).astype(BF16)
        update(_dot_nt(q_ref[...], kn), biasn_ref[...], lambda p: _dot(p, vn))
        o_ref[...] = acc_ref[...] / l_ref[...]


def _sample_attend(page_table, qbd, bias, proj, cache_k_t, cache_v_t, *, n_new):
    n_seq, n_pages = page_table.shape
    ps = min(ATTEND_PAGES_PER_STEP, n_pages)
    n_chunks = n_pages // ps
    rows = qbd.shape[1]
    chunk_keys = ps * PAGE_SIZE
    grid_spec = pltpu.PrefetchScalarGridSpec(
        num_scalar_prefetch=1,
        grid=(n_seq, n_chunks),
        in_specs=[
            pl.BlockSpec((None, rows, KV_W), lambda b, c, pt: (b, 0, 0)),
            pl.BlockSpec((None, n_new, chunk_keys), lambda b, c, pt: (b, 0, c)),
            pl.BlockSpec((None, n_new, LANES), lambda b, c, pt: (b, 0, n_pages)),
            pl.BlockSpec((n_new, KV_W), lambda b, c, pt: (b, 12)),
            pl.BlockSpec((n_new, KV_W), lambda b, c, pt: (b, 13)),
            pl.BlockSpec(memory_space=pl.ANY),
            pl.BlockSpec(memory_space=pl.ANY),
        ],
        out_specs=pl.BlockSpec((None, rows, KV_W), lambda b, c, pt: (b, 0, 0)),
        scratch_shapes=[
            pltpu.VMEM((rows, 1), F32),
            pltpu.VMEM((rows, 1), F32),
            pltpu.VMEM((rows, KV_W), F32),
            pltpu.VMEM((2, ps, KV_W, PAGE_SIZE), F32),
            pltpu.VMEM((2, ps, KV_W, PAGE_SIZE), F32),
            pltpu.SemaphoreType.DMA((2,)),
            pltpu.SemaphoreType.DMA((2,)),
        ],
    )
    return pl.pallas_call(
        functools.partial(_sattn_kernel, n_chunks=n_chunks, n_new=n_new, ps=ps),
        grid_spec=grid_spec,
        out_shape=jax.ShapeDtypeStruct((n_seq, rows, KV_W), F32),
        compiler_params=pltpu.CompilerParams(
            dimension_semantics=("arbitrary", "arbitrary"), vmem_limit_bytes=VMEM_LIMIT),
        name="sample_attend",
    )(page_table, qbd, bias, bias, proj, proj, cache_k_t, cache_v_t)


def _merge_kernel(x_ref, rnn_ref, att_ref, gr_ref, ga_ref, wr_ref, wa_ref, wo_ref, o_ref, *, n_valid):
    i = pl.program_id(0)

    @pl.when(i < n_valid)
    def _():
        mixed = (_sigmoid(gr_ref[...]) * _dot(rnn_ref[...].astype(BF16), wr_ref[...])
                 + _sigmoid(ga_ref[...]) * _dot(att_ref[...].astype(BF16), wa_ref[...]))
        o_ref[...] = x_ref[...] + _dot(mixed.astype(BF16), wo_ref[...])

    @pl.when(i >= n_valid)
    def _():
        o_ref[...] = jnp.zeros(o_ref.shape, F32)


def _merge_into_kernel(x_ref, rnn_ref, att_ref, gr_ref, ga_ref, wr_ref, wa_ref, wo_ref, dst_ref, o_ref,
                       *, n_valid):
    del dst_ref
    _merge_kernel(x_ref, rnn_ref, att_ref, gr_ref, ga_ref, wr_ref, wa_ref, wo_ref, o_ref, n_valid=n_valid)


def _merge(x, rnn, attn, proj, wr, wa, wo, *, tm, out_rows=None, row0=0, into=None):
    t = x.shape[0]
    out_rows = t if out_rows is None else out_rows
    blk0 = row0 // tm
    n_valid = t // tm
    n_steps = n_valid if into is not None else out_rows // tm
    row = lambda i: jnp.minimum(i, n_valid - 1)
    in_specs = [
        pl.BlockSpec((tm, D_MODEL), lambda i: (row(i), 0)),
        pl.BlockSpec((tm, D_RNN), lambda i: (row(i), 0)),
        pl.BlockSpec((tm, ATTN_W), lambda i: (row(i), 0)),
        pl.BlockSpec((tm, D_MODEL), lambda i: (row(i), 2)),
        pl.BlockSpec((tm, D_MODEL), lambda i: (row(i), 3)),
        pl.BlockSpec((D_RNN, D_MODEL), lambda i: (0, 0)),
        pl.BlockSpec((ATTN_W, D_MODEL), lambda i: (0, 0)),
        pl.BlockSpec((D_MODEL, D_MODEL), lambda i: (0, 0)),
    ]
    args = (x, rnn, attn, proj, proj, wr, wa, wo)
    if into is not None:
        in_specs.append(pl.BlockSpec(memory_space=pl.ANY))
        args = args + (into,)
    return pl.pallas_call(
        functools.partial(_merge_kernel if into is None else _merge_into_kernel, n_valid=n_valid),
        grid=(n_steps,),
        in_specs=in_specs,
        out_specs=pl.BlockSpec((tm, D_MODEL), lambda i: (blk0 + i, 0)),
        out_shape=jax.ShapeDtypeStruct((out_rows, D_MODEL), F32),
        input_output_aliases={} if into is None else {len(args) - 1: 0},
        compiler_params=pltpu.CompilerParams(
            dimension_semantics=("arbitrary",), vmem_limit_bytes=VMEM_LIMIT),
        name="merge",
    )(*args)


def _route(h, rwh, rwl, rb):
    lane = lax.broadcasted_iota(jnp.int32, (1, LANES), 1)
    lanef = lane.astype(F32)
    hh, hl = _split_bf16(h)
    lg = (_dot(hh, rwh) + _dot(hl, rwh) + _dot(hh, rwl)) + rb
    is_g = (lane >= N_EXPERTS) & (lane < N_EXPERTS + N_GROUPS)
    gl = jnp.where(is_g, lg, -jnp.inf)
    gmax = jnp.max(gl, axis=1, keepdims=True)
    gprob = 1.0 / jnp.sum(jnp.exp(gl - gmax), axis=1, keepdims=True)
    gsel = jnp.min(jnp.where(is_g & (lg == gmax), lanef - N_EXPERTS, 1e9), axis=1, keepdims=True)
    in_grp = (lane < N_EXPERTS) & (jnp.floor(lanef * (1.0 / EXPERTS_PER_GROUP)) == gsel)
    v1 = jnp.where(in_grp, lg, -jnp.inf)
    t1 = jnp.max(v1, axis=1, keepdims=True)
    i1 = jnp.min(jnp.where(in_grp & (lg == t1), lanef, 1e9), axis=1, keepdims=True)
    rest = in_grp & (lanef != i1)
    v2 = jnp.where(rest, lg, -jnp.inf)
    t2 = jnp.max(v2, axis=1, keepdims=True)
    i2 = jnp.min(jnp.where(rest & (lg == t2), lanef, 1e9), axis=1, keepdims=True)
    d = jnp.exp(t2 - t1)
    return i1, i2, gprob / (1.0 + d), gprob * d / (1.0 + d)


def _moe_kernel(x_ref, ln_ref, rwh_ref, rwl_ref, rb_ref, w1_ref, w3_ref, w2_ref, o_ref, h_ref, gate_ref):
    e = pl.program_id(1)
    lane = lax.broadcasted_iota(jnp.int32, (1, LANES), 1)

    @pl.when(e == 0)
    def _():
        h = _rms(x_ref[...], ln_ref[...])
        h_ref[...] = h.astype(BF16)
        i1, i2, g1, g2 = _route(h, rwh_ref[...], rwl_ref[...], rb_ref[...])
        lanef = lane.astype(F32)
        gate_ref[...] = jnp.where(lanef == i1, g1, 0.0) + jnp.where(lanef == i2, g2, 0.0)

    ge = jnp.sum(jnp.where(lane == e, gate_ref[...], 0.0), axis=1, keepdims=True)
    up = _dot(h_ref[...], w1_ref[...].astype(BF16))
    hid = (up * _sigmoid(up)) * _dot(h_ref[...], w3_ref[...].astype(BF16))
    contrib = _dot((hid * ge).astype(BF16), w2_ref[...].astype(BF16))

    @pl.when(e == 0)
    def _():
        o_ref[...] = x_ref[...] + contrib

    @pl.when(e > 0)
    def _():
        o_ref[...] += contrib


def _moe(x, ln2, rw_hi, rw_lo, rb, w1, w3, w2, *, tm):
    t = x.shape[0]
    return pl.pallas_call(
        _moe_kernel,
        grid=(t // tm, N_EXPERTS),
        in_specs=[
            pl.BlockSpec((tm, D_MODEL), lambda i, e: (i, 0)),
            pl.BlockSpec((1, D_MODEL), lambda i, e: (0, 0)),
            pl.BlockSpec((D_MODEL, LANES), lambda i, e: (0, 0)),
            pl.BlockSpec((D_MODEL, LANES), lambda i, e: (0, 0)),
            pl.BlockSpec((1, LANES), lambda i, e: (0, 0)),
            pl.BlockSpec((None, D_MODEL, D_EXPERT), lambda i, e: (e, 0, 0)),
            pl.BlockSpec((None, D_MODEL, D_EXPERT), lambda i, e: (e, 0, 0)),
            pl.BlockSpec((None, D_EXPERT, D_MODEL), lambda i, e: (e, 0, 0)),
        ],
        out_specs=pl.BlockSpec((tm, D_MODEL), lambda i, e: (i, 0)),
        out_shape=jax.ShapeDtypeStruct((t, D_MODEL), F32),
        scratch_shapes=[pltpu.VMEM((tm, D_MODEL), BF16), pltpu.VMEM((tm, LANES), F32)],
        compiler_params=pltpu.CompilerParams(
            dimension_semantics=("arbitrary", "arbitrary"), vmem_limit_bytes=VMEM_LIMIT),
        name="moe",
    )(x, ln2, rw_hi, rw_lo, rb, w1, w3, w2)


def _ple_update(x2, p, ln, wg, wp):
    gate = _sigmoid(_dot(_rms(x2, ln).astype(BF16), wg))
    return x2 + gate * _dot(p.astype(BF16), wp)


def _ple_kernel(x_ref, p_ref, ln_ref, wg_ref, wp_ref, o_ref):
    o_ref[...] = _ple_update(x_ref[...], p_ref[...], ln_ref[...], wg_ref[...], wp_ref[...])


def _ple(x, p, ln3, wg, wp, *, tm):
    t = x.shape[0]
    return pl.pallas_call(
        _ple_kernel,
        grid=(t // tm,),
        in_specs=[
            pl.BlockSpec((tm, D_MODEL), lambda i: (i, 0)),
            pl.BlockSpec((tm, PLE_DIM), lambda i: (i, 0)),
            pl.BlockSpec((1, D_MODEL), lambda i: (0, 0)),
            pl.BlockSpec((D_MODEL, D_MODEL), lambda i: (0, 0)),
            pl.BlockSpec((PLE_DIM, D_MODEL), lambda i: (0, 0)),
        ],
        out_specs=pl.BlockSpec((tm, D_MODEL), lambda i: (i, 0)),
        out_shape=jax.ShapeDtypeStruct((t, D_MODEL), F32),
        compiler_params=pltpu.CompilerParams(
            dimension_semantics=("arbitrary",), vmem_limit_bytes=VMEM_LIMIT),
        name="ple",
    )(x, p, ln3, wg, wp)


MOE_ROW_TILE = 256
META_E, META_G, META_RANK = 0, 2, 4


def _router_kernel(x_ref, ln_ref, rwh_ref, rwl_ref, rb_ref, tri_ref, meta_ref, cnt_ref, carry_ref):
    i = pl.program_id(0)
    lane = lax.broadcasted_iota(jnp.int32, (1, LANES), 1)
    lanef = lane.astype(F32)

    @pl.when(i == 0)
    def _():
        carry_ref[...] = jnp.zeros(carry_ref.shape, F32)

    i1, i2, g1, g2 = _route(_rms(x_ref[...], ln_ref[...]), rwh_ref[...], rwl_ref[...], rb_ref[...])
    onehot = jnp.where((lanef == i1) | (lanef == i2), 1.0, 0.0)
    before = _dot(tri_ref[...], onehot.astype(BF16)) + carry_ref[...]
    r1 = jnp.sum(jnp.where(lanef == i1, before, 0.0), axis=1, keepdims=True)
    r2 = jnp.sum(jnp.where(lanef == i2, before, 0.0), axis=1, keepdims=True)
    carry_ref[...] += jnp.sum(onehot, axis=0, keepdims=True)
    rec = jnp.zeros((x_ref.shape[0], LANES), F32)
    for k, val in ((META_E, i1), (META_E + 1, i2), (META_G, g1), (META_G + 1, g2),
                   (META_RANK, r1), (META_RANK + 1, r2)):
        rec = jnp.where(lane == k, val, rec)
    meta_ref[...] = rec
    cnt_ref[...] = jnp.broadcast_to(carry_ref[...], cnt_ref.shape)


def _router(x, ln2, rw_hi, rw_lo, rb, *, tm):
    t = x.shape[0]
    r = lax.broadcasted_iota(jnp.int32, (tm, tm), 0)
    c = lax.broadcasted_iota(jnp.int32, (tm, tm), 1)
    tri = (c < r).astype(BF16)
    return pl.pallas_call(
        _router_kernel,
        grid=(t // tm,),
        in_specs=[
            pl.BlockSpec((tm, D_MODEL), lambda i: (i, 0)),
            pl.BlockSpec((1, D_MODEL), lambda i: (0, 0)),
            pl.BlockSpec((D_MODEL, LANES), lambda i: (0, 0)),
            pl.BlockSpec((D_MODEL, LANES), lambda i: (0, 0)),
            pl.BlockSpec((1, LANES), lambda i: (0, 0)),
            pl.BlockSpec((tm, tm), lambda i: (0, 0)),
        ],
        out_specs=[
            pl.BlockSpec((tm, LANES), lambda i: (i, 0)),
            pl.BlockSpec((8, LANES), lambda i: (0, 0)),
        ],
        out_shape=[
            jax.ShapeDtypeStruct((t, LANES), F32),
            jax.ShapeDtypeStruct((8, LANES), F32),
        ],
        scratch_shapes=[pltpu.VMEM((1, LANES), F32)],
        compiler_params=pltpu.CompilerParams(
            dimension_semantics=("arbitrary",), vmem_limit_bytes=VMEM_LIMIT),
        name="moe_router",
    )(x, ln2, rw_hi, rw_lo, rb, tri)


def _row_copy(src_ref, src_row, dst_ref, dst_row, sem):
    return pltpu.make_async_copy(src_ref.at[pl.ds(src_row, 1), :], dst_ref.at[pl.ds(dst_row, 1), :], sem)


def _scatter_kernel(slot_ref, pad0_ref, npad_ref, nu_ref, x_ref, hs_ref, zero_ref, sem, zsem,
                    *, n_tok, n_tiles):
    tm = x_ref.shape[0]
    base = pl.program_id(0) * tm
    rt = zero_ref.shape[0]
    pieces = [1 << k for k in range(rt.bit_length() - 2, 2, -1)]

    def zero_fill(go):
        def per_expert(e, carry):
            first = pad0_ref[e]
            first8 = (first + 7) & -8
            for k in range(7):
                @pl.when(first + k < jnp.minimum(first8, first + npad_ref[e]))
                def _():
                    go(pltpu.make_async_copy(zero_ref.at[pl.ds(0, 1), :],
                                             hs_ref.at[pl.ds(first + k, 1), :], zsem))
            n = jnp.maximum(first + npad_ref[e] - first8, 0)
            for bit in pieces:
                @pl.when((n & bit) != 0)
                def _():
                    row = pl.multiple_of(first8 + (n & (-2 * bit)), 8)
                    go(pltpu.make_async_copy(zero_ref.at[pl.ds(0, bit), :],
                                             hs_ref.at[pl.ds(row, bit), :], zsem))
            return carry

        def per_tile(tile, carry):
            go(pltpu.make_async_copy(zero_ref, hs_ref.at[pl.ds(pl.multiple_of(tile * rt, rt), rt), :],
                                     zsem))
            return carry

        lax.fori_loop(0, N_EXPERTS, per_expert, 0)
        lax.fori_loop(nu_ref[0], n_tiles, per_tile, 0)

    @pl.when(pl.program_id(0) == 0)
    def _():
        zero_ref[...] = jnp.zeros(zero_ref.shape, F32)
        zero_fill(lambda cp: cp.start())

    def start(r, carry):
        for k in range(2):
            _row_copy(x_ref, r, hs_ref, slot_ref[k * n_tok + base + r], sem).start()
        return carry

    lax.fori_loop(0, tm, start, 0)
    for k in range(2):
        pltpu.make_async_copy(x_ref, hs_ref.at[pl.ds(0, tm), :], sem).wait()

    @pl.when(pl.program_id(0) == pl.num_programs(0) - 1)
    def _():
        zero_fill(lambda cp: cp.wait())


def _scatter_rows(slots, pad0, npad, n_used, x, *, tm, n_tiles):
    t = x.shape[0]
    grid_spec = pltpu.PrefetchScalarGridSpec(
        num_scalar_prefetch=4,
        grid=(t // tm,),
        in_specs=[pl.BlockSpec((tm, D_MODEL), lambda i, *_: (i, 0))],
        out_specs=pl.BlockSpec(memory_space=pl.ANY),
        scratch_shapes=[
            pltpu.VMEM((MOE_ROW_TILE, D_MODEL), F32),
            pltpu.SemaphoreType.DMA(()),
            pltpu.SemaphoreType.DMA(()),
        ],
    )
    return pl.pallas_call(
        functools.partial(_scatter_kernel, n_tok=t, n_tiles=n_tiles),
        grid_spec=grid_spec,
        out_shape=jax.ShapeDtypeStruct((n_tiles * MOE_ROW_TILE, D_MODEL), F32),
        compiler_params=pltpu.CompilerParams(
            dimension_semantics=("arbitrary",), vmem_limit_bytes=VMEM_LIMIT),
        name="moe_scatter",
    )(slots, pad0, npad, n_used, x)


def _expert_kernel(te_ref, nu_ref, hs_ref, ln_ref, w1_ref, w3_ref, w2_ref, y_ref):
    i = pl.program_id(0)

    @pl.when(i < nu_ref[0])
    def _():
        h = _rms(hs_ref[...], ln_ref[...]).astype(BF16)
        up = _dot(h, w1_ref[...].astype(BF16))
        hid = (up * _sigmoid(up)) * _dot(h, w3_ref[...].astype(BF16))
        y_ref[...] = _dot(hid.astype(BF16), w2_ref[...].astype(BF16))

    @pl.when(i >= nu_ref[0])
    def _():
        y_ref[...] = jnp.zeros(y_ref.shape, F32)


def _expert_mlp(tile_expert, n_used, hs, ln2, w1, w3, w2):
    n_tiles = tile_expert.shape[0]
    grid_spec = pltpu.PrefetchScalarGridSpec(
        num_scalar_prefetch=2,
        grid=(n_tiles,),
        in_specs=[
            pl.BlockSpec((MOE_ROW_TILE, D_MODEL), lambda i, te, nu: (i, 0)),
            pl.BlockSpec((1, D_MODEL), lambda i, te, nu: (0, 0)),
            pl.BlockSpec((None, D_MODEL, D_EXPERT), lambda i, te, nu: (te[i], 0, 0)),
            pl.BlockSpec((None, D_MODEL, D_EXPERT), lambda i, te, nu: (te[i], 0, 0)),
            pl.BlockSpec((None, D_EXPERT, D_MODEL), lambda i, te, nu: (te[i], 0, 0)),
        ],
        out_specs=pl.BlockSpec((MOE_ROW_TILE, D_MODEL), lambda i, te, nu: (i, 0)),
    )
    return pl.pallas_call(
        _expert_kernel,
        grid_spec=grid_spec,
        out_shape=jax.ShapeDtypeStruct(hs.shape, F32),
        compiler_params=pltpu.CompilerParams(
            dimension_semantics=("arbitrary",), vmem_limit_bytes=VMEM_LIMIT),
        name="moe_experts",
    )(tile_expert, n_used, hs, ln2, w1, w3, w2)


def _combine_ple_kernel(slot_ref, x_ref, meta_ref, p_ref, ln_ref, wg_ref, wp_ref, y_ref, o_ref, o2_ref,
                        ybuf_ref, sem, *, n_tok, n_first):
    i = pl.program_id(0)
    tm = x_ref.shape[0]

    def gather(tile, buf):
        def start(r, carry):
            for k in range(2):
                _row_copy(y_ref, slot_ref[k * n_tok + tile * tm + r], ybuf_ref.at[buf, k], r,
                          sem.at[buf]).start()
            return carry
        lax.fori_loop(0, tm, start, 0)

    @pl.when(i == 0)
    def _():
        gather(0, 0)

    @pl.when(i + 1 < pl.num_programs(0))
    def _():
        gather(i + 1, (i + 1) % 2)

    cur = i % 2
    for k in range(2):
        pltpu.make_async_copy(y_ref.at[pl.ds(0, tm), :], ybuf_ref.at[cur, k], sem.at[cur]).wait()
    meta = meta_ref[...]
    x2 = (x_ref[...] + meta[:, META_G:META_G + 1] * ybuf_ref[cur, 0]
          + meta[:, META_G + 1:META_G + 2] * ybuf_ref[cur, 1])
    out = _ple_update(x2, p_ref[...], ln_ref[...], wg_ref[...], wp_ref[...])

    @pl.when(i < n_first)
    def _():
        o_ref[...] = out

    @pl.when(i >= n_first)
    def _():
        o2_ref[...] = out


def _combine_ple(slots, x, meta, p, ln3, wg, wp, y, *, tm, t_first):
    t = x.shape[0]
    n_first = t_first // tm
    grid_spec = pltpu.PrefetchScalarGridSpec(
        num_scalar_prefetch=1,
        grid=(t // tm,),
        in_specs=[
            pl.BlockSpec((tm, D_MODEL), lambda i, s: (i, 0)),
            pl.BlockSpec((tm, LANES), lambda i, s: (i, 0)),
            pl.BlockSpec((tm, PLE_DIM), lambda i, s: (i, 0)),
            pl.BlockSpec((1, D_MODEL), lambda i, s: (0, 0)),
            pl.BlockSpec((D_MODEL, D_MODEL), lambda i, s: (0, 0)),
            pl.BlockSpec((PLE_DIM, D_MODEL), lambda i, s: (0, 0)),
            pl.BlockSpec(memory_space=pl.ANY),
        ],
        out_specs=[
            pl.BlockSpec((tm, D_MODEL), lambda i, s: (jnp.minimum(i, n_first - 1), 0)),
            pl.BlockSpec((tm, D_MODEL), lambda i, s: (jnp.maximum(i - n_first, 0), 0)),
        ],
        scratch_shapes=[
            pltpu.VMEM((2, 2, tm, D_MODEL), F32),
            pltpu.SemaphoreType.DMA((2,)),
        ],
    )
    return pl.pallas_call(
        functools.partial(_combine_ple_kernel, n_tok=t, n_first=n_first),
        grid_spec=grid_spec,
        out_shape=[jax.ShapeDtypeStruct((t_first, D_MODEL), F32),
                   jax.ShapeDtypeStruct((t - t_first, D_MODEL), F32)],
        compiler_params=pltpu.CompilerParams(
            dimension_semantics=("arbitrary",), vmem_limit_bytes=VMEM_LIMIT),
        name="moe_combine_ple",
    )(slots, x, meta, p, ln3, wg, wp, y)


def _sparse_moe_ple(x, p, w, *, tm, t_first):
    t = x.shape[0]
    rt = MOE_ROW_TILE
    meta, cnt = _router(x, w["ln2"], w["rw_hi"], w["rw_lo"], w["rb"], tm=tm)
    counts = cnt[0, :N_EXPERTS].astype(jnp.int32)
    padded = ((counts + rt - 1) // rt) * rt
    ends = jnp.cumsum(padded)
    offs = ends - padded
    eid = meta[:, META_E:META_E + 2].astype(jnp.int32)
    rank = meta[:, META_RANK:META_RANK + 2].astype(jnp.int32)
    base = jnp.sum(jnp.where(eid[:, :, None] == jnp.arange(N_EXPERTS), offs, 0), axis=-1)
    slots = jnp.transpose(base + rank).reshape(2 * t)
    n_tiles = (2 * t + N_EXPERTS * (rt - 1)) // rt
    tile_start = jnp.arange(n_tiles, dtype=jnp.int32) * rt
    tile_expert = jnp.minimum(jnp.sum((tile_start[:, None] >= ends[None, :]).astype(jnp.int32), axis=1),
                              N_EXPERTS - 1)
    n_used = (ends[N_EXPERTS - 1] // rt).reshape(1)
    hs = _scatter_rows(slots, offs + counts, padded - counts, n_used, x, tm=tm, n_tiles=n_tiles)
    y = _expert_mlp(tile_expert, n_used, hs, w["ln2"], w["w1"], w["w3"], w["w2"])
    return _combine_ple(slots, x, meta, p, w["ln3"], w["wg"], w["wp"], y, tm=tm, t_first=t_first)


def _rope_tables(pos):
    half = HEAD_DIM // 2
    inv = ROPE_THETA ** (-jnp.arange(half, dtype=F32) / half)
    ang = pos.astype(F32)[:, None] * inv[None, :]
    cos, sin = jnp.cos(ang), jnp.sin(ang)
    return (jnp.concatenate([cos, cos, cos, cos], axis=1),
            jnp.concatenate([-sin, sin, -sin, sin], axis=1))


def _block_diag(w, per):
    n, r, _ = w.shape
    eye = jnp.eye(per, dtype=w.dtype)
    wg = w.reshape(n // per, per, r, r)
    return jnp.einsum("gpij,pq->gpiqj", wg, eye).reshape(n // per, per * r, per * r)


def _layer_weights(ln1, w_in, q_norm, k_norm, conv_w, conv_b, w_a, b_a, w_x, b_x, lam, w_br_rnn,
                   w_br_attn, w_out, ln2, w_rg, b_rg, w_re, b_re, w1, w3, w2, ln3, w_ple_gate,
                   w_ple_proj):
    o_q = 2 * D_RNN
    o_k = o_q + ATTN_W
    o_v = o_k + KV_W
    o_qi = o_v + KV_W
    o_ki = o_qi + N_IDX_HEADS * IDX_DIM
    o_wi = o_ki + IDX_DIM
    o_gr = o_wi + N_IDX_HEADS
    o_ga = o_gr + D_MODEL
    assert o_ki == COL_NR_END
    w_t = jnp.transpose(w_in)
    w_small_t = jnp.concatenate(
        [w_t[o_ki:o_gr], jnp.zeros((LANES - IDX_DIM - N_IDX_HEADS, D_MODEL), F32)], axis=0)
    ones = lambda n: jnp.ones((n,), F32)
    zeros = lambda n: jnp.zeros((n,), F32)
    n_gate = 2 * D_MODEL
    gain = jnp.concatenate([ones(o_q), jnp.tile(q_norm, N_HEADS), jnp.tile(k_norm, N_KV_HEADS),
                            ones(KV_W + N_IDX_HEADS * IDX_DIM + n_gate)])
    norm_on = jnp.concatenate([zeros(o_q), ones(ATTN_W + KV_W), zeros(KV_W + N_IDX_HEADS * IDX_DIM + n_gate)])
    rope_on = jnp.concatenate([zeros(o_q), ones(ATTN_W + KV_W), zeros(KV_W), ones(N_IDX_HEADS * IDX_DIM),
                               zeros(n_gate)])
    post = jnp.concatenate([ones(o_q), jnp.full((ATTN_W,), QK_SCALE, F32),
                            ones(2 * KV_W + N_IDX_HEADS * IDX_DIM + n_gate)])
    colctl = jnp.concatenate([jnp.stack([gain, norm_on, rope_on, post]), jnp.zeros((4, N_MAIN), F32)], axis=0)
    tn = 512
    head_of = jnp.arange(tn) // HEAD_DIM
    bd = (head_of[:, None] == head_of[None, :]).astype(BF16)
    rw = jnp.concatenate([w_re, w_rg, jnp.zeros((D_MODEL, LANES - N_EXPERTS - N_GROUPS), F32)], axis=1)
    rw_hi = rw.astype(BF16)
    rw_lo = (rw - rw_hi.astype(F32)).astype(BF16)
    rb = jnp.concatenate([b_re, b_rg, jnp.zeros((LANES - N_EXPERTS - N_GROUPS,), F32)])[None, :]
    return dict(
        ln1=ln1[None, :], w_t=w_t, w_small_t=w_small_t, colctl=colctl, bd=bd,
        cw=conv_w, cb=conv_b[None, :],
        wa_bd=_block_diag(w_a, 4).astype(BF16), ba=b_a[None, :],
        wx_bd=_block_diag(w_x, 4).astype(BF16), bx=b_x[None, :], lam=lam[None, :],
        wr=w_br_rnn.astype(BF16), wa=w_br_attn.astype(BF16), wo=w_out.astype(BF16),
        ln2=ln2[None, :], rw_hi=rw_hi, rw_lo=rw_lo, rb=rb,
        w1=w1, w3=w3, w2=w2,
        ln3=ln3[None, :], wg=w_ple_gate.astype(BF16), wp=w_ple_proj.astype(BF16),
    )


def _tail_dense(branches, p, w):
    x, rnn, attn, proj = branches
    t = x.shape[0]
    x1 = _merge(x, rnn, attn, proj, w["wr"], w["wa"], w["wo"], tm=min(t, 256))
    x2 = _moe(x1, w["ln2"], w["rw_hi"], w["rw_lo"], w["rb"], w["w1"], w["w3"], w["w2"], tm=min(t, 512))
    return _ple(x2, p, w["ln3"], w["wg"], w["wp"], tm=min(t, 512))


def _tail(branches_a, p_a, branches_b, p_b, w):
    ta, tb = branches_a[0].shape[0], branches_b[0].shape[0]
    if 2 * (ta + tb) < N_EXPERTS * MOE_ROW_TILE:
        return _tail_dense(branches_a, p_a, w), _tail_dense(branches_b, p_b, w)
    tm = math.gcd(256, ta, tb)
    mw = (w["wr"], w["wa"], w["wo"])
    x1 = _merge(*branches_a, *mw, tm=tm, out_rows=ta + tb)
    x1 = _merge(*branches_b, *mw, tm=tm, out_rows=ta + tb, row0=ta, into=x1)
    return _sparse_moe_ple(x1, jnp.concatenate([p_a, p_b], axis=0), w, tm=tm, t_first=ta)


def _prompt_layer(x, w):
    bp, tp, _ = x.shape
    xt = x.reshape(bp * tp, D_MODEL)
    cs, sn = _rope_tables(jnp.arange(tp, dtype=jnp.int32))
    proj, small, kvb, kib = _inproj(xt, w["ln1"], w["w_t"], w["w_small_t"], w["colctl"],
                                    cs, sn, w["bd"], tm=min(tp, 1024))
    conv0 = jnp.zeros((bp, 8, D_RNN), F32)
    h0 = jnp.zeros((bp, 1, D_RNN), F32)
    rnn, h_last = _rglru(proj, conv0, h0, w["cw"], w["cb"], w["wa_bd"], w["ba"], w["wx_bd"], w["bx"],
                         w["lam"], n_seq=bp, tt=min(tp, 256))
    attn = _prompt_attention(proj, small, kvb, kib, n_batch=bp, seq=tp)
    o_k = 2 * D_RNN + ATTN_W
    k = proj[:, o_k:o_k + KV_W].reshape(bp, tp, N_KV_HEADS, HEAD_DIM)
    v = proj[:, o_k + KV_W:o_k + 2 * KV_W].reshape(bp, tp, N_KV_HEADS, HEAD_DIM)
    ki = small[:, :IDX_DIM].reshape(bp, tp, IDX_DIM)
    conv_new = proj.reshape(bp, tp, N_MAIN)[:, tp - (CONV_W - 1):, :D_RNN]
    return (xt, rnn, attn, proj), (k, v, ki, conv_new, h_last.reshape(bp, D_RNN))


def _sample_layer(x, cache_k, cache_v, cache_kidx, state_conv, state_h, page_table, w):
    bs, ts, _ = x.shape
    n_pages = page_table.shape[1]
    past = n_pages * PAGE_SIZE
    xt = x.reshape(bs * ts, D_MODEL)
    cs, sn = _rope_tables(past + jnp.tile(jnp.arange(ts, dtype=jnp.int32), bs))
    proj, small, _, _ = _inproj(xt, w["ln1"], w["w_t"], w["w_small_t"], w["colctl"], cs, sn,
                                w["bd"], tm=bs * ts)
    conv0 = jnp.concatenate([jnp.zeros((bs, 8 - (CONV_W - 1), D_RNN), F32), state_conv], axis=1)
    rnn, h_last = _rglru(proj, conv0, state_h[:, None, :], w["cw"], w["cb"], w["wa_bd"], w["ba"],
                         w["wx_bd"], w["bx"], w["lam"], n_seq=bs, tt=ts)
    o_q = 2 * D_RNN
    o_qi = o_q + ATTN_W + 2 * KV_W
    qi = proj[:, o_qi:o_qi + N_IDX_HEADS * IDX_DIM].reshape(bs, ts, N_IDX_HEADS, IDX_DIM)
    qst = jnp.transpose(qi, (0, 2, 1, 3)).reshape(bs, N_IDX_HEADS * ts, IDX_DIM).astype(BF16)
    wi = small[:, IDX_DIM:IDX_DIM + N_IDX_HEADS].reshape(bs, ts, N_IDX_HEADS)
    wcol = jnp.transpose(wi, (0, 2, 1)).reshape(bs, N_IDX_HEADS * ts, 1)
    n_pool = cache_k.shape[0]
    kidx_t = jnp.transpose(cache_kidx, (0, 2, 1))
    k_t = jnp.transpose(cache_k, (0, 2, 3, 1)).reshape(n_pool, KV_W, PAGE_SIZE)
    v_t = jnp.transpose(cache_v, (0, 2, 3, 1)).reshape(n_pool, KV_W, PAGE_SIZE)
    scores = _sample_select(page_table, qst, wcol, small, kidx_t, n_new=ts)
    bias = _select_bias(scores.reshape(bs * ts, -1), n_new=ts, past=past).reshape(scores.shape)
    q = proj[:, o_q:o_q + ATTN_W].reshape(bs, ts, N_KV_HEADS, N_HEADS // N_KV_HEADS, HEAD_DIM)
    eye = jnp.eye(N_KV_HEADS, dtype=F32)
    qbd = jnp.einsum("btgjd,gk->bgjtkd", q, eye).reshape(bs, N_HEADS * ts, KV_W).astype(BF16)
    att = _sample_attend(page_table, qbd, bias, proj, k_t, v_t, n_new=ts)
    att = att.reshape(bs, N_KV_HEADS, N_HEADS // N_KV_HEADS, ts, N_KV_HEADS, HEAD_DIM)
    att = jnp.stack([att[:, g, :, :, g, :] for g in range(N_KV_HEADS)], axis=1)
    attn = jnp.transpose(att, (0, 3, 1, 2, 4)).reshape(bs * ts, ATTN_W)
    o_k = o_q + ATTN_W
    k = proj[:, o_k:o_k + KV_W].reshape(bs, ts, N_KV_HEADS, HEAD_DIM)
    v = proj[:, o_k + KV_W:o_k + 2 * KV_W].reshape(bs, ts, N_KV_HEADS, HEAD_DIM)
    ki = small[:, :IDX_DIM].reshape(bs, ts, IDX_DIM)
    conv_new = proj.reshape(bs, ts, N_MAIN)[:, ts - (CONV_W - 1):, :D_RNN]
    return (xt, rnn, attn, proj), (k, v, ki, conv_new, h_last.reshape(bs, D_RNN))


def kernel(x_prompt, x_sample, p_prompt, p_sample, cache_k, cache_v, cache_kidx, state_conv, state_h,
           page_table, ln1, w_in, q_norm, k_norm, conv_w, conv_b, w_a, b_a, w_x, b_x, lam, w_br_rnn,
           w_br_attn, w_out, ln2, w_rg, b_rg, w_re, b_re, w1, w3, w2, ln3, w_ple_gate, w_ple_proj):
    weights = (ln1, w_in, q_norm, k_norm, conv_w, conv_b, w_a, b_a, w_x, b_x, lam, w_br_rnn, w_br_attn,
               w_out, ln2, w_rg, b_rg, w_re, b_re, w1, w3, w2, ln3, w_ple_gate, w_ple_proj)
    depth = ln1.shape[0]
    yp, ys = x_prompt, x_sample
    st_p, st_s = [], []
    for i in range(depth):
        w = _layer_weights(*[wt[i] for wt in weights])
        br_p, sp = _prompt_layer(yp, w)
        br_s, ss = _sample_layer(ys, cache_k[i], cache_v[i], cache_kidx[i], state_conv[i], state_h[i],
                                 page_table, w)
        out_p, out_s = _tail(br_p, p_prompt[i].reshape(-1, PLE_DIM), br_s,
                             p_sample[i].reshape(-1, PLE_DIM), w)
        yp, ys = out_p.reshape(yp.shape), out_s.reshape(ys.shape)
        st_p.append(sp)
        st_s.append(ss)
    stack = lambda sts, j: jnp.stack([s[j] for s in sts])
    return (yp, ys, stack(st_p, 0), stack(st_p, 1), stack(st_p, 2), stack(st_p, 3), stack(st_p, 4),
            stack(st_s, 0), stack(st_s, 1), stack(st_s, 2), stack(st_s, 3), stack(st_s, 4))
```

```python
import functools
import math

import jax
import jax.numpy as jnp
import numpy as np
from jax import lax
from jax.experimental import pallas as pl
from jax.experimental.pallas import tpu as pltpu

F32 = jnp.float32
BF16 = jnp.bfloat16

D_MODEL = 2048
HEAD_DIM = 64
N_HEADS = 16
N_KV_HEADS = 4
ATTN_W = N_HEADS * HEAD_DIM
KV_W = N_KV_HEADS * HEAD_DIM
N_IDX_HEADS = 8
IDX_DIM = 64
TOPK_MAX = 256
ROPE_THETA = 10000.0
D_RNN = 1024
N_RNN_BLOCKS = 16
RNN_BLOCK = 64
CONV_W = 4
LRU_C = 8.0
N_GROUPS = 4
EXPERTS_PER_GROUP = 8
N_EXPERTS = 32
D_EXPERT = 256
PLE_DIM = 256
PAGE_SIZE = 128
EPS = 1e-6

LANES = 128
N_MAIN = 8192
COL_RAW_END = 2 * D_RNN
COL_NR_END = 4096
IDX_SCALE = (IDX_DIM ** -0.5) * (N_IDX_HEADS ** -0.5)
QK_SCALE = HEAD_DIM ** -0.5
F32_MIN = float(np.finfo(np.float32).min)
INT_MIN = -2147483648
KEY_NEG_INF = INT_MIN + 0x7FFFFF
VMEM_LIMIT = 56 * 1024 * 1024


def _dot(a, b):
    return jnp.dot(a, b, preferred_element_type=F32)


def _dot_nt(a, b):
    return lax.dot_general(a, b, (((1,), (1,)), ((), ())), preferred_element_type=F32)


def _sigmoid(x):
    return 1.0 / (1.0 + jnp.exp(-x))


def _rms(x, g):
    return x * lax.rsqrt(jnp.mean(x * x, axis=-1, keepdims=True) + EPS) * g


def _split_bf16(x):
    hi = x.astype(BF16)
    lo = (x - hi.astype(F32)).astype(BF16)
    return hi, lo


def _rope_chunks(y, c, s):
    lane = lax.broadcasted_iota(jnp.int32, (1, LANES), 1)
    first_half = (lane % HEAD_DIM) < (HEAD_DIM // 2)
    outs = []
    for k in range(y.shape[1] // LANES):
        yc = y[:, k * LANES:(k + 1) * LANES]
        partner = jnp.where(first_half, pltpu.roll(yc, LANES - HEAD_DIM // 2, 1),
                            pltpu.roll(yc, HEAD_DIM // 2, 1))
        outs.append(yc * c + partner * s)
    return outs[0] if len(outs) == 1 else jnp.concatenate(outs, axis=1)


def _inproj_kernel(x_ref, ln_ref, wlo_ref, whi_ref, ws_ref, ctl_ref, cs_ref, sn_ref, bd_ref,
                   o_ref, os_ref, kvb_ref, kib_ref, h_ref, *, tn):
    j = pl.program_id(1)

    @pl.when(j == 0)
    def _():
        hb = _rms(x_ref[...], ln_ref[...]).astype(BF16)
        h_ref[...] = hb
        ys = _dot_nt(hb, ws_ref[...].astype(BF16))
        lane = lax.broadcasted_iota(jnp.int32, (1, LANES), 1)
        roped = _rope_chunks(ys, cs_ref[...], sn_ref[...])
        os_ref[...] = jnp.where(lane < IDX_DIM, roped, ys)
        kib_ref[...] = jnp.where(lane < IDX_DIM, roped, 0.0).astype(BF16)

    @pl.when(j < COL_NR_END // tn)
    def _():
        o_ref[...] = _dot_nt(h_ref[...], wlo_ref[...].astype(BF16))

    @pl.when(j >= COL_NR_END // tn)
    def _():
        o_ref[...] = _dot_nt(h_ref[...], whi_ref[...].astype(BF16))

    j_idx = (COL_NR_END - N_IDX_HEADS * IDX_DIM) // tn

    @pl.when((j >= COL_RAW_END // tn) & (j < j_idx))
    def _():
        y = o_ref[...]
        ctl = ctl_ref[...]
        gain, norm_on, rope_on, post = ctl[0:1], ctl[1:2], ctl[2:3], ctl[3:4]
        hi, lo = _split_bf16(y * y)
        ss = _dot(hi, bd_ref[...]) + _dot(lo, bd_ref[...])
        yn = jnp.where(norm_on > 0.0, y * lax.rsqrt(ss * (1.0 / HEAD_DIM) + EPS) * gain, y)
        yr = jnp.where(rope_on > 0.0, _rope_chunks(yn, cs_ref[...], sn_ref[...]), yn)
        o_ref[...] = yr * post

    @pl.when(j == j_idx)
    def _():
        o_ref[...] = _rope_chunks(o_ref[...], cs_ref[...], sn_ref[...])

    @pl.when(j == (COL_RAW_END + ATTN_W) // tn)
    def _():
        kvb_ref[...] = o_ref[...].astype(BF16)


def _inproj(x, ln1, w_t, w_small_t, colctl, cs, sn, bd, *, tm, tn=512):
    t = x.shape[0]
    assert tn == 2 * KV_W
    grid = (t // tm, N_MAIN // tn)
    n_lo = COL_NR_END // tn
    n_rope = cs.shape[0] // tm
    gate_row0 = COL_NR_END + IDX_DIM + N_IDX_HEADS
    return pl.pallas_call(
        functools.partial(_inproj_kernel, tn=tn),
        grid=grid,
        in_specs=[
            pl.BlockSpec((tm, D_MODEL), lambda i, j: (i, 0)),
            pl.BlockSpec((1, D_MODEL), lambda i, j: (0, 0)),
            pl.BlockSpec((tn, D_MODEL), lambda i, j: (jnp.minimum(j, n_lo - 1), 0)),
            pl.BlockSpec((pl.Element(tn), pl.Element(D_MODEL)),
                         lambda i, j: (pl.multiple_of(gate_row0 + jnp.maximum(j - n_lo, 0) * tn, 8), 0)),
            pl.BlockSpec((LANES, D_MODEL), lambda i, j: (0, 0)),
            pl.BlockSpec((8, tn), lambda i, j: (0, j)),
            pl.BlockSpec((tm, LANES), lambda i, j: (i % n_rope, 0)),
            pl.BlockSpec((tm, LANES), lambda i, j: (i % n_rope, 0)),
            pl.BlockSpec((tn, tn), lambda i, j: (0, 0)),
        ],
        out_specs=[
            pl.BlockSpec((tm, tn), lambda i, j: (i, j)),
            pl.BlockSpec((tm, LANES), lambda i, j: (i, 0)),
            pl.BlockSpec((tm, 2 * KV_W), lambda i, j: (i, 0)),
            pl.BlockSpec((tm, LANES), lambda i, j: (i, 0)),
        ],
        out_shape=[
            jax.ShapeDtypeStruct((t, N_MAIN), F32),
            jax.ShapeDtypeStruct((t, LANES), F32),
            jax.ShapeDtypeStruct((t, 2 * KV_W), BF16),
            jax.ShapeDtypeStruct((t, LANES), BF16),
        ],
        scratch_shapes=[pltpu.VMEM((tm, D_MODEL), BF16)],
        compiler_params=pltpu.CompilerParams(
            dimension_semantics=("arbitrary", "arbitrary"), vmem_limit_bytes=VMEM_LIMIT),
        name="inproj",
    )(x, ln1, w_t, w_t, w_small_t, colctl, cs, sn, bd)


def _rglru_kernel(x_ref, g_ref, c0_ref, h0_ref, cw_ref, cb_ref, wa_ref, ba_ref, wx_ref, bx_ref,
                  lam_ref, o_ref, hl_ref, xs_ref, a_ref, b_ref, hc_ref):
    t = pl.program_id(1)
    tt = x_ref.shape[0]

    @pl.when(t == 0)
    def _():
        xs_ref[0:8, :] = c0_ref[...]
        hc_ref[...] = h0_ref[...]

    xs_ref[8:8 + tt, :] = x_ref[...]
    cw = cw_ref[...]
    taps = (xs_ref[5:5 + tt, :] * cw[0:1] + xs_ref[6:6 + tt, :] * cw[1:2]
            + xs_ref[7:7 + tt, :] * cw[2:3] + xs_ref[8:8 + tt, :] * cw[3:4])
    xc = cb_ref[...] + taps
    xs_ref[0:8, :] = xs_ref[tt:tt + 8, :]

    xcb = xc.astype(BF16)
    ra, ri = [], []
    for c in range(wa_ref.shape[0]):
        blk = xcb[:, c * 256:(c + 1) * 256]
        ra.append(_dot(blk, wa_ref[c]))
        ri.append(_dot(blk, wx_ref[c]))
    r = _sigmoid(jnp.concatenate(ra, axis=1) + ba_ref[...])
    ig = _sigmoid(jnp.concatenate(ri, axis=1) + bx_ref[...])
    nlam = -lam_ref[...]
    softplus = jnp.maximum(nlam, 0.0) + jnp.log1p(jnp.exp(-jnp.abs(nlam)))
    log_a = (-LRU_C) * r * softplus
    a = jnp.exp(log_a)
    u = jnp.sqrt(jnp.tanh(-log_a) * (a * a + 1.0)) * (ig * xc)

    n8 = tt // 8
    a3 = a.reshape(n8, 8, D_RNN)
    b3 = u.reshape(n8, 8, D_RNN)
    sub = lax.broadcasted_iota(jnp.int32, (1, 8, 1), 1)
    for s in (1, 2, 4):
        a_prev = pltpu.roll(a3, s, 1)
        b_prev = pltpu.roll(b3, s, 1)
        m = sub >= s
        b3 = jnp.where(m, a3 * b_prev + b3, b3)
        a3 = jnp.where(m, a3 * a_prev, a3)
    a_ref[...] = a3.reshape(tt, D_RNN)
    b_ref[...] = b3.reshape(tt, D_RNN)

    def chain(k, carry):
        i0 = pl.multiple_of(k * 8, 8)
        h8 = a_ref[pl.ds(i0, 8), :] * carry + b_ref[pl.ds(i0, 8), :]
        b_ref[pl.ds(i0, 8), :] = h8
        return h8[7:8, :]

    carry = lax.fori_loop(0, n8, chain, hc_ref[...])
    hc_ref[...] = carry
    g = g_ref[...]
    gelu = 0.5 * g * (1.0 + jnp.tanh(0.7978845608028654 * (g + 0.044715 * (g * g * g))))
    o_ref[...] = b_ref[...] * gelu

    @pl.when(t == pl.num_programs(1) - 1)
    def _():
        hl_ref[...] = carry


def _rglru(proj, conv0, h0, cw, cb, wa_bd, ba, wx_bd, bx, lam, *, n_seq, tt):
    t_total = proj.shape[0]
    nt = t_total // (n_seq * tt)
    full = lambda shape: pl.BlockSpec(shape, lambda b, t: (0,) * len(shape))
    return pl.pallas_call(
        _rglru_kernel,
        grid=(n_seq, nt),
        in_specs=[
            pl.BlockSpec((tt, D_RNN), lambda b, t: (b * nt + t, 0)),
            pl.BlockSpec((tt, D_RNN), lambda b, t: (b * nt + t, 1)),
            pl.BlockSpec((None, 8, D_RNN), lambda b, t: (b, 0, 0)),
            pl.BlockSpec((None, 1, D_RNN), lambda b, t: (b, 0, 0)),
            full((CONV_W, D_RNN)), full((1, D_RNN)),
            full(wa_bd.shape), full((1, D_RNN)),
            full(wx_bd.shape), full((1, D_RNN)),
            full((1, D_RNN)),
        ],
        out_specs=[
            pl.BlockSpec((tt, D_RNN), lambda b, t: (b * nt + t, 0)),
            pl.BlockSpec((None, 1, D_RNN), lambda b, t: (b, 0, 0)),
        ],
        out_shape=[
            jax.ShapeDtypeStruct((t_total, D_RNN), F32),
            jax.ShapeDtypeStruct((n_seq, 1, D_RNN), F32),
        ],
        scratch_shapes=[
            pltpu.VMEM((tt + 8, D_RNN), F32),
            pltpu.VMEM((tt, D_RNN), F32),
            pltpu.VMEM((tt, D_RNN), F32),
            pltpu.VMEM((1, D_RNN), F32),
        ],
        compiler_params=pltpu.CompilerParams(
            dimension_semantics=("arbitrary", "arbitrary"), vmem_limit_bytes=VMEM_LIMIT),
        name="rglru",
    )(proj, proj, conv0, h0, cw, cb, wa_bd, ba, wx_bd, bx, lam)


def _select_topk(s, kk):
    rows, n = s.shape
    kkf = float(kk)

    def key_to_f32(w):
        k = w ^ INT_MIN
        bits = jnp.where(k >= 0, k, k ^ 0x7FFFFFFF)
        return k, lax.bitcast_convert_type(bits, F32)

    def vbody(it, w):
        cand_w = w | jnp.left_shift(jnp.int32(1), 31 - it)
        cand_k, cand_f = key_to_f32(cand_w)
        cnt = jnp.sum(jnp.where(s >= cand_f, 1.0, 0.0), axis=1, keepdims=True)
        ok = (cnt >= kkf) | (cand_k < KEY_NEG_INF)
        return jnp.where(ok, cand_w, w)

    w = lax.fori_loop(0, 32, vbody, jnp.zeros((rows, 1), jnp.int32))
    _, thr = key_to_f32(w)
    gt = s > thr
    eq = s == thr
    need = kkf - jnp.sum(jnp.where(gt, 1.0, 0.0), axis=1, keepdims=True)
    col = lax.broadcasted_iota(jnp.int32, (1, n), 1)
    nbits = int(n).bit_length()

    def jbody(it, jmax):
        cand = jmax | jnp.left_shift(jnp.int32(1), nbits - 1 - it)
        cnt = jnp.sum(jnp.where(eq & (col < cand), 1.0, 0.0), axis=1, keepdims=True)
        return jnp.where(cnt <= need, cand, jmax)

    n_ge = jnp.sum(jnp.where(s >= thr, 1.0, 0.0), axis=1, keepdims=True)
    jmax = lax.cond(
        jnp.max(n_ge) > kkf,
        lambda: lax.fori_loop(0, nbits, jbody, jnp.zeros((rows, 1), jnp.int32)),
        lambda: jnp.full((rows, 1), (1 << nbits) - 1, jnp.int32))
    return gt | (eq & (col < jmax))


def _pattn_kernel(q_ref, qi_ref, sm_ref, k_ref, v_ref, ki_ref, o_ref, s_ref, *, i0, n_keys, kc, topk):
    i = pl.program_id(0)
    n_batch, tq = q_ref.shape[0], q_ref.shape[1]
    lane = lax.broadcasted_iota(jnp.int32, (1, LANES), 1)
    qpos = (i0 + i) * tq + lax.broadcasted_iota(jnp.int32, (tq, 1), 0)

    def score(b, carry):
        sm = sm_ref[b]
        qi = qi_ref[b]
        qrows, wrows = [], []
        for h in range(N_IDX_HEADS):
            blk = qi[:, (h // 2) * LANES:(h // 2 + 1) * LANES]
            if h % 2 == 1:
                blk = pltpu.roll(blk, IDX_DIM, 1)
            qrows.append(jnp.where(lane < IDX_DIM, blk, 0.0))
            wrows.append(sm[:, IDX_DIM + h:IDX_DIM + h + 1])
        qst = jnp.concatenate(qrows, axis=0).astype(BF16)
        wst = jnp.concatenate(wrows, axis=0) * IDX_SCALE
        r0 = pl.multiple_of(b * tq, tq)
        for c in range(n_keys // kc):
            s = jnp.maximum(_dot_nt(qst, ki_ref[b, c * kc:(c + 1) * kc, :]), 0.0) * wst
            sc = s[0:tq]
            for h in range(1, N_IDX_HEADS):
                sc = sc + s[h * tq:(h + 1) * tq]
            col = c * kc + lax.broadcasted_iota(jnp.int32, (1, kc), 1)
            s_ref[pl.ds(r0, tq), c * kc:(c + 1) * kc] = jnp.where(col <= qpos, sc, F32_MIN)
        return carry

    lax.fori_loop(0, n_batch, score, 0)

    sel = _select_topk(s_ref[...], topk)
    colf = lax.broadcasted_iota(jnp.int32, (1, n_keys), 1)
    qpos_all = jnp.concatenate([qpos] * n_batch, axis=0)
    s_ref[...] = jnp.where(sel & (colf <= qpos_all), 0.0, -jnp.inf)

    def attend(b, carry):
        _attend_tile(q_ref[b], s_ref[pl.ds(pl.multiple_of(b * tq, tq), tq), :],
                     k_ref[b, 0:n_keys, :], v_ref[b, 0:n_keys, :], o_ref.at[b])
        return carry

    lax.fori_loop(0, n_batch, attend, 0)


def _attend_tile(q, bias, kb, vb, o_ref):
    tq = q.shape[0]
    lane = lax.broadcasted_iota(jnp.int32, (1, LANES), 1)
    bias4 = jnp.concatenate([bias] * 4, axis=0)
    outs = [None] * N_HEADS
    for g in range(N_KV_HEADS):
        lo = (g % 2) * HEAD_DIM
        keep = (lane >= lo) & (lane < lo + HEAD_DIM)
        rows = []
        for j in range(4):
            h = 4 * g + j
            blk = q[:, (h // 2) * LANES:(h // 2 + 1) * LANES]
            if h % 2 != g % 2:
                blk = pltpu.roll(blk, HEAD_DIM, 1)
            piece = jnp.where(keep, blk, 0.0)
            zero = jnp.zeros_like(piece)
            rows.append(jnp.concatenate([piece, zero] if g < 2 else [zero, piece], axis=1))
        qbd = jnp.concatenate(rows, axis=0).astype(BF16)
        logits = _dot_nt(qbd, kb) + bias4
        m = jnp.max(logits, axis=1, keepdims=True)
        p = jnp.exp(logits - m)
        denom = jnp.sum(p, axis=1, keepdims=True)
        acc = _dot(p.astype(BF16), vb) / denom
        for j in range(4):
            outs[4 * g + j] = acc[j * tq:(j + 1) * tq, (g // 2) * LANES:(g // 2 + 1) * LANES]
    for c in range(N_HEADS // 2):
        g = (2 * c) // 4
        even, odd = outs[2 * c], outs[2 * c + 1]
        if g % 2 == 1:
            even = pltpu.roll(even, HEAD_DIM, 1)
        else:
            odd = pltpu.roll(odd, HEAD_DIM, 1)
        o_ref[:, c * LANES:(c + 1) * LANES] = jnp.where(lane < HEAD_DIM, even, odd).astype(BF16)


def _prompt_attention_part(proj3, small3, kvb, kib, *, i0, n_tiles, tq):
    n_batch, seq, _ = proj3.shape
    n_keys = (i0 + n_tiles) * tq
    kc = next(c for c in (512, 256, 128) if n_keys % c == 0)
    return pl.pallas_call(
        functools.partial(_pattn_kernel, i0=i0, n_keys=n_keys, kc=kc, topk=min(TOPK_MAX, seq // 4)),
        grid=(n_tiles,),
        in_specs=[
            pl.BlockSpec((n_batch, tq, ATTN_W), lambda i: (0, i0 + i, 2)),
            pl.BlockSpec((n_batch, tq, N_IDX_HEADS * IDX_DIM), lambda i: (0, i0 + i, 7)),
            pl.BlockSpec((n_batch, tq, LANES), lambda i: (0, i0 + i, 0)),
            pl.BlockSpec((n_batch, seq, KV_W), lambda i: (0, 0, 0)),
            pl.BlockSpec((n_batch, seq, KV_W), lambda i: (0, 0, 1)),
            pl.BlockSpec((n_batch, seq, LANES), lambda i: (0, 0, 0)),
        ],
        out_specs=pl.BlockSpec((n_batch, tq, ATTN_W), lambda i: (0, i, 0)),
        out_shape=jax.ShapeDtypeStruct((n_batch, n_tiles * tq, ATTN_W), BF16),
        scratch_shapes=[pltpu.VMEM((n_batch * tq, n_keys), F32)],
        compiler_params=pltpu.CompilerParams(
            dimension_semantics=("arbitrary",), vmem_limit_bytes=VMEM_LIMIT),
        name=f"prompt_attention_{i0}",
    )(proj3, proj3, small3, kvb, kvb, kib)


def _prompt_attention(proj, small, kvb, kib, *, n_batch, seq, tq=128, tiles_per_part=2):
    nq = seq // tq
    proj3 = proj.reshape(n_batch, seq, N_MAIN)
    small3 = small.reshape(n_batch, seq, LANES)
    kvb3 = kvb.reshape(n_batch, seq, 2 * KV_W)
    kib3 = kib.reshape(n_batch, seq, LANES)
    parts = [
        _prompt_attention_part(proj3, small3, kvb3, kib3, i0=i0,
                               n_tiles=min(tiles_per_part, nq - i0), tq=tq)
        for i0 in range(0, nq, tiles_per_part)
    ]
    return jnp.concatenate(parts, axis=1).reshape(n_batch * seq, ATTN_W)


SELECT_PAGES_PER_STEP = 64
ATTEND_PAGES_PER_STEP = 64
SUB_PAGES = 8


def _sidx_kernel(pt_ref, qst_ref, w_ref, sm_ref, kidx_hbm, o_ref, s_ref, kbuf_ref, ksem,
                 *, n_chunks, n_new, ps):
    c = pl.program_id(1)
    step = pl.program_id(0) * n_chunks + c
    n_steps = pl.num_programs(0) * n_chunks

    def fetch(s, buf):
        sb = s // n_chunks
        sc = s % n_chunks

        def start(r, carry):
            pltpu.make_async_copy(kidx_hbm.at[pt_ref[sb, sc * ps + r]], kbuf_ref.at[buf, r],
                                  ksem.at[buf]).start()
            return carry

        lax.fori_loop(0, ps, start, 0)

    @pl.when(step == 0)
    def _():
        fetch(0, 0)

    @pl.when(step + 1 < n_steps)
    def _():
        fetch(step + 1, (step + 1) % 2)

    cur = step % 2
    pltpu.make_async_copy(kidx_hbm.at[pl.ds(0, ps)], kbuf_ref.at[cur], ksem.at[cur]).wait()
    pages = [kbuf_ref.at[cur, r] for r in range(ps)]
    qst = qst_ref[...]
    w = w_ref[...] * IDX_SCALE

    def head_sum(s):
        s = jnp.maximum(s, 0.0) * w
        out = s[0:n_new]
        for h in range(1, N_IDX_HEADS):
            out = out + s[h * n_new:(h + 1) * n_new]
        return out

    for r0 in range(0, ps, SUB_PAGES):
        kt = jnp.concatenate([pages[r0 + r][...] for r in range(SUB_PAGES)], axis=1).astype(BF16)
        part = head_sum(_dot(qst, kt))
        for r in range(SUB_PAGES):
            s_ref[ps * c + r0 + r] = part[:, r * PAGE_SIZE:(r + 1) * PAGE_SIZE]

    @pl.when(c == n_chunks - 1)
    def _():
        n_past_blocks = n_chunks * ps
        past = n_past_blocks * PAGE_SIZE
        k_new = sm_ref[...][:, 0:IDX_DIM]
        kp = jnp.concatenate([k_new, jnp.zeros((PAGE_SIZE - n_new, IDX_DIM), F32)], axis=0)
        lane = lax.broadcasted_iota(jnp.int32, (n_new, LANES), 1)
        trow = lax.broadcasted_iota(jnp.int32, (n_new, LANES), 0)
        s_new = head_sum(_dot_nt(qst, kp.astype(BF16)))
        s_ref[n_past_blocks] = jnp.where(lane < n_new, jnp.where(lane <= trow, s_new, F32_MIN), -jnp.inf)
        o_ref[...] = jnp.concatenate([s_ref[k] for k in range(n_past_blocks + 1)], axis=1)


def _select_bias_kernel(s_ref, o_ref, *, n_new, past, topk):
    s = s_ref[...]
    rows, n_all = s.shape
    sel = _select_topk(s, topk)
    col = lax.broadcasted_iota(jnp.int32, (1, n_all), 1)
    tq = lax.broadcasted_iota(jnp.int32, (rows, 1), 0) % n_new
    o_ref[...] = jnp.where(sel & ((col - past) <= tq), 0.0, -jnp.inf)


def _select_bias(scores, *, n_new, past, rows_per_step=128):
    rows, n_all = scores.shape
    rows_per_step = min(rows_per_step, rows)
    return pl.pallas_call(
        functools.partial(_select_bias_kernel, n_new=n_new, past=past,
                          topk=min(TOPK_MAX, (past + n_new) // 4)),
        grid=(rows // rows_per_step,),
        in_specs=[pl.BlockSpec((rows_per_step, n_all), lambda i: (i, 0))],
        out_specs=pl.BlockSpec((rows_per_step, n_all), lambda i: (i, 0)),
        out_shape=jax.ShapeDtypeStruct((rows, n_all), F32),
        compiler_params=pltpu.CompilerParams(
            dimension_semantics=("arbitrary",), vmem_limit_bytes=VMEM_LIMIT),
        name="sample_select_bias",
    )(scores)


def _sample_select(page_table, qst, wcol, small, cache_kidx_t, *, n_new):
    n_seq, n_pages = page_table.shape
    ps = min(SELECT_PAGES_PER_STEP, n_pages)
    n_chunks = n_pages // ps
    n_all = n_pages * PAGE_SIZE + LANES
    rows = N_IDX_HEADS * n_new
    grid_spec = pltpu.PrefetchScalarGridSpec(
        num_scalar_prefetch=1,
        grid=(n_seq, n_chunks),
        in_specs=[
            pl.BlockSpec((None, rows, IDX_DIM), lambda b, c, pt: (b, 0, 0)),
            pl.BlockSpec((None, rows, 1), lambda b, c, pt: (b, 0, 0)),
            pl.BlockSpec((n_new, LANES), lambda b, c, pt: (b, 0)),
            pl.BlockSpec(memory_space=pl.ANY),
        ],
        out_specs=pl.BlockSpec((None, n_new, n_all), lambda b, c, pt: (b, 0, 0)),
        scratch_shapes=[
            pltpu.VMEM((n_pages + 1, n_new, LANES), F32),
            pltpu.VMEM((2, ps, IDX_DIM, PAGE_SIZE), F32),
            pltpu.SemaphoreType.DMA((2,)),
        ],
    )
    return pl.pallas_call(
        functools.partial(_sidx_kernel, n_chunks=n_chunks, n_new=n_new, ps=ps),
        grid_spec=grid_spec,
        out_shape=jax.ShapeDtypeStruct((n_seq, n_new, n_all), F32),
        compiler_params=pltpu.CompilerParams(
            dimension_semantics=("arbitrary", "arbitrary"), vmem_limit_bytes=VMEM_LIMIT),
        name="sample_select",
    )(page_table, qst, wcol, small, cache_kidx_t)


def _sattn_kernel(pt_ref, q_ref, bias_ref, biasn_ref, kn_ref, vn_ref, k_hbm, v_hbm, o_ref,
                  m_ref, l_ref, acc_ref, kbuf_ref, vbuf_ref, ksem, vsem, *, n_chunks, n_new, ps):
    b = pl.program_id(0)
    c = pl.program_id(1)
    rows = q_ref.shape[0]
    reps = rows // n_new
    step = b * n_chunks + c
    n_steps = pl.num_programs(0) * n_chunks

    def fetch(s, buf):
        sb = s // n_chunks
        sc = s % n_chunks

        def start(r, carry):
            page = pt_ref[sb, sc * ps + r]
            pltpu.make_async_copy(k_hbm.at[page], kbuf_ref.at[buf, r], ksem.at[buf]).start()
            pltpu.make_async_copy(v_hbm.at[page], vbuf_ref.at[buf, r], vsem.at[buf]).start()
            return carry

        lax.fori_loop(0, ps, start, 0)

    @pl.when(step == 0)
    def _():
        fetch(0, 0)

    @pl.when(step + 1 < n_steps)
    def _():
        fetch(step + 1, (step + 1) % 2)

    cur = step % 2
    pltpu.make_async_copy(k_hbm.at[pl.ds(0, ps)], kbuf_ref.at[cur], ksem.at[cur]).wait()
    pltpu.make_async_copy(v_hbm.at[pl.ds(0, ps)], vbuf_ref.at[cur], vsem.at[cur]).wait()
    kpages = [kbuf_ref.at[cur, r] for r in range(ps)]
    vpages = [vbuf_ref.at[cur, r] for r in range(ps)]

    @pl.when(c == 0)
    def _():
        m_ref[...] = jnp.full(m_ref.shape, -1e30, F32)
        l_ref[...] = jnp.zeros(l_ref.shape, F32)
        acc_ref[...] = jnp.zeros(acc_ref.shape, F32)

    def update(logits, bias, pv):
        logits = logits + jnp.concatenate([bias] * reps, axis=0)
        m_old = m_ref[...]
        m_new = jnp.maximum(m_old, jnp.max(logits, axis=1, keepdims=True))
        alpha = jnp.exp(m_old - m_new)
        p = jnp.exp(logits - m_new)
        l_ref[...] = alpha * l_ref[...] + jnp.sum(p, axis=1, keepdims=True)
        acc_ref[...] = alpha * acc_ref[...] + pv(p.astype(BF16))
        m_ref[...] = m_new

    sub_keys = SUB_PAGES * PAGE_SIZE
    logits, vts = [], []
    for r0 in range(0, ps, SUB_PAGES):
        kt = jnp.concatenate([kpages[r0 + r][...] for r in range(SUB_PAGES)], axis=1).astype(BF16)
        vts.append(jnp.concatenate([vpages[r0 + r][...] for r in range(SUB_PAGES)], axis=1).astype(BF16))
        logits.append(_dot(q_ref[...], kt))

    def pv(p):
        acc = _dot_nt(p[:, 0:sub_keys], vts[0])
        for n in range(1, len(vts)):
            acc = acc + _dot_nt(p[:, n * sub_keys:(n + 1) * sub_keys], vts[n])
        return acc

    update(jnp.concatenate(logits, axis=1), bias_ref[...], pv)

    @pl.when(c == n_chunks - 1)
    def _():
        pad = jnp.zeros((PAGE_SIZE - n_new, KV_W), F32)
        kn = jnp.concatenate([kn_ref[...], pad], axis=0).astype(BF16)
        vn = jnp.concatenate([vn_ref[...], pad], axis=0).astype(BF16)
        update(_dot_nt(q_ref[...], kn), biasn_ref[...], lambda p: _dot(p, vn))
        o_ref[...] = acc_ref[...] / l_ref[...]


def _sample_attend(page_table, qbd, bias, proj, cache_k_t, cache_v_t, *, n_new):
    n_seq, n_pages = page_table.shape
    ps = min(ATTEND_PAGES_PER_STEP, n_pages)
    n_chunks = n_pages // ps
    rows = qbd.shape[1]
    chunk_keys = ps * PAGE_SIZE
    grid_spec = pltpu.PrefetchScalarGridSpec(
        num_scalar_prefetch=1,
        grid=(n_seq, n_chunks),
        in_specs=[
            pl.BlockSpec((None, rows, KV_W), lambda b, c, pt: (b, 0, 0)),
            pl.BlockSpec((None, n_new, chunk_keys), lambda b, c, pt: (b, 0, c)),
            pl.BlockSpec((None, n_new, LANES), lambda b, c, pt: (b, 0, n_pages)),
            pl.BlockSpec((n_new, KV_W), lambda b, c, pt: (b, 12)),
            pl.BlockSpec((n_new, KV_W), lambda b, c, pt: (b, 13)),
            pl.BlockSpec(memory_space=pl.ANY),
            pl.BlockSpec(memory_space=pl.ANY),
        ],
        out_specs=pl.BlockSpec((None, rows, KV_W), lambda b, c, pt: (b, 0, 0)),
        scratch_shapes=[
            pltpu.VMEM((rows, 1), F32),
            pltpu.VMEM((rows, 1), F32),
            pltpu.VMEM((rows, KV_W), F32),
            pltpu.VMEM((2, ps, KV_W, PAGE_SIZE), F32),
            pltpu.VMEM((2, ps, KV_W, PAGE_SIZE), F32),
            pltpu.SemaphoreType.DMA((2,)),
            pltpu.SemaphoreType.DMA((2,)),
        ],
    )
    return pl.pallas_call(
        functools.partial(_sattn_kernel, n_chunks=n_chunks, n_new=n_new, ps=ps),
        grid_spec=grid_spec,
        out_shape=jax.ShapeDtypeStruct((n_seq, rows, KV_W), F32),
        compiler_params=pltpu.CompilerParams(
            dimension_semantics=("arbitrary", "arbitrary"), vmem_limit_bytes=VMEM_LIMIT),
        name="sample_attend",
    )(page_table, qbd, bias, bias, proj, proj, cache_k_t, cache_v_t)


def _merge_kernel(x_ref, rnn_ref, att_ref, gr_ref, ga_ref, wr_ref, wa_ref, wo_ref, o_ref, *, n_valid):
    i = pl.program_id(0)

    @pl.when(i < n_valid)
    def _():
        mixed = (_sigmoid(gr_ref[...]) * _dot(rnn_ref[...].astype(BF16), wr_ref[...])
                 + _sigmoid(ga_ref[...]) * _dot(att_ref[...].astype(BF16), wa_ref[...]))
        o_ref[...] = x_ref[...] + _dot(mixed.astype(BF16), wo_ref[...])

    @pl.when(i >= n_valid)
    def _():
        o_ref[...] = jnp.zeros(o_ref.shape, F32)


def _merge_into_kernel(x_ref, rnn_ref, att_ref, gr_ref, ga_ref, wr_ref, wa_ref, wo_ref, dst_ref, o_ref,
                       *, n_valid):
    del dst_ref
    _merge_kernel(x_ref, rnn_ref, att_ref, gr_ref, ga_ref, wr_ref, wa_ref, wo_ref, o_ref, n_valid=n_valid)


def _merge(x, rnn, attn, proj, wr, wa, wo, *, tm, out_rows=None, row0=0, into=None):
    t = x.shape[0]
    out_rows = t if out_rows is None else out_rows
    blk0 = row0 // tm
    n_valid = t // tm
    n_steps = n_valid if into is not None else out_rows // tm
    row = lambda i: jnp.minimum(i, n_valid - 1)
    in_specs = [
        pl.BlockSpec((tm, D_MODEL), lambda i: (row(i), 0)),
        pl.BlockSpec((tm, D_RNN), lambda i: (row(i), 0)),
        pl.BlockSpec((tm, ATTN_W), lambda i: (row(i), 0)),
        pl.BlockSpec((tm, D_MODEL), lambda i: (row(i), 2)),
        pl.BlockSpec((tm, D_MODEL), lambda i: (row(i), 3)),
        pl.BlockSpec((D_RNN, D_MODEL), lambda i: (0, 0)),
        pl.BlockSpec((ATTN_W, D_MODEL), lambda i: (0, 0)),
        pl.BlockSpec((D_MODEL, D_MODEL), lambda i: (0, 0)),
    ]
    args = (x, rnn, attn, proj, proj, wr, wa, wo)
    if into is not None:
        in_specs.append(pl.BlockSpec(memory_space=pl.ANY))
        args = args + (into,)
    return pl.pallas_call(
        functools.partial(_merge_kernel if into is None else _merge_into_kernel, n_valid=n_valid),
        grid=(n_steps,),
        in_specs=in_specs,
        out_specs=pl.BlockSpec((tm, D_MODEL), lambda i: (blk0 + i, 0)),
        out_shape=jax.ShapeDtypeStruct((out_rows, D_MODEL), F32),
        input_output_aliases={} if into is None else {len(args) - 1: 0},
        compiler_params=pltpu.CompilerParams(
            dimension_semantics=("arbitrary",), vmem_limit_bytes=VMEM_LIMIT),
        name="merge",
    )(*args)


def _route(h, rwh, rwl, rb):
    lane = lax.broadcasted_iota(jnp.int32, (1, LANES), 1)
    lanef = lane.astype(F32)
    hh, hl = _split_bf16(h)
    lg = (_dot(hh, rwh) + _dot(hl, rwh) + _dot(hh, rwl)) + rb
    is_g = (lane >= N_EXPERTS) & (lane < N_EXPERTS + N_GROUPS)
    gl = jnp.where(is_g, lg, -jnp.inf)
    gmax = jnp.max(gl, axis=1, keepdims=True)
    gprob = 1.0 / jnp.sum(jnp.exp(gl - gmax), axis=1, keepdims=True)
    gsel = jnp.min(jnp.where(is_g & (lg == gmax), lanef - N_EXPERTS, 1e9), axis=1, keepdims=True)
    in_grp = (lane < N_EXPERTS) & (jnp.floor(lanef * (1.0 / EXPERTS_PER_GROUP)) == gsel)
    v1 = jnp.where(in_grp, lg, -jnp.inf)
    t1 = jnp.max(v1, axis=1, keepdims=True)
    i1 = jnp.min(jnp.where(in_grp & (lg == t1), lanef, 1e9), axis=1, keepdims=True)
    rest = in_grp & (lanef != i1)
    v2 = jnp.where(rest, lg, -jnp.inf)
    t2 = jnp.max(v2, axis=1, keepdims=True)
    i2 = jnp.min(jnp.where(rest & (lg == t2), lanef, 1e9), axis=1, keepdims=True)
    d = jnp.exp(t2 - t1)
    return i1, i2, gprob / (1.0 + d), gprob * d / (1.0 + d)


def _moe_kernel(x_ref, ln_ref, rwh_ref, rwl_ref, rb_ref, w1_ref, w3_ref, w2_ref, o_ref, h_ref, gate_ref):
    e = pl.program_id(1)
    lane = lax.broadcasted_iota(jnp.int32, (1, LANES), 1)

    @pl.when(e == 0)
    def _():
        h = _rms(x_ref[...], ln_ref[...])
        h_ref[...] = h.astype(BF16)
        i1, i2, g1, g2 = _route(h, rwh_ref[...], rwl_ref[...], rb_ref[...])
        lanef = lane.astype(F32)
        gate_ref[...] = jnp.where(lanef == i1, g1, 0.0) + jnp.where(lanef == i2, g2, 0.0)

    ge = jnp.sum(jnp.where(lane == e, gate_ref[...], 0.0), axis=1, keepdims=True)
    up = _dot(h_ref[...], w1_ref[...].astype(BF16))
    hid = (up * _sigmoid(up)) * _dot(h_ref[...], w3_ref[...].astype(BF16))
    contrib = _dot((hid * ge).astype(BF16), w2_ref[...].astype(BF16))

    @pl.when(e == 0)
    def _():
        o_ref[...] = x_ref[...] + contrib

    @pl.when(e > 0)
    def _():
        o_ref[...] += contrib


def _moe(x, ln2, rw_hi, rw_lo, rb, w1, w3, w2, *, tm):
    t = x.shape[0]
    return pl.pallas_call(
        _moe_kernel,
        grid=(t // tm, N_EXPERTS),
        in_specs=[
            pl.BlockSpec((tm, D_MODEL), lambda i, e: (i, 0)),
            pl.BlockSpec((1, D_MODEL), lambda i, e: (0, 0)),
            pl.BlockSpec((D_MODEL, LANES), lambda i, e: (0, 0)),
            pl.BlockSpec((D_MODEL, LANES), lambda i, e: (0, 0)),
            pl.BlockSpec((1, LANES), lambda i, e: (0, 0)),
            pl.BlockSpec((None, D_MODEL, D_EXPERT), lambda i, e: (e, 0, 0)),
            pl.BlockSpec((None, D_MODEL, D_EXPERT), lambda i, e: (e, 0, 0)),
            pl.BlockSpec((None, D_EXPERT, D_MODEL), lambda i, e: (e, 0, 0)),
        ],
        out_specs=pl.BlockSpec((tm, D_MODEL), lambda i, e: (i, 0)),
        out_shape=jax.ShapeDtypeStruct((t, D_MODEL), F32),
        scratch_shapes=[pltpu.VMEM((tm, D_MODEL), BF16), pltpu.VMEM((tm, LANES), F32)],
        compiler_params=pltpu.CompilerParams(
            dimension_semantics=("arbitrary", "arbitrary"), vmem_limit_bytes=VMEM_LIMIT),
        name="moe",
    )(x, ln2, rw_hi, rw_lo, rb, w1, w3, w2)


def _ple_update(x2, p, ln, wg, wp):
    gate = _sigmoid(_dot(_rms(x2, ln).astype(BF16), wg))
    return x2 + gate * _dot(p.astype(BF16), wp)


def _ple_kernel(x_ref, p_ref, ln_ref, wg_ref, wp_ref, o_ref):
    o_ref[...] = _ple_update(x_ref[...], p_ref[...], ln_ref[...], wg_ref[...], wp_ref[...])


def _ple(x, p, ln3, wg, wp, *, tm):
    t = x.shape[0]
    return pl.pallas_call(
        _ple_kernel,
        grid=(t // tm,),
        in_specs=[
            pl.BlockSpec((tm, D_MODEL), lambda i: (i, 0)),
            pl.BlockSpec((tm, PLE_DIM), lambda i: (i, 0)),
            pl.BlockSpec((1, D_MODEL), lambda i: (0, 0)),
            pl.BlockSpec((D_MODEL, D_MODEL), lambda i: (0, 0)),
            pl.BlockSpec((PLE_DIM, D_MODEL), lambda i: (0, 0)),
        ],
        out_specs=pl.BlockSpec((tm, D_MODEL), lambda i: (i, 0)),
        out_shape=jax.ShapeDtypeStruct((t, D_MODEL), F32),
        compiler_params=pltpu.CompilerParams(
            dimension_semantics=("arbitrary",), vmem_limit_bytes=VMEM_LIMIT),
        name="ple",
    )(x, p, ln3, wg, wp)


MOE_ROW_TILE = 256
META_E, META_G, META_RANK = 0, 2, 4


def _router_kernel(x_ref, ln_ref, rwh_ref, rwl_ref, rb_ref, tri_ref, meta_ref, cnt_ref, carry_ref):
    i = pl.program_id(0)
    lane = lax.broadcasted_iota(jnp.int32, (1, LANES), 1)
    lanef = lane.astype(F32)

    @pl.when(i == 0)
    def _():
        carry_ref[...] = jnp.zeros(carry_ref.shape, F32)

    i1, i2, g1, g2 = _route(_rms(x_ref[...], ln_ref[...]), rwh_ref[...], rwl_ref[...], rb_ref[...])
    onehot = jnp.where((lanef == i1) | (lanef == i2), 1.0, 0.0)
    before = _dot(tri_ref[...], onehot.astype(BF16)) + carry_ref[...]
    r1 = jnp.sum(jnp.where(lanef == i1, before, 0.0), axis=1, keepdims=True)
    r2 = jnp.sum(jnp.where(lanef == i2, before, 0.0), axis=1, keepdims=True)
    carry_ref[...] += jnp.sum(onehot, axis=0, keepdims=True)
    rec = jnp.zeros((x_ref.shape[0], LANES), F32)
    for k, val in ((META_E, i1), (META_E + 1, i2), (META_G, g1), (META_G + 1, g2),
                   (META_RANK, r1), (META_RANK + 1, r2)):
        rec = jnp.where(lane == k, val, rec)
    meta_ref[...] = rec
    cnt_ref[...] = jnp.broadcast_to(carry_ref[...], cnt_ref.shape)


def _router(x, ln2, rw_hi, rw_lo, rb, *, tm):
    t = x.shape[0]
    r = lax.broadcasted_iota(jnp.int32, (tm, tm), 0)
    c = lax.broadcasted_iota(jnp.int32, (tm, tm), 1)
    tri = (c < r).astype(BF16)
    return pl.pallas_call(
        _router_kernel,
        grid=(t // tm,),
        in_specs=[
            pl.BlockSpec((tm, D_MODEL), lambda i: (i, 0)),
            pl.BlockSpec((1, D_MODEL), lambda i: (0, 0)),
            pl.BlockSpec((D_MODEL, LANES), lambda i: (0, 0)),
            pl.BlockSpec((D_MODEL, LANES), lambda i: (0, 0)),
            pl.BlockSpec((1, LANES), lambda i: (0, 0)),
            pl.BlockSpec((tm, tm), lambda i: (0, 0)),
        ],
        out_specs=[
            pl.BlockSpec((tm, LANES), lambda i: (i, 0)),
            pl.BlockSpec((8, LANES), lambda i: (0, 0)),
        ],
        out_shape=[
            jax.ShapeDtypeStruct((t, LANES), F32),
            jax.ShapeDtypeStruct((8, LANES), F32),
        ],
        scratch_shapes=[pltpu.VMEM((1, LANES), F32)],
        compiler_params=pltpu.CompilerParams(
            dimension_semantics=("arbitrary",), vmem_limit_bytes=VMEM_LIMIT),
        name="moe_router",
    )(x, ln2, rw_hi, rw_lo, rb, tri)


def _row_copy(src_ref, src_row, dst_ref, dst_row, sem):
    return pltpu.make_async_copy(src_ref.at[pl.ds(src_row, 1), :], dst_ref.at[pl.ds(dst_row, 1), :], sem)


def _scatter_kernel(slot_ref, pad0_ref, npad_ref, nu_ref, x_ref, hs_ref, zero_ref, sem, zsem,
                    *, n_tok, n_tiles):
    tm = x_ref.shape[0]
    base = pl.program_id(0) * tm
    rt = zero_ref.shape[0]
    pieces = [1 << k for k in range(rt.bit_length() - 2, 2, -1)]

    def zero_fill(go):
        def per_expert(e, carry):
            first = pad0_ref[e]
            first8 = (first + 7) & -8
            for k in range(7):
                @pl.when(first + k < jnp.minimum(first8, first + npad_ref[e]))
                def _():
                    go(pltpu.make_async_copy(zero_ref.at[pl.ds(0, 1), :],
                                             hs_ref.at[pl.ds(first + k, 1), :], zsem))
            n = jnp.maximum(first + npad_ref[e] - first8, 0)
            for bit in pieces:
                @pl.when((n & bit) != 0)
                def _():
                    row = pl.multiple_of(first8 + (n & (-2 * bit)), 8)
                    go(pltpu.make_async_copy(zero_ref.at[pl.ds(0, bit), :],
                                             hs_ref.at[pl.ds(row, bit), :], zsem))
            return carry

        def per_tile(tile, carry):
            go(pltpu.make_async_copy(zero_ref, hs_ref.at[pl.ds(pl.multiple_of(tile * rt, rt), rt), :],
                                     zsem))
            return carry

        lax.fori_loop(0, N_EXPERTS, per_expert, 0)
        lax.fori_loop(nu_ref[0], n_tiles, per_tile, 0)

    @pl.when(pl.program_id(0) == 0)
    def _():
        zero_ref[...] = jnp.zeros(zero_ref.shape, F32)
        zero_fill(lambda cp: cp.start())
        zero_fill(lambda cp: cp.wait())

    def start(r, carry):
        for k in range(2):
            _row_copy(x_ref, r, hs_ref, slot_ref[k * n_tok + base + r], sem).start(priority=k)
        return carry

    lax.fori_loop(0, tm, start, 0)
    for k in range(2):
        pltpu.make_async_copy(x_ref, hs_ref.at[pl.ds(0, tm), :], sem).wait()


def _scatter_rows(slots, pad0, npad, n_used, x, *, tm, n_tiles):
    t = x.shape[0]
    grid_spec = pltpu.PrefetchScalarGridSpec(
        num_scalar_prefetch=4,
        grid=(t // tm,),
        in_specs=[pl.BlockSpec((tm, D_MODEL), lambda i, *_: (i, 0))],
        out_specs=pl.BlockSpec(memory_space=pl.ANY),
        scratch_shapes=[
            pltpu.VMEM((MOE_ROW_TILE, D_MODEL), F32),
            pltpu.SemaphoreType.DMA(()),
            pltpu.SemaphoreType.DMA(()),
        ],
    )
    return pl.pallas_call(
        functools.partial(_scatter_kernel, n_tok=t, n_tiles=n_tiles),
        grid_spec=grid_spec,
        out_shape=jax.ShapeDtypeStruct((n_tiles * MOE_ROW_TILE, D_MODEL), F32),
        compiler_params=pltpu.CompilerParams(
            dimension_semantics=("arbitrary",), vmem_limit_bytes=VMEM_LIMIT),
        name="moe_scatter",
    )(slots, pad0, npad, n_used, x)


def _expert_kernel(te_ref, nu_ref, hs_ref, ln_ref, w1_ref, w3_ref, w2_ref, y_ref):
    i = pl.program_id(0)

    @pl.when(i < nu_ref[0])
    def _():
        h = _rms(hs_ref[...], ln_ref[...]).astype(BF16)
        up = _dot(h, w1_ref[...].astype(BF16))
        hid = (up * _sigmoid(up)) * _dot(h, w3_ref[...].astype(BF16))
        y_ref[...] = _dot(hid.astype(BF16), w2_ref[...].astype(BF16))

    @pl.when(i >= nu_ref[0])
    def _():
        y_ref[...] = jnp.zeros(y_ref.shape, F32)


def _expert_mlp(tile_expert, n_used, hs, ln2, w1, w3, w2):
    n_tiles = tile_expert.shape[0]
    grid_spec = pltpu.PrefetchScalarGridSpec(
        num_scalar_prefetch=2,
        grid=(n_tiles,),
        in_specs=[
            pl.BlockSpec((MOE_ROW_TILE, D_MODEL), lambda i, te, nu: (i, 0)),
            pl.BlockSpec((1, D_MODEL), lambda i, te, nu: (0, 0)),
            pl.BlockSpec((None, D_MODEL, D_EXPERT), lambda i, te, nu: (te[i], 0, 0)),
            pl.BlockSpec((None, D_MODEL, D_EXPERT), lambda i, te, nu: (te[i], 0, 0)),
            pl.BlockSpec((None, D_EXPERT, D_MODEL), lambda i, te, nu: (te[i], 0, 0)),
        ],
        out_specs=pl.BlockSpec((MOE_ROW_TILE, D_MODEL), lambda i, te, nu: (i, 0)),
    )
    return pl.pallas_call(
        _expert_kernel,
        grid_spec=grid_spec,
        out_shape=jax.ShapeDtypeStruct(hs.shape, F32),
        compiler_params=pltpu.CompilerParams(
            dimension_semantics=("arbitrary",), vmem_limit_bytes=VMEM_LIMIT),
        name="moe_experts",
    )(tile_expert, n_used, hs, ln2, w1, w3, w2)


def _combine_ple_kernel(slot_ref, x_ref, meta_ref, p_ref, ln_ref, wg_ref, wp_ref, y_ref, o_ref, o2_ref,
                        ybuf_ref, sem, *, n_tok, n_first):
    i = pl.program_id(0)
    tm = x_ref.shape[0]

    def gather(tile, buf):
        def start(r, carry):
            for k in range(2):
                _row_copy(y_ref, slot_ref[k * n_tok + tile * tm + r], ybuf_ref.at[buf, k], r,
                          sem.at[buf]).start(priority=k)
            return carry
        lax.fori_loop(0, tm, start, 0)

    @pl.when(i == 0)
    def _():
        gather(0, 0)

    @pl.when(i + 1 < pl.num_programs(0))
    def _():
        gather(i + 1, (i + 1) % 2)

    cur = i % 2
    for k in range(2):
        pltpu.make_async_copy(y_ref.at[pl.ds(0, tm), :], ybuf_ref.at[cur, k], sem.at[cur]).wait()
    meta = meta_ref[...]
    x2 = (x_ref[...] + meta[:, META_G:META_G + 1] * ybuf_ref[cur, 0]
          + meta[:, META_G + 1:META_G + 2] * ybuf_ref[cur, 1])
    out = _ple_update(x2, p_ref[...], ln_ref[...], wg_ref[...], wp_ref[...])

    @pl.when(i < n_first)
    def _():
        o_ref[...] = out

    @pl.when(i >= n_first)
    def _():
        o2_ref[...] = out


def _combine_ple(slots, x, meta, p, ln3, wg, wp, y, *, tm, t_first):
    t = x.shape[0]
    n_first = t_first // tm
    grid_spec = pltpu.PrefetchScalarGridSpec(
        num_scalar_prefetch=1,
        grid=(t // tm,),
        in_specs=[
            pl.BlockSpec((tm, D_MODEL), lambda i, s: (i, 0)),
            pl.BlockSpec((tm, LANES), lambda i, s: (i, 0)),
            pl.BlockSpec((tm, PLE_DIM), lambda i, s: (i, 0)),
            pl.BlockSpec((1, D_MODEL), lambda i, s: (0, 0)),
            pl.BlockSpec((D_MODEL, D_MODEL), lambda i, s: (0, 0)),
            pl.BlockSpec((PLE_DIM, D_MODEL), lambda i, s: (0, 0)),
            pl.BlockSpec(memory_space=pl.ANY),
        ],
        out_specs=[
            pl.BlockSpec((tm, D_MODEL), lambda i, s: (jnp.minimum(i, n_first - 1), 0)),
            pl.BlockSpec((tm, D_MODEL), lambda i, s: (jnp.maximum(i - n_first, 0), 0)),
        ],
        scratch_shapes=[
            pltpu.VMEM((2, 2, tm, D_MODEL), F32),
            pltpu.SemaphoreType.DMA((2,)),
        ],
    )
    return pl.pallas_call(
        functools.partial(_combine_ple_kernel, n_tok=t, n_first=n_first),
        grid_spec=grid_spec,
        out_shape=[jax.ShapeDtypeStruct((t_first, D_MODEL), F32),
                   jax.ShapeDtypeStruct((t - t_first, D_MODEL), F32)],
        compiler_params=pltpu.CompilerParams(
            dimension_semantics=("arbitrary",), vmem_limit_bytes=VMEM_LIMIT),
        name="moe_combine_ple",
    )(slots, x, meta, p, ln3, wg, wp, y)


def _sparse_moe_ple(x, p, w, *, tm, t_first):
    t = x.shape[0]
    rt = MOE_ROW_TILE
    meta, cnt = _router(x, w["ln2"], w["rw_hi"], w["rw_lo"], w["rb"], tm=tm)
    counts = cnt[0, :N_EXPERTS].astype(jnp.int32)
    padded = ((counts + rt - 1) // rt) * rt
    ends = jnp.cumsum(padded)
    offs = ends - padded
    eid = meta[:, META_E:META_E + 2].astype(jnp.int32)
    rank = meta[:, META_RANK:META_RANK + 2].astype(jnp.int32)
    base = jnp.sum(jnp.where(eid[:, :, None] == jnp.arange(N_EXPERTS), offs, 0), axis=-1)
    slots = jnp.transpose(base + rank).reshape(2 * t)
    n_tiles = (2 * t + N_EXPERTS * (rt - 1)) // rt
    tile_start = jnp.arange(n_tiles, dtype=jnp.int32) * rt
    tile_expert = jnp.minimum(jnp.sum((tile_start[:, None] >= ends[None, :]).astype(jnp.int32), axis=1),
                              N_EXPERTS - 1)
    n_used = (ends[N_EXPERTS - 1] // rt).reshape(1)
    hs = _scatter_rows(slots, offs + counts, padded - counts, n_used, x, tm=tm, n_tiles=n_tiles)
    y = _expert_mlp(tile_expert, n_used, hs, w["ln2"], w["w1"], w["w3"], w["w2"])
    return _combine_ple(slots, x, meta, p, w["ln3"], w["wg"], w["wp"], y, tm=tm, t_first=t_first)


def _rope_tables(pos):
    half = HEAD_DIM // 2
    inv = ROPE_THETA ** (-jnp.arange(half, dtype=F32) / half)
    ang = pos.astype(F32)[:, None] * inv[None, :]
    cos, sin = jnp.cos(ang), jnp.sin(ang)
    return (jnp.concatenate([cos, cos, cos, cos], axis=1),
            jnp.concatenate([-sin, sin, -sin, sin], axis=1))


def _block_diag(w, per):
    n, r, _ = w.shape
    eye = jnp.eye(per, dtype=w.dtype)
    wg = w.reshape(n // per, per, r, r)
    return jnp.einsum("gpij,pq->gpiqj", wg, eye).reshape(n // per, per * r, per * r)


def _layer_weights(ln1, w_in, q_norm, k_norm, conv_w, conv_b, w_a, b_a, w_x, b_x, lam, w_br_rnn,
                   w_br_attn, w_out, ln2, w_rg, b_rg, w_re, b_re, w1, w3, w2, ln3, w_ple_gate,
                   w_ple_proj):
    o_q = 2 * D_RNN
    o_k = o_q + ATTN_W
    o_v = o_k + KV_W
    o_qi = o_v + KV_W
    o_ki = o_qi + N_IDX_HEADS * IDX_DIM
    o_wi = o_ki + IDX_DIM
    o_gr = o_wi + N_IDX_HEADS
    o_ga = o_gr + D_MODEL
    assert o_ki == COL_NR_END
    w_t = jnp.transpose(w_in)
    w_small_t = jnp.concatenate(
        [w_t[o_ki:o_gr], jnp.zeros((LANES - IDX_DIM - N_IDX_HEADS, D_MODEL), F32)], axis=0)
    ones = lambda n: jnp.ones((n,), F32)
    zeros = lambda n: jnp.zeros((n,), F32)
    n_gate = 2 * D_MODEL
    gain = jnp.concatenate([ones(o_q), jnp.tile(q_norm, N_HEADS), jnp.tile(k_norm, N_KV_HEADS),
                            ones(KV_W + N_IDX_HEADS * IDX_DIM + n_gate)])
    norm_on = jnp.concatenate([zeros(o_q), ones(ATTN_W + KV_W), zeros(KV_W + N_IDX_HEADS * IDX_DIM + n_gate)])
    rope_on = jnp.concatenate([zeros(o_q), ones(ATTN_W + KV_W), zeros(KV_W), ones(N_IDX_HEADS * IDX_DIM),
                               zeros(n_gate)])
    post = jnp.concatenate([ones(o_q), jnp.full((ATTN_W,), QK_SCALE, F32),
                            ones(2 * KV_W + N_IDX_HEADS * IDX_DIM + n_gate)])
    colctl = jnp.concatenate([jnp.stack([gain, norm_on, rope_on, post]), jnp.zeros((4, N_MAIN), F32)], axis=0)
    tn = 512
    head_of = jnp.arange(tn) // HEAD_DIM
    bd = (head_of[:, None] == head_of[None, :]).astype(BF16)
    rw = jnp.concatenate([w_re, w_rg, jnp.zeros((D_MODEL, LANES - N_EXPERTS - N_GROUPS), F32)], axis=1)
    rw_hi = rw.astype(BF16)
    rw_lo = (rw - rw_hi.astype(F32)).astype(BF16)
    rb = jnp.concatenate([b_re, b_rg, jnp.zeros((LANES - N_EXPERTS - N_GROUPS,), F32)])[None, :]
    return dict(
        ln1=ln1[None, :], w_t=w_t, w_small_t=w_small_t, colctl=colctl, bd=bd,
        cw=conv_w, cb=conv_b[None, :],
        wa_bd=_block_diag(w_a, 4).astype(BF16), ba=b_a[None, :],
        wx_bd=_block_diag(w_x, 4).astype(BF16), bx=b_x[None, :], lam=lam[None, :],
        wr=w_br_rnn.astype(BF16), wa=w_br_attn.astype(BF16), wo=w_out.astype(BF16),
        ln2=ln2[None, :], rw_hi=rw_hi, rw_lo=rw_lo, rb=rb,
        w1=w1, w3=w3, w2=w2,
        ln3=ln3[None, :], wg=w_ple_gate.astype(BF16), wp=w_ple_proj.astype(BF16),
    )


def _tail_dense(branches, p, w):
    x, rnn, attn, proj = branches
    t = x.shape[0]
    x1 = _merge(x, rnn, attn, proj, w["wr"], w["wa"], w["wo"], tm=min(t, 256))
    x2 = _moe(x1, w["ln2"], w["rw_hi"], w["rw_lo"], w["rb"], w["w1"], w["w3"], w["w2"], tm=min(t, 512))
    return _ple(x2, p, w["ln3"], w["wg"], w["wp"], tm=min(t, 512))


def _tail(branches_a, p_a, branches_b, p_b, w):
    ta, tb = branches_a[0].shape[0], branches_b[0].shape[0]
    if 2 * (ta + tb) < N_EXPERTS * MOE_ROW_TILE:
        return _tail_dense(branches_a, p_a, w), _tail_dense(branches_b, p_b, w)
    tm = math.gcd(256, ta, tb)
    mw = (w["wr"], w["wa"], w["wo"])
    x1 = _merge(*branches_a, *mw, tm=tm, out_rows=ta + tb)
    x1 = _merge(*branches_b, *mw, tm=tm, out_rows=ta + tb, row0=ta, into=x1)
    return _sparse_moe_ple(x1, jnp.concatenate([p_a, p_b], axis=0), w, tm=tm, t_first=ta)


def _prompt_layer(x, w):
    bp, tp, _ = x.shape
    xt = x.reshape(bp * tp, D_MODEL)
    cs, sn = _rope_tables(jnp.arange(tp, dtype=jnp.int32))
    proj, small, kvb, kib = _inproj(xt, w["ln1"], w["w_t"], w["w_small_t"], w["colctl"],
                                    cs, sn, w["bd"], tm=min(tp, 1024))
    conv0 = jnp.zeros((bp, 8, D_RNN), F32)
    h0 = jnp.zeros((bp, 1, D_RNN), F32)
    rnn, h_last = _rglru(proj, conv0, h0, w["cw"], w["cb"], w["wa_bd"], w["ba"], w["wx_bd"], w["bx"],
                         w["lam"], n_seq=bp, tt=min(tp, 256))
    attn = _prompt_attention(proj, small, kvb, kib, n_batch=bp, seq=tp)
    o_k = 2 * D_RNN + ATTN_W
    k = proj[:, o_k:o_k + KV_W].reshape(bp, tp, N_KV_HEADS, HEAD_DIM)
    v = proj[:, o_k + KV_W:o_k + 2 * KV_W].reshape(bp, tp, N_KV_HEADS, HEAD_DIM)
    ki = small[:, :IDX_DIM].reshape(bp, tp, IDX_DIM)
    conv_new = proj.reshape(bp, tp, N_MAIN)[:, tp - (CONV_W - 1):, :D_RNN]
    return (xt, rnn, attn, proj), (k, v, ki, conv_new, h_last.reshape(bp, D_RNN))


def _sample_layer(x, cache_k, cache_v, cache_kidx, state_conv, state_h, page_table, w):
    bs, ts, _ = x.shape
    n_pages = page_table.shape[1]
    past = n_pages * PAGE_SIZE
    xt = x.reshape(bs * ts, D_MODEL)
    cs, sn = _rope_tables(past + jnp.tile(jnp.arange(ts, dtype=jnp.int32), bs))
    proj, small, _, _ = _inproj(xt, w["ln1"], w["w_t"], w["w_small_t"], w["colctl"], cs, sn,
                                w["bd"], tm=bs * ts)
    conv0 = jnp.concatenate([jnp.zeros((bs, 8 - (CONV_W - 1), D_RNN), F32), state_conv], axis=1)
    rnn, h_last = _rglru(proj, conv0, state_h[:, None, :], w["cw"], w["cb"], w["wa_bd"], w["ba"],
                         w["wx_bd"], w["bx"], w["lam"], n_seq=bs, tt=ts)
    o_q = 2 * D_RNN
    o_qi = o_q + ATTN_W + 2 * KV_W
    qi = proj[:, o_qi:o_qi + N_IDX_HEADS * IDX_DIM].reshape(bs, ts, N_IDX_HEADS, IDX_DIM)
    qst = jnp.transpose(qi, (0, 2, 1, 3)).reshape(bs, N_IDX_HEADS * ts, IDX_DIM).astype(BF16)
    wi = small[:, IDX_DIM:IDX_DIM + N_IDX_HEADS].reshape(bs, ts, N_IDX_HEADS)
    wcol = jnp.transpose(wi, (0, 2, 1)).reshape(bs, N_IDX_HEADS * ts, 1)
    n_pool = cache_k.shape[0]
    kidx_t = jnp.transpose(cache_kidx, (0, 2, 1))
    k_t = jnp.transpose(cache_k, (0, 2, 3, 1)).reshape(n_pool, KV_W, PAGE_SIZE)
    v_t = jnp.transpose(cache_v, (0, 2, 3, 1)).reshape(n_pool, KV_W, PAGE_SIZE)
    scores = _sample_select(page_table, qst, wcol, small, kidx_t, n_new=ts)
    bias = _select_bias(scores.reshape(bs * ts, -1), n_new=ts, past=past).reshape(scores.shape)
    q = proj[:, o_q:o_q + ATTN_W].reshape(bs, ts, N_KV_HEADS, N_HEADS // N_KV_HEADS, HEAD_DIM)
    eye = jnp.eye(N_KV_HEADS, dtype=F32)
    qbd = jnp.einsum("btgjd,gk->bgjtkd", q, eye).reshape(bs, N_HEADS * ts, KV_W).astype(BF16)
    att = _sample_attend(page_table, qbd, bias, proj, k_t, v_t, n_new=ts)
    att = att.reshape(bs, N_KV_HEADS, N_HEADS // N_KV_HEADS, ts, N_KV_HEADS, HEAD_DIM)
    att = jnp.stack([att[:, g, :, :, g, :] for g in range(N_KV_HEADS)], axis=1)
    attn = jnp.transpose(att, (0, 3, 1, 2, 4)).reshape(bs * ts, ATTN_W)
    o_k = o_q + ATTN_W
    k = proj[:, o_k:o_k + KV_W].reshape(bs, ts, N_KV_HEADS, HEAD_DIM)
    v = proj[:, o_k + KV_W:o_k + 2 * KV_W].reshape(bs, ts, N_KV_HEADS, HEAD_DIM)
    ki = small[:, :IDX_DIM].reshape(bs, ts, IDX_DIM)
    conv_new = proj.reshape(bs, ts, N_MAIN)[:, ts - (CONV_W - 1):, :D_RNN]
    return (xt, rnn, attn, proj), (k, v, ki, conv_new, h_last.reshape(bs, D_RNN))


def kernel(x_prompt, x_sample, p_prompt, p_sample, cache_k, cache_v, cache_kidx, state_conv, state_h,
           page_table, ln1, w_in, q_norm, k_norm, conv_w, conv_b, w_a, b_a, w_x, b_x, lam, w_br_rnn,
           w_br_attn, w_out, ln2, w_rg, b_rg, w_re, b_re, w1, w3, w2, ln3, w_ple_gate, w_ple_proj):
    weights = (ln1, w_in, q_norm, k_norm, conv_w, conv_b, w_a, b_a, w_x, b_x, lam, w_br_rnn, w_br_attn,
               w_out, ln2, w_rg, b_rg, w_re, b_re, w1, w3, w2, ln3, w_ple_gate, w_ple_proj)
    depth = ln1.shape[0]
    yp, ys = x_prompt, x_sample
    st_p, st_s = [], []
    for i in range(depth):
        w = _layer_weights(*[wt[i] for wt in weights])
        br_p, sp = _prompt_layer(yp, w)
        br_s, ss = _sample_layer(ys, cache_k[i], cache_v[i], cache_kidx[i], state_conv[i], state_h[i],
                                 page_table, w)
        out_p, out_s = _tail(br_p, p_prompt[i].reshape(-1, PLE_DIM), br_s,
                             p_sample[i].reshape(-1, PLE_DIM), w)
        yp, ys = out_p.reshape(yp.shape), out_s.reshape(ys.shape)
        st_p.append(sp)
        st_s.append(ss)
    stack = lambda sts, j: jnp.stack([s[j] for s in sts])
    return (yp, ys, stack(st_p, 0), stack(st_p, 1), stack(st_p, 2), stack(st_p, 3), stack(st_p, 4),
            stack(st_s, 0), stack(st_s, 1), stack(st_s, 2), stack(st_s, 3), stack(st_s, 4))
```
